```python
import math
import jax, jax.numpy as jnp
from jax import lax
import numpy as np

D_MODEL = 2048
BATCH = 4
SEQ = 2048
DEPTH = 2
DEC_BATCH = 32
DEC_SEQ = 4
PAST_LEN = 8192
PAGE_SIZE = 128

HEAD_DIM = 128
H_FOX = 8
H_NSA = 8
G_NSA = 2
HPG_NSA = H_NSA // G_NSA
CMP_STRIDE = 16
CMP_BLOCK = 2 * CMP_STRIDE
SEL_BLOCK = 64
N_SELECT = 16
WINDOW = 512
FORCE_SCORE = 1.0e4
Q_BLOCK = 128
SEL_Q_BLOCK = 64
H_GLA = 4
DK_GLA = 128
DV_GLA = 256
GLA_RANK = 16
GLA_GATE_NORM = 16.0
H_HG = 8
DK_HG = 128
DV_HG = 128
LA_CHUNK = 64
D_MIX_EVEN = (H_FOX + H_NSA) * HEAD_DIM
D_MIX_ODD = H_GLA * DV_GLA + H_HG * DV_HG
D_FF = ((8 * D_MODEL // 3 + 255) // 256) * 256
N_EVEN = (DEPTH + 1) // 2
N_ODD = DEPTH // 2
EVEN_SIZES = (H_FOX * HEAD_DIM, H_FOX * HEAD_DIM, H_FOX * HEAD_DIM, H_FOX,
              H_NSA * HEAD_DIM,
              G_NSA * HEAD_DIM, G_NSA * HEAD_DIM, G_NSA * HEAD_DIM,
              G_NSA * HEAD_DIM, G_NSA * HEAD_DIM, G_NSA * HEAD_DIM,
              3 * H_NSA)
ODD_SIZES = (H_GLA * DK_GLA, H_GLA * DK_GLA, H_GLA * DV_GLA, GLA_RANK, H_GLA * DV_GLA,
             H_HG * DK_HG, H_HG * DK_HG, H_HG * DV_HG, H_HG * DV_HG)
N_IN_EVEN = sum(EVEN_SIZES)
N_IN_ODD = sum(ODD_SIZES)
ATTN_SCALE = HEAD_DIM ** -0.5
NEG = -1.0e30
EPS = 1.0e-6

kernel_name = 'fox_nsa_gla_hgrn2_hybrid_step'


def _split(z, sizes):
    outs, off = [], 0
    for s in sizes:
        outs.append(z[..., off:off + s])
        off += s
    return outs


def rmsnorm(x, g):
    xf = x.astype(jnp.float32)
    y = xf * lax.rsqrt(jnp.mean(xf * xf, axis=-1, keepdims=True) + EPS)
    return (y * g.astype(jnp.float32)).astype(x.dtype)


def swiglu(h, w1, w3, w2):
    return (jax.nn.silu(h @ w1) * (h @ w3)) @ w2


def fox_attention(q, k, v, logf):
    B, Tq, H, D = q.shape
    Tk = k.shape[1]
    off = Tk - Tq
    cT = jnp.cumsum(logf.astype(jnp.float32), axis=1).transpose(0, 2, 1)
    qb = math.gcd(Tq, Q_BLOCK)
    nb = Tq // qb
    qs = q.reshape(B, nb, qb, H, D).swapaxes(0, 1)
    cqs = cT[:, :, off:].reshape(B, H, nb, qb).transpose(2, 0, 1, 3)
    pqs = (off + jnp.arange(Tq)).reshape(nb, qb)
    pos_k = jnp.arange(Tk)

    def block(args):
        qi, ci, pi = args
        s = jnp.einsum('bqhd,bkhd->bhqk', qi, k, preferred_element_type=jnp.float32) * ATTN_SCALE
        s = s + ci[..., :, None] - cT[:, :, None, :]
        s = jnp.where(pos_k[None, :] <= pi[:, None], s, NEG)
        p = jax.nn.softmax(s, axis=-1)
        return jnp.einsum('bhqk,bkhd->bqhd', p.astype(v.dtype), v)

    o = lax.map(block, (qs, cqs, pqs))
    return o.swapaxes(0, 1).reshape(B, Tq, H, D)


def nsa_compress(x, pe, w):
    B, L, G, D = x.shape
    n_cmp = -(-L // CMP_STRIDE)
    xp = jnp.pad(x, ((0, 0), (0, (n_cmp + 1) * CMP_STRIDE - L), (0, 0), (0, 0)))
    halves = xp.reshape(B, n_cmp + 1, CMP_STRIDE, G, D)
    blocks = jnp.concatenate([halves[:, :-1], halves[:, 1:]], axis=2) + pe[None, None, :, None, :]
    return jnp.einsum('bnlgd,lde->bnge', blocks, w.reshape(CMP_BLOCK, D, D))


def nsa_cmp_attn(q, kc, vc, pos_q):
    n = kc.shape[1]
    s = jnp.einsum('bqghd,bngd->bghqn', q, kc, preferred_element_type=jnp.float32) * ATTN_SCALE
    valid = (jnp.arange(n) * CMP_STRIDE + CMP_BLOCK - 1)[None, :] <= pos_q[:, None]
    s = jnp.where(valid, s, NEG)
    e = jnp.exp(s - jnp.max(s, axis=-1, keepdims=True)) * valid
    p = e / jnp.maximum(jnp.sum(e, axis=-1, keepdims=True), 1e-30)
    o = jnp.einsum('bghqn,bngd->bqghd', p.astype(vc.dtype), vc)
    return o, p


def nsa_select(p_cmp, pos_q, n_sel):
    imp = p_cmp.sum(axis=2)
    r = SEL_BLOCK // CMP_STRIDE
    imp = jnp.pad(imp, ((0, 0), (0, 0), (0, 0), (0, r * n_sel - imp.shape[-1])))
    grp = imp.reshape(imp.shape[:3] + (n_sel, r))
    prev = jnp.pad(grp[..., :-1, r - 1], ((0, 0), (0, 0), (0, 0), (1, 0)))
    score = grp.sum(-1) + prev
    blk = jnp.arange(n_sel)
    forced = (blk[None, :] == (pos_q // SEL_BLOCK)[:, None]) | (blk[None, :] == 0)
    future = blk[None, :] * SEL_BLOCK > pos_q[:, None]
    score = jnp.where(forced, FORCE_SCORE, jnp.where(future, -1.0, score))
    _, idx = lax.top_k(score, min(N_SELECT, n_sel))
    return idx


def nsa_slc_attn(q, ks, vs, idx, pos_q):
    B, Tq, G, HPG, D = q.shape
    L = ks.shape[1]
    n_sel = -(-L // SEL_BLOCK)
    K = idx.shape[-1]
    padw = ((0, 0), (0, n_sel * SEL_BLOCK - L), (0, 0), (0, 0))
    kb = jnp.pad(ks, padw).reshape(B, n_sel, SEL_BLOCK, G, D).transpose(0, 3, 1, 2, 4)
    vb = jnp.pad(vs, padw).reshape(B, n_sel, SEL_BLOCK, G, D).transpose(0, 3, 1, 2, 4)
    qb = math.gcd(Tq, SEL_Q_BLOCK)
    nb = Tq // qb
    qs = q.reshape(B, nb, qb, G, HPG, D).swapaxes(0, 1)
    ids = idx.reshape(B, G, nb, qb, K).transpose(2, 0, 1, 3, 4)
    pqs = pos_q.reshape(nb, qb)
    take = jax.vmap(jax.vmap(lambda blocks, ix: blocks[ix]))
    offs = jnp.arange(SEL_BLOCK)

    def block(args):
        qi, ix, pi = args
        kg = take(kb, ix)
        vg = take(vb, ix)
        s = jnp.einsum('bqghd,bgqkjd->bghqkj', qi, kg, preferred_element_type=jnp.float32) * ATTN_SCALE
        kpos = ix[..., None] * SEL_BLOCK + offs
        valid = kpos <= pi[None, None, :, None, None]
        s = jnp.where(valid[:, :, None], s, NEG)
        p = jax.nn.softmax(s.reshape(B, G, HPG, qb, K * SEL_BLOCK), axis=-1).reshape(s.shape)
        return jnp.einsum('bghqkj,bgqkjd->bqghd', p.astype(vg.dtype), vg)

    o = lax.map(block, (qs, ids, pqs))
    return o.swapaxes(0, 1).reshape(B, Tq, G, HPG, D)


def nsa_win_attn(q, kw, vw, pos0):
    B, Tq, G, HPG, D = q.shape
    Lk = kw.shape[1]
    off = Lk - Tq
    padw = ((0, 0), (WINDOW, 0), (0, 0), (0, 0))
    kp = jnp.pad(kw, padw)
    vp = jnp.pad(vw, padw)
    qb = math.gcd(Tq, Q_BLOCK)
    nb = Tq // qb
    span = WINDOW + qb
    qs = q.reshape(B, nb, qb, G, HPG, D).swapaxes(0, 1)

    def block(args):
        i, qi = args
        start = off + i * qb
        kblk = lax.dynamic_slice_in_dim(kp, start, span, axis=1)
        vblk = lax.dynamic_slice_in_dim(vp, start, span, axis=1)
        pidx = start + jnp.arange(span)
        kpos = pos0 - WINDOW + pidx
        qpos = pos0 + start + jnp.arange(qb)
        d = qpos[:, None] - kpos[None, :]
        valid = (pidx >= WINDOW)[None, :] & (d >= 0) & (d < WINDOW)
        s = jnp.einsum('bqghd,bkgd->bghqk', qi, kblk, preferred_element_type=jnp.float32) * ATTN_SCALE
        p = jax.nn.softmax(jnp.where(valid, s, NEG), axis=-1)
        return jnp.einsum('bghqk,bkgd->bqghd', p.astype(vblk.dtype), vblk)

    o = lax.map(block, (jnp.arange(nb), qs))
    return o.swapaxes(0, 1).reshape(B, Tq, G, HPG, D)


def gated_linear_attention(q, k, v, log_g, s0):
    B, T, H, DK = q.shape
    DV = v.shape[-1]
    C = math.gcd(T, LA_CHUNK)
    n = T // C

    def chunks(a):
        return a.reshape(B, n, C, H, a.shape[-1]).transpose(1, 0, 3, 2, 4)

    qc = chunks(q.astype(jnp.float32))
    kc = chunks(k.astype(jnp.float32))
    vc = chunks(v.astype(jnp.float32))
    gc = chunks(log_g.astype(jnp.float32))
    causal = jnp.tril(jnp.ones((C, C), dtype=bool))[:, :, None]

    def step(S, inp):
        qi, ki, vi, gi = inp
        b = jnp.cumsum(gi, axis=2)
        decay = jnp.exp(jnp.where(causal, b[:, :, :, None, :] - b[:, :, None, :, :], -jnp.inf))
        A = jnp.einsum('bhtd,bhsd,bhtsd->bhts', qi, ki, decay)
        o = jnp.einsum('bhts,bhsv->bhtv', A, vi) + jnp.einsum('bhtd,bhdv->bhtv', qi * jnp.exp(b), S)
        bl = b[:, :, -1:, :]
        S = jnp.exp(bl[:, :, 0, :])[..., None] * S + jnp.einsum('bhsd,bhsv->bhdv', ki * jnp.exp(bl - b), vi)
        return S, o

    S, o = lax.scan(step, s0.astype(jnp.float32), (qc, kc, vc, gc))
    o = o.transpose(1, 0, 3, 2, 4).reshape(B, T, H, DV).astype(v.dtype)
    return o, S.astype(s0.dtype)


def even_mixer(h, past, w_buf, w_in, w_out, b_f, fq_gain, fk_gain, nq_gain, nk_gain,
               pe_k, pe_v, cmp_wk, cmp_wv, gate_b):
    B, T, _ = h.shape
    fq, fk, fv, ff, nq, kc, vc, ks, vs, kw, vw, ng = _split(h @ w_in, EVEN_SIZES)
    fq = rmsnorm(fq.reshape(B, T, H_FOX, HEAD_DIM), fq_gain)
    fk = rmsnorm(fk.reshape(B, T, H_FOX, HEAD_DIM), fk_gain)
    fv = fv.reshape(B, T, H_FOX, HEAD_DIM)
    flogf = jax.nn.log_sigmoid((ff + b_f).astype(jnp.float32))
    nq = rmsnorm(nq.reshape(B, T, G_NSA, HPG_NSA, HEAD_DIM), nq_gain)
    grp = lambda a: a.reshape(B, T, G_NSA, HEAD_DIM)
    kc, vc, vs, vw = grp(kc), grp(vc), grp(vs), grp(vw)
    ks = rmsnorm(grp(ks), nk_gain[1])
    kw = rmsnorm(grp(kw), nk_gain[2])
    gates = jax.nn.sigmoid((ng.reshape(B, T, H_NSA, 3) + gate_b).astype(jnp.float32))
    gates = gates.reshape(B, T, G_NSA, HPG_NSA, 3).astype(h.dtype)
    if past is None:
        p_len = 0
        fk_all, fv_all, flogf_all = fk, fv, flogf
        kc_all, vc_all, ks_all, vs_all = kc, vc, ks, vs
        buf_k = jnp.zeros((B, w_buf, G_NSA, HEAD_DIM), kw.dtype)
        buf_v = jnp.zeros((B, w_buf, G_NSA, HEAD_DIM), vw.dtype)
        kw_all, vw_all, pos0_w = kw, vw, 0
    else:
        pk, pv, plogf, pkc, pvc, pks, pvs, buf_k, buf_v = past
        p_len = pk.shape[1]
        cat = lambda a, b: jnp.concatenate([a, b], axis=1)
        fk_all, fv_all, flogf_all = cat(pk, fk), cat(pv, fv), cat(plogf, flogf)
        kc_all, vc_all, ks_all, vs_all = cat(pkc, kc), cat(pvc, vc), cat(pks, ks), cat(pvs, vs)
        kw_all, vw_all = cat(buf_k, kw), cat(buf_v, vw)
        pos0_w = p_len - buf_k.shape[1]
    pos_q = p_len + jnp.arange(T)
    o_fox = fox_attention(fq, fk_all, fv_all, flogf_all)
    k_cmp = rmsnorm(nsa_compress(kc_all, pe_k, cmp_wk), nk_gain[0])
    v_cmp = nsa_compress(vc_all, pe_v, cmp_wv)
    o_cmp, p_cmp = nsa_cmp_attn(nq, k_cmp, v_cmp, pos_q)
    idx = nsa_select(p_cmp, pos_q, -(-(p_len + T) // SEL_BLOCK))
    o_slc = nsa_slc_attn(nq, ks_all, vs_all, idx, pos_q)
    o_win = nsa_win_attn(nq, kw_all, vw_all, pos0_w)
    o_nsa = gates[..., 0:1] * o_cmp + gates[..., 1:2] * o_slc + gates[..., 2:3] * o_win
    o = jnp.concatenate([o_fox.reshape(B, T, -1), o_nsa.reshape(B, T, -1)], axis=-1) @ w_out
    new_kw = jnp.concatenate([buf_k, kw], axis=1)[:, -w_buf:]
    new_vw = jnp.concatenate([buf_v, vw], axis=1)[:, -w_buf:]
    return o, (fk, fv, flogf, kc, vc, ks, vs, new_kw, new_vw)


def odd_mixer(h, state, w_in, w_out, wa2, ba, gla_gain, lb, hg_gain):
    B, T, _ = h.shape
    s_gla, s_hg = state
    gq, gk, gv, ga, gg, hq, hf, hi, hg = _split(h @ w_in, ODD_SIZES)
    gq = gq.reshape(B, T, H_GLA, DK_GLA) * (DK_GLA ** -0.5)
    gk = gk.reshape(B, T, H_GLA, DK_GLA)
    gv = gv.reshape(B, T, H_GLA, DV_GLA)
    log_a = jax.nn.log_sigmoid((ga @ wa2 + ba).astype(jnp.float32)).reshape(B, T, H_GLA, DK_GLA) / GLA_GATE_NORM
    o_gla, s_gla = gated_linear_attention(gq, gk, gv, log_a, s_gla)
    o_gla = rmsnorm(o_gla, gla_gain) * jax.nn.silu(gg.reshape(B, T, H_GLA, DV_GLA))
    f = (lb + (1.0 - lb) * jax.nn.sigmoid(hf.astype(jnp.float32))).reshape(B, T, H_HG, DK_HG)
    hq = jax.nn.silu(hq.reshape(B, T, H_HG, DK_HG))
    o_hg, s_hg = gated_linear_attention(hq, (1.0 - f).astype(h.dtype), hi.reshape(B, T, H_HG, DV_HG),
                                        jnp.log(f), s_hg)
    o_hg = rmsnorm(o_hg, hg_gain) * jax.nn.sigmoid(hg.reshape(B, T, H_HG, DV_HG))
    o = jnp.concatenate([o_gla.reshape(B, T, -1), o_hg.reshape(B, T, -1)], axis=-1) @ w_out
    return o, (s_gla, s_hg)


def setup_inputs(seed: int = 0) -> dict:
    key = jax.random.key(seed)
    keys = iter(jax.random.split(key, 64))

    def nrm(shape, scale=1.0):
        return jax.random.normal(next(keys), shape, jnp.float32) * scale

    n_pages = PAST_LEN // PAGE_SIZE
    n_used = DEC_BATCH * n_pages
    n_pool = n_used + n_used // 4
    w_buf = min(WINDOW, PAST_LEN)
    page_table = jax.random.permutation(next(keys), n_pool)[:n_used].reshape(DEC_BATCH, n_pages).astype(jnp.int32)
    kv_f = (N_EVEN, n_pool, PAGE_SIZE, H_FOX, HEAD_DIM)
    kv_n = (N_EVEN, n_pool, PAGE_SIZE, G_NSA, HEAD_DIM)
    win = (N_EVEN, DEC_BATCH, w_buf, G_NSA, HEAD_DIM)
    return {
        'x_prompt': nrm((BATCH, SEQ, D_MODEL)),
        'x_sample': nrm((DEC_BATCH, DEC_SEQ, D_MODEL)),
        'cache_fox_k': nrm(kv_f),
        'cache_fox_v': nrm(kv_f),
        'cache_fox_logf': jax.nn.log_sigmoid(2.0 + nrm((N_EVEN, n_pool, PAGE_SIZE, H_FOX))),
        'cache_nsa_kc': nrm(kv_n),
        'cache_nsa_vc': nrm(kv_n),
        'cache_nsa_ks': nrm(kv_n),
        'cache_nsa_vs': nrm(kv_n),
        'state_nsa_kw': nrm(win),
        'state_nsa_vw': nrm(win),
        'state_gla': nrm((N_ODD, DEC_BATCH, H_GLA, DK_GLA, DV_GLA), 0.5),
        'state_hgrn': nrm((N_ODD, DEC_BATCH, H_HG, DK_HG, DV_HG), 0.5),
        'page_table': page_table,
        'norm_mix': 1.0 + nrm((DEPTH, D_MODEL), 0.1),
        'norm_ffn': 1.0 + nrm((DEPTH, D_MODEL), 0.1),
        'w_in_even': nrm((N_EVEN, D_MODEL, N_IN_EVEN), D_MODEL ** -0.5),
        'w_out_even': nrm((N_EVEN, D_MIX_EVEN, D_MODEL), D_MIX_EVEN ** -0.5),
        'fox_b_f': 2.0 + nrm((N_EVEN, H_FOX), 0.1),
        'fox_q_gain': 1.0 + nrm((N_EVEN, HEAD_DIM), 0.1),
        'fox_k_gain': 1.0 + nrm((N_EVEN, HEAD_DIM), 0.1),
        'nsa_q_gain': 1.0 + nrm((N_EVEN, HEAD_DIM), 0.1),
        'nsa_k_gain': 1.0 + nrm((N_EVEN, 3, HEAD_DIM), 0.1),
        'nsa_cmp_pe_k': nrm((N_EVEN, CMP_BLOCK, HEAD_DIM), 0.1),
        'nsa_cmp_pe_v': nrm((N_EVEN, CMP_BLOCK, HEAD_DIM), 0.1),
        'nsa_cmp_wk': nrm((N_EVEN, CMP_BLOCK * HEAD_DIM, HEAD_DIM), (CMP_BLOCK * HEAD_DIM) ** -0.5),
        'nsa_cmp_wv': nrm((N_EVEN, CMP_BLOCK * HEAD_DIM, HEAD_DIM), (CMP_BLOCK * HEAD_DIM) ** -0.5),
        'nsa_gate_b': nrm((N_EVEN, H_NSA, 3), 0.1),
        'w_in_odd': nrm((N_ODD, D_MODEL, N_IN_ODD), D_MODEL ** -0.5),
        'w_out_odd': nrm((N_ODD, D_MIX_ODD, D_MODEL), D_MIX_ODD ** -0.5),
        'gla_wa2': nrm((N_ODD, GLA_RANK, H_GLA * DK_GLA), GLA_RANK ** -0.5),
        'gla_ba': nrm((N_ODD, H_GLA * DK_GLA), 0.1),
        'gla_norm': 1.0 + nrm((N_ODD, DV_GLA), 0.1),
        'hgrn_lb': nrm((N_ODD + 1, H_HG * DK_HG), 0.5),
        'hgrn_norm': 1.0 + nrm((N_ODD, DV_HG), 0.1),
        'ffn_w1': nrm((DEPTH, D_MODEL, D_FF), D_MODEL ** -0.5),
        'ffn_w3': nrm((DEPTH, D_MODEL, D_FF), D_MODEL ** -0.5),
        'ffn_w2': nrm((DEPTH, D_FF, D_MODEL), D_FF ** -0.5),
    }


def reference(x_prompt, x_sample, cache_fox_k, cache_fox_v, cache_fox_logf,
              cache_nsa_kc, cache_nsa_vc, cache_nsa_ks, cache_nsa_vs,
              state_nsa_kw, state_nsa_vw, state_gla, state_hgrn, page_table,
              norm_mix, norm_ffn, w_in_even, w_out_even, fox_b_f, fox_q_gain, fox_k_gain,
              nsa_q_gain, nsa_k_gain, nsa_cmp_pe_k, nsa_cmp_pe_v, nsa_cmp_wk, nsa_cmp_wv, nsa_gate_b,
              w_in_odd, w_out_odd, gla_wa2, gla_ba, gla_norm, hgrn_lb, hgrn_norm,
              ffn_w1, ffn_w3, ffn_w2):
    n_pages = page_table.shape[1]
    w_buf = state_nsa_kw.shape[2]
    lb_all = jnp.cumsum(jax.nn.softmax(hgrn_lb.astype(jnp.float32), axis=0), axis=0)

    def paged(cache):
        rows = cache[page_table]
        return rows.reshape((rows.shape[0], n_pages * rows.shape[2]) + rows.shape[3:])

    def run(x, pasts, states):
        new_even, new_odd = [], []
        for i in range(DEPTH):
            j = i // 2
            h = rmsnorm(x, norm_mix[i])
            if i % 2 == 0:
                y, st = even_mixer(h, pasts[j], w_buf, w_in_even[j], w_out_even[j], fox_b_f[j],
                                   fox_q_gain[j], fox_k_gain[j], nsa_q_gain[j], nsa_k_gain[j],
                                   nsa_cmp_pe_k[j], nsa_cmp_pe_v[j], nsa_cmp_wk[j], nsa_cmp_wv[j], nsa_gate_b[j])
                new_even.append(st)
            else:
                y, st = odd_mixer(h, states[j], w_in_odd[j], w_out_odd[j], gla_wa2[j], gla_ba[j],
                                  gla_norm[j], lb_all[j], hgrn_norm[j])
                new_odd.append(st)
            x = x + y
            x = x + swiglu(rmsnorm(x, norm_ffn[i]), ffn_w1[i], ffn_w3[i], ffn_w2[i])
        ev = [jnp.stack([st[m] for st in new_even]) for m in range(9)]
        od = [jnp.stack([st[m] for st in new_odd]) for m in range(2)]
        return x, ev, od

    bp = x_prompt.shape[0]
    states_p = [(jnp.zeros((bp, H_GLA, DK_GLA, DV_GLA), jnp.float32),
                 jnp.zeros((bp, H_HG, DK_HG, DV_HG), jnp.float32)) for _ in range(N_ODD)]
    y_prompt, ev_p, od_p = run(x_prompt, [None] * N_EVEN, states_p)

    pasts_s = [(paged(cache_fox_k[j]), paged(cache_fox_v[j]), paged(cache_fox_logf[j]),
                paged(cache_nsa_kc[j]), paged(cache_nsa_vc[j]), paged(cache_nsa_ks[j]), paged(cache_nsa_vs[j]),
                state_nsa_kw[j], state_nsa_vw[j]) for j in range(N_EVEN)]
    states_s = [(state_gla[j], state_hgrn[j]) for j in range(N_ODD)]
    y_sample, ev_s, od_s = run(x_sample, pasts_s, states_s)

    fox_k_p, fox_v_p, fox_logf_p, nsa_kc_p, nsa_vc_p, nsa_ks_p, nsa_vs_p, nsa_kw_p, nsa_vw_p = ev_p
    fox_k_s, fox_v_s, fox_logf_s, nsa_kc_s, nsa_vc_s, nsa_ks_s, nsa_vs_s, nsa_kw_s, nsa_vw_s = ev_s
    gla_p, hgrn_p = od_p
    gla_s, hgrn_s = od_s
    return (y_prompt, y_sample,
            fox_k_p, fox_k_s, fox_v_p, fox_v_s, fox_logf_p, fox_logf_s,
            nsa_kc_p, nsa_kc_s, nsa_vc_p, nsa_vc_s, nsa_ks_p, nsa_ks_s, nsa_vs_p, nsa_vs_s,
            nsa_kw_p, nsa_kw_s, nsa_vw_p, nsa_vw_s,
            gla_p, gla_s, hgrn_p, hgrn_s)
```

```python
import functools

import jax
import jax.numpy as jnp
from jax import lax
from jax.experimental import pallas as pl
from jax.experimental.pallas import tpu as pltpu

F32 = jnp.float32
BF16 = jnp.bfloat16

D_MODEL = 2048
HD = 128
H_FOX = 8
H_NSA = 8
G_NSA = 2
HPG = H_NSA // G_NSA
CMP_STRIDE = 16
CMP_BLOCK = 2 * CMP_STRIDE
SEL_BLOCK = 64
N_SELECT = 16
WINDOW = 512
FORCE_SCORE = 1.0e4
H_GLA = 4
DK_GLA = 128
DV_GLA = 256
GLA_RANK = 16
GLA_GATE_NORM = 16.0
H_HG = 8
DK_HG = 128
DV_HG = 128
PAGE = 128
EVEN_SIZES = (H_FOX * HD, H_FOX * HD, H_FOX * HD, H_FOX, H_NSA * HD,
              G_NSA * HD, G_NSA * HD, G_NSA * HD, G_NSA * HD, G_NSA * HD, G_NSA * HD, 3 * H_NSA)
ODD_SIZES = (H_GLA * DK_GLA, H_GLA * DK_GLA, H_GLA * DV_GLA, GLA_RANK, H_GLA * DV_GLA,
             H_HG * DK_HG, H_HG * DK_HG, H_HG * DV_HG, H_HG * DV_HG)
FQ, FK, FV, NQ, KC, VC, KS, VS, KW, VW = 0, 1024, 2048, 3072, 4096, 4352, 4608, 4864, 5120, 5376
N_BIG_EVEN = 5632
GQ, GK, GV, GG, HQ, HF, HI, HO = 0, 512, 1024, 2048, 3072, 4096, 5120, 6144
N_BIG_ODD = 7168
ATTN_SCALE = HD ** -0.5
NEG = -1.0e30
EPS = 1.0e-6
EXP_CLAMP = 80.0
VMEM_LIMIT = 56 * 1024 * 1024


def _cparams(sem):
    return pltpu.CompilerParams(dimension_semantics=sem, vmem_limit_bytes=VMEM_LIMIT)


def _dot(a, b):
    return jnp.dot(a, b, preferred_element_type=F32)


def _dot_nt(a, b):
    return lax.dot_general(a, b, (((1,), (1,)), ((), ())), preferred_element_type=F32)


def _dot_tn(a, b):
    return lax.dot_general(a, b, (((0,), (0,)), ((), ())), preferred_element_type=F32)


def _split3(x):
    hi = x.astype(BF16)
    r = x - hi.astype(F32)
    mid = r.astype(BF16)
    lo = (r - mid.astype(F32)).astype(BF16)
    return hi, mid, lo


def _dot3_l(x, w):
    hi, mid, lo = _split3(x)
    return _dot(hi, w) + _dot(mid, w) + _dot(lo, w)


def _dot3_r(w, x):
    hi, mid, lo = _split3(x)
    return _dot(w, hi) + _dot(w, mid) + _dot(w, lo)


def _sigmoid(x):
    return 1.0 / (1.0 + jnp.exp(-x))


def _log_sigmoid(x):
    return jnp.minimum(x, 0.0) - jnp.log1p(jnp.exp(-jnp.abs(x)))


def _tri(n, upper):
    r = lax.broadcasted_iota(jnp.int32, (n, n), 0)
    c = lax.broadcasted_iota(jnp.int32, (n, n), 1)
    return jnp.where((r <= c) if upper else (r >= c), 1.0, 0.0).astype(BF16)


def _proj_body(x_ref, g_ref, w_ref, ws_ref, cg_ref, cf_ref, z_ref, zs_ref, h_scr, *, tn):
    @pl.when(pl.program_id(1) == 0)
    def _():
        x = x_ref[...]
        r = lax.rsqrt(jnp.mean(x * x, axis=-1, keepdims=True) + EPS)
        h = (x * r * g_ref[...]).astype(BF16)
        h_scr[...] = h
        zs_ref[...] = _dot(h, ws_ref[...])

    z = _dot(h_scr[...], w_ref[...])
    for c in range(tn // HD):
        sl = slice(c * HD, (c + 1) * HD)
        zc = z[:, sl]
        r = lax.rsqrt(jnp.mean(zc * zc, axis=-1, keepdims=True) + EPS)
        f = cf_ref[:, sl]
        z_ref[:, sl] = zc * (f * r + (1.0 - f)) * cg_ref[:, sl]


def _proj(x, g, w_big, w_small, col_gain, col_flag):
    m, d = x.shape
    n = w_big.shape[1]
    tm = min(m, 1024)
    tn = 512
    return pl.pallas_call(
        functools.partial(_proj_body, tn=tn),
        grid=(m // tm, n // tn),
        in_specs=[pl.BlockSpec((tm, d), lambda i, j: (i, 0)),
                  pl.BlockSpec((1, d), lambda i, j: (0, 0)),
                  pl.BlockSpec((d, tn), lambda i, j: (0, j)),
                  pl.BlockSpec((d, HD), lambda i, j: (0, 0)),
                  pl.BlockSpec((1, tn), lambda i, j: (0, j)),
                  pl.BlockSpec((1, tn), lambda i, j: (0, j))],
        out_specs=[pl.BlockSpec((tm, tn), lambda i, j: (i, j)),
                   pl.BlockSpec((tm, HD), lambda i, j: (i, 0))],
        out_shape=[jax.ShapeDtypeStruct((m, n), F32), jax.ShapeDtypeStruct((m, HD), F32)],
        scratch_shapes=[pltpu.VMEM((tm, d), BF16)],
        compiler_params=_cparams(("parallel", "arbitrary")),
    )(x, g, w_big, w_small, col_gain, col_flag)


def _ffn_up_body(x_ref, g_ref, w1_ref, w3_ref, o_ref, h_scr):
    @pl.when(pl.program_id(1) == 0)
    def _():
        x = x_ref[...]
        r = lax.rsqrt(jnp.mean(x * x, axis=-1, keepdims=True) + EPS)
        h_scr[...] = (x * r * g_ref[...]).astype(BF16)

    h = h_scr[...]
    a = _dot(h, w1_ref[...])
    b = _dot(h, w3_ref[...])
    o_ref[...] = (a * _sigmoid(a) * b).astype(BF16)


def _ffn_up(x, g, w1, w3):
    m, d = x.shape
    n = w1.shape[1]
    tm = min(m, 1024)
    tn = 512
    return pl.pallas_call(
        _ffn_up_body,
        grid=(m // tm, n // tn),
        in_specs=[pl.BlockSpec((tm, d), lambda i, j: (i, 0)),
                  pl.BlockSpec((1, d), lambda i, j: (0, 0)),
                  pl.BlockSpec((d, tn), lambda i, j: (0, j)),
                  pl.BlockSpec((d, tn), lambda i, j: (0, j))],
        out_specs=pl.BlockSpec((tm, tn), lambda i, j: (i, j)),
        out_shape=jax.ShapeDtypeStruct((m, n), BF16),
        scratch_shapes=[pltpu.VMEM((tm, d), BF16)],
        compiler_params=_cparams(("parallel", "arbitrary")),
    )(x, g, w1, w3)


def _mm_res_body(*refs, n_in):
    res_ref = refs[2 * n_in]
    o_ref = refs[2 * n_in + 1]
    acc = res_ref[...]
    for a_ref, w_ref in zip(refs[:n_in], refs[n_in:2 * n_in]):
        acc = acc + _dot(a_ref[...], w_ref[...])
    o_ref[...] = acc


def _mm_res(a_list, w, res):
    m, n = res.shape
    tm = min(m, 512)
    tn = 512
    in_specs, w_specs, off = [], [], 0
    for a in a_list:
        k = a.shape[1]
        in_specs.append(pl.BlockSpec((tm, k), lambda i, j: (i, 0)))
        w_specs.append(pl.BlockSpec((k, tn), lambda i, j, _o=off // k: (_o, j)))
        off += k
    return pl.pallas_call(
        functools.partial(_mm_res_body, n_in=len(a_list)),
        grid=(m // tm, n // tn),
        in_specs=in_specs + w_specs + [pl.BlockSpec((tm, tn), lambda i, j: (i, j))],
        out_specs=pl.BlockSpec((tm, tn), lambda i, j: (i, j)),
        out_shape=jax.ShapeDtypeStruct((m, n), F32),
        compiler_params=_cparams(("parallel", "parallel")),
    )(*a_list, *([w] * len(a_list)), res)


def _fox_prep_body(zs_ref, b_ref, lf_ref, c_ref, carry, *, tc):
    @pl.when(pl.program_id(1) == 0)
    def _():
        carry[...] = jnp.zeros_like(carry)

    lf = _log_sigmoid(zs_ref[0] + b_ref[...])
    c = _dot3_r(_tri(tc, upper=False), lf) + carry[...]
    lf_ref[0] = lf
    c_ref[0] = c
    carry[...] = c[tc - 1:tc, :]


def _fox_prep(zs3, b_pad):
    bsz, t, _ = zs3.shape
    tc = min(t, 256)
    spec = pl.BlockSpec((1, tc, HD), lambda b, i: (b, i, 0))
    return pl.pallas_call(
        functools.partial(_fox_prep_body, tc=tc),
        grid=(bsz, t // tc),
        in_specs=[spec, pl.BlockSpec((1, HD), lambda b, i: (0, 0))],
        out_specs=[spec, spec],
        out_shape=[jax.ShapeDtypeStruct(zs3.shape, F32)] * 2,
        scratch_shapes=[pltpu.VMEM((1, HD), F32)],
        compiler_params=_cparams(("parallel", "arbitrary")),
    )(zs3, b_pad)


def _flash_body(*refs, mode, tq, tk, hpg, off, t_mod):
    it = iter(refs)
    q_ref, k_ref, v_ref = next(it), next(it), next(it)
    bias_ref = next(it) if mode == "fox" else None
    sel_ref = next(it) if mode == "slc" else None
    o_ref, kb, vb = next(it), next(it), next(it)
    i = pl.program_id(2)

    @pl.when(i == 0)
    def _():
        kb[...] = k_ref[0].astype(BF16)
        vb[...] = v_ref[0].astype(BF16)

    q = q_ref[0]
    if hpg > 1:
        q = jnp.concatenate([q[:, h * HD:(h + 1) * HD] for h in range(hpg)], axis=0)
    qa = (q * ATTN_SCALE).astype(BF16)
    rows = hpg * tq
    row = lax.broadcasted_iota(jnp.int32, (tq, tk), 0)
    lane = lax.broadcasted_iota(jnp.int32, (tq, tk), 1)
    if t_mod is None:
        q_lo = off + i * tq
        q_hi = q_lo + tq - 1
        qpos = q_lo + row
    else:
        q_lo = off
        q_hi = off + t_mod - 1
        qpos = off + row % t_mod
    j_hi = q_hi // tk + 1
    j_lo = jnp.maximum(q_lo - (WINDOW - 1), 0) // tk if mode == "win" else 0

    def body(j, carry):
        m, l, acc = carry
        ks = pl.multiple_of(j * tk, tk)
        kt = kb[pl.ds(ks, tk), :]
        vt = vb[pl.ds(ks, tk), :]
        s = _dot_nt(qa, kt)
        kidx = ks + lane
        valid = kidx <= qpos
        if mode == "win":
            valid = valid & ((qpos - kidx) < WINDOW)
        if mode == "slc":
            cb = lax.broadcasted_iota(jnp.int32, (HD, tk), 0)
            kk = lax.broadcasted_iota(jnp.int32, (HD, tk), 1)
            expand = jnp.where(cb == (ks + kk) // SEL_BLOCK, 1.0, 0.0).astype(BF16)
            valid = valid & (_dot(sel_ref[0, 0].astype(BF16), expand) > 0.5)
        if hpg > 1:
            valid = jnp.concatenate([valid] * hpg, axis=0)
        if mode == "fox":
            s = s - bias_ref[0, pl.ds(j, 1), :]
        s = jnp.where(valid, s, NEG)
        m_new = jnp.maximum(m, jnp.max(s, axis=-1, keepdims=True))
        a = jnp.exp(m - m_new)
        p = jnp.exp(s - m_new)
        l = a * l + jnp.sum(p, axis=-1, keepdims=True)
        acc = a * acc + _dot(p.astype(BF16), vt)
        return m_new, l, acc

    init = (jnp.full((rows, 1), NEG, F32), jnp.zeros((rows, 1), F32), jnp.zeros((rows, HD), F32))
    _, l, acc = lax.fori_loop(j_lo, j_hi, body, init)
    o = acc / l
    for h in range(hpg):
        o_ref[0, :, h * HD:(h + 1) * HD] = o[h * tq:(h + 1) * tq].astype(o_ref.dtype)


def _flash(mode, q_arr, q_col, k_arr, k_col, v_arr, v_col, *, n_kv, hpg, tq, tk, off=0, t_mod=None,
           bias=None, sel=None, out_dtype=F32):
    bsz, t_q = q_arr.shape[0], q_arr.shape[1]
    t_k = k_arr.shape[1]
    nq = t_q // tq
    w = hpg * HD
    in_specs = [pl.BlockSpec((1, tq, w), lambda b, g, i: (b, i, q_col // w + g)),
                pl.BlockSpec((1, t_k, HD), lambda b, g, i: (b, 0, k_col // HD + g)),
                pl.BlockSpec((1, t_k, HD), lambda b, g, i: (b, 0, v_col // HD + g))]
    args = [q_arr, k_arr, v_arr]
    if mode == "fox":
        in_specs.append(pl.BlockSpec((1, t_k // tk, tk), lambda b, g, i: (b * n_kv + g, 0, 0)))
        args.append(bias)
    if mode == "slc":
        in_specs.append(pl.BlockSpec((1, 1, tq, HD), lambda b, g, i: (b, g, i, 0)))
        args.append(sel)
    return pl.pallas_call(
        functools.partial(_flash_body, mode=mode, tq=tq, tk=tk, hpg=hpg, off=off, t_mod=t_mod),
        grid=(bsz, n_kv, nq),
        in_specs=in_specs,
        out_specs=pl.BlockSpec((1, tq, w), lambda b, g, i: (b, i, g)),
        out_shape=jax.ShapeDtypeStruct((bsz, t_q, n_kv * w), out_dtype),
        scratch_shapes=[pltpu.VMEM((t_k, HD), BF16), pltpu.VMEM((t_k, HD), BF16)],
        compiler_params=_cparams(("parallel", "parallel", "arbitrary")),
    )(*args)


def _compress_body(*refs, paged, n_chunks, mn, norm, t_new):
    it = iter(refs)
    if paged:
        next(it)
    x_refs = [next(it) for _ in range(G_NSA)]
    new_ref = next(it) if paged else None
    pe_ref, w_ref, gain_ref, o_ref, hcat, tb = next(it), next(it), next(it), next(it), next(it), next(it)
    p = pl.program_id(1)
    mh = mn + 8
    n_real = n_chunks * (PAGE // CMP_STRIDE)

    @pl.when(p == 0)
    def _():
        hcat[:, n_real:, :] = jnp.zeros((G_NSA, mh - n_real, CMP_STRIDE * HD), F32)
        tb[mh:, :] = jnp.zeros((8, 2 * HD), F32)

    r0 = pl.multiple_of(p * 8, 8)
    for l in range(CMP_STRIDE):
        for g in range(G_NSA):
            hcat[g, pl.ds(r0, 8), l * HD:(l + 1) * HD] = x_refs[g][0, pl.ds(l, PAGE // CMP_STRIDE, stride=CMP_STRIDE), :]

    @pl.when(p == n_chunks - 1)
    def _():
        if paged:
            xn = new_ref[0]
            rid = lax.broadcasted_iota(jnp.int32, xn.shape, 0)
            xn = jnp.where(rid < t_new, xn, 0.0)
            for g in range(G_NSA):
                for l in range(8):
                    hcat[g, n_real:n_real + 1, l * HD:(l + 1) * HD] = xn[l:l + 1, g * HD:(g + 1) * HD]
        w = w_ref[...]
        pe2 = _dot(pe_ref[...].astype(BF16), w)
        pe_bias = pe2[0:1, 0:HD] + pe2[1:2, HD:2 * HD]
        for g in range(G_NSA):
            tb[0:mh, :] = _dot(hcat[g].astype(BF16), w)
            y = tb[0:mn, 0:HD] + tb[1:mn + 1, HD:2 * HD] + pe_bias
            if norm:
                y = y * lax.rsqrt(jnp.mean(y * y, axis=-1, keepdims=True) + EPS) * gain_ref[...]
            o_ref[0, :, g * HD:(g + 1) * HD] = y


def _compress(x_arr, x_col, pe8, wcat, gain, *, mn, norm, page_table=None, pool=None, pool_off=0, t_new=0):
    paged = page_table is not None
    bsz = x_arr.shape[0]
    mh = mn + 8
    w2 = G_NSA * HD
    const = lambda shape: pl.BlockSpec(shape, (lambda b, p, *_: (0,) * len(shape)))
    tail = [const((8, CMP_STRIDE * HD)), const((CMP_STRIDE * HD, 2 * HD)), const((1, HD))]
    out_spec = pl.BlockSpec((1, mn, w2), lambda b, p, *_: (b, 0, 0))
    scratch = [pltpu.VMEM((G_NSA, mh, CMP_STRIDE * HD), F32), pltpu.VMEM((mh + 8, 2 * HD), F32)]
    out_shape = jax.ShapeDtypeStruct((bsz, mn, w2), F32)
    if not paged:
        n_chunks = x_arr.shape[1] // PAGE
        body = functools.partial(_compress_body, paged=False, n_chunks=n_chunks, mn=mn, norm=norm, t_new=0)
        return pl.pallas_call(
            body, grid=(bsz, n_chunks),
            in_specs=[pl.BlockSpec((1, PAGE, HD), lambda b, p, _g=g: (b, p, x_col // HD + _g)) for g in range(G_NSA)] + tail,
            out_specs=out_spec, out_shape=out_shape, scratch_shapes=scratch,
            compiler_params=_cparams(("parallel", "arbitrary")),
        )(*([x_arr] * G_NSA), pe8, wcat, gain)
    n_chunks = page_table.shape[1]
    body = functools.partial(_compress_body, paged=True, n_chunks=n_chunks, mn=mn, norm=norm, t_new=t_new)
    grid_spec = pltpu.PrefetchScalarGridSpec(
        num_scalar_prefetch=1, grid=(bsz, n_chunks),
        in_specs=[pl.BlockSpec((1, PAGE, HD), lambda b, p, pt, _g=g: (pool_off + pt[b, p], 0, _g)) for g in range(G_NSA)]
        + [pl.BlockSpec((1, 8, w2), lambda b, p, pt: (b, 0, x_col // w2))] + tail,
        out_specs=out_spec, scratch_shapes=scratch)
    return pl.pallas_call(body, grid_spec=grid_spec, out_shape=out_shape,
                          compiler_params=_cparams(("parallel", "arbitrary")))(
        page_table, *([pool] * G_NSA), x_arr, pe8, wcat, gain)


def _cmpsel_body(q_ref, kc_ref, vc_ref, o_ref, sel_ref, *, tq, ncp, nsp, n_sel, pos0):
    i = pl.program_id(2)
    q = q_ref[0]
    kc = kc_ref[0].astype(BF16)
    vc = vc_ref[0].astype(BF16)
    pos = pos0 + i * tq + lax.broadcasted_iota(jnp.int32, (tq, ncp), 0)
    col = lax.broadcasted_iota(jnp.int32, (tq, ncp), 1)
    valid = (col * CMP_STRIDE + (CMP_BLOCK - 1)) <= pos
    validf = jnp.where(valid, 1.0, 0.0)
    imp = jnp.zeros((tq, ncp), F32)
    for h in range(HPG):
        qh = (q[:, h * HD:(h + 1) * HD] * ATTN_SCALE).astype(BF16)
        s = jnp.where(valid, _dot_nt(qh, kc), NEG)
        e = jnp.exp(s - jnp.max(s, axis=-1, keepdims=True)) * validf
        p = e / jnp.maximum(jnp.sum(e, axis=-1, keepdims=True), 1e-30)
        o_ref[0, :, h * HD:(h + 1) * HD] = _dot(p.astype(BF16), vc)
        imp = imp + p
    cc = lax.broadcasted_iota(jnp.int32, (ncp, nsp), 0)
    jj = lax.broadcasted_iota(jnp.int32, (ncp, nsp), 1)
    r = SEL_BLOCK // CMP_STRIDE
    gather = jnp.where((cc >= r * jj - 1) & (cc <= r * jj + r - 1), 1.0, 0.0).astype(BF16)
    score = _dot3_l(imp, gather)
    blk = lax.broadcasted_iota(jnp.int32, (tq, nsp), 1)
    pq = pos0 + i * tq + lax.broadcasted_iota(jnp.int32, (tq, nsp), 0)
    forced = (blk == pq // SEL_BLOCK) | (blk == 0)
    future = blk * SEL_BLOCK > pq
    score = jnp.where(forced, FORCE_SCORE, jnp.where(future, -1.0, score))
    score = jnp.where(blk < n_sel, score, -2.0)
    rank = jnp.zeros((tq, nsp), F32)
    for c in range(n_sel):
        sc = score[:, c:c + 1]
        ahead = (sc > score) | ((sc == score) & (blk > c))
        rank = rank + jnp.where(ahead, 1.0, 0.0)
    sel_ref[0, 0] = jnp.where(rank < float(N_SELECT), 1.0, 0.0)


def _cmpsel(z3, kcmp, vcmp, *, tq, pos0, n_sel):
    bsz, t, _ = z3.shape
    ncp = kcmp.shape[1]
    nsp = -(-n_sel // HD) * HD
    w = HPG * HD
    return pl.pallas_call(
        functools.partial(_cmpsel_body, tq=tq, ncp=ncp, nsp=nsp, n_sel=n_sel, pos0=pos0),
        grid=(bsz, G_NSA, t // tq),
        in_specs=[pl.BlockSpec((1, tq, w), lambda b, g, i: (b, i, NQ // w + g)),
                  pl.BlockSpec((1, ncp, HD), lambda b, g, i: (b, 0, g)),
                  pl.BlockSpec((1, ncp, HD), lambda b, g, i: (b, 0, g))],
        out_specs=[pl.BlockSpec((1, tq, w), lambda b, g, i: (b, i, g)),
                   pl.BlockSpec((1, 1, tq, nsp), lambda b, g, i: (b, g, i, 0))],
        out_shape=[jax.ShapeDtypeStruct((bsz, t, G_NSA * w), F32),
                   jax.ShapeDtypeStruct((bsz, G_NSA, t, nsp), F32)],
        compiler_params=_cparams(("parallel", "parallel", "parallel")),
    )(z3, kcmp, vcmp)


def _combine_body(oc_ref, os_ref, ow_ref, zs_ref, gb_ref, o_ref):
    gates = _sigmoid(zs_ref[...] + gb_ref[...])
    for h in range(H_NSA):
        sl = slice(h * HD, (h + 1) * HD)
        c0 = H_FOX + 3 * h
        o = (gates[:, c0:c0 + 1] * oc_ref[:, sl] + gates[:, c0 + 1:c0 + 2] * os_ref[:, sl]
             + gates[:, c0 + 2:c0 + 3] * ow_ref[:, sl])
        o_ref[:, sl] = o.astype(BF16)


def _nsa_combine(oc, osl, ow, zs, gb_pad):
    m, n = oc.shape
    tm = min(m, 512)
    big = pl.BlockSpec((tm, n), lambda i: (i, 0))
    return pl.pallas_call(
        _combine_body, grid=(m // tm,),
        in_specs=[big, big, big, pl.BlockSpec((tm, HD), lambda i: (i, 0)), pl.BlockSpec((1, HD), lambda i: (0, 0))],
        out_specs=big, out_shape=jax.ShapeDtypeStruct((m, n), BF16),
        compiler_params=_cparams(("parallel",)),
    )(oc, osl, ow, zs, gb_pad)


def _paged_body(*refs, mode, pg, n_steps, n_w, hpg_kv, t_new):
    it = iter(refs)
    next(it)
    q_ref = next(it)
    k_refs = [next(it) for _ in range(pg)]
    v_refs = [next(it) for _ in range(pg)]
    lf_refs = [next(it) for _ in range(pg)] if mode == "fox" else None
    kn_ref, vn_ref = next(it), next(it)
    lfn_ref = next(it) if mode == "fox" else None
    sel_ref = next(it) if mode == "slc" else None
    o_ref = next(it)
    qbd, m_scr, l_scr, acc, kn_scr, vn_scr, carry = (next(it) for _ in range(7))
    p = pl.program_id(1)
    rows = 4 * 8
    w = n_w * HD
    row = lax.broadcasted_iota(jnp.int32, (rows, HD), 0)
    kv_of_row = (row % 8) // hpg_kv

    @pl.when(p == 0)
    def _():
        q = q_ref[0] * ATTN_SCALE
        qbd[...] = jnp.concatenate([jnp.where(kv_of_row == c, q, 0.0) for c in range(n_w)], axis=1).astype(BF16)
        m_scr[...] = jnp.full(m_scr.shape, NEG, F32)
        l_scr[...] = jnp.zeros(l_scr.shape, F32)
        acc[...] = jnp.zeros(acc.shape, F32)
        carry[...] = jnp.zeros(carry.shape, F32)
        kn_scr[...] = jnp.zeros(kn_scr.shape, BF16)
        vn_scr[...] = jnp.zeros(vn_scr.shape, BF16)

    def scores(kb, lf, page_idx, is_new):
        s = _dot_nt(qbd[...], kb)
        if mode == "fox":
            c = _dot3_l(lf, _tri(HD, upper=True)) + carry[...]
            carry[...] = jnp.broadcast_to(c[:, HD - 1:HD], carry.shape)
            s = s - jnp.concatenate([c] * 4, axis=0)
        valid = None
        if is_new:
            valid = lax.broadcasted_iota(jnp.int32, (rows, HD), 1) <= row // 8
        if mode == "slc":
            nsp = sel_ref.shape[2]
            jb = lax.broadcasted_iota(jnp.int32, (nsp, HD), 0)
            kk = lax.broadcasted_iota(jnp.int32, (nsp, HD), 1)
            expand = jnp.where(jb == page_idx * (PAGE // SEL_BLOCK) + kk // SEL_BLOCK, 1.0, 0.0).astype(BF16)
            picked = _dot(sel_ref[0].astype(BF16), expand) > 0.5
            valid = picked if valid is None else (valid & picked)
        if valid is not None:
            s = jnp.where(valid, s, NEG)
        return s

    def update(s_list, vb_list):
        m_old = m_scr[...]
        m_new = m_old
        for s in s_list:
            m_new = jnp.maximum(m_new, jnp.max(s, axis=-1, keepdims=True))
        a = jnp.exp(m_old - m_new)
        l = a * l_scr[...]
        o = a * acc[...]
        for s, vb in zip(s_list, vb_list):
            pr = jnp.exp(s - m_new)
            l = l + jnp.sum(pr, axis=-1, keepdims=True)
            o = o + _dot(pr.astype(BF16), vb)
        m_scr[...] = m_new
        l_scr[...] = l
        acc[...] = o

    s_list, vb_list = [], []
    for pp in range(pg):
        kb = k_refs[pp][0].astype(BF16)
        lf = lf_refs[pp][0] if mode == "fox" else None
        s_list.append(scores(kb, lf, p * pg + pp, False))
        vb_list.append(v_refs[pp][0].astype(BF16))
    update(s_list, vb_list)

    @pl.when(p == n_steps - 1)
    def _():
        kn_scr[0:8, :] = kn_ref[0].astype(BF16)
        vn_scr[0:8, :] = vn_ref[0].astype(BF16)
        lf = lfn_ref[0] if mode == "fox" else None
        s = scores(kn_scr[...], lf, n_steps * pg, True)
        update([s], [vn_scr[...]])
        o = acc[...] / l_scr[...]
        out = jnp.zeros((rows, HD), F32)
        for c in range(n_w):
            out = out + jnp.where(kv_of_row == c, o[:, c * HD:(c + 1) * HD], 0.0)
        o_ref[0] = out


def _paged_attn(mode, q_rows, page_table, k_pool, v_pool, pool_off, new_arr, kn_col, vn_col, *, hpg_kv, pg,
                lf_pool=None, lf_new=None, sel_rows=None):
    bsz = q_rows.shape[0]
    n_pages = page_table.shape[1]
    n_steps = n_pages // pg
    w = k_pool.shape[2]
    n_w = w // HD
    page = lambda pp: (lambda b, p, pt: (pool_off + pt[b, p * pg + pp], 0, 0))
    in_specs = [pl.BlockSpec((1, 32, HD), lambda b, p, pt: (b, 0, 0))]
    in_specs += [pl.BlockSpec((1, PAGE, w), page(pp)) for pp in range(pg)]
    in_specs += [pl.BlockSpec((1, PAGE, w), page(pp)) for pp in range(pg)]
    args = [q_rows] + [k_pool] * pg + [v_pool] * pg
    if mode == "fox":
        in_specs += [pl.BlockSpec((1, 8, HD), page(pp)) for pp in range(pg)]
        args += [lf_pool] * pg
    in_specs += [pl.BlockSpec((1, 8, w), lambda b, p, pt: (b, 0, kn_col // w)),
                 pl.BlockSpec((1, 8, w), lambda b, p, pt: (b, 0, vn_col // w))]
    args += [new_arr, new_arr]
    if mode == "fox":
        in_specs.append(pl.BlockSpec((1, 8, HD), lambda b, p, pt: (b, 0, 0)))
        args.append(lf_new)
    if mode == "slc":
        nsp = sel_rows.shape[2]
        in_specs.append(pl.BlockSpec((1, 32, nsp), lambda b, p, pt: (b, 0, 0)))
        args.append(sel_rows)
    grid_spec = pltpu.PrefetchScalarGridSpec(
        num_scalar_prefetch=1, grid=(bsz, n_steps), in_specs=in_specs,
        out_specs=pl.BlockSpec((1, 32, HD), lambda b, p, pt: (b, 0, 0)),
        scratch_shapes=[pltpu.VMEM((32, w), BF16), pltpu.VMEM((32, 1), F32), pltpu.VMEM((32, 1), F32),
                        pltpu.VMEM((32, w), F32), pltpu.VMEM((PAGE, w), BF16), pltpu.VMEM((PAGE, w), BF16),
                        pltpu.VMEM((8, HD), F32)])
    return pl.pallas_call(
        functools.partial(_paged_body, mode=mode, pg=pg, n_steps=n_steps, n_w=n_w, hpg_kv=hpg_kv, t_new=4),
        grid_spec=grid_spec, out_shape=jax.ShapeDtypeStruct((bsz, 32, HD), F32),
        compiler_params=_cparams(("parallel", "arbitrary")),
    )(page_table, *args)


def _gla_body(*refs, mode, t_in, chunk, dv, t_real, has_state):
    it = iter(refs)
    q_ref, k_ref, v_ref, og_ref = next(it), next(it), next(it), next(it)
    if mode == "gla":
        zs_ref, wa_ref, ba_ref = next(it), next(it), next(it)
    else:
        lb_ref = next(it)
    gain_ref = next(it)
    s0_ref = next(it) if has_state else None
    o_ref, s_ref = next(it), next(it)
    n_chunks = max(t_in // chunk, 1)
    pad = chunk - t_in if t_in < chunk else 0
    tri = lax.broadcasted_iota(jnp.int32, (chunk, chunk), 0) >= lax.broadcasted_iota(jnp.int32, (chunk, chunk), 1)
    ltri = jnp.where(tri, 1.0, 0.0).astype(BF16)
    if mode == "hg":
        lbr = lb_ref[...]
        e = jnp.exp(lbr - jnp.max(lbr, axis=0, keepdims=True))
        lb = (e / jnp.sum(e, axis=0, keepdims=True))[0:1, :]

    def load(ref, r0, n):
        x = ref[0, pl.ds(r0, min(chunk, t_in)), :]
        if pad:
            x = jnp.concatenate([x, jnp.zeros((pad, n), F32)], axis=0)
        return x

    def body(c, st):
        r0 = pl.multiple_of(c * chunk, 8)
        qr, kr, v, og = load(q_ref, r0, HD), load(k_ref, r0, HD), load(v_ref, r0, dv), load(og_ref, r0, dv)
        if mode == "gla":
            ga = load(zs_ref, r0, HD)
            g = _log_sigmoid(_dot(ga.astype(BF16), wa_ref[...]) + ba_ref[...]) / GLA_GATE_NORM
            q, k = qr, kr
        else:
            f = lb + (1.0 - lb) * _sigmoid(kr)
            q, k, g = qr * _sigmoid(qr), 1.0 - f, jnp.log(f)
        live = (r0 + lax.broadcasted_iota(jnp.int32, (chunk, HD), 0)) < t_real
        g = jnp.where(live, g, 0.0)
        k = jnp.where(live, k, 0.0)
        b = _dot3_r(ltri, g)
        bm = b[chunk // 2 - 1:chunk // 2, :]
        bl = b[chunk - 1:chunk, :]
        qe = (q * jnp.exp(jnp.minimum(b - bm, EXP_CLAMP))).astype(BF16)
        ke = (k * jnp.exp(jnp.minimum(bm - b, EXP_CLAMP))).astype(BF16)
        a = jnp.where(tri, _dot_nt(qe, ke), 0.0)
        vb = v.astype(BF16)
        o = _dot(a.astype(BF16), vb) + _dot_nt((q * jnp.exp(b)).astype(BF16), st.astype(BF16))
        kd = (k * jnp.exp(bl - b)).astype(BF16)
        st = st * jnp.exp(bl) + _dot_tn(vb, kd)
        o = o * lax.rsqrt(jnp.mean(o * o, axis=-1, keepdims=True) + EPS) * gain_ref[...]
        o = o * (og * _sigmoid(og) if mode == "gla" else _sigmoid(og))
        o_ref[0, pl.ds(r0, min(chunk, t_in)), :] = o[:min(chunk, t_in)].astype(BF16)
        return st

    st0 = s0_ref[0, 0].T if has_state else jnp.zeros((dv, HD), F32)
    st = lax.fori_loop(0, n_chunks, body, st0)
    s_ref[0, 0] = st.T


def _gla(mode, z3, zs3, cols, n_heads, dv, extra, gain, s0, *, t_real, chunk):
    bsz, t_in, _ = z3.shape
    qc, kc, vc, oc = cols
    has_state = s0 is not None
    in_specs = [pl.BlockSpec((1, t_in, HD), lambda b, h: (b, 0, qc // HD + h)),
                pl.BlockSpec((1, t_in, HD), lambda b, h: (b, 0, kc // HD + h)),
                pl.BlockSpec((1, t_in, dv), lambda b, h: (b, 0, vc // dv + h)),
                pl.BlockSpec((1, t_in, dv), lambda b, h: (b, 0, oc // dv + h))]
    args = [z3, z3, z3, z3]
    if mode == "gla":
        wa, ba = extra
        in_specs += [pl.BlockSpec((1, t_in, HD), lambda b, h: (b, 0, 0)),
                     pl.BlockSpec((HD, HD), lambda b, h: (0, h)),
                     pl.BlockSpec((1, HD), lambda b, h: (0, h))]
        args += [zs3, wa, ba]
    else:
        in_specs.append(pl.BlockSpec((8, HD), lambda b, h: (0, h)))
        args.append(extra)
    in_specs.append(pl.BlockSpec((1, dv), lambda b, h: (0, 0)))
    args.append(gain)
    st_spec = pl.BlockSpec((1, 1, HD, dv), lambda b, h: (b, h, 0, 0))
    if has_state:
        in_specs.append(st_spec)
        args.append(s0)
    return pl.pallas_call(
        functools.partial(_gla_body, mode=mode, t_in=t_in, chunk=chunk, dv=dv, t_real=t_real, has_state=has_state),
        grid=(bsz, n_heads),
        in_specs=in_specs,
        out_specs=[pl.BlockSpec((1, t_in, dv), lambda b, h: (b, 0, h)), st_spec],
        out_shape=[jax.ShapeDtypeStruct((bsz, t_in, n_heads * dv), BF16),
                   jax.ShapeDtypeStruct((bsz, n_heads, HD, dv), F32)],
        compiler_params=_cparams(("parallel", "parallel")),
    )(*args)


def _split_cols(w, sizes):
    outs, off = [], 0
    for s in sizes:
        outs.append(w[:, off:off + s])
        off += s
    return outs


def _pad_cols(w, n):
    return jnp.pad(w, ((0, 0), (0, n - w.shape[1])))


def _row(v):
    return v.reshape(1, -1).astype(F32)


def _even_weights(w_in, fq_gain, fk_gain, nq_gain, nk_gain):
    fq, fk, fv, ff, nq, kc, vc, ks, vs, kw, vw, ng = _split_cols(w_in, EVEN_SIZES)
    big = jnp.concatenate([fq, fk, fv, nq, kc, vc, ks, vs, kw, vw], axis=1).astype(BF16)
    small = _pad_cols(jnp.concatenate([ff, ng], axis=1), HD).astype(BF16)
    ones = lambda n: jnp.ones((n,), F32)
    gain = jnp.concatenate([jnp.tile(fq_gain, H_FOX), jnp.tile(fk_gain, H_FOX), ones(H_FOX * HD),
                            jnp.tile(nq_gain, H_NSA), ones(2 * G_NSA * HD), jnp.tile(nk_gain[1], G_NSA),
                            ones(G_NSA * HD), jnp.tile(nk_gain[2], G_NSA), ones(G_NSA * HD)])
    z, o = jnp.zeros, jnp.ones
    flag = jnp.concatenate([o((2 * H_FOX * HD,), F32), z((H_FOX * HD,), F32), o((H_NSA * HD,), F32),
                            z((2 * G_NSA * HD,), F32), o((G_NSA * HD,), F32), z((G_NSA * HD,), F32),
                            o((G_NSA * HD,), F32), z((G_NSA * HD,), F32)])
    return big, small, _row(gain), _row(flag)


def _odd_weights(w_in):
    gq, gk, gv, ga, gg, hq, hf, hi, hg = _split_cols(w_in, ODD_SIZES)
    big = jnp.concatenate([gq, gk, gv, gg, hq, hf, hi, hg], axis=1).astype(BF16)
    small = _pad_cols(ga, HD).astype(BF16)
    gain = jnp.concatenate([jnp.full((H_GLA * DK_GLA,), DK_GLA ** -0.5, F32), jnp.ones((N_BIG_ODD - H_GLA * DK_GLA,), F32)])
    return big, small, _row(gain), jnp.zeros((1, N_BIG_ODD), F32)


def _cmp_weights(pe, w):
    half = CMP_STRIDE * HD
    wcat = jnp.concatenate([w[:half], w[half:]], axis=1).astype(BF16)
    pe8 = jnp.pad(pe.reshape(2, half), ((0, 6), (0, 0)))
    return pe8, wcat


def _ffn(x, g, w1, w3, w2):
    return _mm_res([_ffn_up(x, g, w1, w3)], w2, x)


def _even_layer(x, bsz, t, t_real, ew, past):
    (w_big, w_small, cgain, cflag, g_mix, w_out, b_f_pad, gb_pad, pe_k, wc_k, pe_v, wc_v, nk0) = ew
    z, zs = _proj(x, g_mix, w_big, w_small, cgain, cflag)
    z3 = z.reshape(bsz, t, N_BIG_EVEN)
    zs3 = zs.reshape(bsz, t, HD)
    one = jnp.ones((1, HD), F32)
    if past is None:
        lf, c = _fox_prep(zs3, b_f_pad)
        logf_new = lf[:, :, :H_FOX]
        tq = min(t, 256)
        bias = jnp.swapaxes(c[:, :, :H_FOX], 1, 2).reshape(bsz * H_FOX, t // tq, tq)
        o_fox = _flash("fox", z3, FQ, z3, FK, z3, FV, n_kv=H_FOX, hpg=1, tq=tq, tk=tq, bias=bias, out_dtype=BF16)
        n_cmp = t // CMP_STRIDE
        mn = -(-n_cmp // HD) * HD
        kcmp = _compress(z3, KC, pe_k, wc_k, nk0, mn=mn, norm=True)
        vcmp = _compress(z3, VC, pe_v, wc_v, one, mn=mn, norm=False)
        tqn = min(t, 64)
        o_cmp, sel = _cmpsel(z3, kcmp, vcmp, tq=min(t, 128), pos0=0, n_sel=-(-t // SEL_BLOCK))
        o_slc = _flash("slc", z3, NQ, z3, KS, z3, VS, n_kv=G_NSA, hpg=HPG, tq=tqn, tk=min(t, 256), sel=sel)
        o_win = _flash("win", z3, NQ, z3, KW, z3, VW, n_kv=G_NSA, hpg=HPG, tq=tqn, tk=min(t, 256))
        w_buf = WINDOW
        kw_new = jnp.concatenate([jnp.zeros((bsz, w_buf, G_NSA * HD), F32), z3[:, :, KW:KW + G_NSA * HD]], axis=1)[:, -w_buf:]
        vw_new = jnp.concatenate([jnp.zeros((bsz, w_buf, G_NSA * HD), F32), z3[:, :, VW:VW + G_NSA * HD]], axis=1)[:, -w_buf:]
        o_fox = o_fox.reshape(bsz * t, H_FOX * HD)
    else:
        (page_table, pool_off, fk_pool, fv_pool, lf_pool, kc_pool, vc_pool, ks_pool, vs_pool, buf_k, buf_v) = past
        p_len = page_table.shape[1] * PAGE
        lf_all, _ = _fox_prep(zs.reshape(1, bsz * t, HD), b_f_pad)
        lf_new = lf_all.reshape(bsz, t, HD)[:, :t_real, :H_FOX]
        logf_new = lf_new
        lfn_pad = jnp.pad(jnp.swapaxes(lf_new, 1, 2), ((0, 0), (0, 0), (0, HD - t_real)))
        q_rows = z3[:, :t_real, FQ:FQ + H_FOX * HD].reshape(bsz, 32, HD)
        o_fox = _paged_attn("fox", q_rows, page_table, fk_pool, fv_pool, pool_off, z3, FK, FV, hpg_kv=1, pg=4,
                            lf_pool=lf_pool, lf_new=lfn_pad)
        o_fox = o_fox.reshape(bsz, t_real, H_FOX * HD)
        n_cmp = -(-(p_len + t_real) // CMP_STRIDE)
        mn = -(-n_cmp // HD) * HD
        kcmp = _compress(z3, KC, pe_k, wc_k, nk0, mn=mn, norm=True, page_table=page_table, pool=kc_pool,
                         pool_off=pool_off, t_new=t_real)
        vcmp = _compress(z3, VC, pe_v, wc_v, one, mn=mn, norm=False, page_table=page_table, pool=vc_pool,
                         pool_off=pool_off, t_new=t_real)
        n_sel = -(-(p_len + t_real) // SEL_BLOCK)
        o_cmp, sel = _cmpsel(z3, kcmp, vcmp, tq=t, pos0=p_len, n_sel=n_sel)
        nsp = sel.shape[-1]
        sel_rows = jnp.broadcast_to(jnp.swapaxes(sel[:, :, :t_real], 1, 2)[:, :, :, None, :],
                                    (bsz, t_real, G_NSA, HPG, nsp)).reshape(bsz, 32, nsp)
        nq_rows = z3[:, :t_real, NQ:NQ + H_NSA * HD].reshape(bsz, 32, HD)
        o_slc = _paged_attn("slc", nq_rows, page_table, ks_pool, vs_pool, pool_off, z3, KS, VS, hpg_kv=HPG, pg=8,
                            sel_rows=sel_rows).reshape(bsz, t_real, H_NSA * HD)
        kw_all = jnp.concatenate([buf_k, z3[:, :t_real, KW:KW + G_NSA * HD]], axis=1)
        vw_all = jnp.concatenate([buf_v, z3[:, :t_real, VW:VW + G_NSA * HD]], axis=1)
        w_buf = buf_k.shape[1]
        kw_new, vw_new = kw_all[:, -w_buf:], vw_all[:, -w_buf:]
        lk = -(-(w_buf + t) // HD) * HD
        padk = lambda a: jnp.pad(a, ((0, 0), (0, lk - a.shape[1]), (0, 0)))
        o_win = _flash("win", z3, NQ, padk(kw_all), 0, padk(vw_all), 0, n_kv=G_NSA, hpg=HPG, tq=t, tk=HD, off=w_buf)
        padt = lambda a: jnp.pad(a, ((0, 0), (0, t - t_real), (0, 0)))
        o_fox = padt(o_fox).astype(BF16).reshape(bsz * t, H_FOX * HD)
        o_slc = padt(o_slc)
    m = bsz * t
    o_nsa = _nsa_combine(o_cmp.reshape(m, -1), o_slc.reshape(m, -1), o_win.reshape(m, -1), zs, gb_pad)
    x = _mm_res([o_fox, o_nsa], w_out, x)
    zr = z3[:, :t_real]
    grp = lambda c0: zr[:, :, c0:c0 + G_NSA * HD].reshape(1, bsz, t_real, G_NSA, HD)
    fox = lambda c0: zr[:, :, c0:c0 + H_FOX * HD].reshape(1, bsz, t_real, H_FOX, HD)
    w_rows = kw_new.shape[1]
    outs = (fox(FK), fox(FV), logf_new[None], grp(KC), grp(VC), grp(KS), grp(VS),
            kw_new.reshape(1, bsz, w_rows, G_NSA, HD), vw_new.reshape(1, bsz, w_rows, G_NSA, HD))
    return x, outs


def _odd_layer(x, bsz, t, t_real, ow, state):
    (w_big, w_small, cgain, cflag, g_mix, w_out, wa_pad, ba, gla_gain, lb_pad, hg_gain) = ow
    z, zs = _proj(x, g_mix, w_big, w_small, cgain, cflag)
    z3 = z.reshape(bsz, t, N_BIG_ODD)
    zs3 = zs.reshape(bsz, t, HD)
    s_gla, s_hg = state if state is not None else (None, None)
    chunk = 64 if t >= 64 else 128
    o_gla, s_gla = _gla("gla", z3, zs3, (GQ, GK, GV, GG), H_GLA, DV_GLA, (wa_pad, ba), gla_gain, s_gla,
                        t_real=t_real, chunk=chunk)
    o_hg, s_hg = _gla("hg", z3, zs3, (HQ, HF, HI, HO), H_HG, DV_HG, lb_pad, hg_gain, s_hg,
                      t_real=t_real, chunk=chunk)
    m = bsz * t
    x = _mm_res([o_gla.reshape(m, -1), o_hg.reshape(m, -1)], w_out, x)
    return x, (s_gla[None], s_hg[None])


def kernel(x_prompt, x_sample, cache_fox_k, cache_fox_v, cache_fox_logf, cache_nsa_kc, cache_nsa_vc, cache_nsa_ks, cache_nsa_vs, state_nsa_kw, state_nsa_vw, state_gla, state_hgrn, page_table, norm_mix, norm_ffn, w_in_even, w_out_even, fox_b_f, fox_q_gain, fox_k_gain, nsa_q_gain, nsa_k_gain, nsa_cmp_pe_k, nsa_cmp_pe_v, nsa_cmp_wk, nsa_cmp_wv, nsa_gate_b, w_in_odd, w_out_odd, gla_wa2, gla_ba, gla_norm, hgrn_lb, hgrn_norm, ffn_w1, ffn_w3, ffn_w2):
    bp, tp, d = x_prompt.shape
    bs, ts, _ = x_sample.shape
    n_pool = cache_fox_k.shape[1]
    ts_pad = 8
    pe_k, wc_k = _cmp_weights(nsa_cmp_pe_k[0], nsa_cmp_wk[0])
    pe_v, wc_v = _cmp_weights(nsa_cmp_pe_v[0], nsa_cmp_wv[0])
    b_f_pad = _pad_cols(_row(fox_b_f[0]), HD)
    gb_pad = _pad_cols(jnp.concatenate([jnp.zeros((1, H_FOX), F32), _row(nsa_gate_b[0])], axis=1), HD)
    ew = _even_weights(w_in_even[0], fox_q_gain[0], fox_k_gain[0], nsa_q_gain[0], nsa_k_gain[0]) + (
        _row(norm_mix[0]), w_out_even[0].astype(BF16), b_f_pad, gb_pad, pe_k, wc_k, pe_v, wc_v, _row(nsa_k_gain[0, 0]))
    wa_pad = jnp.pad(gla_wa2[0], ((0, HD - GLA_RANK), (0, 0))).astype(BF16)
    lb_pad = jnp.pad(hgrn_lb.astype(F32), ((0, 8 - hgrn_lb.shape[0]), (0, 0)), constant_values=-1e30)
    ow = _odd_weights(w_in_odd[0]) + (_row(norm_mix[1]), w_out_odd[0].astype(BF16), wa_pad, _row(gla_ba[0]),
                                      _row(gla_norm[0]), lb_pad, _row(hgrn_norm[0]))
    ffn = [(_row(norm_ffn[i]), ffn_w1[i].astype(BF16), ffn_w3[i].astype(BF16), ffn_w2[i].astype(BF16)) for i in range(2)]

    flat = lambda c: c.reshape((c.shape[0] * c.shape[1], PAGE, -1))
    lf_pool = jnp.swapaxes(flat(cache_fox_logf), 1, 2)
    win = lambda s: s.reshape(bs, s.shape[2], G_NSA * HD)
    past = (page_table, 0, flat(cache_fox_k), flat(cache_fox_v), lf_pool, flat(cache_nsa_kc), flat(cache_nsa_vc),
            flat(cache_nsa_ks), flat(cache_nsa_vs), win(state_nsa_kw), win(state_nsa_vw))

    def run(x, bsz, t, t_real, past_, state_):
        x, ev = _even_layer(x, bsz, t, t_real, ew, past_)
        x = _ffn(x, *ffn[0])
        x, od = _odd_layer(x, bsz, t, t_real, ow, state_)
        x = _ffn(x, *ffn[1])
        return x, ev, od

    yp, ev_p, od_p = run(x_prompt.reshape(bp * tp, d), bp, tp, tp, None, None)
    xs = jnp.pad(x_sample, ((0, 0), (0, ts_pad - ts), (0, 0))).reshape(bs * ts_pad, d)
    ys, ev_s, od_s = run(xs, bs, ts_pad, ts, past, (state_gla.reshape(state_gla.shape[1:]),
                                                    state_hgrn.reshape(state_hgrn.shape[1:])))
    y_prompt = yp.reshape(bp, tp, d)
    y_sample = ys.reshape(bs, ts_pad, d)[:, :ts]
    outs = [y_prompt, y_sample]
    for a, b in zip(ev_p, ev_s):
        outs += [a, b]
    for a, b in zip(od_p, od_s):
        outs += [a, b]
    return tuple(outs)
```

```python
import functools

import jax
import jax.numpy as jnp
from jax import lax
from jax.experimental import pallas as pl
from jax.experimental.pallas import tpu as pltpu

F32 = jnp.float32
BF16 = jnp.bfloat16

D_MODEL = 2048
HD = 128
H_FOX = 8
H_NSA = 8
G_NSA = 2
HPG = H_NSA // G_NSA
CMP_STRIDE = 16
CMP_BLOCK = 2 * CMP_STRIDE
SEL_BLOCK = 64
N_SELECT = 16
WINDOW = 512
FORCE_SCORE = 1.0e4
H_GLA = 4
DK_GLA = 128
DV_GLA = 256
GLA_RANK = 16
GLA_GATE_NORM = 16.0
H_HG = 8
DK_HG = 128
DV_HG = 128
PAGE = 128
EVEN_SIZES = (H_FOX * HD, H_FOX * HD, H_FOX * HD, H_FOX, H_NSA * HD,
              G_NSA * HD, G_NSA * HD, G_NSA * HD, G_NSA * HD, G_NSA * HD, G_NSA * HD, 3 * H_NSA)
ODD_SIZES = (H_GLA * DK_GLA, H_GLA * DK_GLA, H_GLA * DV_GLA, GLA_RANK, H_GLA * DV_GLA,
             H_HG * DK_HG, H_HG * DK_HG, H_HG * DV_HG, H_HG * DV_HG)
FQ, FK, FV, NQ, KC, VC, KS, VS, KW, VW = 0, 1024, 2048, 3072, 4096, 4352, 4608, 4864, 5120, 5376
N_BIG_EVEN = 5632
GQ, GK, GV, GG, HQ, HF, HI, HO = 0, 512, 1024, 2048, 3072, 4096, 5120, 6144
N_BIG_ODD = 7168
ATTN_SCALE = HD ** -0.5
NEG = -1.0e30
EPS = 1.0e-6
EXP_CLAMP = 80.0
VMEM_LIMIT = 56 * 1024 * 1024


def _cparams(sem):
    return pltpu.CompilerParams(dimension_semantics=sem, vmem_limit_bytes=VMEM_LIMIT)


def _dot(a, b):
    return jnp.dot(a, b, preferred_element_type=F32)


def _dot_nt(a, b):
    return lax.dot_general(a, b, (((1,), (1,)), ((), ())), preferred_element_type=F32)


def _dot_tn(a, b):
    return lax.dot_general(a, b, (((0,), (0,)), ((), ())), preferred_element_type=F32)


def _split3(x):
    hi = x.astype(BF16)
    r = x - hi.astype(F32)
    mid = r.astype(BF16)
    lo = (r - mid.astype(F32)).astype(BF16)
    return hi, mid, lo


def _dot3_l(x, w):
    hi, mid, lo = _split3(x)
    return _dot(hi, w) + _dot(mid, w) + _dot(lo, w)


def _dot3_r(w, x):
    hi, mid, lo = _split3(x)
    return _dot(w, hi) + _dot(w, mid) + _dot(w, lo)


def _sigmoid(x):
    return 1.0 / (1.0 + jnp.exp(-x))


def _log_sigmoid(x):
    return jnp.minimum(x, 0.0) - jnp.log1p(jnp.exp(-jnp.abs(x)))


def _tri(n, upper):
    r = lax.broadcasted_iota(jnp.int32, (n, n), 0)
    c = lax.broadcasted_iota(jnp.int32, (n, n), 1)
    return jnp.where((r <= c) if upper else (r >= c), 1.0, 0.0).astype(BF16)


def _proj_body(x_ref, g_ref, w_ref, ws_ref, cg_ref, cf_ref, z_ref, zs_ref, h_scr, *, tn):
    @pl.when(pl.program_id(1) == 0)
    def _():
        x = x_ref[...]
        r = lax.rsqrt(jnp.mean(x * x, axis=-1, keepdims=True) + EPS)
        h = (x * r * g_ref[...]).astype(BF16)
        h_scr[...] = h
        zs_ref[...] = _dot(h, ws_ref[...])

    z = _dot(h_scr[...], w_ref[...])
    for c in range(tn // HD):
        sl = slice(c * HD, (c + 1) * HD)
        zc = z[:, sl]
        r = lax.rsqrt(jnp.mean(zc * zc, axis=-1, keepdims=True) + EPS)
        f = cf_ref[:, sl]
        z_ref[:, sl] = zc * (f * r + (1.0 - f)) * cg_ref[:, sl]


def _proj(x, g, w_big, w_small, col_gain, col_flag):
    m, d = x.shape
    n = w_big.shape[1]
    tm = min(m, 1024)
    tn = 512
    return pl.pallas_call(
        functools.partial(_proj_body, tn=tn),
        grid=(m // tm, n // tn),
        in_specs=[pl.BlockSpec((tm, d), lambda i, j: (i, 0)),
                  pl.BlockSpec((1, d), lambda i, j: (0, 0)),
                  pl.BlockSpec((d, tn), lambda i, j: (0, j)),
                  pl.BlockSpec((d, HD), lambda i, j: (0, 0)),
                  pl.BlockSpec((1, tn), lambda i, j: (0, j)),
                  pl.BlockSpec((1, tn), lambda i, j: (0, j))],
        out_specs=[pl.BlockSpec((tm, tn), lambda i, j: (i, j)),
                   pl.BlockSpec((tm, HD), lambda i, j: (i, 0))],
        out_shape=[jax.ShapeDtypeStruct((m, n), F32), jax.ShapeDtypeStruct((m, HD), F32)],
        scratch_shapes=[pltpu.VMEM((tm, d), BF16)],
        compiler_params=_cparams(("parallel", "arbitrary")),
    )(x, g, w_big, w_small, col_gain, col_flag)


def _ffn_up_body(x_ref, g_ref, w1_ref, w3_ref, o_ref, h_scr):
    @pl.when(pl.program_id(1) == 0)
    def _():
        x = x_ref[...]
        r = lax.rsqrt(jnp.mean(x * x, axis=-1, keepdims=True) + EPS)
        h_scr[...] = (x * r * g_ref[...]).astype(BF16)

    h = h_scr[...]
    a = _dot(h, w1_ref[...])
    b = _dot(h, w3_ref[...])
    o_ref[...] = (a * _sigmoid(a) * b).astype(BF16)


def _ffn_up(x, g, w1, w3):
    m, d = x.shape
    n = w1.shape[1]
    tm = min(m, 1024)
    tn = 512
    return pl.pallas_call(
        _ffn_up_body,
        grid=(m // tm, n // tn),
        in_specs=[pl.BlockSpec((tm, d), lambda i, j: (i, 0)),
                  pl.BlockSpec((1, d), lambda i, j: (0, 0)),
                  pl.BlockSpec((d, tn), lambda i, j: (0, j)),
                  pl.BlockSpec((d, tn), lambda i, j: (0, j))],
        out_specs=pl.BlockSpec((tm, tn), lambda i, j: (i, j)),
        out_shape=jax.ShapeDtypeStruct((m, n), BF16),
        scratch_shapes=[pltpu.VMEM((tm, d), BF16)],
        compiler_params=_cparams(("parallel", "arbitrary")),
    )(x, g, w1, w3)


def _mm_res_body(*refs, n_in):
    res_ref = refs[2 * n_in]
    o_ref = refs[2 * n_in + 1]
    acc = res_ref[...]
    for a_ref, w_ref in zip(refs[:n_in], refs[n_in:2 * n_in]):
        acc = acc + _dot(a_ref[...], w_ref[...])
    o_ref[...] = acc


def _mm_res(a_list, w, res):
    m, n = res.shape
    tm = min(m, 512)
    tn = 512
    in_specs, w_specs, off = [], [], 0
    for a in a_list:
        k = a.shape[1]
        in_specs.append(pl.BlockSpec((tm, k), lambda i, j: (i, 0)))
        w_specs.append(pl.BlockSpec((k, tn), lambda i, j, _o=off // k: (_o, j)))
        off += k
    return pl.pallas_call(
        functools.partial(_mm_res_body, n_in=len(a_list)),
        grid=(m // tm, n // tn),
        in_specs=in_specs + w_specs + [pl.BlockSpec((tm, tn), lambda i, j: (i, j))],
        out_specs=pl.BlockSpec((tm, tn), lambda i, j: (i, j)),
        out_shape=jax.ShapeDtypeStruct((m, n), F32),
        compiler_params=_cparams(("parallel", "parallel")),
    )(*a_list, *([w] * len(a_list)), res)


def _fox_prep_body(zs_ref, b_ref, lf_ref, c_ref, carry, *, tc):
    @pl.when(pl.program_id(1) == 0)
    def _():
        carry[...] = jnp.zeros_like(carry)

    lf = _log_sigmoid(zs_ref[0] + b_ref[...])
    c = _dot3_r(_tri(tc, upper=False), lf) + carry[...]
    lf_ref[0] = lf
    c_ref[0] = c
    carry[...] = c[tc - 1:tc, :]


def _fox_prep(zs3, b_pad):
    bsz, t, _ = zs3.shape
    tc = min(t, 256)
    spec = pl.BlockSpec((1, tc, HD), lambda b, i: (b, i, 0))
    return pl.pallas_call(
        functools.partial(_fox_prep_body, tc=tc),
        grid=(bsz, t // tc),
        in_specs=[spec, pl.BlockSpec((1, HD), lambda b, i: (0, 0))],
        out_specs=[spec, spec],
        out_shape=[jax.ShapeDtypeStruct(zs3.shape, F32)] * 2,
        scratch_shapes=[pltpu.VMEM((1, HD), F32)],
        compiler_params=_cparams(("parallel", "arbitrary")),
    )(zs3, b_pad)


def _flash_body(*refs, mode, tq, tk, hpg, off):
    it = iter(refs)
    q_ref, k_ref, v_ref = next(it), next(it), next(it)
    bias_ref = next(it) if mode == "fox" else None
    sel_ref = next(it) if mode == "slc" else None
    o_ref, kb, vb = next(it), next(it), next(it)
    i = pl.program_id(2)

    @pl.when(i == 0)
    def _():
        kb[...] = k_ref[0].astype(BF16)
        vb[...] = v_ref[0].astype(BF16)

    q = q_ref[0]
    if hpg > 1:
        q = jnp.concatenate([q[:, h * HD:(h + 1) * HD] for h in range(hpg)], axis=0)
    qa = (q * ATTN_SCALE).astype(BF16)
    rows = hpg * tq
    tile_rows = (lambda x: jnp.concatenate([x] * hpg, axis=0)) if hpg > 1 else (lambda x: x)
    lane = lax.broadcasted_iota(jnp.int32, (tq, tk), 1)
    q_lo = off + i * tq
    q_hi = q_lo + tq - 1
    qpos = q_lo + lax.broadcasted_iota(jnp.int32, (tq, tk), 0)
    j_hi = q_hi // tk + 1
    if mode == "slc":
        unpicked = (sel_ref[0, 0] - 1.0).astype(BF16)

    def tile(j, carry, masked):
        m, l, acc = carry
        ks = pl.multiple_of(j * tk, tk)
        kt = kb[pl.ds(ks, tk), :]
        vt = vb[pl.ds(ks, tk), :]
        s = _dot_nt(qa, kt)
        if mode == "fox":
            s = s - bias_ref[0, pl.ds(j, 1), :]
        if mode == "slc":
            cb = lax.broadcasted_iota(jnp.int32, (HD, tk), 0)
            kk = lax.broadcasted_iota(jnp.int32, (HD, tk), 1)
            expand = jnp.where(cb == (ks + kk) // SEL_BLOCK, 2.0 ** 100, 0.0).astype(BF16)
            s = s + tile_rows(_dot(unpicked, expand))
        if masked:
            kidx = ks + lane
            valid = kidx <= qpos
            if mode == "win":
                valid = valid & ((qpos - kidx) < WINDOW)
            s = jnp.where(tile_rows(valid), s, NEG)
        m_new = jnp.maximum(m, jnp.max(s, axis=-1, keepdims=True))
        a = jnp.exp(m - m_new)
        p = jnp.exp(s - m_new)
        l = a * l + jnp.sum(p, axis=-1, keepdims=True)
        acc = a * acc + _dot(p.astype(BF16), vt)
        return m_new, l, acc

    carry = (jnp.full((rows, 1), NEG, F32), jnp.zeros((rows, 1), F32), jnp.zeros((rows, HD), F32))
    if mode == "win":
        j_lo = jnp.maximum(q_lo - (WINDOW - 1), 0) // tk
    else:
        j_lo = (q_lo + 1) // tk
        carry = lax.fori_loop(0, j_lo, functools.partial(tile, masked=False), carry)
    _, l, acc = lax.fori_loop(j_lo, j_hi, functools.partial(tile, masked=True), carry)
    o = acc / l
    for h in range(hpg):
        o_ref[0, :, h * HD:(h + 1) * HD] = o[h * tq:(h + 1) * tq].astype(o_ref.dtype)


def _flash(mode, q_arr, q_col, k_arr, k_col, v_arr, v_col, *, n_kv, hpg, tq, tk, off=0,
           bias=None, sel=None, out_dtype=F32):
    bsz, t_q = q_arr.shape[0], q_arr.shape[1]
    t_k = k_arr.shape[1]
    nq = t_q // tq
    w = hpg * HD
    in_specs = [pl.BlockSpec((1, tq, w), lambda b, g, i: (b, i, q_col // w + g)),
                pl.BlockSpec((1, t_k, HD), lambda b, g, i: (b, 0, k_col // HD + g)),
                pl.BlockSpec((1, t_k, HD), lambda b, g, i: (b, 0, v_col // HD + g))]
    args = [q_arr, k_arr, v_arr]
    if mode == "fox":
        in_specs.append(pl.BlockSpec((1, t_k // tk, tk), lambda b, g, i: (b * n_kv + g, 0, 0)))
        args.append(bias)
    if mode == "slc":
        in_specs.append(pl.BlockSpec((1, 1, tq, HD), lambda b, g, i: (b, g, i, 0)))
        args.append(sel)
    return pl.pallas_call(
        functools.partial(_flash_body, mode=mode, tq=tq, tk=tk, hpg=hpg, off=off),
        grid=(bsz, n_kv, nq),
        in_specs=in_specs,
        out_specs=pl.BlockSpec((1, tq, w), lambda b, g, i: (b, i, g)),
        out_shape=jax.ShapeDtypeStruct((bsz, t_q, n_kv * w), out_dtype),
        scratch_shapes=[pltpu.VMEM((t_k, HD), BF16), pltpu.VMEM((t_k, HD), BF16)],
        compiler_params=_cparams(("parallel", "parallel", "arbitrary")),
    )(*args)


def _compress_body(*refs, paged, pg, n_steps, mn, norm, t_new):
    it = iter(refs)
    if paged:
        next(it)
    x_refs = [next(it) for _ in range(pg if paged else G_NSA)]
    new_ref = next(it) if paged else None
    pe_ref, w_ref, gain_ref, o_ref, hcat, tb = next(it), next(it), next(it), next(it), next(it), next(it)
    p = pl.program_id(1)
    mh = mn + 8
    per_page = PAGE // CMP_STRIDE
    n_real = n_steps * pg * per_page

    @pl.when(p == 0)
    def _():
        hcat[:, n_real:, :] = jnp.zeros((G_NSA, mh - n_real, CMP_STRIDE * HD), F32)

    for pp in range(pg):
        r0 = pl.multiple_of((p * pg + pp) * per_page, per_page)
        for l in range(CMP_STRIDE):
            for g in range(G_NSA):
                if paged:
                    piece = x_refs[pp][0, pl.ds(G_NSA * l + g, per_page, stride=G_NSA * CMP_STRIDE), :]
                else:
                    piece = x_refs[g][0, pl.ds(l, per_page, stride=CMP_STRIDE), :]
                hcat[g, pl.ds(r0, per_page), l * HD:(l + 1) * HD] = piece

    @pl.when(p == n_steps - 1)
    def _():
        if paged:
            xn = new_ref[0]
            rid = lax.broadcasted_iota(jnp.int32, xn.shape, 0)
            xn = jnp.where(rid < t_new, xn, 0.0)
            for g in range(G_NSA):
                for l in range(8):
                    hcat[g, n_real:n_real + 1, l * HD:(l + 1) * HD] = xn[l:l + 1, g * HD:(g + 1) * HD]
        w = w_ref[...]
        pe2 = _dot(pe_ref[...].astype(BF16), w)
        pe_bias = pe2[0:1, 0:HD] + pe2[1:2, HD:2 * HD]
        for g in range(G_NSA):
            tb[...] = _dot(hcat[g].astype(BF16), w)
            y = tb[0:mn, 0:HD] + tb[1:mn + 1, HD:2 * HD] + pe_bias
            if norm:
                y = y * lax.rsqrt(jnp.mean(y * y, axis=-1, keepdims=True) + EPS) * gain_ref[...]
            o_ref[0, :, g * HD:(g + 1) * HD] = y


def _compress(x_arr, x_col, pe8, wcat, gain, *, mn, norm, page_table=None, pool=None, pool_off=0, t_new=0, pg=8):
    paged = page_table is not None
    bsz = x_arr.shape[0]
    mh = mn + 8
    w2 = G_NSA * HD
    const = lambda shape: pl.BlockSpec(shape, (lambda b, p, *_: (0,) * len(shape)))
    tail = [const((8, CMP_STRIDE * HD)), const((CMP_STRIDE * HD, 2 * HD)), const((1, HD))]
    out_spec = pl.BlockSpec((1, mn, w2), lambda b, p, *_: (b, 0, 0))
    scratch = [pltpu.VMEM((G_NSA, mh, CMP_STRIDE * HD), F32), pltpu.VMEM((mh, 2 * HD), F32)]
    out_shape = jax.ShapeDtypeStruct((bsz, mn, w2), F32)
    if not paged:
        n_steps = x_arr.shape[1] // PAGE
        body = functools.partial(_compress_body, paged=False, pg=1, n_steps=n_steps, mn=mn, norm=norm, t_new=0)
        return pl.pallas_call(
            body, grid=(bsz, n_steps),
            in_specs=[pl.BlockSpec((1, PAGE, HD), lambda b, p, _g=g: (b, p, x_col // HD + _g)) for g in range(G_NSA)] + tail,
            out_specs=out_spec, out_shape=out_shape, scratch_shapes=scratch,
            compiler_params=_cparams(("parallel", "arbitrary")),
        )(*([x_arr] * G_NSA), pe8, wcat, gain)
    pg = min(pg, page_table.shape[1])
    n_steps = page_table.shape[1] // pg
    body = functools.partial(_compress_body, paged=True, pg=pg, n_steps=n_steps, mn=mn, norm=norm, t_new=t_new)
    grid_spec = pltpu.PrefetchScalarGridSpec(
        num_scalar_prefetch=1, grid=(bsz, n_steps),
        in_specs=[pl.BlockSpec((1, G_NSA * PAGE, HD), lambda b, p, pt, _pp=pp: (pool_off + pt[b, p * pg + _pp], 0, 0))
                  for pp in range(pg)]
        + [pl.BlockSpec((1, 8, w2), lambda b, p, pt: (b, 0, x_col // w2))] + tail,
        out_specs=out_spec, scratch_shapes=scratch)
    return pl.pallas_call(body, grid_spec=grid_spec, out_shape=out_shape,
                          compiler_params=_cparams(("parallel", "arbitrary")))(
        page_table, *([pool] * pg), x_arr, pe8, wcat, gain)


def _cmpsel_body(q_ref, kc_ref, vc_ref, o_ref, sel_ref, *, tq, ncp, nsp, n_sel, pos0):
    i = pl.program_id(2)
    q = q_ref[0]
    kc = kc_ref[0].astype(BF16)
    vc = vc_ref[0].astype(BF16)
    pos = pos0 + i * tq + lax.broadcasted_iota(jnp.int32, (tq, ncp), 0)
    col = lax.broadcasted_iota(jnp.int32, (tq, ncp), 1)
    valid = (col * CMP_STRIDE + (CMP_BLOCK - 1)) <= pos
    validf = jnp.where(valid, 1.0, 0.0)
    imp = jnp.zeros((tq, ncp), F32)
    for h in range(HPG):
        qh = (q[:, h * HD:(h + 1) * HD] * ATTN_SCALE).astype(BF16)
        s = jnp.where(valid, _dot_nt(qh, kc), NEG)
        e = jnp.exp(s - jnp.max(s, axis=-1, keepdims=True)) * validf
        p = e / jnp.maximum(jnp.sum(e, axis=-1, keepdims=True), 1e-30)
        o_ref[0, :, h * HD:(h + 1) * HD] = _dot(p.astype(BF16), vc)
        imp = imp + p
    cc = lax.broadcasted_iota(jnp.int32, (ncp, nsp), 0)
    jj = lax.broadcasted_iota(jnp.int32, (ncp, nsp), 1)
    r = SEL_BLOCK // CMP_STRIDE
    gather = jnp.where((cc >= r * jj - 1) & (cc <= r * jj + r - 1), 1.0, 0.0).astype(BF16)
    score = _dot3_l(imp, gather)
    blk = lax.broadcasted_iota(jnp.int32, (tq, nsp), 1)
    pq = pos0 + i * tq + lax.broadcasted_iota(jnp.int32, (tq, nsp), 0)
    forced = (blk == pq // SEL_BLOCK) | (blk == 0)
    future = blk * SEL_BLOCK > pq
    score = jnp.where(forced, FORCE_SCORE, jnp.where(future, -1.0, score))
    score = jnp.where(blk < n_sel, score, -2.0)
    rank = jnp.zeros((tq, nsp), F32)
    for c in range(n_sel):
        sc = score[:, c:c + 1]
        ahead = (sc > score) | ((sc == score) & (blk > c))
        rank = rank + jnp.where(ahead, 1.0, 0.0)
    sel_ref[0, 0] = jnp.where(rank < float(N_SELECT), 1.0, 0.0)


def _cmpsel(z3, kcmp, vcmp, *, tq, pos0, n_sel):
    bsz, t, _ = z3.shape
    ncp = kcmp.shape[1]
    nsp = -(-n_sel // HD) * HD
    w = HPG * HD
    return pl.pallas_call(
        functools.partial(_cmpsel_body, tq=tq, ncp=ncp, nsp=nsp, n_sel=n_sel, pos0=pos0),
        grid=(bsz, G_NSA, t // tq),
        in_specs=[pl.BlockSpec((1, tq, w), lambda b, g, i: (b, i, NQ // w + g)),
                  pl.BlockSpec((1, ncp, HD), lambda b, g, i: (b, 0, g)),
                  pl.BlockSpec((1, ncp, HD), lambda b, g, i: (b, 0, g))],
        out_specs=[pl.BlockSpec((1, tq, w), lambda b, g, i: (b, i, g)),
                   pl.BlockSpec((1, 1, tq, nsp), lambda b, g, i: (b, g, i, 0))],
        out_shape=[jax.ShapeDtypeStruct((bsz, t, G_NSA * w), F32),
                   jax.ShapeDtypeStruct((bsz, G_NSA, t, nsp), F32)],
        compiler_params=_cparams(("parallel", "parallel", "parallel")),
    )(z3, kcmp, vcmp)


def _combine_body(oc_ref, os_ref, ow_ref, zs_ref, gb_ref, o_ref):
    gates = _sigmoid(zs_ref[...] + gb_ref[...])
    for h in range(H_NSA):
        sl = slice(h * HD, (h + 1) * HD)
        c0 = H_FOX + 3 * h
        o = (gates[:, c0:c0 + 1] * oc_ref[:, sl] + gates[:, c0 + 1:c0 + 2] * os_ref[:, sl]
             + gates[:, c0 + 2:c0 + 3] * ow_ref[:, sl])
        o_ref[:, sl] = o.astype(BF16)


def _nsa_combine(oc, osl, ow, zs, gb_pad):
    m, n = oc.shape
    tm = min(m, 512)
    big = pl.BlockSpec((tm, n), lambda i: (i, 0))
    return pl.pallas_call(
        _combine_body, grid=(m // tm,),
        in_specs=[big, big, big, pl.BlockSpec((tm, HD), lambda i: (i, 0)), pl.BlockSpec((1, HD), lambda i: (0, 0))],
        out_specs=big, out_shape=jax.ShapeDtypeStruct((m, n), BF16),
        compiler_params=_cparams(("parallel",)),
    )(oc, osl, ow, zs, gb_pad)


def _paged_body(*refs, mode, pg, n_steps, n_w, hpg_kv, t_new):
    it = iter(refs)
    next(it)
    q_ref = next(it)
    k_refs = [next(it) for _ in range(pg)]
    v_refs = [next(it) for _ in range(pg)]
    cl_refs = [next(it) for _ in range(pg)] if mode == "fox" else None
    tf_refs = [next(it) for _ in range(pg)] if mode == "fox" else None
    kn_ref, vn_ref = next(it), next(it)
    lfn_ref = next(it) if mode == "fox" else None
    sel_ref = next(it) if mode == "slc" else None
    o_ref = next(it)
    qb, m_scr, l_scr, acc, kn_scr, vn_scr, carry = (next(it) for _ in range(7))
    p = pl.program_id(1)
    rows = 4 * 8
    r2 = n_w * PAGE
    n_new = t_new * n_w

    def kv_match(ncols):
        row = lax.broadcasted_iota(jnp.int32, (rows, ncols), 0)
        col = lax.broadcasted_iota(jnp.int32, (rows, ncols), 1)
        return row, col, (col % n_w) == (row % 8) // hpg_kv

    @pl.when(p == 0)
    def _():
        qb[...] = (q_ref[0] * ATTN_SCALE).astype(BF16)
        m_scr[...] = jnp.full(m_scr.shape, NEG, F32)
        l_scr[...] = jnp.zeros(l_scr.shape, F32)
        acc[...] = jnp.zeros(acc.shape, F32)
        carry[...] = jnp.zeros(carry.shape, F32)
        kn_scr[...] = jnp.zeros(kn_scr.shape, BF16)
        vn_scr[...] = jnp.zeros(vn_scr.shape, BF16)

    def picked(page_idx, ncols):
        nsp = sel_ref.shape[2]
        jb = lax.broadcasted_iota(jnp.int32, (nsp, ncols), 0)
        key = lax.broadcasted_iota(jnp.int32, (nsp, ncols), 1) // n_w
        expand = jnp.where(jb == page_idx * (PAGE // SEL_BLOCK) + key // SEL_BLOCK, 1.0, 0.0).astype(BF16)
        return _dot(sel_ref[0].astype(BF16), expand) > 0.5

    def update(s_list, vb_list):
        m_old = m_scr[...]
        m_new = m_old
        for s in s_list:
            m_new = jnp.maximum(m_new, jnp.max(s, axis=-1, keepdims=True))
        a = jnp.exp(m_old - m_new)
        l = a * l_scr[...]
        o = a * acc[...]
        for s, vb in zip(s_list, vb_list):
            pr = jnp.exp(s - m_new)
            l = l + jnp.sum(pr, axis=-1, keepdims=True)
            o = o + _dot(pr.astype(BF16), vb)
        m_scr[...] = m_new
        l_scr[...] = l
        acc[...] = o

    _, _, match = kv_match(r2)
    s_list, vb_list = [], []
    c_run = carry[...] if mode == "fox" else None
    for pp in range(pg):
        s = _dot_nt(qb[...], k_refs[pp][0].astype(BF16))
        valid = match
        if mode == "fox":
            s = s - (c_run + cl_refs[pp][0])
            c_run = c_run + tf_refs[pp][0]
        if mode == "slc":
            valid = valid & picked(p * pg + pp, r2)
        s_list.append(jnp.where(valid, s, NEG))
        vb_list.append(v_refs[pp][0].astype(BF16))
    if mode == "fox":
        carry[...] = c_run
    update(s_list, vb_list)

    @pl.when(p == n_steps - 1)
    def _():
        kn_scr[0:n_new, :] = kn_ref[0].astype(BF16)
        vn_scr[0:n_new, :] = vn_ref[0].astype(BF16)
        s = _dot_nt(qb[...], kn_scr[...])
        row, col, valid = kv_match(HD)
        valid = valid & (col // n_w <= row // 8) & (col < n_new)
        if mode == "fox":
            rr = lax.broadcasted_iota(jnp.int32, (HD, HD), 0)
            cc = lax.broadcasted_iota(jnp.int32, (HD, HD), 1)
            pre = jnp.where((rr % n_w == cc % n_w) & (rr // n_w <= cc // n_w), 1.0, 0.0).astype(BF16)
            c_new = _dot3_l(jnp.broadcast_to(lfn_ref[0], (8, HD)), pre)[0:1, :] + carry[:, 0:HD]
            s = s - c_new
        if mode == "slc":
            valid = valid & picked(n_steps * pg, HD)
        update([jnp.where(valid, s, NEG)], [vn_scr[...]])
        o_ref[0] = acc[...] / l_scr[...]


def _paged_attn(mode, q_rows, page_table, k_pool, v_pool, pool_off, k_new, v_new, *, hpg_kv, pg,
                c_local=None, c_total=None, lf_new=None, sel_rows=None):
    bsz = q_rows.shape[0]
    n_pages = page_table.shape[1]
    pg = min(pg, n_pages)
    n_steps = n_pages // pg
    r2 = k_pool.shape[1]
    n_w = r2 // PAGE
    n_new = k_new.shape[1]
    page = lambda pp: (lambda b, p, pt: (pool_off + pt[b, p * pg + pp], 0, 0))
    per_b = lambda b, p, pt: (b, 0, 0)
    in_specs = [pl.BlockSpec((1, 32, HD), per_b)]
    in_specs += [pl.BlockSpec((1, r2, HD), page(pp)) for pp in range(pg)] * 2
    args = [q_rows] + [k_pool] * pg + [v_pool] * pg
    if mode == "fox":
        in_specs += [pl.BlockSpec((1, 1, r2), page(pp)) for pp in range(pg)] * 2
        args += [c_local] * pg + [c_total] * pg
    in_specs += [pl.BlockSpec((1, n_new, HD), per_b)] * 2
    args += [k_new, v_new]
    if mode == "fox":
        in_specs.append(pl.BlockSpec((1, 1, HD), per_b))
        args.append(lf_new)
    if mode == "slc":
        in_specs.append(pl.BlockSpec((1, 32, sel_rows.shape[2]), per_b))
        args.append(sel_rows)
    grid_spec = pltpu.PrefetchScalarGridSpec(
        num_scalar_prefetch=1, grid=(bsz, n_steps), in_specs=in_specs,
        out_specs=pl.BlockSpec((1, 32, HD), per_b),
        scratch_shapes=[pltpu.VMEM((32, HD), BF16), pltpu.VMEM((32, 1), F32), pltpu.VMEM((32, 1), F32),
                        pltpu.VMEM((32, HD), F32), pltpu.VMEM((PAGE, HD), BF16), pltpu.VMEM((PAGE, HD), BF16),
                        pltpu.VMEM((1, r2), F32)])
    return pl.pallas_call(
        functools.partial(_paged_body, mode=mode, pg=pg, n_steps=n_steps, n_w=n_w, hpg_kv=hpg_kv, t_new=n_new // n_w),
        grid_spec=grid_spec, out_shape=jax.ShapeDtypeStruct((bsz, 32, HD), F32),
        compiler_params=_cparams(("parallel", "arbitrary")),
    )(page_table, *args)


def _pool_prefix_body(x_ref, cl_ref, ct_ref, w_scr, *, n_w):
    r2 = x_ref.shape[1]

    @pl.when(pl.program_id(0) == 0)
    def _():
        r = lax.broadcasted_iota(jnp.int32, (r2, 2 * r2), 0)
        c = lax.broadcasted_iota(jnp.int32, (r2, 2 * r2), 1)
        same = (r % n_w) == (c % n_w)
        local = same & (r // n_w <= c // n_w) & (c < r2)
        w_scr[...] = jnp.where(local | (same & (c >= r2)), 1.0, 0.0).astype(BF16)

    y = _dot3_l(x_ref[...], w_scr[...])
    cl_ref[...] = y[:, :r2]
    ct_ref[...] = y[:, r2:]


def _pool_prefix(x, n_w):
    n, r2 = x.shape
    tm = next((c for c in (256, 128, 64, 32, 16, 8) if n % c == 0), n)
    spec = pl.BlockSpec((tm, r2), lambda i: (i, 0))
    return pl.pallas_call(
        functools.partial(_pool_prefix_body, n_w=n_w), grid=(n // tm,),
        in_specs=[spec], out_specs=[spec, spec],
        out_shape=[jax.ShapeDtypeStruct((n, r2), F32)] * 2,
        scratch_shapes=[pltpu.VMEM((r2, 2 * r2), BF16)],
        compiler_params=_cparams(("arbitrary",)),
    )(x)


def _gla_body(*refs, mode, t_in, chunk, dv, t_real, has_state, hb):
    it = iter(refs)
    q_ref, k_ref, v_ref, og_ref = next(it), next(it), next(it), next(it)
    if mode == "gla":
        zs_ref, wa_ref, ba_ref = next(it), next(it), next(it)
    else:
        lb_ref = next(it)
    gain_ref = next(it)
    s0_ref = next(it) if has_state else None
    o_ref, s_ref, st_scr = next(it), next(it), next(it)
    n_chunks = max(t_in // chunk, 1)
    n_rows = min(chunk, t_in)
    pad = chunk - n_rows
    tri = lax.broadcasted_iota(jnp.int32, (chunk, chunk), 0) >= lax.broadcasted_iota(jnp.int32, (chunk, chunk), 1)
    ltri = jnp.where(tri, 1.0, 0.0).astype(BF16)
    if mode == "hg":
        lbr = lb_ref[...]
        e = jnp.exp(lbr - jnp.max(lbr, axis=0, keepdims=True))
        lb_all = (e / jnp.sum(e, axis=0, keepdims=True))[0:1, :]
    for h in range(hb):
        st_scr[h] = s0_ref[0, h].T if has_state else jnp.zeros((dv, HD), F32)

    def load(ref, r0, sl):
        x = ref[0, pl.ds(r0, n_rows), sl]
        if pad:
            x = jnp.concatenate([x, jnp.zeros((pad, x.shape[1]), F32)], axis=0)
        return x

    def body(c, _):
        r0 = pl.multiple_of(c * chunk, 8)
        live = (r0 + lax.broadcasted_iota(jnp.int32, (chunk, HD), 0)) < t_real
        if mode == "gla":
            ga = load(zs_ref, r0, slice(0, HD)).astype(BF16)
        for h in range(hb):
            ks, vs = slice(h * HD, (h + 1) * HD), slice(h * dv, (h + 1) * dv)
            qr, kr, v, og = load(q_ref, r0, ks), load(k_ref, r0, ks), load(v_ref, r0, vs), load(og_ref, r0, vs)
            if mode == "gla":
                g = _log_sigmoid(_dot(ga, wa_ref[:, ks]) + ba_ref[:, ks]) / GLA_GATE_NORM
                q, k = qr, kr
            else:
                lb = lb_all[:, ks]
                f = lb + (1.0 - lb) * _sigmoid(kr)
                q, k, g = qr * _sigmoid(qr), 1.0 - f, jnp.log(f)
            g = jnp.where(live, g, 0.0)
            k = jnp.where(live, k, 0.0)
            b = _dot3_r(ltri, g)
            bm = b[chunk // 2 - 1:chunk // 2, :]
            bl = b[chunk - 1:chunk, :]
            qe = (q * jnp.exp(jnp.minimum(b - bm, EXP_CLAMP))).astype(BF16)
            ke = (k * jnp.exp(jnp.minimum(bm - b, EXP_CLAMP))).astype(BF16)
            a = jnp.where(tri, _dot_nt(qe, ke), 0.0)
            vb = v.astype(BF16)
            st = st_scr[h]
            o = _dot(a.astype(BF16), vb) + _dot_nt((q * jnp.exp(b)).astype(BF16), st.astype(BF16))
            kd = (k * jnp.exp(bl - b)).astype(BF16)
            st_scr[h] = st * jnp.exp(bl) + _dot_tn(vb, kd)
            o = o * lax.rsqrt(jnp.mean(o * o, axis=-1, keepdims=True) + EPS) * gain_ref[...]
            o = o * (og * _sigmoid(og) if mode == "gla" else _sigmoid(og))
            o_ref[0, pl.ds(r0, n_rows), vs] = o[:n_rows].astype(BF16)
        return 0

    lax.fori_loop(0, n_chunks, body, 0)
    for h in range(hb):
        s_ref[0, h] = st_scr[h].T


def _gla(mode, z3, zs3, cols, n_heads, dv, extra, gain, s0, *, t_real, chunk, hb):
    bsz, t_in, _ = z3.shape
    qc, kc, vc, oc = cols
    has_state = s0 is not None
    wk, wv = hb * HD, hb * dv
    in_specs = [pl.BlockSpec((1, t_in, wk), lambda b, h: (b, 0, qc // wk + h)),
                pl.BlockSpec((1, t_in, wk), lambda b, h: (b, 0, kc // wk + h)),
                pl.BlockSpec((1, t_in, wv), lambda b, h: (b, 0, vc // wv + h)),
                pl.BlockSpec((1, t_in, wv), lambda b, h: (b, 0, oc // wv + h))]
    args = [z3, z3, z3, z3]
    if mode == "gla":
        wa, ba = extra
        in_specs += [pl.BlockSpec((1, t_in, HD), lambda b, h: (b, 0, 0)),
                     pl.BlockSpec((HD, wk), lambda b, h: (0, h)),
                     pl.BlockSpec((1, wk), lambda b, h: (0, h))]
        args += [zs3, wa, ba]
    else:
        in_specs.append(pl.BlockSpec((8, wk), lambda b, h: (0, h)))
        args.append(extra)
    in_specs.append(pl.BlockSpec((1, dv), lambda b, h: (0, 0)))
    args.append(gain)
    st_spec = pl.BlockSpec((1, hb, HD, dv), lambda b, h: (b, h, 0, 0))
    if has_state:
        in_specs.append(st_spec)
        args.append(s0)
    return pl.pallas_call(
        functools.partial(_gla_body, mode=mode, t_in=t_in, chunk=chunk, dv=dv, t_real=t_real, has_state=has_state,
                          hb=hb),
        grid=(bsz, n_heads // hb),
        in_specs=in_specs,
        out_specs=[pl.BlockSpec((1, t_in, wv), lambda b, h: (b, 0, h)), st_spec],
        out_shape=[jax.ShapeDtypeStruct((bsz, t_in, n_heads * dv), BF16),
                   jax.ShapeDtypeStruct((bsz, n_heads, HD, dv), F32)],
        scratch_shapes=[pltpu.VMEM((hb, dv, HD), F32)],
        compiler_params=_cparams(("parallel", "parallel")),
    )(*args)


def _split_cols(w, sizes):
    outs, off = [], 0
    for s in sizes:
        outs.append(w[:, off:off + s])
        off += s
    return outs


def _pad_cols(w, n):
    return jnp.pad(w, ((0, 0), (0, n - w.shape[1])))


def _row(v):
    return v.reshape(1, -1).astype(F32)


def _even_weights(w_in, fq_gain, fk_gain, nq_gain, nk_gain):
    fq, fk, fv, ff, nq, kc, vc, ks, vs, kw, vw, ng = _split_cols(w_in, EVEN_SIZES)
    big = jnp.concatenate([fq, fk, fv, nq, kc, vc, ks, vs, kw, vw], axis=1).astype(BF16)
    small = _pad_cols(jnp.concatenate([ff, ng], axis=1), HD).astype(BF16)
    ones = lambda n: jnp.ones((n,), F32)
    gain = jnp.concatenate([jnp.tile(fq_gain, H_FOX), jnp.tile(fk_gain, H_FOX), ones(H_FOX * HD),
                            jnp.tile(nq_gain, H_NSA), ones(2 * G_NSA * HD), jnp.tile(nk_gain[1], G_NSA),
                            ones(G_NSA * HD), jnp.tile(nk_gain[2], G_NSA), ones(G_NSA * HD)])
    z, o = jnp.zeros, jnp.ones
    flag = jnp.concatenate([o((2 * H_FOX * HD,), F32), z((H_FOX * HD,), F32), o((H_NSA * HD,), F32),
                            z((2 * G_NSA * HD,), F32), o((G_NSA * HD,), F32), z((G_NSA * HD,), F32),
                            o((G_NSA * HD,), F32), z((G_NSA * HD,), F32)])
    return big, small, _row(gain), _row(flag)


def _odd_weights(w_in):
    gq, gk, gv, ga, gg, hq, hf, hi, hg = _split_cols(w_in, ODD_SIZES)
    big = jnp.concatenate([gq, gk, gv, gg, hq, hf, hi, hg], axis=1).astype(BF16)
    small = _pad_cols(ga, HD).astype(BF16)
    gain = jnp.concatenate([jnp.full((H_GLA * DK_GLA,), DK_GLA ** -0.5, F32), jnp.ones((N_BIG_ODD - H_GLA * DK_GLA,), F32)])
    return big, small, _row(gain), jnp.zeros((1, N_BIG_ODD), F32)


def _cmp_weights(pe, w):
    half = CMP_STRIDE * HD
    wcat = jnp.concatenate([w[:half], w[half:]], axis=1).astype(BF16)
    pe8 = jnp.pad(pe.reshape(2, half), ((0, 6), (0, 0)))
    return pe8, wcat


def _ffn(x, g, w1, w3, w2):
    return _mm_res([_ffn_up(x, g, w1, w3)], w2, x)


def _even_layer(x, bsz, t, t_real, ew, past):
    (w_big, w_small, cgain, cflag, g_mix, w_out, b_f_pad, gb_pad, pe_k, wc_k, pe_v, wc_v, nk0) = ew
    z, zs = _proj(x, g_mix, w_big, w_small, cgain, cflag)
    z3 = z.reshape(bsz, t, N_BIG_EVEN)
    zs3 = zs.reshape(bsz, t, HD)
    one = jnp.ones((1, HD), F32)
    if past is None:
        lf, c = _fox_prep(zs3, b_f_pad)
        logf_new = lf[:, :, :H_FOX]
        tk = min(t, 256)
        bias = jnp.swapaxes(c[:, :, :H_FOX], 1, 2).reshape(bsz * H_FOX, t // tk, tk)
        o_fox = _flash("fox", z3, FQ, z3, FK, z3, FV, n_kv=H_FOX, hpg=1, tq=min(t, 128), tk=tk, bias=bias,
                       out_dtype=BF16)
        n_cmp = t // CMP_STRIDE
        mn = -(-n_cmp // HD) * HD
        kcmp = _compress(z3, KC, pe_k, wc_k, nk0, mn=mn, norm=True)
        vcmp = _compress(z3, VC, pe_v, wc_v, one, mn=mn, norm=False)
        tqn = min(t, 128 // HPG)
        o_cmp, sel = _cmpsel(z3, kcmp, vcmp, tq=min(t, 128), pos0=0, n_sel=-(-t // SEL_BLOCK))
        o_slc = _flash("slc", z3, NQ, z3, KS, z3, VS, n_kv=G_NSA, hpg=HPG, tq=tqn, tk=tk, sel=sel)
        o_win = _flash("win", z3, NQ, z3, KW, z3, VW, n_kv=G_NSA, hpg=HPG, tq=tqn, tk=tk)
        w_buf = WINDOW
        kw_new = jnp.concatenate([jnp.zeros((bsz, w_buf, G_NSA * HD), F32), z3[:, :, KW:KW + G_NSA * HD]], axis=1)[:, -w_buf:]
        vw_new = jnp.concatenate([jnp.zeros((bsz, w_buf, G_NSA * HD), F32), z3[:, :, VW:VW + G_NSA * HD]], axis=1)[:, -w_buf:]
        o_fox = o_fox.reshape(bsz * t, H_FOX * HD)
    else:
        (page_table, pool_off, fk_pool, fv_pool, lf_pool, kc_pool, vc_pool, ks_pool, vs_pool, buf_k, buf_v) = past
        p_len = page_table.shape[1] * PAGE
        lf_all, _ = _fox_prep(zs.reshape(1, bsz * t, HD), b_f_pad)
        lf_new = lf_all.reshape(bsz, t, HD)[:, :t_real, :H_FOX]
        logf_new = lf_new
        lfn_pad = _pad_cols(lf_new.reshape(bsz, t_real * H_FOX), HD).reshape(bsz, 1, HD)
        c_local, c_total = _pool_prefix(lf_pool, H_FOX)
        rows_of = lambda c0, n: z3[:, :t_real, c0:c0 + n * HD].reshape(bsz, t_real * n, HD)
        o_fox = _paged_attn("fox", rows_of(FQ, H_FOX), page_table, fk_pool, fv_pool, pool_off,
                            rows_of(FK, H_FOX), rows_of(FV, H_FOX), hpg_kv=1, pg=4,
                            c_local=c_local[:, None, :], c_total=c_total[:, None, :], lf_new=lfn_pad)
        o_fox = o_fox.reshape(bsz, t_real, H_FOX * HD)
        n_cmp = -(-(p_len + t_real) // CMP_STRIDE)
        mn = -(-n_cmp // HD) * HD
        kcmp = _compress(z3, KC, pe_k, wc_k, nk0, mn=mn, norm=True, page_table=page_table, pool=kc_pool,
                         pool_off=pool_off, t_new=t_real)
        vcmp = _compress(z3, VC, pe_v, wc_v, one, mn=mn, norm=False, page_table=page_table, pool=vc_pool,
                         pool_off=pool_off, t_new=t_real)
        n_sel = -(-(p_len + t_real) // SEL_BLOCK)
        o_cmp, sel = _cmpsel(z3, kcmp, vcmp, tq=t, pos0=p_len, n_sel=n_sel)
        nsp = sel.shape[-1]
        sel_rows = jnp.broadcast_to(jnp.swapaxes(sel[:, :, :t_real], 1, 2)[:, :, :, None, :],
                                    (bsz, t_real, G_NSA, HPG, nsp)).reshape(bsz, 32, nsp)
        o_slc = _paged_attn("slc", rows_of(NQ, H_NSA), page_table, ks_pool, vs_pool, pool_off,
                            rows_of(KS, G_NSA), rows_of(VS, G_NSA), hpg_kv=HPG, pg=8,
                            sel_rows=sel_rows).reshape(bsz, t_real, H_NSA * HD)
        kw_all = jnp.concatenate([buf_k, z3[:, :t_real, KW:KW + G_NSA * HD]], axis=1)
        vw_all = jnp.concatenate([buf_v, z3[:, :t_real, VW:VW + G_NSA * HD]], axis=1)
        w_buf = buf_k.shape[1]
        kw_new, vw_new = kw_all[:, -w_buf:], vw_all[:, -w_buf:]
        lk = -(-(w_buf + t) // HD) * HD
        padk = lambda a: jnp.pad(a, ((0, 0), (0, lk - a.shape[1]), (0, 0)))
        o_win = _flash("win", z3, NQ, padk(kw_all), 0, padk(vw_all), 0, n_kv=G_NSA, hpg=HPG, tq=t, tk=HD, off=w_buf)
        padt = lambda a: jnp.pad(a, ((0, 0), (0, t - t_real), (0, 0)))
        o_fox = padt(o_fox).astype(BF16).reshape(bsz * t, H_FOX * HD)
        o_slc = padt(o_slc)
    m = bsz * t
    o_nsa = _nsa_combine(o_cmp.reshape(m, -1), o_slc.reshape(m, -1), o_win.reshape(m, -1), zs, gb_pad)
    x = _mm_res([o_fox, o_nsa], w_out, x)
    zr = z3[:, :t_real]
    grp = lambda c0: zr[:, :, c0:c0 + G_NSA * HD].reshape(1, bsz, t_real, G_NSA, HD)
    fox = lambda c0: zr[:, :, c0:c0 + H_FOX * HD].reshape(1, bsz, t_real, H_FOX, HD)
    w_rows = kw_new.shape[1]
    outs = (fox(FK), fox(FV), logf_new[None], grp(KC), grp(VC), grp(KS), grp(VS),
            kw_new.reshape(1, bsz, w_rows, G_NSA, HD), vw_new.reshape(1, bsz, w_rows, G_NSA, HD))
    return x, outs


def _odd_layer(x, bsz, t, t_real, ow, state):
    (w_big, w_small, cgain, cflag, g_mix, w_out, wa_pad, ba, gla_gain, lb_pad, hg_gain) = ow
    z, zs = _proj(x, g_mix, w_big, w_small, cgain, cflag)
    z3 = z.reshape(bsz, t, N_BIG_ODD)
    zs3 = zs.reshape(bsz, t, HD)
    s_gla, s_hg = state if state is not None else (None, None)
    chunk = 64 if t >= 64 else 128
    o_gla, s_gla = _gla("gla", z3, zs3, (GQ, GK, GV, GG), H_GLA, DV_GLA, (wa_pad, ba), gla_gain, s_gla,
                        t_real=t_real, chunk=chunk, hb=2)
    o_hg, s_hg = _gla("hg", z3, zs3, (HQ, HF, HI, HO), H_HG, DV_HG, lb_pad, hg_gain, s_hg,
                      t_real=t_real, chunk=chunk, hb=4)
    m = bsz * t
    x = _mm_res([o_gla.reshape(m, -1), o_hg.reshape(m, -1)], w_out, x)
    return x, (s_gla[None], s_hg[None])


def kernel(x_prompt, x_sample, cache_fox_k, cache_fox_v, cache_fox_logf, cache_nsa_kc, cache_nsa_vc, cache_nsa_ks, cache_nsa_vs, state_nsa_kw, state_nsa_vw, state_gla, state_hgrn, page_table, norm_mix, norm_ffn, w_in_even, w_out_even, fox_b_f, fox_q_gain, fox_k_gain, nsa_q_gain, nsa_k_gain, nsa_cmp_pe_k, nsa_cmp_pe_v, nsa_cmp_wk, nsa_cmp_wv, nsa_gate_b, w_in_odd, w_out_odd, gla_wa2, gla_ba, gla_norm, hgrn_lb, hgrn_norm, ffn_w1, ffn_w3, ffn_w2):
    bp, tp, d = x_prompt.shape
    bs, ts, _ = x_sample.shape
    n_pool = cache_fox_k.shape[1]
    ts_pad = 8
    pe_k, wc_k = _cmp_weights(nsa_cmp_pe_k[0], nsa_cmp_wk[0])
    pe_v, wc_v = _cmp_weights(nsa_cmp_pe_v[0], nsa_cmp_wv[0])
    b_f_pad = _pad_cols(_row(fox_b_f[0]), HD)
    gb_pad = _pad_cols(jnp.concatenate([jnp.zeros((1, H_FOX), F32), _row(nsa_gate_b[0])], axis=1), HD)
    ew = _even_weights(w_in_even[0], fox_q_gain[0], fox_k_gain[0], nsa_q_gain[0], nsa_k_gain[0]) + (
        _row(norm_mix[0]), w_out_even[0].astype(BF16), b_f_pad, gb_pad, pe_k, wc_k, pe_v, wc_v, _row(nsa_k_gain[0, 0]))
    wa_pad = jnp.pad(gla_wa2[0], ((0, HD - GLA_RANK), (0, 0))).astype(BF16)
    lb_pad = jnp.pad(hgrn_lb.astype(F32), ((0, 8 - hgrn_lb.shape[0]), (0, 0)), constant_values=-1e30)
    ow = _odd_weights(w_in_odd[0]) + (_row(norm_mix[1]), w_out_odd[0].astype(BF16), wa_pad, _row(gla_ba[0]),
                                      _row(gla_norm[0]), lb_pad, _row(hgrn_norm[0]))
    ffn = [(_row(norm_ffn[i]), ffn_w1[i].astype(BF16), ffn_w3[i].astype(BF16), ffn_w2[i].astype(BF16)) for i in range(2)]

    flat = lambda c: c.reshape((c.shape[0] * c.shape[1], PAGE * c.shape[3], HD))
    lf_pool = cache_fox_logf.reshape(-1, PAGE * H_FOX)
    win = lambda s: s.reshape(bs, s.shape[2], G_NSA * HD)
    past = (page_table, 0, flat(cache_fox_k), flat(cache_fox_v), lf_pool, flat(cache_nsa_kc), flat(cache_nsa_vc),
            flat(cache_nsa_ks), flat(cache_nsa_vs), win(state_nsa_kw), win(state_nsa_vw))

    def run(x, bsz, t, t_real, past_, state_):
        x, ev = _even_layer(x, bsz, t, t_real, ew, past_)
        x = _ffn(x, *ffn[0])
        x, od = _odd_layer(x, bsz, t, t_real, ow, state_)
        x = _ffn(x, *ffn[1])
        return x, ev, od

    yp, ev_p, od_p = run(x_prompt.reshape(bp * tp, d), bp, tp, tp, None, None)
    xs = jnp.pad(x_sample, ((0, 0), (0, ts_pad - ts), (0, 0))).reshape(bs * ts_pad, d)
    ys, ev_s, od_s = run(xs, bs, ts_pad, ts, past, (state_gla.reshape(state_gla.shape[1:]),
                                                    state_hgrn.reshape(state_hgrn.shape[1:])))
    y_prompt = yp.reshape(bp, tp, d)
    y_sample = ys.reshape(bs, ts_pad, d)[:, :ts]
    outs = [y_prompt, y_sample]
    for a, b in zip(ev_p, ev_s):
        outs += [a, b]
    for a, b in zip(od_p, od_s):
        outs += [a, b]
    return tuple(outs)
```

```python
import functools

import jax
import jax.numpy as jnp
from jax import lax
from jax.experimental import pallas as pl
from jax.experimental.pallas import tpu as pltpu

F32 = jnp.float32
BF16 = jnp.bfloat16

D_MODEL = 2048
HD = 128
H_FOX = 8
H_NSA = 8
G_NSA = 2
HPG = H_NSA // G_NSA
CMP_STRIDE = 16
CMP_BLOCK = 2 * CMP_STRIDE
SEL_BLOCK = 64
N_SELECT = 16
WINDOW = 512
FORCE_SCORE = 1.0e4
H_GLA = 4
DK_GLA = 128
DV_GLA = 256
GLA_RANK = 16
GLA_GATE_NORM = 16.0
H_HG = 8
DK_HG = 128
DV_HG = 128
PAGE = 128
EVEN_SIZES = (H_FOX * HD, H_FOX * HD, H_FOX * HD, H_FOX, H_NSA * HD,
              G_NSA * HD, G_NSA * HD, G_NSA * HD, G_NSA * HD, G_NSA * HD, G_NSA * HD, 3 * H_NSA)
ODD_SIZES = (H_GLA * DK_GLA, H_GLA * DK_GLA, H_GLA * DV_GLA, GLA_RANK, H_GLA * DV_GLA,
             H_HG * DK_HG, H_HG * DK_HG, H_HG * DV_HG, H_HG * DV_HG)
FQ, FK, FV, NQ, KC, VC, KS, VS, KW, VW = 0, 1024, 2048, 3072, 4096, 4352, 4608, 4864, 5120, 5376
N_BIG_EVEN = 5632
GQ, GK, GV, GG, HQ, HF, HI, HO = 0, 512, 1024, 2048, 3072, 4096, 5120, 6144
N_BIG_ODD = 7168
ATTN_SCALE = HD ** -0.5
NEG = -1.0e30
EPS = 1.0e-6
EXP_CLAMP = 80.0
VMEM_LIMIT = 56 * 1024 * 1024


def _cparams(sem):
    return pltpu.CompilerParams(dimension_semantics=sem, vmem_limit_bytes=VMEM_LIMIT)


def _dot(a, b):
    return jnp.dot(a, b, preferred_element_type=F32)


def _dot_nt(a, b):
    return lax.dot_general(a, b, (((1,), (1,)), ((), ())), preferred_element_type=F32)


def _dot_tn(a, b):
    return lax.dot_general(a, b, (((0,), (0,)), ((), ())), preferred_element_type=F32)


def _split3(x):
    hi = x.astype(BF16)
    r = x - hi.astype(F32)
    mid = r.astype(BF16)
    lo = (r - mid.astype(F32)).astype(BF16)
    return hi, mid, lo


def _dot3_l(x, w):
    hi, mid, lo = _split3(x)
    return _dot(hi, w) + _dot(mid, w) + _dot(lo, w)


def _dot3_r(w, x):
    hi, mid, lo = _split3(x)
    return _dot(w, hi) + _dot(w, mid) + _dot(w, lo)


def _sigmoid(x):
    return 1.0 / (1.0 + jnp.exp(-x))


def _log_sigmoid(x):
    return jnp.minimum(x, 0.0) - jnp.log1p(jnp.exp(-jnp.abs(x)))


def _tri(n, upper):
    r = lax.broadcasted_iota(jnp.int32, (n, n), 0)
    c = lax.broadcasted_iota(jnp.int32, (n, n), 1)
    return jnp.where((r <= c) if upper else (r >= c), 1.0, 0.0).astype(BF16)


def _proj_body(x_ref, g_ref, w_ref, ws_ref, cg_ref, cf_ref, z_ref, zs_ref, h_scr, *, tn):
    @pl.when(pl.program_id(1) == 0)
    def _():
        x = x_ref[...]
        r = lax.rsqrt(jnp.mean(x * x, axis=-1, keepdims=True) + EPS)
        h = (x * r * g_ref[...]).astype(BF16)
        h_scr[...] = h
        zs_ref[...] = _dot(h, ws_ref[...])

    z = _dot(h_scr[...], w_ref[...])
    for c in range(tn // HD):
        sl = slice(c * HD, (c + 1) * HD)
        zc = z[:, sl]
        r = lax.rsqrt(jnp.mean(zc * zc, axis=-1, keepdims=True) + EPS)
        f = cf_ref[:, sl]
        z_ref[:, sl] = zc * (f * r + (1.0 - f)) * cg_ref[:, sl]


def _proj(x, g, w_big, w_small, col_gain, col_flag):
    m, d = x.shape
    n = w_big.shape[1]
    tm = min(m, 1024)
    tn = 512
    return pl.pallas_call(
        functools.partial(_proj_body, tn=tn),
        grid=(m // tm, n // tn),
        in_specs=[pl.BlockSpec((tm, d), lambda i, j: (i, 0)),
                  pl.BlockSpec((1, d), lambda i, j: (0, 0)),
                  pl.BlockSpec((d, tn), lambda i, j: (0, j)),
                  pl.BlockSpec((d, HD), lambda i, j: (0, 0)),
                  pl.BlockSpec((1, tn), lambda i, j: (0, j)),
                  pl.BlockSpec((1, tn), lambda i, j: (0, j))],
        out_specs=[pl.BlockSpec((tm, tn), lambda i, j: (i, j)),
                   pl.BlockSpec((tm, HD), lambda i, j: (i, 0))],
        out_shape=[jax.ShapeDtypeStruct((m, n), F32), jax.ShapeDtypeStruct((m, HD), F32)],
        scratch_shapes=[pltpu.VMEM((tm, d), BF16)],
        compiler_params=_cparams(("parallel", "arbitrary")),
    )(x, g, w_big, w_small, col_gain, col_flag)


def _ffn_up_body(x_ref, g_ref, w1_ref, w3_ref, o_ref, h_scr):
    @pl.when(pl.program_id(1) == 0)
    def _():
        x = x_ref[...]
        r = lax.rsqrt(jnp.mean(x * x, axis=-1, keepdims=True) + EPS)
        h_scr[...] = (x * r * g_ref[...]).astype(BF16)

    h = h_scr[...]
    a = _dot(h, w1_ref[...])
    b = _dot(h, w3_ref[...])
    o_ref[...] = (a * _sigmoid(a) * b).astype(BF16)


def _ffn_up(x, g, w1, w3):
    m, d = x.shape
    n = w1.shape[1]
    tm = min(m, 1024)
    tn = 512
    return pl.pallas_call(
        _ffn_up_body,
        grid=(m // tm, n // tn),
        in_specs=[pl.BlockSpec((tm, d), lambda i, j: (i, 0)),
                  pl.BlockSpec((1, d), lambda i, j: (0, 0)),
                  pl.BlockSpec((d, tn), lambda i, j: (0, j)),
                  pl.BlockSpec((d, tn), lambda i, j: (0, j))],
        out_specs=pl.BlockSpec((tm, tn), lambda i, j: (i, j)),
        out_shape=jax.ShapeDtypeStruct((m, n), BF16),
        scratch_shapes=[pltpu.VMEM((tm, d), BF16)],
        compiler_params=_cparams(("parallel", "arbitrary")),
    )(x, g, w1, w3)


def _mm_res_body(*refs, n_in):
    res_ref = refs[2 * n_in]
    o_ref = refs[2 * n_in + 1]
    acc = res_ref[...]
    for a_ref, w_ref in zip(refs[:n_in], refs[n_in:2 * n_in]):
        acc = acc + _dot(a_ref[...], w_ref[...])
    o_ref[...] = acc


def _mm_res(a_list, w, res):
    m, n = res.shape
    tm = min(m, 512)
    tn = 512
    in_specs, w_specs, off = [], [], 0
    for a in a_list:
        k = a.shape[1]
        in_specs.append(pl.BlockSpec((tm, k), lambda i, j: (i, 0)))
        w_specs.append(pl.BlockSpec((k, tn), lambda i, j, _o=off // k: (_o, j)))
        off += k
    return pl.pallas_call(
        functools.partial(_mm_res_body, n_in=len(a_list)),
        grid=(m // tm, n // tn),
        in_specs=in_specs + w_specs + [pl.BlockSpec((tm, tn), lambda i, j: (i, j))],
        out_specs=pl.BlockSpec((tm, tn), lambda i, j: (i, j)),
        out_shape=jax.ShapeDtypeStruct((m, n), F32),
        compiler_params=_cparams(("parallel", "parallel")),
    )(*a_list, *([w] * len(a_list)), res)


def _fox_prep_body(zs_ref, b_ref, lf_ref, c_ref, carry, *, tc):
    @pl.when(pl.program_id(1) == 0)
    def _():
        carry[...] = jnp.zeros_like(carry)

    lf = _log_sigmoid(zs_ref[0] + b_ref[...])
    c = _dot3_r(_tri(tc, upper=False), lf) + carry[...]
    lf_ref[0] = lf
    c_ref[0] = c
    carry[...] = c[tc - 1:tc, :]


def _fox_prep(zs3, b_pad):
    bsz, t, _ = zs3.shape
    tc = min(t, 256)
    spec = pl.BlockSpec((1, tc, HD), lambda b, i: (b, i, 0))
    return pl.pallas_call(
        functools.partial(_fox_prep_body, tc=tc),
        grid=(bsz, t // tc),
        in_specs=[spec, pl.BlockSpec((1, HD), lambda b, i: (0, 0))],
        out_specs=[spec, spec],
        out_shape=[jax.ShapeDtypeStruct(zs3.shape, F32)] * 2,
        scratch_shapes=[pltpu.VMEM((1, HD), F32)],
        compiler_params=_cparams(("parallel", "arbitrary")),
    )(zs3, b_pad)


def _flash_body(*refs, mode, tq, tk, hpg, off):
    it = iter(refs)
    q_ref, k_ref, v_ref = next(it), next(it), next(it)
    bias_ref = next(it) if mode == "fox" else None
    sel_ref = next(it) if mode == "slc" else None
    o_ref, kb, vb = next(it), next(it), next(it)
    i = pl.program_id(2)

    @pl.when(i == 0)
    def _():
        kb[...] = k_ref[0].astype(BF16)
        vb[...] = v_ref[0].astype(BF16)

    q = q_ref[0]
    if hpg > 1:
        q = jnp.concatenate([q[:, h * HD:(h + 1) * HD] for h in range(hpg)], axis=0)
    qa = (q * ATTN_SCALE).astype(BF16)
    tile_rows = (lambda x: jnp.concatenate([x] * hpg, axis=0)) if hpg > 1 else (lambda x: x)
    q_lo = off + i * tq
    t_k = kb.shape[0]
    if mode == "slc":
        unpicked = (sel_ref[0, 0] - 1.0).astype(BF16)

    def attend(blocks):
        ss = []
        for b, (ks, size, masked) in enumerate(blocks):
            s = _dot_nt(qa, kb[pl.ds(ks, size), :])
            if mode == "fox":
                s = s - bias_ref[0, b:b + 1, :]
            if mode == "slc":
                cb = lax.broadcasted_iota(jnp.int32, (HD, size), 0)
                kk = lax.broadcasted_iota(jnp.int32, (HD, size), 1)
                expand = jnp.where(cb == (ks + kk) // SEL_BLOCK, 2.0 ** 100, 0.0).astype(BF16)
                s = s + tile_rows(_dot(unpicked, expand))
            if masked:
                kidx = ks + lax.broadcasted_iota(jnp.int32, (tq, size), 1)
                qpos = q_lo + lax.broadcasted_iota(jnp.int32, (tq, size), 0)
                valid = kidx <= qpos
                if mode == "win":
                    valid = valid & ((qpos - kidx) < WINDOW)
                s = jnp.where(tile_rows(valid), s, NEG)
            ss.append(s)
        m = ss[0].max(axis=-1, keepdims=True)
        for s in ss[1:]:
            m = jnp.maximum(m, s.max(axis=-1, keepdims=True))
        l, acc = 0.0, 0.0
        for s, (ks, size, _) in zip(ss, blocks):
            p = jnp.exp(s - m)
            l = l + jnp.sum(p, axis=-1, keepdims=True)
            acc = acc + _dot(p.astype(BF16), vb[pl.ds(ks, size), :])
        o = acc / l
        for h in range(hpg):
            o_ref[0, :, h * HD:(h + 1) * HD] = o[h * tq:(h + 1) * tq].astype(o_ref.dtype)

    if mode == "win":
        span = min(t_k, -(-(WINDOW + tq) // HD) * HD + HD)
        start = jnp.clip((q_lo - (WINDOW - 1)) // HD * HD, 0, t_k - span)
        attend([(pl.multiple_of(start, HD), span, True)])
    else:
        n_max = t_k // tk
        need = (q_lo + tq - 1) // tk + 1
        for nb in range(1, n_max + 1):
            @pl.when(need == nb)
            def _(nb=nb):
                attend([(b * tk, tk, b == nb - 1) for b in range(nb)])


def _flash(mode, q_arr, q_col, k_arr, k_col, v_arr, v_col, *, n_kv, hpg, tq, tk, off=0,
           bias=None, sel=None, out_dtype=F32):
    bsz, t_q = q_arr.shape[0], q_arr.shape[1]
    t_k = k_arr.shape[1]
    nq = t_q // tq
    w = hpg * HD
    in_specs = [pl.BlockSpec((1, tq, w), lambda b, g, i: (b, i, q_col // w + g)),
                pl.BlockSpec((1, t_k, HD), lambda b, g, i: (b, 0, k_col // HD + g)),
                pl.BlockSpec((1, t_k, HD), lambda b, g, i: (b, 0, v_col // HD + g))]
    args = [q_arr, k_arr, v_arr]
    if mode == "fox":
        in_specs.append(pl.BlockSpec((1, t_k // tk, tk), lambda b, g, i: (b * n_kv + g, 0, 0)))
        args.append(bias)
    if mode == "slc":
        in_specs.append(pl.BlockSpec((1, 1, tq, HD), lambda b, g, i: (b, g, i, 0)))
        args.append(sel)
    return pl.pallas_call(
        functools.partial(_flash_body, mode=mode, tq=tq, tk=tk, hpg=hpg, off=off),
        grid=(bsz, n_kv, nq),
        in_specs=in_specs,
        out_specs=pl.BlockSpec((1, tq, w), lambda b, g, i: (b, i, g)),
        out_shape=jax.ShapeDtypeStruct((bsz, t_q, n_kv * w), out_dtype),
        scratch_shapes=[pltpu.VMEM((t_k, HD), BF16), pltpu.VMEM((t_k, HD), BF16)],
        compiler_params=_cparams(("parallel", "parallel", "arbitrary")),
    )(*args)


def _compress_body(*refs, paged, pg, n_steps, mn, norm, t_new):
    it = iter(refs)
    if paged:
        next(it)
    x_refs = [next(it) for _ in range(pg if paged else G_NSA)]
    new_ref = next(it) if paged else None
    pe_ref, w_ref, gain_ref, o_ref, hcat, tb = next(it), next(it), next(it), next(it), next(it), next(it)
    p = pl.program_id(1)
    mh = mn + 8
    per_page = PAGE // CMP_STRIDE
    n_real = n_steps * pg * per_page

    @pl.when(p == 0)
    def _():
        hcat[:, n_real:, :] = jnp.zeros((G_NSA, mh - n_real, CMP_STRIDE * HD), F32)

    for pp in range(pg):
        r0 = pl.multiple_of((p * pg + pp) * per_page, per_page)
        for l in range(CMP_STRIDE):
            for g in range(G_NSA):
                if paged:
                    piece = x_refs[pp][0, pl.ds(G_NSA * l + g, per_page, stride=G_NSA * CMP_STRIDE), :]
                else:
                    piece = x_refs[g][0, pl.ds(l, per_page, stride=CMP_STRIDE), :]
                hcat[g, pl.ds(r0, per_page), l * HD:(l + 1) * HD] = piece

    @pl.when(p == n_steps - 1)
    def _():
        if paged:
            xn = new_ref[0]
            rid = lax.broadcasted_iota(jnp.int32, xn.shape, 0)
            xn = jnp.where(rid < t_new, xn, 0.0)
            for g in range(G_NSA):
                for l in range(8):
                    hcat[g, n_real:n_real + 1, l * HD:(l + 1) * HD] = xn[l:l + 1, g * HD:(g + 1) * HD]
        w = w_ref[...]
        pe2 = _dot(pe_ref[...].astype(BF16), w)
        pe_bias = pe2[0:1, 0:HD] + pe2[1:2, HD:2 * HD]
        for g in range(G_NSA):
            tb[...] = _dot(hcat[g].astype(BF16), w)
            y = tb[0:mn, 0:HD] + tb[1:mn + 1, HD:2 * HD] + pe_bias
            if norm:
                y = y * lax.rsqrt(jnp.mean(y * y, axis=-1, keepdims=True) + EPS) * gain_ref[...]
            o_ref[0, :, g * HD:(g + 1) * HD] = y


def _compress(x_arr, x_col, pe8, wcat, gain, *, mn, norm, page_table=None, pool=None, pool_off=0, t_new=0, pg=8):
    paged = page_table is not None
    bsz = x_arr.shape[0]
    mh = mn + 8
    w2 = G_NSA * HD
    const = lambda shape: pl.BlockSpec(shape, (lambda b, p, *_: (0,) * len(shape)))
    tail = [const((8, CMP_STRIDE * HD)), const((CMP_STRIDE * HD, 2 * HD)), const((1, HD))]
    out_spec = pl.BlockSpec((1, mn, w2), lambda b, p, *_: (b, 0, 0))
    scratch = [pltpu.VMEM((G_NSA, mh, CMP_STRIDE * HD), F32), pltpu.VMEM((mh, 2 * HD), F32)]
    out_shape = jax.ShapeDtypeStruct((bsz, mn, w2), F32)
    if not paged:
        n_steps = x_arr.shape[1] // PAGE
        body = functools.partial(_compress_body, paged=False, pg=1, n_steps=n_steps, mn=mn, norm=norm, t_new=0)
        return pl.pallas_call(
            body, grid=(bsz, n_steps),
            in_specs=[pl.BlockSpec((1, PAGE, HD), lambda b, p, _g=g: (b, p, x_col // HD + _g)) for g in range(G_NSA)] + tail,
            out_specs=out_spec, out_shape=out_shape, scratch_shapes=scratch,
            compiler_params=_cparams(("parallel", "arbitrary")),
        )(*([x_arr] * G_NSA), pe8, wcat, gain)
    pg = min(pg, page_table.shape[1])
    n_steps = page_table.shape[1] // pg
    body = functools.partial(_compress_body, paged=True, pg=pg, n_steps=n_steps, mn=mn, norm=norm, t_new=t_new)
    grid_spec = pltpu.PrefetchScalarGridSpec(
        num_scalar_prefetch=1, grid=(bsz, n_steps),
        in_specs=[pl.BlockSpec((1, G_NSA * PAGE, HD), lambda b, p, pt, _pp=pp: (pool_off + pt[b, p * pg + _pp], 0, 0))
                  for pp in range(pg)]
        + [pl.BlockSpec((1, 8, w2), lambda b, p, pt: (b, 0, x_col // w2))] + tail,
        out_specs=out_spec, scratch_shapes=scratch)
    return pl.pallas_call(body, grid_spec=grid_spec, out_shape=out_shape,
                          compiler_params=_cparams(("parallel", "arbitrary")))(
        page_table, *([pool] * pg), x_arr, pe8, wcat, gain)


def _cmpsel_body(q_ref, kc_ref, vc_ref, o_ref, sel_ref, *, tq, ncp, nsp, n_sel, pos0):
    i = pl.program_id(2)
    q = q_ref[0]
    kc = kc_ref[0].astype(BF16)
    vc = vc_ref[0].astype(BF16)
    pos = pos0 + i * tq + lax.broadcasted_iota(jnp.int32, (tq, ncp), 0)
    col = lax.broadcasted_iota(jnp.int32, (tq, ncp), 1)
    valid = (col * CMP_STRIDE + (CMP_BLOCK - 1)) <= pos
    validf = jnp.where(valid, 1.0, 0.0)
    imp = jnp.zeros((tq, ncp), F32)
    for h in range(HPG):
        qh = (q[:, h * HD:(h + 1) * HD] * ATTN_SCALE).astype(BF16)
        s = jnp.where(valid, _dot_nt(qh, kc), NEG)
        e = jnp.exp(s - jnp.max(s, axis=-1, keepdims=True)) * validf
        p = e / jnp.maximum(jnp.sum(e, axis=-1, keepdims=True), 1e-30)
        o_ref[0, :, h * HD:(h + 1) * HD] = _dot(p.astype(BF16), vc)
        imp = imp + p
    cc = lax.broadcasted_iota(jnp.int32, (ncp, nsp), 0)
    jj = lax.broadcasted_iota(jnp.int32, (ncp, nsp), 1)
    r = SEL_BLOCK // CMP_STRIDE
    gather = jnp.where((cc >= r * jj - 1) & (cc <= r * jj + r - 1), 1.0, 0.0).astype(BF16)
    score = _dot3_l(imp, gather)
    blk = lax.broadcasted_iota(jnp.int32, (tq, nsp), 1)
    pq = pos0 + i * tq + lax.broadcasted_iota(jnp.int32, (tq, nsp), 0)
    forced = (blk == pq // SEL_BLOCK) | (blk == 0)
    future = blk * SEL_BLOCK > pq
    score = jnp.where(forced, FORCE_SCORE, jnp.where(future, -1.0, score))
    score = jnp.where(blk < n_sel, score, -2.0)
    rank = jnp.zeros((tq, nsp), F32)
    for c in range(n_sel):
        sc = score[:, c:c + 1]
        ahead = (sc > score) | ((sc == score) & (blk > c))
        rank = rank + jnp.where(ahead, 1.0, 0.0)
    sel_ref[0, 0] = jnp.where(rank < float(N_SELECT), 1.0, 0.0)


def _cmpsel(z3, kcmp, vcmp, *, tq, pos0, n_sel):
    bsz, t, _ = z3.shape
    ncp = kcmp.shape[1]
    nsp = -(-n_sel // HD) * HD
    w = HPG * HD
    return pl.pallas_call(
        functools.partial(_cmpsel_body, tq=tq, ncp=ncp, nsp=nsp, n_sel=n_sel, pos0=pos0),
        grid=(bsz, G_NSA, t // tq),
        in_specs=[pl.BlockSpec((1, tq, w), lambda b, g, i: (b, i, NQ // w + g)),
                  pl.BlockSpec((1, ncp, HD), lambda b, g, i: (b, 0, g)),
                  pl.BlockSpec((1, ncp, HD), lambda b, g, i: (b, 0, g))],
        out_specs=[pl.BlockSpec((1, tq, w), lambda b, g, i: (b, i, g)),
                   pl.BlockSpec((1, 1, tq, nsp), lambda b, g, i: (b, g, i, 0))],
        out_shape=[jax.ShapeDtypeStruct((bsz, t, G_NSA * w), F32),
                   jax.ShapeDtypeStruct((bsz, G_NSA, t, nsp), F32)],
        compiler_params=_cparams(("parallel", "parallel", "parallel")),
    )(z3, kcmp, vcmp)


def _combine_body(oc_ref, os_ref, ow_ref, zs_ref, gb_ref, o_ref):
    gates = _sigmoid(zs_ref[...] + gb_ref[...])
    for h in range(H_NSA):
        sl = slice(h * HD, (h + 1) * HD)
        c0 = H_FOX + 3 * h
        o = (gates[:, c0:c0 + 1] * oc_ref[:, sl] + gates[:, c0 + 1:c0 + 2] * os_ref[:, sl]
             + gates[:, c0 + 2:c0 + 3] * ow_ref[:, sl])
        o_ref[:, sl] = o.astype(BF16)


def _nsa_combine(oc, osl, ow, zs, gb_pad):
    m, n = oc.shape
    tm = min(m, 512)
    big = pl.BlockSpec((tm, n), lambda i: (i, 0))
    return pl.pallas_call(
        _combine_body, grid=(m // tm,),
        in_specs=[big, big, big, pl.BlockSpec((tm, HD), lambda i: (i, 0)), pl.BlockSpec((1, HD), lambda i: (0, 0))],
        out_specs=big, out_shape=jax.ShapeDtypeStruct((m, n), BF16),
        compiler_params=_cparams(("parallel",)),
    )(oc, osl, ow, zs, gb_pad)


def _paged_body(*refs, mode, pg, n_steps, n_w, hpg_kv, t_new):
    it = iter(refs)
    next(it)
    q_ref = next(it)
    k_refs = [next(it) for _ in range(pg)]
    v_refs = [next(it) for _ in range(pg)]
    cl_refs = [next(it) for _ in range(pg)] if mode == "fox" else None
    tf_refs = [next(it) for _ in range(pg)] if mode == "fox" else None
    kn_ref, vn_ref = next(it), next(it)
    lfn_ref = next(it) if mode == "fox" else None
    sel_ref = next(it) if mode == "slc" else None
    o_ref = next(it)
    qb, m_scr, l_scr, acc, kn_scr, vn_scr, carry = (next(it) for _ in range(7))
    p = pl.program_id(1)
    rows = 4 * 8
    r2 = n_w * PAGE
    n_new = t_new * n_w

    def kv_match(ncols):
        row = lax.broadcasted_iota(jnp.int32, (rows, ncols), 0)
        col = lax.broadcasted_iota(jnp.int32, (rows, ncols), 1)
        return row, col, (col % n_w) == (row % 8) // hpg_kv

    @pl.when(p == 0)
    def _():
        qb[...] = (q_ref[0] * ATTN_SCALE).astype(BF16)
        m_scr[...] = jnp.full(m_scr.shape, NEG, F32)
        l_scr[...] = jnp.zeros(l_scr.shape, F32)
        acc[...] = jnp.zeros(acc.shape, F32)
        carry[...] = jnp.zeros(carry.shape, F32)
        kn_scr[...] = jnp.zeros(kn_scr.shape, BF16)
        vn_scr[...] = jnp.zeros(vn_scr.shape, BF16)

    def picked(page_idx, ncols):
        nsp = sel_ref.shape[2]
        jb = lax.broadcasted_iota(jnp.int32, (nsp, ncols), 0)
        key = lax.broadcasted_iota(jnp.int32, (nsp, ncols), 1) // n_w
        expand = jnp.where(jb == page_idx * (PAGE // SEL_BLOCK) + key // SEL_BLOCK, 1.0, 0.0).astype(BF16)
        return _dot(sel_ref[0].astype(BF16), expand) > 0.5

    def update(s_list, vb_list):
        m_old = m_scr[...]
        m_new = m_old
        for s in s_list:
            m_new = jnp.maximum(m_new, jnp.max(s, axis=-1, keepdims=True))
        a = jnp.exp(m_old - m_new)
        l = a * l_scr[...]
        o = a * acc[...]
        for s, vb in zip(s_list, vb_list):
            pr = jnp.exp(s - m_new)
            l = l + jnp.sum(pr, axis=-1, keepdims=True)
            o = o + _dot(pr.astype(BF16), vb)
        m_scr[...] = m_new
        l_scr[...] = l
        acc[...] = o

    _, _, match = kv_match(r2)
    s_list, vb_list = [], []
    c_run = carry[...] if mode == "fox" else None
    for pp in range(pg):
        s = _dot_nt(qb[...], k_refs[pp][0].astype(BF16))
        valid = match
        if mode == "fox":
            s = s - (c_run + cl_refs[pp][0])
            c_run = c_run + tf_refs[pp][0]
        if mode == "slc":
            valid = valid & picked(p * pg + pp, r2)
        s_list.append(jnp.where(valid, s, NEG))
        vb_list.append(v_refs[pp][0].astype(BF16))
    if mode == "fox":
        carry[...] = c_run
    update(s_list, vb_list)

    @pl.when(p == n_steps - 1)
    def _():
        kn_scr[0:n_new, :] = kn_ref[0].astype(BF16)
        vn_scr[0:n_new, :] = vn_ref[0].astype(BF16)
        s = _dot_nt(qb[...], kn_scr[...])
        row, col, valid = kv_match(HD)
        valid = valid & (col // n_w <= row // 8) & (col < n_new)
        if mode == "fox":
            rr = lax.broadcasted_iota(jnp.int32, (HD, HD), 0)
            cc = lax.broadcasted_iota(jnp.int32, (HD, HD), 1)
            pre = jnp.where((rr % n_w == cc % n_w) & (rr // n_w <= cc // n_w), 1.0, 0.0).astype(BF16)
            c_new = _dot3_l(jnp.broadcast_to(lfn_ref[0], (8, HD)), pre)[0:1, :] + carry[:, 0:HD]
            s = s - c_new
        if mode == "slc":
            valid = valid & picked(n_steps * pg, HD)
        update([jnp.where(valid, s, NEG)], [vn_scr[...]])
        o_ref[0] = acc[...] / l_scr[...]


def _paged_attn(mode, q_rows, page_table, k_pool, v_pool, pool_off, k_new, v_new, *, hpg_kv, pg,
                c_local=None, c_total=None, lf_new=None, sel_rows=None):
    bsz = q_rows.shape[0]
    n_pages = page_table.shape[1]
    pg = min(pg, n_pages)
    n_steps = n_pages // pg
    r2 = k_pool.shape[1]
    n_w = r2 // PAGE
    n_new = k_new.shape[1]
    page = lambda pp: (lambda b, p, pt: (pool_off + pt[b, p * pg + pp], 0, 0))
    per_b = lambda b, p, pt: (b, 0, 0)
    in_specs = [pl.BlockSpec((1, 32, HD), per_b)]
    in_specs += [pl.BlockSpec((1, r2, HD), page(pp)) for pp in range(pg)] * 2
    args = [q_rows] + [k_pool] * pg + [v_pool] * pg
    if mode == "fox":
        in_specs += [pl.BlockSpec((1, 1, r2), page(pp)) for pp in range(pg)] * 2
        args += [c_local] * pg + [c_total] * pg
    in_specs += [pl.BlockSpec((1, n_new, HD), per_b)] * 2
    args += [k_new, v_new]
    if mode == "fox":
        in_specs.append(pl.BlockSpec((1, 1, HD), per_b))
        args.append(lf_new)
    if mode == "slc":
        in_specs.append(pl.BlockSpec((1, 32, sel_rows.shape[2]), per_b))
        args.append(sel_rows)
    grid_spec = pltpu.PrefetchScalarGridSpec(
        num_scalar_prefetch=1, grid=(bsz, n_steps), in_specs=in_specs,
        out_specs=pl.BlockSpec((1, 32, HD), per_b),
        scratch_shapes=[pltpu.VMEM((32, HD), BF16), pltpu.VMEM((32, 1), F32), pltpu.VMEM((32, 1), F32),
                        pltpu.VMEM((32, HD), F32), pltpu.VMEM((PAGE, HD), BF16), pltpu.VMEM((PAGE, HD), BF16),
                        pltpu.VMEM((1, r2), F32)])
    return pl.pallas_call(
        functools.partial(_paged_body, mode=mode, pg=pg, n_steps=n_steps, n_w=n_w, hpg_kv=hpg_kv, t_new=n_new // n_w),
        grid_spec=grid_spec, out_shape=jax.ShapeDtypeStruct((bsz, 32, HD), F32),
        compiler_params=_cparams(("parallel", "arbitrary")),
    )(page_table, *args)


def _pool_prefix_body(x_ref, cl_ref, ct_ref, w_scr, *, n_w):
    r2 = x_ref.shape[1]

    @pl.when(pl.program_id(0) == 0)
    def _():
        r = lax.broadcasted_iota(jnp.int32, (r2, 2 * r2), 0)
        c = lax.broadcasted_iota(jnp.int32, (r2, 2 * r2), 1)
        same = (r % n_w) == (c % n_w)
        local = same & (r // n_w <= c // n_w) & (c < r2)
        w_scr[...] = jnp.where(local | (same & (c >= r2)), 1.0, 0.0).astype(BF16)

    y = _dot3_l(x_ref[...], w_scr[...])
    cl_ref[...] = y[:, :r2]
    ct_ref[...] = y[:, r2:]


def _pool_prefix(x, n_w):
    n, r2 = x.shape
    tm = next((c for c in (256, 128, 64, 32, 16, 8) if n % c == 0), n)
    spec = pl.BlockSpec((tm, r2), lambda i: (i, 0))
    return pl.pallas_call(
        functools.partial(_pool_prefix_body, n_w=n_w), grid=(n // tm,),
        in_specs=[spec], out_specs=[spec, spec],
        out_shape=[jax.ShapeDtypeStruct((n, r2), F32)] * 2,
        scratch_shapes=[pltpu.VMEM((r2, 2 * r2), BF16)],
        compiler_params=_cparams(("arbitrary",)),
    )(x)


def _gla_body(*refs, mode, t_in, chunk, dv, t_real, has_state, hb):
    it = iter(refs)
    q_ref, k_ref, v_ref, og_ref = next(it), next(it), next(it), next(it)
    if mode == "gla":
        zs_ref, wa_ref, ba_ref = next(it), next(it), next(it)
    else:
        lb_ref = next(it)
    gain_ref = next(it)
    s0_ref = next(it) if has_state else None
    o_ref, s_ref, st_scr = next(it), next(it), next(it)
    n_chunks = max(t_in // chunk, 1)
    n_rows = min(chunk, t_in)
    pad = chunk - n_rows
    tri = lax.broadcasted_iota(jnp.int32, (chunk, chunk), 0) >= lax.broadcasted_iota(jnp.int32, (chunk, chunk), 1)
    ltri = jnp.where(tri, 1.0, 0.0).astype(BF16)
    if mode == "hg":
        lbr = lb_ref[...]
        e = jnp.exp(lbr - jnp.max(lbr, axis=0, keepdims=True))
        lb_all = (e / jnp.sum(e, axis=0, keepdims=True))[0:1, :]
    for h in range(hb):
        st_scr[h] = s0_ref[0, h].T if has_state else jnp.zeros((dv, HD), F32)

    def load(ref, r0, sl):
        x = ref[0, pl.ds(r0, n_rows), sl]
        if pad:
            x = jnp.concatenate([x, jnp.zeros((pad, x.shape[1]), F32)], axis=0)
        return x

    def body(c, _):
        r0 = pl.multiple_of(c * chunk, 8)
        live = (r0 + lax.broadcasted_iota(jnp.int32, (chunk, HD), 0)) < t_real
        if mode == "gla":
            ga = load(zs_ref, r0, slice(0, HD)).astype(BF16)
        for h in range(hb):
            ks, vs = slice(h * HD, (h + 1) * HD), slice(h * dv, (h + 1) * dv)
            qr, kr, v, og = load(q_ref, r0, ks), load(k_ref, r0, ks), load(v_ref, r0, vs), load(og_ref, r0, vs)
            if mode == "gla":
                g = _log_sigmoid(_dot(ga, wa_ref[:, ks]) + ba_ref[:, ks]) / GLA_GATE_NORM
                q, k = qr, kr
            else:
                lb = lb_all[:, ks]
                f = lb + (1.0 - lb) * _sigmoid(kr)
                q, k, g = qr * _sigmoid(qr), 1.0 - f, jnp.log(f)
            g = jnp.where(live, g, 0.0)
            k = jnp.where(live, k, 0.0)
            b = _dot3_r(ltri, g)
            bm = b[chunk // 2 - 1:chunk // 2, :]
            bl = b[chunk - 1:chunk, :]
            qe = (q * jnp.exp(jnp.minimum(b - bm, EXP_CLAMP))).astype(BF16)
            ke = (k * jnp.exp(jnp.minimum(bm - b, EXP_CLAMP))).astype(BF16)
            a = jnp.where(tri, _dot_nt(qe, ke), 0.0)
            vb = v.astype(BF16)
            st = st_scr[h]
            o = _dot(a.astype(BF16), vb) + _dot_nt((q * jnp.exp(b)).astype(BF16), st.astype(BF16))
            kd = (k * jnp.exp(bl - b)).astype(BF16)
            st_scr[h] = st * jnp.exp(bl) + _dot_tn(vb, kd)
            o = o * lax.rsqrt(jnp.mean(o * o, axis=-1, keepdims=True) + EPS) * gain_ref[...]
            o = o * (og * _sigmoid(og) if mode == "gla" else _sigmoid(og))
            o_ref[0, pl.ds(r0, n_rows), vs] = o[:n_rows].astype(BF16)
        return 0

    lax.fori_loop(0, n_chunks, body, 0)
    for h in range(hb):
        s_ref[0, h] = st_scr[h].T


def _gla(mode, z3, zs3, cols, n_heads, dv, extra, gain, s0, *, t_real, chunk, hb):
    bsz, t_in, _ = z3.shape
    qc, kc, vc, oc = cols
    has_state = s0 is not None
    wk, wv = hb * HD, hb * dv
    in_specs = [pl.BlockSpec((1, t_in, wk), lambda b, h: (b, 0, qc // wk + h)),
                pl.BlockSpec((1, t_in, wk), lambda b, h: (b, 0, kc // wk + h)),
                pl.BlockSpec((1, t_in, wv), lambda b, h: (b, 0, vc // wv + h)),
                pl.BlockSpec((1, t_in, wv), lambda b, h: (b, 0, oc // wv + h))]
    args = [z3, z3, z3, z3]
    if mode == "gla":
        wa, ba = extra
        in_specs += [pl.BlockSpec((1, t_in, HD), lambda b, h: (b, 0, 0)),
                     pl.BlockSpec((HD, wk), lambda b, h: (0, h)),
                     pl.BlockSpec((1, wk), lambda b, h: (0, h))]
        args += [zs3, wa, ba]
    else:
        in_specs.append(pl.BlockSpec((8, wk), lambda b, h: (0, h)))
        args.append(extra)
    in_specs.append(pl.BlockSpec((1, dv), lambda b, h: (0, 0)))
    args.append(gain)
    st_spec = pl.BlockSpec((1, hb, HD, dv), lambda b, h: (b, h, 0, 0))
    if has_state:
        in_specs.append(st_spec)
        args.append(s0)
    return pl.pallas_call(
        functools.partial(_gla_body, mode=mode, t_in=t_in, chunk=chunk, dv=dv, t_real=t_real, has_state=has_state,
                          hb=hb),
        grid=(bsz, n_heads // hb),
        in_specs=in_specs,
        out_specs=[pl.BlockSpec((1, t_in, wv), lambda b, h: (b, 0, h)), st_spec],
        out_shape=[jax.ShapeDtypeStruct((bsz, t_in, n_heads * dv), BF16),
                   jax.ShapeDtypeStruct((bsz, n_heads, HD, dv), F32)],
        scratch_shapes=[pltpu.VMEM((hb, dv, HD), F32)],
        compiler_params=_cparams(("parallel", "parallel")),
    )(*args)


def _split_cols(w, sizes):
    outs, off = [], 0
    for s in sizes:
        outs.append(w[:, off:off + s])
        off += s
    return outs


def _pad_cols(w, n):
    return jnp.pad(w, ((0, 0), (0, n - w.shape[1])))


def _row(v):
    return v.reshape(1, -1).astype(F32)


def _even_weights(w_in, fq_gain, fk_gain, nq_gain, nk_gain):
    fq, fk, fv, ff, nq, kc, vc, ks, vs, kw, vw, ng = _split_cols(w_in, EVEN_SIZES)
    big = jnp.concatenate([fq, fk, fv, nq, kc, vc, ks, vs, kw, vw], axis=1).astype(BF16)
    small = _pad_cols(jnp.concatenate([ff, ng], axis=1), HD).astype(BF16)
    ones = lambda n: jnp.ones((n,), F32)
    gain = jnp.concatenate([jnp.tile(fq_gain, H_FOX), jnp.tile(fk_gain, H_FOX), ones(H_FOX * HD),
                            jnp.tile(nq_gain, H_NSA), ones(2 * G_NSA * HD), jnp.tile(nk_gain[1], G_NSA),
                            ones(G_NSA * HD), jnp.tile(nk_gain[2], G_NSA), ones(G_NSA * HD)])
    z, o = jnp.zeros, jnp.ones
    flag = jnp.concatenate([o((2 * H_FOX * HD,), F32), z((H_FOX * HD,), F32), o((H_NSA * HD,), F32),
                            z((2 * G_NSA * HD,), F32), o((G_NSA * HD,), F32), z((G_NSA * HD,), F32),
                            o((G_NSA * HD,), F32), z((G_NSA * HD,), F32)])
    return big, small, _row(gain), _row(flag)


def _odd_weights(w_in):
    gq, gk, gv, ga, gg, hq, hf, hi, hg = _split_cols(w_in, ODD_SIZES)
    big = jnp.concatenate([gq, gk, gv, gg, hq, hf, hi, hg], axis=1).astype(BF16)
    small = _pad_cols(ga, HD).astype(BF16)
    gain = jnp.concatenate([jnp.full((H_GLA * DK_GLA,), DK_GLA ** -0.5, F32), jnp.ones((N_BIG_ODD - H_GLA * DK_GLA,), F32)])
    return big, small, _row(gain), jnp.zeros((1, N_BIG_ODD), F32)


def _cmp_weights(pe, w):
    half = CMP_STRIDE * HD
    wcat = jnp.concatenate([w[:half], w[half:]], axis=1).astype(BF16)
    pe8 = jnp.pad(pe.reshape(2, half), ((0, 6), (0, 0)))
    return pe8, wcat


def _ffn(x, g, w1, w3, w2):
    return _mm_res([_ffn_up(x, g, w1, w3)], w2, x)


def _even_layer(x, bsz, t, t_real, ew, past):
    (w_big, w_small, cgain, cflag, g_mix, w_out, b_f_pad, gb_pad, pe_k, wc_k, pe_v, wc_v, nk0) = ew
    z, zs = _proj(x, g_mix, w_big, w_small, cgain, cflag)
    z3 = z.reshape(bsz, t, N_BIG_EVEN)
    zs3 = zs.reshape(bsz, t, HD)
    one = jnp.ones((1, HD), F32)
    if past is None:
        lf, c = _fox_prep(zs3, b_f_pad)
        logf_new = lf[:, :, :H_FOX]
        tk = min(t, 512)
        bias = jnp.swapaxes(c[:, :, :H_FOX], 1, 2).reshape(bsz * H_FOX, t // tk, tk)
        o_fox = _flash("fox", z3, FQ, z3, FK, z3, FV, n_kv=H_FOX, hpg=1, tq=min(t, 256), tk=tk, bias=bias,
                       out_dtype=BF16)
        n_cmp = t // CMP_STRIDE
        mn = -(-n_cmp // HD) * HD
        kcmp = _compress(z3, KC, pe_k, wc_k, nk0, mn=mn, norm=True)
        vcmp = _compress(z3, VC, pe_v, wc_v, one, mn=mn, norm=False)
        tqn = min(t, 256 // HPG)
        o_cmp, sel = _cmpsel(z3, kcmp, vcmp, tq=min(t, 128), pos0=0, n_sel=-(-t // SEL_BLOCK))
        o_slc = _flash("slc", z3, NQ, z3, KS, z3, VS, n_kv=G_NSA, hpg=HPG, tq=tqn, tk=tk, sel=sel)
        o_win = _flash("win", z3, NQ, z3, KW, z3, VW, n_kv=G_NSA, hpg=HPG, tq=tqn, tk=tk)
        w_buf = WINDOW
        kw_new = jnp.concatenate([jnp.zeros((bsz, w_buf, G_NSA * HD), F32), z3[:, :, KW:KW + G_NSA * HD]], axis=1)[:, -w_buf:]
        vw_new = jnp.concatenate([jnp.zeros((bsz, w_buf, G_NSA * HD), F32), z3[:, :, VW:VW + G_NSA * HD]], axis=1)[:, -w_buf:]
        o_fox = o_fox.reshape(bsz * t, H_FOX * HD)
    else:
        (page_table, pool_off, fk_pool, fv_pool, lf_pool, kc_pool, vc_pool, ks_pool, vs_pool, buf_k, buf_v) = past
        p_len = page_table.shape[1] * PAGE
        lf_all, _ = _fox_prep(zs.reshape(1, bsz * t, HD), b_f_pad)
        lf_new = lf_all.reshape(bsz, t, HD)[:, :t_real, :H_FOX]
        logf_new = lf_new
        lfn_pad = _pad_cols(lf_new.reshape(bsz, t_real * H_FOX), HD).reshape(bsz, 1, HD)
        c_local, c_total = _pool_prefix(lf_pool, H_FOX)
        rows_of = lambda c0, n: z3[:, :t_real, c0:c0 + n * HD].reshape(bsz, t_real * n, HD)
        o_fox = _paged_attn("fox", rows_of(FQ, H_FOX), page_table, fk_pool, fv_pool, pool_off,
                            rows_of(FK, H_FOX), rows_of(FV, H_FOX), hpg_kv=1, pg=8,
                            c_local=c_local[:, None, :], c_total=c_total[:, None, :], lf_new=lfn_pad)
        o_fox = o_fox.reshape(bsz, t_real, H_FOX * HD)
        n_cmp = -(-(p_len + t_real) // CMP_STRIDE)
        mn = -(-n_cmp // HD) * HD
        kcmp = _compress(z3, KC, pe_k, wc_k, nk0, mn=mn, norm=True, page_table=page_table, pool=kc_pool,
                         pool_off=pool_off, t_new=t_real)
        vcmp = _compress(z3, VC, pe_v, wc_v, one, mn=mn, norm=False, page_table=page_table, pool=vc_pool,
                         pool_off=pool_off, t_new=t_real)
        n_sel = -(-(p_len + t_real) // SEL_BLOCK)
        o_cmp, sel = _cmpsel(z3, kcmp, vcmp, tq=t, pos0=p_len, n_sel=n_sel)
        nsp = sel.shape[-1]
        sel_rows = jnp.broadcast_to(jnp.swapaxes(sel[:, :, :t_real], 1, 2)[:, :, :, None, :],
                                    (bsz, t_real, G_NSA, HPG, nsp)).reshape(bsz, 32, nsp)
        o_slc = _paged_attn("slc", rows_of(NQ, H_NSA), page_table, ks_pool, vs_pool, pool_off,
                            rows_of(KS, G_NSA), rows_of(VS, G_NSA), hpg_kv=HPG, pg=16,
                            sel_rows=sel_rows).reshape(bsz, t_real, H_NSA * HD)
        kw_all = jnp.concatenate([buf_k, z3[:, :t_real, KW:KW + G_NSA * HD]], axis=1)
        vw_all = jnp.concatenate([buf_v, z3[:, :t_real, VW:VW + G_NSA * HD]], axis=1)
        w_buf = buf_k.shape[1]
        kw_new, vw_new = kw_all[:, -w_buf:], vw_all[:, -w_buf:]
        lk = -(-(w_buf + t) // HD) * HD
        padk = lambda a: jnp.pad(a, ((0, 0), (0, lk - a.shape[1]), (0, 0)))
        o_win = _flash("win", z3, NQ, padk(kw_all), 0, padk(vw_all), 0, n_kv=G_NSA, hpg=HPG, tq=t, tk=HD, off=w_buf)
        padt = lambda a: jnp.pad(a, ((0, 0), (0, t - t_real), (0, 0)))
        o_fox = padt(o_fox).astype(BF16).reshape(bsz * t, H_FOX * HD)
        o_slc = padt(o_slc)
    m = bsz * t
    o_nsa = _nsa_combine(o_cmp.reshape(m, -1), o_slc.reshape(m, -1), o_win.reshape(m, -1), zs, gb_pad)
    x = _mm_res([o_fox, o_nsa], w_out, x)
    zr = z3[:, :t_real]
    grp = lambda c0: zr[:, :, c0:c0 + G_NSA * HD].reshape(1, bsz, t_real, G_NSA, HD)
    fox = lambda c0: zr[:, :, c0:c0 + H_FOX * HD].reshape(1, bsz, t_real, H_FOX, HD)
    w_rows = kw_new.shape[1]
    outs = (fox(FK), fox(FV), logf_new[None], grp(KC), grp(VC), grp(KS), grp(VS),
            kw_new.reshape(1, bsz, w_rows, G_NSA, HD), vw_new.reshape(1, bsz, w_rows, G_NSA, HD))
    return x, outs


def _odd_layer(x, bsz, t, t_real, ow, state):
    (w_big, w_small, cgain, cflag, g_mix, w_out, wa_pad, ba, gla_gain, lb_pad, hg_gain) = ow
    z, zs = _proj(x, g_mix, w_big, w_small, cgain, cflag)
    z3 = z.reshape(bsz, t, N_BIG_ODD)
    zs3 = zs.reshape(bsz, t, HD)
    s_gla, s_hg = state if state is not None else (None, None)
    chunk = 64 if t >= 64 else 128
    o_gla, s_gla = _gla("gla", z3, zs3, (GQ, GK, GV, GG), H_GLA, DV_GLA, (wa_pad, ba), gla_gain, s_gla,
                        t_real=t_real, chunk=chunk, hb=2)
    o_hg, s_hg = _gla("hg", z3, zs3, (HQ, HF, HI, HO), H_HG, DV_HG, lb_pad, hg_gain, s_hg,
                      t_real=t_real, chunk=chunk, hb=4)
    m = bsz * t
    x = _mm_res([o_gla.reshape(m, -1), o_hg.reshape(m, -1)], w_out, x)
    return x, (s_gla[None], s_hg[None])


def kernel(x_prompt, x_sample, cache_fox_k, cache_fox_v, cache_fox_logf, cache_nsa_kc, cache_nsa_vc, cache_nsa_ks, cache_nsa_vs, state_nsa_kw, state_nsa_vw, state_gla, state_hgrn, page_table, norm_mix, norm_ffn, w_in_even, w_out_even, fox_b_f, fox_q_gain, fox_k_gain, nsa_q_gain, nsa_k_gain, nsa_cmp_pe_k, nsa_cmp_pe_v, nsa_cmp_wk, nsa_cmp_wv, nsa_gate_b, w_in_odd, w_out_odd, gla_wa2, gla_ba, gla_norm, hgrn_lb, hgrn_norm, ffn_w1, ffn_w3, ffn_w2):
    bp, tp, d = x_prompt.shape
    bs, ts, _ = x_sample.shape
    n_pool = cache_fox_k.shape[1]
    ts_pad = 8
    pe_k, wc_k = _cmp_weights(nsa_cmp_pe_k[0], nsa_cmp_wk[0])
    pe_v, wc_v = _cmp_weights(nsa_cmp_pe_v[0], nsa_cmp_wv[0])
    b_f_pad = _pad_cols(_row(fox_b_f[0]), HD)
    gb_pad = _pad_cols(jnp.concatenate([jnp.zeros((1, H_FOX), F32), _row(nsa_gate_b[0])], axis=1), HD)
    ew = _even_weights(w_in_even[0], fox_q_gain[0], fox_k_gain[0], nsa_q_gain[0], nsa_k_gain[0]) + (
        _row(norm_mix[0]), w_out_even[0].astype(BF16), b_f_pad, gb_pad, pe_k, wc_k, pe_v, wc_v, _row(nsa_k_gain[0, 0]))
    wa_pad = jnp.pad(gla_wa2[0], ((0, HD - GLA_RANK), (0, 0))).astype(BF16)
    lb_pad = jnp.pad(hgrn_lb.astype(F32), ((0, 8 - hgrn_lb.shape[0]), (0, 0)), constant_values=-1e30)
    ow = _odd_weights(w_in_odd[0]) + (_row(norm_mix[1]), w_out_odd[0].astype(BF16), wa_pad, _row(gla_ba[0]),
                                      _row(gla_norm[0]), lb_pad, _row(hgrn_norm[0]))
    ffn = [(_row(norm_ffn[i]), ffn_w1[i].astype(BF16), ffn_w3[i].astype(BF16), ffn_w2[i].astype(BF16)) for i in range(2)]

    flat = lambda c: c.reshape((c.shape[0] * c.shape[1], PAGE * c.shape[3], HD))
    lf_pool = cache_fox_logf.reshape(-1, PAGE * H_FOX)
    win = lambda s: s.reshape(bs, s.shape[2], G_NSA * HD)
    past = (page_table, 0, flat(cache_fox_k), flat(cache_fox_v), lf_pool, flat(cache_nsa_kc), flat(cache_nsa_vc),
            flat(cache_nsa_ks), flat(cache_nsa_vs), win(state_nsa_kw), win(state_nsa_vw))

    def run(x, bsz, t, t_real, past_, state_):
        x, ev = _even_layer(x, bsz, t, t_real, ew, past_)
        x = _ffn(x, *ffn[0])
        x, od = _odd_layer(x, bsz, t, t_real, ow, state_)
        x = _ffn(x, *ffn[1])
        return x, ev, od

    yp, ev_p, od_p = run(x_prompt.reshape(bp * tp, d), bp, tp, tp, None, None)
    xs = jnp.pad(x_sample, ((0, 0), (0, ts_pad - ts), (0, 0))).reshape(bs * ts_pad, d)
    ys, ev_s, od_s = run(xs, bs, ts_pad, ts, past, (state_gla.reshape(state_gla.shape[1:]),
                                                    state_hgrn.reshape(state_hgrn.shape[1:])))
    y_prompt = yp.reshape(bp, tp, d)
    y_sample = ys.reshape(bs, ts_pad, d)[:, :ts]
    outs = [y_prompt, y_sample]
    for a, b in zip(ev_p, ev_s):
        outs += [a, b]
    for a, b in zip(od_p, od_s):
        outs += [a, b]
    return tuple(outs)
```

```python
import functools

import jax
import jax.numpy as jnp
from jax import lax
from jax.experimental import pallas as pl
from jax.experimental.pallas import tpu as pltpu

F32 = jnp.float32
BF16 = jnp.bfloat16

D_MODEL = 2048
HD = 128
H_FOX = 8
H_NSA = 8
G_NSA = 2
HPG = H_NSA // G_NSA
CMP_STRIDE = 16
CMP_BLOCK = 2 * CMP_STRIDE
SEL_BLOCK = 64
N_SELECT = 16
WINDOW = 512
FORCE_SCORE = 1.0e4
H_GLA = 4
DK_GLA = 128
DV_GLA = 256
GLA_RANK = 16
GLA_GATE_NORM = 16.0
H_HG = 8
DK_HG = 128
DV_HG = 128
PAGE = 128
EVEN_SIZES = (H_FOX * HD, H_FOX * HD, H_FOX * HD, H_FOX, H_NSA * HD,
              G_NSA * HD, G_NSA * HD, G_NSA * HD, G_NSA * HD, G_NSA * HD, G_NSA * HD, 3 * H_NSA)
ODD_SIZES = (H_GLA * DK_GLA, H_GLA * DK_GLA, H_GLA * DV_GLA, GLA_RANK, H_GLA * DV_GLA,
             H_HG * DK_HG, H_HG * DK_HG, H_HG * DV_HG, H_HG * DV_HG)
FQ, FK, FV, NQ, KC, VC, KS, VS, KW, VW = 0, 1024, 2048, 3072, 4096, 4352, 4608, 4864, 5120, 5376
N_BIG_EVEN = 5632
GQ, GK, GV, GG, HQ, HF, HI, HO = 0, 512, 1024, 2048, 3072, 4096, 5120, 6144
N_BIG_ODD = 7168
ATTN_SCALE = HD ** -0.5
NEG = -1.0e30
EPS = 1.0e-6
EXP_CLAMP = 80.0
VMEM_LIMIT = 56 * 1024 * 1024


def _cparams(sem):
    return pltpu.CompilerParams(dimension_semantics=sem, vmem_limit_bytes=VMEM_LIMIT)


def _dot(a, b):
    return jnp.dot(a, b, preferred_element_type=F32)


def _dot_nt(a, b):
    return lax.dot_general(a, b, (((1,), (1,)), ((), ())), preferred_element_type=F32)


def _dot_tn(a, b):
    return lax.dot_general(a, b, (((0,), (0,)), ((), ())), preferred_element_type=F32)


def _split3(x):
    hi = x.astype(BF16)
    r = x - hi.astype(F32)
    mid = r.astype(BF16)
    lo = (r - mid.astype(F32)).astype(BF16)
    return hi, mid, lo


def _dot3_l(x, w):
    hi, mid, lo = _split3(x)
    return _dot(hi, w) + _dot(mid, w) + _dot(lo, w)


def _dot3_r(w, x):
    hi, mid, lo = _split3(x)
    return _dot(w, hi) + _dot(w, mid) + _dot(w, lo)


def _sigmoid(x):
    return 1.0 / (1.0 + jnp.exp(-x))


def _log_sigmoid(x):
    return jnp.minimum(x, 0.0) - jnp.log1p(jnp.exp(-jnp.abs(x)))


def _tri(n, upper):
    r = lax.broadcasted_iota(jnp.int32, (n, n), 0)
    c = lax.broadcasted_iota(jnp.int32, (n, n), 1)
    return jnp.where((r <= c) if upper else (r >= c), 1.0, 0.0).astype(BF16)


def _proj_body(x_ref, g_ref, w_ref, ws_ref, cg_ref, cf_ref, z_ref, zs_ref, h_scr, *, tn, norm_tiles):
    j = pl.program_id(1)

    @pl.when(j == 0)
    def _():
        x = x_ref[...]
        r = lax.rsqrt(jnp.mean(x * x, axis=-1, keepdims=True) + EPS)
        h = (x * r * g_ref[...]).astype(BF16)
        h_scr[...] = h
        zs_ref[...] = _dot(h, ws_ref[...])

    z = _dot(h_scr[...], w_ref[...])
    if not norm_tiles:
        z_ref[...] = z * cg_ref[...]
        return
    has_norm = functools.reduce(jnp.logical_or, [j == t for t in norm_tiles])

    @pl.when(has_norm)
    def _():
        for c in range(tn // HD):
            sl = slice(c * HD, (c + 1) * HD)
            zc = z[:, sl]
            r = lax.rsqrt(jnp.mean(zc * zc, axis=-1, keepdims=True) + EPS)
            f = cf_ref[:, sl]
            z_ref[:, sl] = zc * (f * r + (1.0 - f)) * cg_ref[:, sl]

    @pl.when(jnp.logical_not(has_norm))
    def _():
        z_ref[...] = z * cg_ref[...]


def _proj(x, g, w_big, w_small, col_gain, col_flag, norm_cols):
    m, d = x.shape
    n = w_big.shape[1]
    tm = min(m, 1024)
    tn = 512
    norm_tiles = tuple(t for t in range(n // tn) if any(a < (t + 1) * tn and t * tn < b for a, b in norm_cols))
    return pl.pallas_call(
        functools.partial(_proj_body, tn=tn, norm_tiles=norm_tiles),
        grid=(m // tm, n // tn),
        in_specs=[pl.BlockSpec((tm, d), lambda i, j: (i, 0)),
                  pl.BlockSpec((1, d), lambda i, j: (0, 0)),
                  pl.BlockSpec((d, tn), lambda i, j: (0, j)),
                  pl.BlockSpec((d, HD), lambda i, j: (0, 0)),
                  pl.BlockSpec((1, tn), lambda i, j: (0, j)),
                  pl.BlockSpec((1, tn), lambda i, j: (0, j))],
        out_specs=[pl.BlockSpec((tm, tn), lambda i, j: (i, j)),
                   pl.BlockSpec((tm, HD), lambda i, j: (i, 0))],
        out_shape=[jax.ShapeDtypeStruct((m, n), F32), jax.ShapeDtypeStruct((m, HD), F32)],
        scratch_shapes=[pltpu.VMEM((tm, d), BF16)],
        compiler_params=_cparams(("parallel", "arbitrary")),
    )(x, g, w_big, w_small, col_gain, col_flag)


def _ffn_up_body(x_ref, g_ref, w1_ref, w3_ref, o_ref, h_scr):
    @pl.when(pl.program_id(1) == 0)
    def _():
        x = x_ref[...]
        r = lax.rsqrt(jnp.mean(x * x, axis=-1, keepdims=True) + EPS)
        h_scr[...] = (x * r * g_ref[...]).astype(BF16)

    h = h_scr[...]
    a = _dot(h, w1_ref[...].astype(BF16))
    b = _dot(h, w3_ref[...].astype(BF16))
    o_ref[...] = (a * _sigmoid(a) * b).astype(BF16)


def _ffn_up(x, g, w1, w3, layer):
    m, d = x.shape
    n = w1.shape[2]
    tm = min(m, 1024)
    tn = 512
    return pl.pallas_call(
        _ffn_up_body,
        grid=(m // tm, n // tn),
        in_specs=[pl.BlockSpec((tm, d), lambda i, j: (i, 0)),
                  pl.BlockSpec((1, d), lambda i, j: (0, 0)),
                  pl.BlockSpec((None, d, tn), lambda i, j: (layer, 0, j)),
                  pl.BlockSpec((None, d, tn), lambda i, j: (layer, 0, j))],
        out_specs=pl.BlockSpec((tm, tn), lambda i, j: (i, j)),
        out_shape=jax.ShapeDtypeStruct((m, n), BF16),
        scratch_shapes=[pltpu.VMEM((tm, d), BF16)],
        compiler_params=_cparams(("parallel", "arbitrary")),
    )(x, g, w1, w3)


def _mm_res_body(*refs, n_in):
    res_ref = refs[2 * n_in]
    o_ref = refs[2 * n_in + 1]
    wb_refs = refs[2 * n_in + 2:]

    @pl.when(pl.program_id(1) == 0)
    def _():
        for w_ref, wb_ref in zip(refs[n_in:2 * n_in], wb_refs):
            wb_ref[...] = w_ref[...].astype(BF16)

    acc = res_ref[...]
    for a_ref, wb_ref in zip(refs[:n_in], wb_refs):
        acc = acc + _dot(a_ref[...], wb_ref[...])
    o_ref[...] = acc


def _mm_res(a_list, w, layer, res):
    m, n = res.shape
    tm = min(m, 512)
    tn = 512
    in_specs, w_specs, scratch, off = [], [], [], 0
    for a in a_list:
        k = a.shape[1]
        in_specs.append(pl.BlockSpec((tm, k), lambda j, i: (i, 0)))
        w_specs.append(pl.BlockSpec((None, k, tn), lambda j, i, _o=off // k: (layer, _o, j)))
        scratch.append(pltpu.VMEM((k, tn), BF16))
        off += k
    return pl.pallas_call(
        functools.partial(_mm_res_body, n_in=len(a_list)),
        grid=(n // tn, m // tm),
        in_specs=in_specs + w_specs + [pl.BlockSpec((tm, tn), lambda j, i: (i, j))],
        out_specs=pl.BlockSpec((tm, tn), lambda j, i: (i, j)),
        out_shape=jax.ShapeDtypeStruct((m, n), F32),
        scratch_shapes=scratch,
        compiler_params=_cparams(("parallel", "arbitrary")),
    )(*a_list, *([w] * len(a_list)), res)


def _fox_prep_body(zs_ref, b_ref, lf_ref, c_ref, carry, *, tc):
    @pl.when(pl.program_id(1) == 0)
    def _():
        carry[...] = jnp.zeros_like(carry)

    lf = _log_sigmoid(zs_ref[0] + b_ref[...])
    c = _dot3_r(_tri(tc, upper=False), lf) + carry[...]
    lf_ref[0] = lf
    c_ref[0] = c
    carry[...] = c[tc - 1:tc, :]


def _fox_prep(zs3, b_pad):
    bsz, t, _ = zs3.shape
    tc = min(t, 256)
    spec = pl.BlockSpec((1, tc, HD), lambda b, i: (b, i, 0))
    return pl.pallas_call(
        functools.partial(_fox_prep_body, tc=tc),
        grid=(bsz, t // tc),
        in_specs=[spec, pl.BlockSpec((1, HD), lambda b, i: (0, 0))],
        out_specs=[spec, spec],
        out_shape=[jax.ShapeDtypeStruct(zs3.shape, F32)] * 2,
        scratch_shapes=[pltpu.VMEM((1, HD), F32)],
        compiler_params=_cparams(("parallel", "arbitrary")),
    )(zs3, b_pad)


def _flash_body(*refs, mode, tq, tk, hpg, off):
    it = iter(refs)
    q_ref, k_ref, v_ref = next(it), next(it), next(it)
    bias_ref = next(it) if mode == "fox" else None
    sel_ref = next(it) if mode == "slc" else None
    o_ref, kb, vb = next(it), next(it), next(it)
    i = pl.program_id(2)

    @pl.when(i == 0)
    def _():
        kb[...] = k_ref[0].astype(BF16)
        vb[...] = v_ref[0].astype(BF16)

    q = q_ref[0]
    if hpg > 1:
        q = jnp.concatenate([q[:, h * HD:(h + 1) * HD] for h in range(hpg)], axis=0)
    qa = (q * ATTN_SCALE).astype(BF16)
    tile_rows = (lambda x: jnp.concatenate([x] * hpg, axis=0)) if hpg > 1 else (lambda x: x)
    q_lo = off + i * tq
    t_k = kb.shape[0]
    if mode == "slc":
        unpicked = (sel_ref[0, 0] - 1.0).astype(BF16)

    def attend(blocks):
        ss = []
        for b, (ks, size, masked) in enumerate(blocks):
            s = _dot_nt(qa, kb[pl.ds(ks, size), :])
            if mode == "fox":
                s = s - bias_ref[0, b:b + 1, :]
            if mode == "slc":
                cb = lax.broadcasted_iota(jnp.int32, (HD, size), 0)
                kk = lax.broadcasted_iota(jnp.int32, (HD, size), 1)
                expand = jnp.where(cb == (ks + kk) // SEL_BLOCK, 2.0 ** 100, 0.0).astype(BF16)
                s = s + tile_rows(_dot(unpicked, expand))
            if masked:
                kidx = ks + lax.broadcasted_iota(jnp.int32, (tq, size), 1)
                qpos = q_lo + lax.broadcasted_iota(jnp.int32, (tq, size), 0)
                valid = kidx <= qpos
                if mode == "win":
                    valid = valid & ((qpos - kidx) < WINDOW)
                s = jnp.where(tile_rows(valid), s, NEG)
            ss.append(s)
        m = ss[0].max(axis=-1, keepdims=True)
        for s in ss[1:]:
            m = jnp.maximum(m, s.max(axis=-1, keepdims=True))
        l, acc = 0.0, 0.0
        for s, (ks, size, _) in zip(ss, blocks):
            p = jnp.exp(s - m)
            l = l + jnp.sum(p, axis=-1, keepdims=True)
            acc = acc + _dot(p.astype(BF16), vb[pl.ds(ks, size), :])
        o = acc / l
        for h in range(hpg):
            o_ref[0, :, h * HD:(h + 1) * HD] = o[h * tq:(h + 1) * tq].astype(o_ref.dtype)

    if mode == "win":
        span = min(t_k, -(-(WINDOW + tq) // HD) * HD + HD)
        start = jnp.clip((q_lo - (WINDOW - 1)) // HD * HD, 0, t_k - span)
        attend([(pl.multiple_of(start, HD), span, True)])
    else:
        n_max = t_k // tk
        need = (q_lo + tq - 1) // tk + 1
        for nb in range(1, n_max + 1):
            @pl.when(need == nb)
            def _(nb=nb):
                attend([(b * tk, tk, b == nb - 1) for b in range(nb)])


def _flash(mode, q_arr, q_col, k_arr, k_col, v_arr, v_col, *, n_kv, hpg, tq, tk, off=0,
           bias=None, sel=None, out_dtype=F32):
    bsz, t_q = q_arr.shape[0], q_arr.shape[1]
    t_k = k_arr.shape[1]
    nq = t_q // tq
    w = hpg * HD
    in_specs = [pl.BlockSpec((1, tq, w), lambda b, g, i: (b, i, q_col // w + g)),
                pl.BlockSpec((1, t_k, HD), lambda b, g, i: (b, 0, k_col // HD + g)),
                pl.BlockSpec((1, t_k, HD), lambda b, g, i: (b, 0, v_col // HD + g))]
    args = [q_arr, k_arr, v_arr]
    if mode == "fox":
        in_specs.append(pl.BlockSpec((1, t_k // tk, tk), lambda b, g, i: (b * n_kv + g, 0, 0)))
        args.append(bias)
    if mode == "slc":
        in_specs.append(pl.BlockSpec((1, 1, tq, HD), lambda b, g, i: (b, g, i, 0)))
        args.append(sel)
    return pl.pallas_call(
        functools.partial(_flash_body, mode=mode, tq=tq, tk=tk, hpg=hpg, off=off),
        grid=(bsz, n_kv, nq),
        in_specs=in_specs,
        out_specs=pl.BlockSpec((1, tq, w), lambda b, g, i: (b, i, g)),
        out_shape=jax.ShapeDtypeStruct((bsz, t_q, n_kv * w), out_dtype),
        scratch_shapes=[pltpu.VMEM((t_k, HD), BF16), pltpu.VMEM((t_k, HD), BF16)],
        compiler_params=_cparams(("parallel", "parallel", "arbitrary")),
    )(*args)


def _compress_body(*refs, paged, pg, n_steps, mn, norm, t_new):
    it = iter(refs)
    if paged:
        next(it)
    x_refs = [next(it) for _ in range(pg if paged else G_NSA)]
    new_ref = next(it) if paged else None
    pe_ref, w_ref, gain_ref, o_ref, hcat, tb, perm = (next(it) for _ in range(7))
    p = pl.program_id(1)
    mh = mn + 8
    per_page = PAGE // CMP_STRIDE
    n_real = n_steps * pg * per_page
    n_g = G_NSA if paged else 1

    @pl.when(p == 0)
    def _():
        hcat[:, n_real:, :] = jnp.zeros((G_NSA, mh - n_real, CMP_STRIDE * HD), F32)
        d = lax.broadcasted_iota(jnp.int32, perm.shape, 0)
        s = lax.broadcasted_iota(jnp.int32, perm.shape, 1)
        src = (CMP_STRIDE * (d % per_page) + (d % PAGE) // per_page) * n_g + d // PAGE
        perm[...] = jnp.where(s == src, 1.0, 0.0).astype(BF16)

    for pp in range(pg):
        r0 = pl.multiple_of((p * pg + pp) * per_page, per_page)
        for x_ref, g0 in ([(x_refs[pp], 0)] if paged else [(x_refs[g], g) for g in range(G_NSA)]):
            rows = _dot(perm[...], x_ref[0].astype(BF16))
            for g in range(n_g):
                for l in range(CMP_STRIDE):
                    a = g * PAGE + l * per_page
                    hcat[g0 + g, pl.ds(r0, per_page), l * HD:(l + 1) * HD] = rows[a:a + per_page]

    @pl.when(p == n_steps - 1)
    def _():
        if paged:
            xn = new_ref[0]
            rid = lax.broadcasted_iota(jnp.int32, xn.shape, 0)
            xn = jnp.where(rid < t_new, xn, 0.0)
            for g in range(G_NSA):
                for l in range(8):
                    hcat[g, n_real:n_real + 1, l * HD:(l + 1) * HD] = xn[l:l + 1, g * HD:(g + 1) * HD]
        w = w_ref[...]
        pe2 = _dot(pe_ref[...].astype(BF16), w)
        pe_bias = pe2[0:1, 0:HD] + pe2[1:2, HD:2 * HD]
        for g in range(G_NSA):
            tb[...] = _dot(hcat[g].astype(BF16), w)
            y = tb[0:mn, 0:HD] + tb[1:mn + 1, HD:2 * HD] + pe_bias
            if norm:
                y = y * lax.rsqrt(jnp.mean(y * y, axis=-1, keepdims=True) + EPS) * gain_ref[...]
            o_ref[0, :, g * HD:(g + 1) * HD] = y


def _compress(x_arr, x_col, pe8, wcat, gain, *, mn, norm, page_table=None, pool=None, pool_off=0, t_new=0, pg=8):
    paged = page_table is not None
    bsz = x_arr.shape[0]
    mh = mn + 8
    w2 = G_NSA * HD
    const = lambda shape: pl.BlockSpec(shape, (lambda b, p, *_: (0,) * len(shape)))
    tail = [const((8, CMP_STRIDE * HD)), const((CMP_STRIDE * HD, 2 * HD)), const((1, HD))]
    out_spec = pl.BlockSpec((1, mn, w2), lambda b, p, *_: (b, 0, 0))
    n_perm = G_NSA * PAGE if paged else PAGE
    scratch = [pltpu.VMEM((G_NSA, mh, CMP_STRIDE * HD), F32), pltpu.VMEM((mh, 2 * HD), F32),
               pltpu.VMEM((n_perm, n_perm), BF16)]
    out_shape = jax.ShapeDtypeStruct((bsz, mn, w2), F32)
    if not paged:
        n_steps = x_arr.shape[1] // PAGE
        body = functools.partial(_compress_body, paged=False, pg=1, n_steps=n_steps, mn=mn, norm=norm, t_new=0)
        return pl.pallas_call(
            body, grid=(bsz, n_steps),
            in_specs=[pl.BlockSpec((1, PAGE, HD), lambda b, p, _g=g: (b, p, x_col // HD + _g)) for g in range(G_NSA)] + tail,
            out_specs=out_spec, out_shape=out_shape, scratch_shapes=scratch,
            compiler_params=_cparams(("parallel", "arbitrary")),
        )(*([x_arr] * G_NSA), pe8, wcat, gain)
    pg = min(pg, page_table.shape[1])
    n_steps = page_table.shape[1] // pg
    body = functools.partial(_compress_body, paged=True, pg=pg, n_steps=n_steps, mn=mn, norm=norm, t_new=t_new)
    grid_spec = pltpu.PrefetchScalarGridSpec(
        num_scalar_prefetch=1, grid=(bsz, n_steps),
        in_specs=[pl.BlockSpec((1, G_NSA * PAGE, HD), lambda b, p, pt, _pp=pp: (pool_off + pt[b, p * pg + _pp], 0, 0))
                  for pp in range(pg)]
        + [pl.BlockSpec((1, 8, w2), lambda b, p, pt: (b, 0, x_col // w2))] + tail,
        out_specs=out_spec, scratch_shapes=scratch)
    return pl.pallas_call(body, grid_spec=grid_spec, out_shape=out_shape,
                          compiler_params=_cparams(("parallel", "arbitrary")))(
        page_table, *([pool] * pg), x_arr, pe8, wcat, gain)


def _cmpsel_body(q_ref, kc_ref, vc_ref, o_ref, sel_ref, *, tq, ncp, nsp, n_sel, pos0):
    i = pl.program_id(2)
    q = q_ref[0]
    kc = kc_ref[0].astype(BF16)
    vc = vc_ref[0].astype(BF16)
    pos = pos0 + i * tq + lax.broadcasted_iota(jnp.int32, (tq, ncp), 0)
    col = lax.broadcasted_iota(jnp.int32, (tq, ncp), 1)
    valid = (col * CMP_STRIDE + (CMP_BLOCK - 1)) <= pos
    validf = jnp.where(valid, 1.0, 0.0)
    imp = jnp.zeros((tq, ncp), F32)
    for h in range(HPG):
        qh = (q[:, h * HD:(h + 1) * HD] * ATTN_SCALE).astype(BF16)
        s = jnp.where(valid, _dot_nt(qh, kc), NEG)
        e = jnp.exp(s - jnp.max(s, axis=-1, keepdims=True)) * validf
        p = e / jnp.maximum(jnp.sum(e, axis=-1, keepdims=True), 1e-30)
        o_ref[0, :, h * HD:(h + 1) * HD] = _dot(p.astype(BF16), vc)
        imp = imp + p
    cc = lax.broadcasted_iota(jnp.int32, (ncp, nsp), 0)
    jj = lax.broadcasted_iota(jnp.int32, (ncp, nsp), 1)
    r = SEL_BLOCK // CMP_STRIDE
    gather = jnp.where((cc >= r * jj - 1) & (cc <= r * jj + r - 1), 1.0, 0.0).astype(BF16)
    score = _dot3_l(imp, gather)
    blk = lax.broadcasted_iota(jnp.int32, (tq, nsp), 1)
    pq = pos0 + i * tq + lax.broadcasted_iota(jnp.int32, (tq, nsp), 0)
    forced = (blk == pq // SEL_BLOCK) | (blk == 0)
    future = blk * SEL_BLOCK > pq
    score = jnp.where(forced, FORCE_SCORE, jnp.where(future, -1.0, score))
    score = jnp.where(blk < n_sel, score, -2.0)
    rank = jnp.zeros((tq, nsp), F32)
    for c in range(n_sel):
        sc = score[:, c:c + 1]
        ahead = (sc > score) | ((sc == score) & (blk > c))
        rank = rank + jnp.where(ahead, 1.0, 0.0)
    sel_ref[0, 0] = jnp.where(rank < float(N_SELECT), 1.0, 0.0)


def _cmpsel(z3, kcmp, vcmp, *, tq, pos0, n_sel):
    bsz, t, _ = z3.shape
    ncp = kcmp.shape[1]
    nsp = -(-n_sel // HD) * HD
    w = HPG * HD
    return pl.pallas_call(
        functools.partial(_cmpsel_body, tq=tq, ncp=ncp, nsp=nsp, n_sel=n_sel, pos0=pos0),
        grid=(bsz, G_NSA, t // tq),
        in_specs=[pl.BlockSpec((1, tq, w), lambda b, g, i: (b, i, NQ // w + g)),
                  pl.BlockSpec((1, ncp, HD), lambda b, g, i: (b, 0, g)),
                  pl.BlockSpec((1, ncp, HD), lambda b, g, i: (b, 0, g))],
        out_specs=[pl.BlockSpec((1, tq, w), lambda b, g, i: (b, i, g)),
                   pl.BlockSpec((1, 1, tq, nsp), lambda b, g, i: (b, g, i, 0))],
        out_shape=[jax.ShapeDtypeStruct((bsz, t, G_NSA * w), F32),
                   jax.ShapeDtypeStruct((bsz, G_NSA, t, nsp), F32)],
        compiler_params=_cparams(("parallel", "parallel", "parallel")),
    )(z3, kcmp, vcmp)


def _combine_body(oc_ref, os_ref, ow_ref, zs_ref, gb_ref, o_ref):
    gates = _sigmoid(zs_ref[...] + gb_ref[...])
    for h in range(H_NSA):
        sl = slice(h * HD, (h + 1) * HD)
        c0 = H_FOX + 3 * h
        o = (gates[:, c0:c0 + 1] * oc_ref[:, sl] + gates[:, c0 + 1:c0 + 2] * os_ref[:, sl]
             + gates[:, c0 + 2:c0 + 3] * ow_ref[:, sl])
        o_ref[:, sl] = o.astype(BF16)


def _nsa_combine(oc, osl, ow, zs, gb_pad):
    m, n = oc.shape
    tm = min(m, 512)
    big = pl.BlockSpec((tm, n), lambda i: (i, 0))
    return pl.pallas_call(
        _combine_body, grid=(m // tm,),
        in_specs=[big, big, big, pl.BlockSpec((tm, HD), lambda i: (i, 0)), pl.BlockSpec((1, HD), lambda i: (0, 0))],
        out_specs=big, out_shape=jax.ShapeDtypeStruct((m, n), BF16),
        compiler_params=_cparams(("parallel",)),
    )(oc, osl, ow, zs, gb_pad)


def _paged_body(*refs, mode, pg, n_steps, n_w, hpg_kv, t_new):
    it = iter(refs)
    next(it)
    q_ref = next(it)
    k_refs = [next(it) for _ in range(pg)]
    v_refs = [next(it) for _ in range(pg)]
    cl_refs = [next(it) for _ in range(pg)] if mode == "fox" else None
    tf_refs = [next(it) for _ in range(pg)] if mode == "fox" else None
    kn_ref, vn_ref = next(it), next(it)
    lfn_ref = next(it) if mode == "fox" else None
    sel_ref = next(it) if mode == "slc" else None
    o_ref = next(it)
    qb, m_scr, l_scr, acc, kn_scr, vn_scr, carry = (next(it) for _ in range(7))
    p = pl.program_id(1)
    rows = 4 * 8
    r2 = n_w * PAGE
    n_new = t_new * n_w

    def kv_match(ncols):
        row = lax.broadcasted_iota(jnp.int32, (rows, ncols), 0)
        col = lax.broadcasted_iota(jnp.int32, (rows, ncols), 1)
        return row, col, (col % n_w) == (row % 8) // hpg_kv

    @pl.when(p == 0)
    def _():
        qb[...] = (q_ref[0] * ATTN_SCALE).astype(BF16)
        m_scr[...] = jnp.full(m_scr.shape, NEG, F32)
        l_scr[...] = jnp.zeros(l_scr.shape, F32)
        acc[...] = jnp.zeros(acc.shape, F32)
        carry[...] = jnp.zeros(carry.shape, F32)
        kn_scr[...] = jnp.zeros(kn_scr.shape, BF16)
        vn_scr[...] = jnp.zeros(vn_scr.shape, BF16)

    def picked(page_idx, ncols):
        nsp = sel_ref.shape[2]
        jb = lax.broadcasted_iota(jnp.int32, (nsp, ncols), 0)
        key = lax.broadcasted_iota(jnp.int32, (nsp, ncols), 1) // n_w
        expand = jnp.where(jb == page_idx * (PAGE // SEL_BLOCK) + key // SEL_BLOCK, 1.0, 0.0).astype(BF16)
        return _dot(sel_ref[0].astype(BF16), expand) > 0.5

    def update(s_list, vb_list):
        m_old = m_scr[...]
        m_new = m_old
        for s in s_list:
            m_new = jnp.maximum(m_new, jnp.max(s, axis=-1, keepdims=True))
        a = jnp.exp(m_old - m_new)
        l = a * l_scr[...]
        o = a * acc[...]
        for s, vb in zip(s_list, vb_list):
            pr = jnp.exp(s - m_new)
            l = l + jnp.sum(pr, axis=-1, keepdims=True)
            o = o + _dot(pr.astype(BF16), vb)
        m_scr[...] = m_new
        l_scr[...] = l
        acc[...] = o

    _, _, match = kv_match(r2)
    s_list, vb_list = [], []
    c_run = carry[...] if mode == "fox" else None
    for pp in range(pg):
        s = _dot_nt(qb[...], k_refs[pp][0].astype(BF16))
        valid = match
        if mode == "fox":
            s = s - (c_run + cl_refs[pp][0])
            c_run = c_run + tf_refs[pp][0]
        if mode == "slc":
            valid = valid & picked(p * pg + pp, r2)
        s_list.append(jnp.where(valid, s, NEG))
        vb_list.append(v_refs[pp][0].astype(BF16))
    if mode == "fox":
        carry[...] = c_run
    update(s_list, vb_list)

    @pl.when(p == n_steps - 1)
    def _():
        kn_scr[0:n_new, :] = kn_ref[0].astype(BF16)
        vn_scr[0:n_new, :] = vn_ref[0].astype(BF16)
        s = _dot_nt(qb[...], kn_scr[...])
        row, col, valid = kv_match(HD)
        valid = valid & (col // n_w <= row // 8) & (col < n_new)
        if mode == "fox":
            rr = lax.broadcasted_iota(jnp.int32, (HD, HD), 0)
            cc = lax.broadcasted_iota(jnp.int32, (HD, HD), 1)
            pre = jnp.where((rr % n_w == cc % n_w) & (rr // n_w <= cc // n_w), 1.0, 0.0).astype(BF16)
            c_new = _dot3_l(jnp.broadcast_to(lfn_ref[0], (8, HD)), pre)[0:1, :] + carry[:, 0:HD]
            s = s - c_new
        if mode == "slc":
            valid = valid & picked(n_steps * pg, HD)
        update([jnp.where(valid, s, NEG)], [vn_scr[...]])
        o_ref[0] = acc[...] / l_scr[...]


def _paged_attn(mode, q_rows, page_table, k_pool, v_pool, pool_off, k_new, v_new, *, hpg_kv, pg,
                c_local=None, c_total=None, lf_new=None, sel_rows=None):
    bsz = q_rows.shape[0]
    n_pages = page_table.shape[1]
    pg = min(pg, n_pages)
    n_steps = n_pages // pg
    r2 = k_pool.shape[1]
    n_w = r2 // PAGE
    n_new = k_new.shape[1]
    page = lambda pp: (lambda b, p, pt: (pool_off + pt[b, p * pg + pp], 0, 0))
    per_b = lambda b, p, pt: (b, 0, 0)
    in_specs = [pl.BlockSpec((1, 32, HD), per_b)]
    in_specs += [pl.BlockSpec((1, r2, HD), page(pp)) for pp in range(pg)] * 2
    args = [q_rows] + [k_pool] * pg + [v_pool] * pg
    if mode == "fox":
        in_specs += [pl.BlockSpec((1, 1, r2), page(pp)) for pp in range(pg)] * 2
        args += [c_local] * pg + [c_total] * pg
    in_specs += [pl.BlockSpec((1, n_new, HD), per_b)] * 2
    args += [k_new, v_new]
    if mode == "fox":
        in_specs.append(pl.BlockSpec((1, 1, HD), per_b))
        args.append(lf_new)
    if mode == "slc":
        in_specs.append(pl.BlockSpec((1, 32, sel_rows.shape[2]), per_b))
        args.append(sel_rows)
    grid_spec = pltpu.PrefetchScalarGridSpec(
        num_scalar_prefetch=1, grid=(bsz, n_steps), in_specs=in_specs,
        out_specs=pl.BlockSpec((1, 32, HD), per_b),
        scratch_shapes=[pltpu.VMEM((32, HD), BF16), pltpu.VMEM((32, 1), F32), pltpu.VMEM((32, 1), F32),
                        pltpu.VMEM((32, HD), F32), pltpu.VMEM((PAGE, HD), BF16), pltpu.VMEM((PAGE, HD), BF16),
                        pltpu.VMEM((1, r2), F32)])
    return pl.pallas_call(
        functools.partial(_paged_body, mode=mode, pg=pg, n_steps=n_steps, n_w=n_w, hpg_kv=hpg_kv, t_new=n_new // n_w),
        grid_spec=grid_spec, out_shape=jax.ShapeDtypeStruct((bsz, 32, HD), F32),
        compiler_params=_cparams(("parallel", "arbitrary")),
    )(page_table, *args)


def _pool_prefix_body(x_ref, cl_ref, ct_ref, w_scr, *, n_w):
    r2 = x_ref.shape[1]

    @pl.when(pl.program_id(0) == 0)
    def _():
        r = lax.broadcasted_iota(jnp.int32, (r2, 2 * r2), 0)
        c = lax.broadcasted_iota(jnp.int32, (r2, 2 * r2), 1)
        same = (r % n_w) == (c % n_w)
        local = same & (r // n_w <= c // n_w) & (c < r2)
        w_scr[...] = jnp.where(local | (same & (c >= r2)), 1.0, 0.0).astype(BF16)

    y = _dot3_l(x_ref[...], w_scr[...])
    cl_ref[...] = y[:, :r2]
    ct_ref[...] = y[:, r2:]


def _pool_prefix(x, n_w):
    n, r2 = x.shape
    tm = next((c for c in (256, 128, 64, 32, 16, 8) if n % c == 0), n)
    spec = pl.BlockSpec((tm, r2), lambda i: (i, 0))
    return pl.pallas_call(
        functools.partial(_pool_prefix_body, n_w=n_w), grid=(n // tm,),
        in_specs=[spec], out_specs=[spec, spec],
        out_shape=[jax.ShapeDtypeStruct((n, r2), F32)] * 2,
        scratch_shapes=[pltpu.VMEM((r2, 2 * r2), BF16)],
        compiler_params=_cparams(("arbitrary",)),
    )(x)


def _gla_body(*refs, mode, t_in, chunk, dv, t_real, has_state, hb):
    it = iter(refs)
    q_ref, k_ref, v_ref, og_ref = next(it), next(it), next(it), next(it)
    if mode == "gla":
        zs_ref, wa_ref, ba_ref = next(it), next(it), next(it)
    else:
        lb_ref = next(it)
    gain_ref = next(it)
    s0_ref = next(it) if has_state else None
    o_ref, s_ref, st_scr = next(it), next(it), next(it)
    n_chunks = max(t_in // chunk, 1)
    n_rows = min(chunk, t_in)
    pad = chunk - n_rows
    tri = lax.broadcasted_iota(jnp.int32, (chunk, chunk), 0) >= lax.broadcasted_iota(jnp.int32, (chunk, chunk), 1)
    ltri = jnp.where(tri, 1.0, 0.0).astype(BF16)
    if mode == "hg":
        lbr = lb_ref[...]
        e = jnp.exp(lbr - jnp.max(lbr, axis=0, keepdims=True))
        lb_all = (e / jnp.sum(e, axis=0, keepdims=True))[0:1, :]
    ti = pl.program_id(2)

    @pl.when(ti == 0)
    def _():
        for h in range(hb):
            st_scr[h] = s0_ref[0, h].T if has_state else jnp.zeros((dv, HD), F32)

    def load(ref, r0, sl):
        x = ref[0, pl.ds(r0, n_rows), sl]
        if pad:
            x = jnp.concatenate([x, jnp.zeros((pad, x.shape[1]), F32)], axis=0)
        return x

    def body(c, _):
        r0 = pl.multiple_of(c * chunk, 8)
        live = (ti * t_in + r0 + lax.broadcasted_iota(jnp.int32, (chunk, HD), 0)) < t_real
        if mode == "gla":
            ga = load(zs_ref, r0, slice(0, HD)).astype(BF16)
        for h in range(hb):
            ks, vs = slice(h * HD, (h + 1) * HD), slice(h * dv, (h + 1) * dv)
            qr, kr, v, og = load(q_ref, r0, ks), load(k_ref, r0, ks), load(v_ref, r0, vs), load(og_ref, r0, vs)
            if mode == "gla":
                g = _log_sigmoid(_dot(ga, wa_ref[:, ks]) + ba_ref[:, ks]) / GLA_GATE_NORM
                q, k = qr, kr
            else:
                lb = lb_all[:, ks]
                f = lb + (1.0 - lb) * _sigmoid(kr)
                q, k, g = qr * _sigmoid(qr), 1.0 - f, jnp.log(f)
            g = jnp.where(live, g, 0.0)
            k = jnp.where(live, k, 0.0)
            b = _dot3_r(ltri, g)
            bm = b[chunk // 2 - 1:chunk // 2, :]
            bl = b[chunk - 1:chunk, :]
            qe = (q * jnp.exp(jnp.minimum(b - bm, EXP_CLAMP))).astype(BF16)
            ke = (k * jnp.exp(jnp.minimum(bm - b, EXP_CLAMP))).astype(BF16)
            a = jnp.where(tri, _dot_nt(qe, ke), 0.0)
            vb = v.astype(BF16)
            st = st_scr[h]
            o = _dot(a.astype(BF16), vb) + _dot_nt((q * jnp.exp(b)).astype(BF16), st.astype(BF16))
            kd = (k * jnp.exp(bl - b)).astype(BF16)
            st_scr[h] = st * jnp.exp(bl) + _dot_tn(vb, kd)
            o = o * lax.rsqrt(jnp.mean(o * o, axis=-1, keepdims=True) + EPS) * gain_ref[...]
            o = o * (og * _sigmoid(og) if mode == "gla" else _sigmoid(og))
            o_ref[0, pl.ds(r0, n_rows), vs] = o[:n_rows].astype(BF16)
        return 0

    lax.fori_loop(0, n_chunks, body, 0)

    @pl.when(ti == pl.num_programs(2) - 1)
    def _():
        for h in range(hb):
            s_ref[0, h] = st_scr[h].T


def _gla(mode, z3, zs3, cols, n_heads, dv, extra, gain, s0, *, t_real, chunk, hb):
    bsz, t_all, _ = z3.shape
    tc = min(t_all, 512)
    qc, kc, vc, oc = cols
    has_state = s0 is not None
    wk, wv = hb * HD, hb * dv
    in_specs = [pl.BlockSpec((1, tc, wk), lambda b, h, t: (b, t, qc // wk + h)),
                pl.BlockSpec((1, tc, wk), lambda b, h, t: (b, t, kc // wk + h)),
                pl.BlockSpec((1, tc, wv), lambda b, h, t: (b, t, vc // wv + h)),
                pl.BlockSpec((1, tc, wv), lambda b, h, t: (b, t, oc // wv + h))]
    args = [z3, z3, z3, z3]
    if mode == "gla":
        wa, ba = extra
        in_specs += [pl.BlockSpec((1, tc, HD), lambda b, h, t: (b, t, 0)),
                     pl.BlockSpec((HD, wk), lambda b, h, t: (0, h)),
                     pl.BlockSpec((1, wk), lambda b, h, t: (0, h))]
        args += [zs3, wa, ba]
    else:
        in_specs.append(pl.BlockSpec((8, wk), lambda b, h, t: (0, h)))
        args.append(extra)
    in_specs.append(pl.BlockSpec((1, dv), lambda b, h, t: (0, 0)))
    args.append(gain)
    st_spec = pl.BlockSpec((1, hb, HD, dv), lambda b, h, t: (b, h, 0, 0))
    if has_state:
        in_specs.append(st_spec)
        args.append(s0)
    return pl.pallas_call(
        functools.partial(_gla_body, mode=mode, t_in=tc, chunk=chunk, dv=dv, t_real=t_real, has_state=has_state,
                          hb=hb),
        grid=(bsz, n_heads // hb, t_all // tc),
        in_specs=in_specs,
        out_specs=[pl.BlockSpec((1, tc, wv), lambda b, h, t: (b, t, h)), st_spec],
        out_shape=[jax.ShapeDtypeStruct((bsz, t_all, n_heads * dv), BF16),
                   jax.ShapeDtypeStruct((bsz, n_heads, HD, dv), F32)],
        scratch_shapes=[pltpu.VMEM((hb, dv, HD), F32)],
        compiler_params=_cparams(("parallel", "parallel", "arbitrary")),
    )(*args)


def _split_cols(w, sizes):
    outs, off = [], 0
    for s in sizes:
        outs.append(w[:, off:off + s])
        off += s
    return outs


def _pad_cols(w, n):
    return jnp.pad(w, ((0, 0), (0, n - w.shape[1])))


def _row(v):
    return v.reshape(1, -1).astype(F32)


def _even_weights(w_in, fq_gain, fk_gain, nq_gain, nk_gain):
    fq, fk, fv, ff, nq, kc, vc, ks, vs, kw, vw, ng = _split_cols(w_in, EVEN_SIZES)
    big = jnp.concatenate([fq, fk, fv, nq, kc, vc, ks, vs, kw, vw], axis=1).astype(BF16)
    small = _pad_cols(jnp.concatenate([ff, ng], axis=1), HD).astype(BF16)
    ones = lambda n: jnp.ones((n,), F32)
    gain = jnp.concatenate([jnp.tile(fq_gain, H_FOX), jnp.tile(fk_gain, H_FOX), ones(H_FOX * HD),
                            jnp.tile(nq_gain, H_NSA), ones(2 * G_NSA * HD), jnp.tile(nk_gain[1], G_NSA),
                            ones(G_NSA * HD), jnp.tile(nk_gain[2], G_NSA), ones(G_NSA * HD)])
    z, o = jnp.zeros, jnp.ones
    flag = jnp.concatenate([o((2 * H_FOX * HD,), F32), z((H_FOX * HD,), F32), o((H_NSA * HD,), F32),
                            z((2 * G_NSA * HD,), F32), o((G_NSA * HD,), F32), z((G_NSA * HD,), F32),
                            o((G_NSA * HD,), F32), z((G_NSA * HD,), F32)])
    return big, small, _row(gain), _row(flag)


def _odd_weights(w_in):
    gq, gk, gv, ga, gg, hq, hf, hi, hg = _split_cols(w_in, ODD_SIZES)
    big = jnp.concatenate([gq, gk, gv, gg, hq, hf, hi, hg], axis=1).astype(BF16)
    small = _pad_cols(ga, HD).astype(BF16)
    gain = jnp.concatenate([jnp.full((H_GLA * DK_GLA,), DK_GLA ** -0.5, F32), jnp.ones((N_BIG_ODD - H_GLA * DK_GLA,), F32)])
    return big, small, _row(gain), jnp.zeros((1, N_BIG_ODD), F32)


def _cmp_weights(pe, w):
    half = CMP_STRIDE * HD
    wcat = jnp.concatenate([w[:half], w[half:]], axis=1).astype(BF16)
    pe8 = jnp.pad(pe.reshape(2, half), ((0, 6), (0, 0)))
    return pe8, wcat


def _ffn(x, g, w1, w3, w2, layer):
    return _mm_res([_ffn_up(x, g, w1, w3, layer)], w2, layer, x)


def _even_layer(x, bsz, t, t_real, ew, past):
    (w_big, w_small, cgain, cflag, g_mix, w_out, b_f_pad, gb_pad, pe_k, wc_k, pe_v, wc_v, nk0) = ew
    z, zs = _proj(x, g_mix, w_big, w_small, cgain, cflag, ((FQ, FV), (NQ, KC), (KS, VS), (KW, VW)))
    z3 = z.reshape(bsz, t, N_BIG_EVEN)
    zs3 = zs.reshape(bsz, t, HD)
    one = jnp.ones((1, HD), F32)
    if past is None:
        lf, c = _fox_prep(zs3, b_f_pad)
        logf_new = lf[:, :, :H_FOX]
        tk = min(t, 512)
        bias = jnp.swapaxes(c[:, :, :H_FOX], 1, 2).reshape(bsz * H_FOX, t // tk, tk)
        o_fox = _flash("fox", z3, FQ, z3, FK, z3, FV, n_kv=H_FOX, hpg=1, tq=min(t, 256), tk=tk, bias=bias,
                       out_dtype=BF16)
        n_cmp = t // CMP_STRIDE
        mn = -(-n_cmp // HD) * HD
        kcmp = _compress(z3, KC, pe_k, wc_k, nk0, mn=mn, norm=True)
        vcmp = _compress(z3, VC, pe_v, wc_v, one, mn=mn, norm=False)
        tqn = min(t, 256 // HPG)
        o_cmp, sel = _cmpsel(z3, kcmp, vcmp, tq=min(t, 128), pos0=0, n_sel=-(-t // SEL_BLOCK))
        o_slc = _flash("slc", z3, NQ, z3, KS, z3, VS, n_kv=G_NSA, hpg=HPG, tq=tqn, tk=tk, sel=sel)
        o_win = _flash("win", z3, NQ, z3, KW, z3, VW, n_kv=G_NSA, hpg=HPG, tq=tqn, tk=tk)
        w_buf = WINDOW
        kw_new = jnp.concatenate([jnp.zeros((bsz, w_buf, G_NSA * HD), F32), z3[:, :, KW:KW + G_NSA * HD]], axis=1)[:, -w_buf:]
        vw_new = jnp.concatenate([jnp.zeros((bsz, w_buf, G_NSA * HD), F32), z3[:, :, VW:VW + G_NSA * HD]], axis=1)[:, -w_buf:]
        o_fox = o_fox.reshape(bsz * t, H_FOX * HD)
    else:
        (page_table, pool_off, fk_pool, fv_pool, lf_pool, kc_pool, vc_pool, ks_pool, vs_pool, buf_k, buf_v) = past
        p_len = page_table.shape[1] * PAGE
        lf_all, _ = _fox_prep(zs.reshape(1, bsz * t, HD), b_f_pad)
        lf_new = lf_all.reshape(bsz, t, HD)[:, :t_real, :H_FOX]
        logf_new = lf_new
        lfn_pad = _pad_cols(lf_new.reshape(bsz, t_real * H_FOX), HD).reshape(bsz, 1, HD)
        c_local, c_total = _pool_prefix(lf_pool, H_FOX)
        rows_of = lambda c0, n: z3[:, :t_real, c0:c0 + n * HD].reshape(bsz, t_real * n, HD)
        o_fox = _paged_attn("fox", rows_of(FQ, H_FOX), page_table, fk_pool, fv_pool, pool_off,
                            rows_of(FK, H_FOX), rows_of(FV, H_FOX), hpg_kv=1, pg=8,
                            c_local=c_local[:, None, :], c_total=c_total[:, None, :], lf_new=lfn_pad)
        o_fox = o_fox.reshape(bsz, t_real, H_FOX * HD)
        n_cmp = -(-(p_len + t_real) // CMP_STRIDE)
        mn = -(-n_cmp // HD) * HD
        kcmp = _compress(z3, KC, pe_k, wc_k, nk0, mn=mn, norm=True, page_table=page_table, pool=kc_pool,
                         pool_off=pool_off, t_new=t_real)
        vcmp = _compress(z3, VC, pe_v, wc_v, one, mn=mn, norm=False, page_table=page_table, pool=vc_pool,
                         pool_off=pool_off, t_new=t_real)
        n_sel = -(-(p_len + t_real) // SEL_BLOCK)
        o_cmp, sel = _cmpsel(z3, kcmp, vcmp, tq=t, pos0=p_len, n_sel=n_sel)
        nsp = sel.shape[-1]
        sel_rows = jnp.broadcast_to(jnp.swapaxes(sel[:, :, :t_real], 1, 2)[:, :, :, None, :],
                                    (bsz, t_real, G_NSA, HPG, nsp)).reshape(bsz, 32, nsp)
        o_slc = _paged_attn("slc", rows_of(NQ, H_NSA), page_table, ks_pool, vs_pool, pool_off,
                            rows_of(KS, G_NSA), rows_of(VS, G_NSA), hpg_kv=HPG, pg=16,
                            sel_rows=sel_rows).reshape(bsz, t_real, H_NSA * HD)
        kw_all = jnp.concatenate([buf_k, z3[:, :t_real, KW:KW + G_NSA * HD]], axis=1)
        vw_all = jnp.concatenate([buf_v, z3[:, :t_real, VW:VW + G_NSA * HD]], axis=1)
        w_buf = buf_k.shape[1]
        kw_new, vw_new = kw_all[:, -w_buf:], vw_all[:, -w_buf:]
        lk = -(-(w_buf + t) // HD) * HD
        padk = lambda a: jnp.pad(a, ((0, 0), (0, lk - a.shape[1]), (0, 0)))
        o_win = _flash("win", z3, NQ, padk(kw_all), 0, padk(vw_all), 0, n_kv=G_NSA, hpg=HPG, tq=t, tk=HD, off=w_buf)
        padt = lambda a: jnp.pad(a, ((0, 0), (0, t - t_real), (0, 0)))
        o_fox = padt(o_fox).astype(BF16).reshape(bsz * t, H_FOX * HD)
        o_slc = padt(o_slc)
    m = bsz * t
    o_nsa = _nsa_combine(o_cmp.reshape(m, -1), o_slc.reshape(m, -1), o_win.reshape(m, -1), zs, gb_pad)
    x = _mm_res([o_fox, o_nsa], w_out, 0, x)
    zr = z3[:, :t_real]
    grp = lambda c0: zr[:, :, c0:c0 + G_NSA * HD].reshape(1, bsz, t_real, G_NSA, HD)
    fox = lambda c0: zr[:, :, c0:c0 + H_FOX * HD].reshape(1, bsz, t_real, H_FOX, HD)
    w_rows = kw_new.shape[1]
    outs = (fox(FK), fox(FV), logf_new[None], grp(KC), grp(VC), grp(KS), grp(VS),
            kw_new.reshape(1, bsz, w_rows, G_NSA, HD), vw_new.reshape(1, bsz, w_rows, G_NSA, HD))
    return x, outs


def _odd_layer(x, bsz, t, t_real, ow, state):
    (w_big, w_small, cgain, cflag, g_mix, w_out, wa_pad, ba, gla_gain, lb_pad, hg_gain) = ow
    z, zs = _proj(x, g_mix, w_big, w_small, cgain, cflag, ())
    z3 = z.reshape(bsz, t, N_BIG_ODD)
    zs3 = zs.reshape(bsz, t, HD)
    s_gla, s_hg = state if state is not None else (None, None)
    chunk = 64 if t >= 64 else 128
    o_gla, s_gla = _gla("gla", z3, zs3, (GQ, GK, GV, GG), H_GLA, DV_GLA, (wa_pad, ba), gla_gain, s_gla,
                        t_real=t_real, chunk=chunk, hb=H_GLA)
    o_hg, s_hg = _gla("hg", z3, zs3, (HQ, HF, HI, HO), H_HG, DV_HG, lb_pad, hg_gain, s_hg,
                      t_real=t_real, chunk=chunk, hb=H_HG)
    m = bsz * t
    x = _mm_res([o_gla.reshape(m, -1), o_hg.reshape(m, -1)], w_out, 0, x)
    return x, (s_gla[None], s_hg[None])


def kernel(x_prompt, x_sample, cache_fox_k, cache_fox_v, cache_fox_logf, cache_nsa_kc, cache_nsa_vc, cache_nsa_ks, cache_nsa_vs, state_nsa_kw, state_nsa_vw, state_gla, state_hgrn, page_table, norm_mix, norm_ffn, w_in_even, w_out_even, fox_b_f, fox_q_gain, fox_k_gain, nsa_q_gain, nsa_k_gain, nsa_cmp_pe_k, nsa_cmp_pe_v, nsa_cmp_wk, nsa_cmp_wv, nsa_gate_b, w_in_odd, w_out_odd, gla_wa2, gla_ba, gla_norm, hgrn_lb, hgrn_norm, ffn_w1, ffn_w3, ffn_w2):
    bp, tp, d = x_prompt.shape
    bs, ts, _ = x_sample.shape
    n_pool = cache_fox_k.shape[1]
    ts_pad = 8
    pe_k, wc_k = _cmp_weights(nsa_cmp_pe_k[0], nsa_cmp_wk[0])
    pe_v, wc_v = _cmp_weights(nsa_cmp_pe_v[0], nsa_cmp_wv[0])
    b_f_pad = _pad_cols(_row(fox_b_f[0]), HD)
    gb_pad = _pad_cols(jnp.concatenate([jnp.zeros((1, H_FOX), F32), _row(nsa_gate_b[0])], axis=1), HD)
    ew = _even_weights(w_in_even[0], fox_q_gain[0], fox_k_gain[0], nsa_q_gain[0], nsa_k_gain[0]) + (
        _row(norm_mix[0]), w_out_even, b_f_pad, gb_pad, pe_k, wc_k, pe_v, wc_v, _row(nsa_k_gain[0, 0]))
    wa_pad = jnp.pad(gla_wa2[0], ((0, HD - GLA_RANK), (0, 0))).astype(BF16)
    lb_pad = jnp.pad(hgrn_lb.astype(F32), ((0, 8 - hgrn_lb.shape[0]), (0, 0)), constant_values=-1e30)
    ow = _odd_weights(w_in_odd[0]) + (_row(norm_mix[1]), w_out_odd, wa_pad, _row(gla_ba[0]),
                                      _row(gla_norm[0]), lb_pad, _row(hgrn_norm[0]))
    ffn = [(_row(norm_ffn[i]), ffn_w1, ffn_w3, ffn_w2, i) for i in range(2)]

    flat = lambda c: c.reshape((c.shape[0] * c.shape[1], PAGE * c.shape[3], HD))
    lf_pool = cache_fox_logf.reshape(-1, PAGE * H_FOX)
    win = lambda s: s.reshape(bs, s.shape[2], G_NSA * HD)
    past = (page_table, 0, flat(cache_fox_k), flat(cache_fox_v), lf_pool, flat(cache_nsa_kc), flat(cache_nsa_vc),
            flat(cache_nsa_ks), flat(cache_nsa_vs), win(state_nsa_kw), win(state_nsa_vw))

    def run(x, bsz, t, t_real, past_, state_):
        x, ev = _even_layer(x, bsz, t, t_real, ew, past_)
        x = _ffn(x, *ffn[0])
        x, od = _odd_layer(x, bsz, t, t_real, ow, state_)
        x = _ffn(x, *ffn[1])
        return x, ev, od

    yp, ev_p, od_p = run(x_prompt.reshape(bp * tp, d), bp, tp, tp, None, None)
    xs = jnp.pad(x_sample, ((0, 0), (0, ts_pad - ts), (0, 0))).reshape(bs * ts_pad, d)
    ys, ev_s, od_s = run(xs, bs, ts_pad, ts, past, (state_gla.reshape(state_gla.shape[1:]),
                                                    state_hgrn.reshape(state_hgrn.shape[1:])))
    y_prompt = yp.reshape(bp, tp, d)
    y_sample = ys.reshape(bs, ts_pad, d)[:, :ts]
    outs = [y_prompt, y_sample]
    for a, b in zip(ev_p, ev_s):
        outs += [a, b]
    for a, b in zip(od_p, od_s):
        outs += [a, b]
    return tuple(outs)
```

```python
import functools

import jax
import jax.numpy as jnp
from jax import lax
from jax.experimental import pallas as pl
from jax.experimental.pallas import tpu as pltpu

F32 = jnp.float32
BF16 = jnp.bfloat16

D_MODEL = 2048
HD = 128
H_FOX = 8
H_NSA = 8
G_NSA = 2
HPG = H_NSA // G_NSA
CMP_STRIDE = 16
CMP_BLOCK = 2 * CMP_STRIDE
SEL_BLOCK = 64
N_SELECT = 16
WINDOW = 512
FORCE_SCORE = 1.0e4
H_GLA = 4
DK_GLA = 128
DV_GLA = 256
GLA_RANK = 16
GLA_GATE_NORM = 16.0
H_HG = 8
DK_HG = 128
DV_HG = 128
PAGE = 128
EVEN_SIZES = (H_FOX * HD, H_FOX * HD, H_FOX * HD, H_FOX, H_NSA * HD,
              G_NSA * HD, G_NSA * HD, G_NSA * HD, G_NSA * HD, G_NSA * HD, G_NSA * HD, 3 * H_NSA)
ODD_SIZES = (H_GLA * DK_GLA, H_GLA * DK_GLA, H_GLA * DV_GLA, GLA_RANK, H_GLA * DV_GLA,
             H_HG * DK_HG, H_HG * DK_HG, H_HG * DV_HG, H_HG * DV_HG)
FQ, FK, FV, NQ, KC, VC, KS, VS, KW, VW = 0, 1024, 2048, 3072, 4096, 4352, 4608, 4864, 5120, 5376
N_BIG_EVEN = 5632
GQ, GK, GV, GG, HQ, HF, HI, HO = 0, 512, 1024, 2048, 3072, 4096, 5120, 6144
N_BIG_ODD = 7168
ATTN_SCALE = HD ** -0.5
NEG = -1.0e30
EPS = 1.0e-6
EXP_CLAMP = 80.0
VMEM_LIMIT = 56 * 1024 * 1024


def _cparams(sem):
    return pltpu.CompilerParams(dimension_semantics=sem, vmem_limit_bytes=VMEM_LIMIT)


def _dot(a, b):
    return jnp.dot(a, b, preferred_element_type=F32)


def _dot_nt(a, b):
    return lax.dot_general(a, b, (((1,), (1,)), ((), ())), preferred_element_type=F32)


def _dot_tn(a, b):
    return lax.dot_general(a, b, (((0,), (0,)), ((), ())), preferred_element_type=F32)


def _split3(x):
    hi = x.astype(BF16)
    r = x - hi.astype(F32)
    mid = r.astype(BF16)
    lo = (r - mid.astype(F32)).astype(BF16)
    return hi, mid, lo


def _dot3_l(x, w):
    hi, mid, lo = _split3(x)
    return _dot(hi, w) + _dot(mid, w) + _dot(lo, w)


def _dot3_r(w, x):
    hi, mid, lo = _split3(x)
    return _dot(w, hi) + _dot(w, mid) + _dot(w, lo)


def _sigmoid(x):
    return 1.0 / (1.0 + jnp.exp(-x))


def _log_sigmoid(x):
    return jnp.minimum(x, 0.0) - jnp.log1p(jnp.exp(-jnp.abs(x)))


def _tri(n, upper):
    r = lax.broadcasted_iota(jnp.int32, (n, n), 0)
    c = lax.broadcasted_iota(jnp.int32, (n, n), 1)
    return jnp.where((r <= c) if upper else (r >= c), 1.0, 0.0).astype(BF16)


def _proj_body(x_ref, g_ref, w_ref, ws_ref, cg_ref, cf_ref, z_ref, zs_ref, h_scr, *, tn, has_norm):
    @pl.when(pl.program_id(1) == 0)
    def _():
        x = x_ref[...]
        r = lax.rsqrt(jnp.mean(x * x, axis=-1, keepdims=True) + EPS)
        h = (x * r * g_ref[...]).astype(BF16)
        h_scr[...] = h
        zs_ref[...] = _dot(h, ws_ref[...])

    z = _dot(h_scr[...], w_ref[...])
    if not has_norm:
        z_ref[...] = z * cg_ref[...]
        return
    for c in range(tn // HD):
        sl = slice(c * HD, (c + 1) * HD)
        zc = z[:, sl]
        r = lax.rsqrt(jnp.mean(zc * zc, axis=-1, keepdims=True) + EPS)
        f = cf_ref[:, sl]
        z_ref[:, sl] = zc * (f * r + (1.0 - f)) * cg_ref[:, sl]


def _proj(x, g, w_big, w_small, col_gain, col_flag, has_norm):
    m, d = x.shape
    n = w_big.shape[1]
    tm = min(m, 1024)
    tn = 512
    return pl.pallas_call(
        functools.partial(_proj_body, tn=tn, has_norm=has_norm),
        grid=(m // tm, n // tn),
        in_specs=[pl.BlockSpec((tm, d), lambda i, j: (i, 0)),
                  pl.BlockSpec((1, d), lambda i, j: (0, 0)),
                  pl.BlockSpec((d, tn), lambda i, j: (0, j)),
                  pl.BlockSpec((d, HD), lambda i, j: (0, 0)),
                  pl.BlockSpec((1, tn), lambda i, j: (0, j)),
                  pl.BlockSpec((1, tn), lambda i, j: (0, j))],
        out_specs=[pl.BlockSpec((tm, tn), lambda i, j: (i, j)),
                   pl.BlockSpec((tm, HD), lambda i, j: (i, 0))],
        out_shape=[jax.ShapeDtypeStruct((m, n), F32), jax.ShapeDtypeStruct((m, HD), F32)],
        scratch_shapes=[pltpu.VMEM((tm, d), BF16)],
        compiler_params=_cparams(("parallel", "arbitrary")),
    )(x, g, w_big, w_small, col_gain, col_flag)


def _ffn_up_body(x_ref, g_ref, w1_ref, w3_ref, o_ref, h_scr):
    @pl.when(pl.program_id(1) == 0)
    def _():
        x = x_ref[...]
        r = lax.rsqrt(jnp.mean(x * x, axis=-1, keepdims=True) + EPS)
        h_scr[...] = (x * r * g_ref[...]).astype(BF16)

    h = h_scr[...]
    a = _dot(h, w1_ref[...].astype(BF16))
    b = _dot(h, w3_ref[...].astype(BF16))
    o_ref[...] = (a * _sigmoid(a) * b).astype(BF16)


def _ffn_up(x, g, w1, w3, layer):
    m, d = x.shape
    n = w1.shape[2]
    tm = min(m, 1024)
    tn = 512
    return pl.pallas_call(
        _ffn_up_body,
        grid=(m // tm, n // tn),
        in_specs=[pl.BlockSpec((tm, d), lambda i, j: (i, 0)),
                  pl.BlockSpec((1, d), lambda i, j: (0, 0)),
                  pl.BlockSpec((None, d, tn), lambda i, j: (layer, 0, j)),
                  pl.BlockSpec((None, d, tn), lambda i, j: (layer, 0, j))],
        out_specs=pl.BlockSpec((tm, tn), lambda i, j: (i, j)),
        out_shape=jax.ShapeDtypeStruct((m, n), BF16),
        scratch_shapes=[pltpu.VMEM((tm, d), BF16)],
        compiler_params=_cparams(("parallel", "arbitrary")),
    )(x, g, w1, w3)


def _mm_res_body(*refs, n_in):
    res_ref = refs[2 * n_in]
    o_ref = refs[2 * n_in + 1]
    wb_refs = refs[2 * n_in + 2:]

    @pl.when(pl.program_id(1) == 0)
    def _():
        for w_ref, wb_ref in zip(refs[n_in:2 * n_in], wb_refs):
            wb_ref[...] = w_ref[...].astype(BF16)

    acc = res_ref[...]
    for a_ref, wb_ref in zip(refs[:n_in], wb_refs):
        acc = acc + _dot(a_ref[...], wb_ref[...])
    o_ref[...] = acc


def _mm_res(a_list, w, layer, res):
    m, n = res.shape
    tm = min(m, 512)
    tn = 512
    in_specs, w_specs, scratch, off = [], [], [], 0
    for a in a_list:
        k = a.shape[1]
        in_specs.append(pl.BlockSpec((tm, k), lambda j, i: (i, 0)))
        w_specs.append(pl.BlockSpec((None, k, tn), lambda j, i, _o=off // k: (layer, _o, j)))
        scratch.append(pltpu.VMEM((k, tn), BF16))
        off += k
    return pl.pallas_call(
        functools.partial(_mm_res_body, n_in=len(a_list)),
        grid=(n // tn, m // tm),
        in_specs=in_specs + w_specs + [pl.BlockSpec((tm, tn), lambda j, i: (i, j))],
        out_specs=pl.BlockSpec((tm, tn), lambda j, i: (i, j)),
        out_shape=jax.ShapeDtypeStruct((m, n), F32),
        scratch_shapes=scratch,
        compiler_params=_cparams(("parallel", "arbitrary")),
    )(*a_list, *([w] * len(a_list)), res)


def _fox_prep_body(zs_ref, b_ref, lf_ref, c_ref, carry, *, tc):
    @pl.when(pl.program_id(1) == 0)
    def _():
        carry[...] = jnp.zeros_like(carry)

    lf = _log_sigmoid(zs_ref[0] + b_ref[...])
    c = _dot3_r(_tri(tc, upper=False), lf) + carry[...]
    lf_ref[0] = lf
    c_ref[0] = c
    carry[...] = c[tc - 1:tc, :]


def _fox_prep(zs3, b_pad):
    bsz, t, _ = zs3.shape
    tc = min(t, 256)
    spec = pl.BlockSpec((1, tc, HD), lambda b, i: (b, i, 0))
    return pl.pallas_call(
        functools.partial(_fox_prep_body, tc=tc),
        grid=(bsz, t // tc),
        in_specs=[spec, pl.BlockSpec((1, HD), lambda b, i: (0, 0))],
        out_specs=[spec, spec],
        out_shape=[jax.ShapeDtypeStruct(zs3.shape, F32)] * 2,
        scratch_shapes=[pltpu.VMEM((1, HD), F32)],
        compiler_params=_cparams(("parallel", "arbitrary")),
    )(zs3, b_pad)


def _flash_body(*refs, mode, tq, tk, hpg, off):
    it = iter(refs)
    q_ref, k_ref, v_ref = next(it), next(it), next(it)
    bias_ref = next(it) if mode == "fox" else None
    sel_ref = next(it) if mode == "slc" else None
    o_ref, kb, vb = next(it), next(it), next(it)
    i = pl.program_id(2)

    @pl.when(i == 0)
    def _():
        kb[...] = k_ref[0].astype(BF16)
        vb[...] = v_ref[0].astype(BF16)

    q = q_ref[0]
    if hpg > 1:
        q = jnp.concatenate([q[:, h * HD:(h + 1) * HD] for h in range(hpg)], axis=0)
    qa = (q * ATTN_SCALE).astype(BF16)
    tile_rows = (lambda x: jnp.concatenate([x] * hpg, axis=0)) if hpg > 1 else (lambda x: x)
    q_lo = off + i * tq
    t_k = kb.shape[0]
    if mode == "slc":
        unpicked = (sel_ref[0, 0] - 1.0).astype(BF16)

    def attend(blocks):
        ss = []
        for b, (ks, size, masked) in enumerate(blocks):
            s = _dot_nt(qa, kb[pl.ds(ks, size), :])
            if mode == "fox":
                s = s - bias_ref[0, b:b + 1, :]
            if mode == "slc":
                cb = lax.broadcasted_iota(jnp.int32, (HD, size), 0)
                kk = lax.broadcasted_iota(jnp.int32, (HD, size), 1)
                expand = jnp.where(cb == (ks + kk) // SEL_BLOCK, 2.0 ** 100, 0.0).astype(BF16)
                s = s + tile_rows(_dot(unpicked, expand))
            if masked:
                kidx = ks + lax.broadcasted_iota(jnp.int32, (tq, size), 1)
                qpos = q_lo + lax.broadcasted_iota(jnp.int32, (tq, size), 0)
                valid = kidx <= qpos
                if mode == "win":
                    valid = valid & ((qpos - kidx) < WINDOW)
                s = jnp.where(tile_rows(valid), s, NEG)
            ss.append(s)
        m = ss[0].max(axis=-1, keepdims=True)
        for s in ss[1:]:
            m = jnp.maximum(m, s.max(axis=-1, keepdims=True))
        l, acc = 0.0, 0.0
        for s, (ks, size, _) in zip(ss, blocks):
            p = jnp.exp(s - m)
            l = l + jnp.sum(p, axis=-1, keepdims=True)
            acc = acc + _dot(p.astype(BF16), vb[pl.ds(ks, size), :])
        o = acc / l
        for h in range(hpg):
            o_ref[0, :, h * HD:(h + 1) * HD] = o[h * tq:(h + 1) * tq].astype(o_ref.dtype)

    if mode == "win":
        span = min(t_k, -(-(WINDOW + tq) // HD) * HD + HD)
        start = jnp.clip((q_lo - (WINDOW - 1)) // HD * HD, 0, t_k - span)
        attend([(pl.multiple_of(start, HD), span, True)])
    else:
        n_max = t_k // tk
        need = (q_lo + tq - 1) // tk + 1
        for nb in range(1, n_max + 1):
            @pl.when(need == nb)
            def _(nb=nb):
                attend([(b * tk, tk, b == nb - 1) for b in range(nb)])


def _flash(mode, q_arr, q_col, k_arr, k_col, v_arr, v_col, *, n_kv, hpg, tq, tk, off=0,
           bias=None, sel=None, out_dtype=F32):
    bsz, t_q = q_arr.shape[0], q_arr.shape[1]
    t_k = k_arr.shape[1]
    nq = t_q // tq
    w = hpg * HD
    in_specs = [pl.BlockSpec((1, tq, w), lambda b, g, i: (b, i, q_col // w + g)),
                pl.BlockSpec((1, t_k, HD), lambda b, g, i: (b, 0, k_col // HD + g)),
                pl.BlockSpec((1, t_k, HD), lambda b, g, i: (b, 0, v_col // HD + g))]
    args = [q_arr, k_arr, v_arr]
    if mode == "fox":
        in_specs.append(pl.BlockSpec((1, t_k // tk, tk), lambda b, g, i: (b * n_kv + g, 0, 0)))
        args.append(bias)
    if mode == "slc":
        in_specs.append(pl.BlockSpec((1, 1, tq, HD), lambda b, g, i: (b, g, i, 0)))
        args.append(sel)
    return pl.pallas_call(
        functools.partial(_flash_body, mode=mode, tq=tq, tk=tk, hpg=hpg, off=off),
        grid=(bsz, n_kv, nq),
        in_specs=in_specs,
        out_specs=pl.BlockSpec((1, tq, w), lambda b, g, i: (b, i, g)),
        out_shape=jax.ShapeDtypeStruct((bsz, t_q, n_kv * w), out_dtype),
        scratch_shapes=[pltpu.VMEM((t_k, HD), BF16), pltpu.VMEM((t_k, HD), BF16)],
        compiler_params=_cparams(("parallel", "parallel", "arbitrary")),
    )(*args)


def _compress_body(*refs, paged, pg, n_steps, mn, norms, t_new):
    it = iter(refs)
    if paged:
        next(it)
    n_t = len(norms)
    x_all = [[next(it) for _ in range(pg if paged else G_NSA)] for _ in range(n_t)]
    new_all = [next(it) for _ in range(n_t)] if paged else None
    par_all = [(next(it), next(it), next(it)) for _ in range(n_t)]
    o_all = [next(it) for _ in range(n_t)]
    hcat, tb, perm = next(it), next(it), next(it)
    p = pl.program_id(1)
    mh = mn + 8
    per_page = PAGE // CMP_STRIDE
    n_real = n_steps * pg * per_page
    n_g = G_NSA if paged else 1

    @pl.when(p == 0)
    def _():
        hcat[:, n_real:, :] = jnp.zeros((n_t * G_NSA, mh - n_real, CMP_STRIDE * HD), F32)
        d = lax.broadcasted_iota(jnp.int32, perm.shape, 0)
        s = lax.broadcasted_iota(jnp.int32, perm.shape, 1)
        src = (CMP_STRIDE * (d % per_page) + (d % PAGE) // per_page) * n_g + d // PAGE
        perm[...] = jnp.where(s == src, 1.0, 0.0).astype(BF16)

    for ti in range(n_t):
        for pp in range(pg):
            r0 = pl.multiple_of((p * pg + pp) * per_page, per_page)
            for x_ref, g0 in ([(x_all[ti][pp], 0)] if paged else [(x_all[ti][g], g) for g in range(G_NSA)]):
                rows = _dot(perm[...], x_ref[0].astype(BF16))
                for g in range(n_g):
                    for l in range(CMP_STRIDE):
                        a = g * PAGE + l * per_page
                        hcat[ti * G_NSA + g0 + g, pl.ds(r0, per_page), l * HD:(l + 1) * HD] = rows[a:a + per_page]

    @pl.when(p == n_steps - 1)
    def _():
        for ti in range(n_t):
            pe_ref, w_ref, gain_ref = par_all[ti]
            if paged:
                xn = new_all[ti][0]
                rid = lax.broadcasted_iota(jnp.int32, xn.shape, 0)
                xn = jnp.where(rid < t_new, xn, 0.0)
                for g in range(G_NSA):
                    for l in range(8):
                        hcat[ti * G_NSA + g, n_real:n_real + 1, l * HD:(l + 1) * HD] = xn[l:l + 1, g * HD:(g + 1) * HD]
            w = w_ref[...]
            pe2 = _dot(pe_ref[...].astype(BF16), w)
            pe_bias = pe2[0:1, 0:HD] + pe2[1:2, HD:2 * HD]
            for g in range(G_NSA):
                tb[...] = _dot(hcat[ti * G_NSA + g].astype(BF16), w)
                y = tb[0:mn, 0:HD] + tb[1:mn + 1, HD:2 * HD] + pe_bias
                if norms[ti]:
                    y = y * lax.rsqrt(jnp.mean(y * y, axis=-1, keepdims=True) + EPS) * gain_ref[...]
                o_all[ti][0, :, g * HD:(g + 1) * HD] = y


def _compress(x_arr, x_cols, params, norms, *, mn, page_table=None, pools=None, pool_off=0, t_new=0, pg=16):
    paged = page_table is not None
    bsz = x_arr.shape[0]
    n_t = len(norms)
    mh = mn + 8
    w2 = G_NSA * HD
    const = lambda shape: pl.BlockSpec(shape, (lambda b, p, *_: (0,) * len(shape)))
    tail = [const((8, CMP_STRIDE * HD)), const((CMP_STRIDE * HD, 2 * HD)), const((1, HD))] * n_t
    tail_args = [a for par in params for a in par]
    out_specs = [pl.BlockSpec((1, mn, w2), lambda b, p, *_: (b, 0, 0))] * n_t
    n_perm = G_NSA * PAGE if paged else PAGE
    scratch = [pltpu.VMEM((n_t * G_NSA, mh, CMP_STRIDE * HD), F32), pltpu.VMEM((mh, 2 * HD), F32),
               pltpu.VMEM((n_perm, n_perm), BF16)]
    out_shape = [jax.ShapeDtypeStruct((bsz, mn, w2), F32)] * n_t
    if not paged:
        n_steps = x_arr.shape[1] // PAGE
        body = functools.partial(_compress_body, paged=False, pg=1, n_steps=n_steps, mn=mn, norms=norms, t_new=0)
        return pl.pallas_call(
            body, grid=(bsz, n_steps),
            in_specs=[pl.BlockSpec((1, PAGE, HD), lambda b, p, _c=c // HD + g: (b, p, _c))
                      for c in x_cols for g in range(G_NSA)] + tail,
            out_specs=out_specs, out_shape=out_shape, scratch_shapes=scratch,
            compiler_params=_cparams(("parallel", "arbitrary")),
        )(*([x_arr] * (n_t * G_NSA)), *tail_args)
    pg = min(pg, page_table.shape[1])
    n_steps = page_table.shape[1] // pg
    body = functools.partial(_compress_body, paged=True, pg=pg, n_steps=n_steps, mn=mn, norms=norms, t_new=t_new)
    grid_spec = pltpu.PrefetchScalarGridSpec(
        num_scalar_prefetch=1, grid=(bsz, n_steps),
        in_specs=[pl.BlockSpec((1, G_NSA * PAGE, HD), lambda b, p, pt, _pp=pp: (pool_off + pt[b, p * pg + _pp], 0, 0))
                  for _ in range(n_t) for pp in range(pg)]
        + [pl.BlockSpec((1, 8, w2), lambda b, p, pt, _c=c // w2: (b, 0, _c)) for c in x_cols] + tail,
        out_specs=out_specs, scratch_shapes=scratch)
    return pl.pallas_call(body, grid_spec=grid_spec, out_shape=out_shape,
                          compiler_params=_cparams(("parallel", "arbitrary")))(
        page_table, *[pool for pool in pools for _ in range(pg)], *([x_arr] * n_t), *tail_args)


def _cmpsel_body(q_ref, kc_ref, vc_ref, o_ref, sel_ref, *, tq, ncp, nsp, n_sel, pos0):
    i = pl.program_id(2)
    q = q_ref[0]
    kc = kc_ref[0].astype(BF16)
    vc = vc_ref[0].astype(BF16)
    pos = pos0 + i * tq + lax.broadcasted_iota(jnp.int32, (tq, ncp), 0)
    col = lax.broadcasted_iota(jnp.int32, (tq, ncp), 1)
    valid = (col * CMP_STRIDE + (CMP_BLOCK - 1)) <= pos
    validf = jnp.where(valid, 1.0, 0.0)
    imp = jnp.zeros((tq, ncp), F32)
    for h in range(HPG):
        qh = (q[:, h * HD:(h + 1) * HD] * ATTN_SCALE).astype(BF16)
        s = jnp.where(valid, _dot_nt(qh, kc), NEG)
        e = jnp.exp(s - jnp.max(s, axis=-1, keepdims=True)) * validf
        p = e / jnp.maximum(jnp.sum(e, axis=-1, keepdims=True), 1e-30)
        o_ref[0, :, h * HD:(h + 1) * HD] = _dot(p.astype(BF16), vc)
        imp = imp + p
    cc = lax.broadcasted_iota(jnp.int32, (ncp, nsp), 0)
    jj = lax.broadcasted_iota(jnp.int32, (ncp, nsp), 1)
    r = SEL_BLOCK // CMP_STRIDE
    gather = jnp.where((cc >= r * jj - 1) & (cc <= r * jj + r - 1), 1.0, 0.0).astype(BF16)
    score = _dot3_l(imp, gather)
    blk = lax.broadcasted_iota(jnp.int32, (tq, nsp), 1)
    pq = pos0 + i * tq + lax.broadcasted_iota(jnp.int32, (tq, nsp), 0)
    forced = (blk == pq // SEL_BLOCK) | (blk == 0)
    future = blk * SEL_BLOCK > pq
    score = jnp.where(forced, FORCE_SCORE, jnp.where(future, -1.0, score))
    score = jnp.where(blk < n_sel, score, -2.0)
    rank = jnp.zeros((tq, nsp), F32)
    for c in range(n_sel):
        sc = score[:, c:c + 1]
        ahead = (sc > score) | ((sc == score) & (blk > c))
        rank = rank + jnp.where(ahead, 1.0, 0.0)
    sel_ref[0, 0] = jnp.where(rank < float(N_SELECT), 1.0, 0.0)


def _cmpsel(z3, kcmp, vcmp, *, tq, pos0, n_sel):
    bsz, t, _ = z3.shape
    ncp = kcmp.shape[1]
    nsp = -(-n_sel // HD) * HD
    w = HPG * HD
    return pl.pallas_call(
        functools.partial(_cmpsel_body, tq=tq, ncp=ncp, nsp=nsp, n_sel=n_sel, pos0=pos0),
        grid=(bsz, G_NSA, t // tq),
        in_specs=[pl.BlockSpec((1, tq, w), lambda b, g, i: (b, i, NQ // w + g)),
                  pl.BlockSpec((1, ncp, HD), lambda b, g, i: (b, 0, g)),
                  pl.BlockSpec((1, ncp, HD), lambda b, g, i: (b, 0, g))],
        out_specs=[pl.BlockSpec((1, tq, w), lambda b, g, i: (b, i, g)),
                   pl.BlockSpec((1, 1, tq, nsp), lambda b, g, i: (b, g, i, 0))],
        out_shape=[jax.ShapeDtypeStruct((bsz, t, G_NSA * w), F32),
                   jax.ShapeDtypeStruct((bsz, G_NSA, t, nsp), F32)],
        compiler_params=_cparams(("parallel", "parallel", "parallel")),
    )(z3, kcmp, vcmp)


def _combine_body(oc_ref, os_ref, ow_ref, zs_ref, gb_ref, o_ref):
    gates = _sigmoid(zs_ref[...] + gb_ref[...])
    for h in range(H_NSA):
        sl = slice(h * HD, (h + 1) * HD)
        c0 = H_FOX + 3 * h
        o = (gates[:, c0:c0 + 1] * oc_ref[:, sl] + gates[:, c0 + 1:c0 + 2] * os_ref[:, sl]
             + gates[:, c0 + 2:c0 + 3] * ow_ref[:, sl])
        o_ref[:, sl] = o.astype(BF16)


def _nsa_combine(oc, osl, ow, zs, gb_pad):
    m, n = oc.shape
    tm = min(m, 512)
    big = pl.BlockSpec((tm, n), lambda i: (i, 0))
    return pl.pallas_call(
        _combine_body, grid=(m // tm,),
        in_specs=[big, big, big, pl.BlockSpec((tm, HD), lambda i: (i, 0)), pl.BlockSpec((1, HD), lambda i: (0, 0))],
        out_specs=big, out_shape=jax.ShapeDtypeStruct((m, n), BF16),
        compiler_params=_cparams(("parallel",)),
    )(oc, osl, ow, zs, gb_pad)


def _paged_body(*refs, mode, pg, n_steps, n_w, hpg_kv, t_new):
    it = iter(refs)
    next(it)
    q_ref = next(it)
    k_refs = [next(it) for _ in range(pg)]
    v_refs = [next(it) for _ in range(pg)]
    cl_refs = [next(it) for _ in range(pg)] if mode == "fox" else None
    tf_refs = [next(it) for _ in range(pg)] if mode == "fox" else None
    kn_ref, vn_ref = next(it), next(it)
    lfn_ref = next(it) if mode == "fox" else None
    sel_ref = next(it) if mode == "slc" else None
    o_ref = next(it)
    qb, m_scr, l_scr, acc, kn_scr, vn_scr, carry = (next(it) for _ in range(7))
    p = pl.program_id(1)
    rows = 4 * 8
    r2 = n_w * PAGE
    n_new = t_new * n_w

    def kv_match(ncols):
        row = lax.broadcasted_iota(jnp.int32, (rows, ncols), 0)
        col = lax.broadcasted_iota(jnp.int32, (rows, ncols), 1)
        return row, col, (col % n_w) == (row % 8) // hpg_kv

    @pl.when(p == 0)
    def _():
        qb[...] = (q_ref[0] * ATTN_SCALE).astype(BF16)
        m_scr[...] = jnp.full(m_scr.shape, NEG, F32)
        l_scr[...] = jnp.zeros(l_scr.shape, F32)
        acc[...] = jnp.zeros(acc.shape, F32)
        carry[...] = jnp.zeros(carry.shape, F32)
        kn_scr[...] = jnp.zeros(kn_scr.shape, BF16)
        vn_scr[...] = jnp.zeros(vn_scr.shape, BF16)

    def picked(page_idx, ncols):
        nsp = sel_ref.shape[2]
        jb = lax.broadcasted_iota(jnp.int32, (nsp, ncols), 0)
        key = lax.broadcasted_iota(jnp.int32, (nsp, ncols), 1) // n_w
        expand = jnp.where(jb == page_idx * (PAGE // SEL_BLOCK) + key // SEL_BLOCK, 1.0, 0.0).astype(BF16)
        return _dot(sel_ref[0].astype(BF16), expand) > 0.5

    def update(s_list, vb_list):
        m_old = m_scr[...]
        m_new = m_old
        for s in s_list:
            m_new = jnp.maximum(m_new, jnp.max(s, axis=-1, keepdims=True))
        a = jnp.exp(m_old - m_new)
        l = a * l_scr[...]
        o = a * acc[...]
        for s, vb in zip(s_list, vb_list):
            pr = jnp.exp(s - m_new)
            l = l + jnp.sum(pr, axis=-1, keepdims=True)
            o = o + _dot(pr.astype(BF16), vb)
        m_scr[...] = m_new
        l_scr[...] = l
        acc[...] = o

    _, _, match = kv_match(r2)
    s_list, vb_list = [], []
    c_run = carry[...] if mode == "fox" else None
    for pp in range(pg):
        s = _dot_nt(qb[...], k_refs[pp][0].astype(BF16))
        valid = match
        if mode == "fox":
            s = s - (c_run + cl_refs[pp][0])
            c_run = c_run + tf_refs[pp][0]
        if mode == "slc":
            valid = valid & picked(p * pg + pp, r2)
        s_list.append(jnp.where(valid, s, NEG))
        vb_list.append(v_refs[pp][0].astype(BF16))
    if mode == "fox":
        carry[...] = c_run
    update(s_list, vb_list)

    @pl.when(p == n_steps - 1)
    def _():
        kn_scr[0:n_new, :] = kn_ref[0].astype(BF16)
        vn_scr[0:n_new, :] = vn_ref[0].astype(BF16)
        s = _dot_nt(qb[...], kn_scr[...])
        row, col, valid = kv_match(HD)
        valid = valid & (col // n_w <= row // 8) & (col < n_new)
        if mode == "fox":
            rr = lax.broadcasted_iota(jnp.int32, (HD, HD), 0)
            cc = lax.broadcasted_iota(jnp.int32, (HD, HD), 1)
            pre = jnp.where((rr % n_w == cc % n_w) & (rr // n_w <= cc // n_w), 1.0, 0.0).astype(BF16)
            c_new = _dot3_l(jnp.broadcast_to(lfn_ref[0], (8, HD)), pre)[0:1, :] + carry[:, 0:HD]
            s = s - c_new
        if mode == "slc":
            valid = valid & picked(n_steps * pg, HD)
        update([jnp.where(valid, s, NEG)], [vn_scr[...]])
        o_ref[0] = acc[...] / l_scr[...]


def _paged_attn(mode, q_rows, page_table, k_pool, v_pool, pool_off, k_new, v_new, *, hpg_kv, pg,
                c_local=None, c_total=None, lf_new=None, sel_rows=None):
    bsz = q_rows.shape[0]
    n_pages = page_table.shape[1]
    pg = min(pg, n_pages)
    n_steps = n_pages // pg
    r2 = k_pool.shape[1]
    n_w = r2 // PAGE
    n_new = k_new.shape[1]
    page = lambda pp: (lambda b, p, pt: (pool_off + pt[b, p * pg + pp], 0, 0))
    per_b = lambda b, p, pt: (b, 0, 0)
    in_specs = [pl.BlockSpec((1, 32, HD), per_b)]
    in_specs += [pl.BlockSpec((1, r2, HD), page(pp)) for pp in range(pg)] * 2
    args = [q_rows] + [k_pool] * pg + [v_pool] * pg
    if mode == "fox":
        in_specs += [pl.BlockSpec((1, 1, r2), page(pp)) for pp in range(pg)] * 2
        args += [c_local] * pg + [c_total] * pg
    in_specs += [pl.BlockSpec((1, n_new, HD), per_b)] * 2
    args += [k_new, v_new]
    if mode == "fox":
        in_specs.append(pl.BlockSpec((1, 1, HD), per_b))
        args.append(lf_new)
    if mode == "slc":
        in_specs.append(pl.BlockSpec((1, 32, sel_rows.shape[2]), per_b))
        args.append(sel_rows)
    grid_spec = pltpu.PrefetchScalarGridSpec(
        num_scalar_prefetch=1, grid=(bsz, n_steps), in_specs=in_specs,
        out_specs=pl.BlockSpec((1, 32, HD), per_b),
        scratch_shapes=[pltpu.VMEM((32, HD), BF16), pltpu.VMEM((32, 1), F32), pltpu.VMEM((32, 1), F32),
                        pltpu.VMEM((32, HD), F32), pltpu.VMEM((PAGE, HD), BF16), pltpu.VMEM((PAGE, HD), BF16),
                        pltpu.VMEM((1, r2), F32)])
    return pl.pallas_call(
        functools.partial(_paged_body, mode=mode, pg=pg, n_steps=n_steps, n_w=n_w, hpg_kv=hpg_kv, t_new=n_new // n_w),
        grid_spec=grid_spec, out_shape=jax.ShapeDtypeStruct((bsz, 32, HD), F32),
        compiler_params=_cparams(("parallel", "arbitrary")),
    )(page_table, *args)


def _pool_prefix_body(x_ref, cl_ref, ct_ref, w_scr, *, n_w):
    r2 = x_ref.shape[1]

    @pl.when(pl.program_id(0) == 0)
    def _():
        r = lax.broadcasted_iota(jnp.int32, (r2, 2 * r2), 0)
        c = lax.broadcasted_iota(jnp.int32, (r2, 2 * r2), 1)
        same = (r % n_w) == (c % n_w)
        local = same & (r // n_w <= c // n_w) & (c < r2)
        w_scr[...] = jnp.where(local | (same & (c >= r2)), 1.0, 0.0).astype(BF16)

    y = _dot3_l(x_ref[...], w_scr[...])
    cl_ref[...] = y[:, :r2]
    ct_ref[...] = y[:, r2:]


def _pool_prefix(x, n_w):
    n, r2 = x.shape
    tm = next((c for c in (256, 128, 64, 32, 16, 8) if n % c == 0), n)
    spec = pl.BlockSpec((tm, r2), lambda i: (i, 0))
    return pl.pallas_call(
        functools.partial(_pool_prefix_body, n_w=n_w), grid=(n // tm,),
        in_specs=[spec], out_specs=[spec, spec],
        out_shape=[jax.ShapeDtypeStruct((n, r2), F32)] * 2,
        scratch_shapes=[pltpu.VMEM((r2, 2 * r2), BF16)],
        compiler_params=_cparams(("arbitrary",)),
    )(x)


def _gla_body(*refs, mode, t_in, chunk, dv, t_real, has_state, hb):
    it = iter(refs)
    q_ref, k_ref, v_ref, og_ref = next(it), next(it), next(it), next(it)
    if mode == "gla":
        zs_ref, wa_ref, ba_ref = next(it), next(it), next(it)
    else:
        lb_ref = next(it)
    gain_ref = next(it)
    s0_ref = next(it) if has_state else None
    o_ref, s_ref, st_scr = next(it), next(it), next(it)
    n_chunks = max(t_in // chunk, 1)
    n_rows = min(chunk, t_in)
    pad = chunk - n_rows
    tri = lax.broadcasted_iota(jnp.int32, (chunk, chunk), 0) >= lax.broadcasted_iota(jnp.int32, (chunk, chunk), 1)
    ltri = jnp.where(tri, 1.0, 0.0).astype(BF16)
    if mode == "hg":
        lbr = lb_ref[...]
        e = jnp.exp(lbr - jnp.max(lbr, axis=0, keepdims=True))
        lb_all = (e / jnp.sum(e, axis=0, keepdims=True))[0:1, :]
    ti = pl.program_id(2)

    @pl.when(ti == 0)
    def _():
        for h in range(hb):
            st_scr[h] = s0_ref[0, h].T if has_state else jnp.zeros((dv, HD), F32)

    def load(ref, r0, sl):
        x = ref[0, pl.ds(r0, n_rows), sl]
        if pad:
            x = jnp.concatenate([x, jnp.zeros((pad, x.shape[1]), F32)], axis=0)
        return x

    def body(c, _):
        r0 = pl.multiple_of(c * chunk, 8)
        live = (ti * t_in + r0 + lax.broadcasted_iota(jnp.int32, (chunk, HD), 0)) < t_real
        if mode == "gla":
            ga = load(zs_ref, r0, slice(0, HD)).astype(BF16)
        for h in range(hb):
            ks, vs = slice(h * HD, (h + 1) * HD), slice(h * dv, (h + 1) * dv)
            qr, kr, v, og = load(q_ref, r0, ks), load(k_ref, r0, ks), load(v_ref, r0, vs), load(og_ref, r0, vs)
            if mode == "gla":
                g = _log_sigmoid(_dot(ga, wa_ref[:, ks]) + ba_ref[:, ks]) / GLA_GATE_NORM
                q, k = qr, kr
            else:
                lb = lb_all[:, ks]
                f = lb + (1.0 - lb) * _sigmoid(kr)
                q, k, g = qr * _sigmoid(qr), 1.0 - f, jnp.log(f)
            g = jnp.where(live, g, 0.0)
            k = jnp.where(live, k, 0.0)
            b = _dot3_r(ltri, g)
            bm = b[chunk // 2 - 1:chunk // 2, :]
            bl = b[chunk - 1:chunk, :]
            qe = (q * jnp.exp(jnp.minimum(b - bm, EXP_CLAMP))).astype(BF16)
            ke = (k * jnp.exp(jnp.minimum(bm - b, EXP_CLAMP))).astype(BF16)
            a = jnp.where(tri, _dot_nt(qe, ke), 0.0)
            vb = v.astype(BF16)
            st = st_scr[h]
            o = _dot(a.astype(BF16), vb) + _dot_nt((q * jnp.exp(b)).astype(BF16), st.astype(BF16))
            kd = (k * jnp.exp(bl - b)).astype(BF16)
            st_scr[h] = st * jnp.exp(bl) + _dot_tn(vb, kd)
            o = o * lax.rsqrt(jnp.mean(o * o, axis=-1, keepdims=True) + EPS) * gain_ref[...]
            o = o * (og * _sigmoid(og) if mode == "gla" else _sigmoid(og))
            o_ref[0, pl.ds(r0, n_rows), vs] = o[:n_rows].astype(BF16)
        return 0

    lax.fori_loop(0, n_chunks, body, 0)

    @pl.when(ti == pl.num_programs(2) - 1)
    def _():
        for h in range(hb):
            s_ref[0, h] = st_scr[h].T


def _gla(mode, z3, zs3, cols, n_heads, dv, extra, gain, s0, *, t_real, chunk, hb):
    bsz, t_all, _ = z3.shape
    tc = min(t_all, 512)
    qc, kc, vc, oc = cols
    has_state = s0 is not None
    wk, wv = hb * HD, hb * dv
    in_specs = [pl.BlockSpec((1, tc, wk), lambda b, h, t: (b, t, qc // wk + h)),
                pl.BlockSpec((1, tc, wk), lambda b, h, t: (b, t, kc // wk + h)),
                pl.BlockSpec((1, tc, wv), lambda b, h, t: (b, t, vc // wv + h)),
                pl.BlockSpec((1, tc, wv), lambda b, h, t: (b, t, oc // wv + h))]
    args = [z3, z3, z3, z3]
    if mode == "gla":
        wa, ba = extra
        in_specs += [pl.BlockSpec((1, tc, HD), lambda b, h, t: (b, t, 0)),
                     pl.BlockSpec((HD, wk), lambda b, h, t: (0, h)),
                     pl.BlockSpec((1, wk), lambda b, h, t: (0, h))]
        args += [zs3, wa, ba]
    else:
        in_specs.append(pl.BlockSpec((8, wk), lambda b, h, t: (0, h)))
        args.append(extra)
    in_specs.append(pl.BlockSpec((1, dv), lambda b, h, t: (0, 0)))
    args.append(gain)
    st_spec = pl.BlockSpec((1, hb, HD, dv), lambda b, h, t: (b, h, 0, 0))
    if has_state:
        in_specs.append(st_spec)
        args.append(s0)
    return pl.pallas_call(
        functools.partial(_gla_body, mode=mode, t_in=tc, chunk=chunk, dv=dv, t_real=t_real, has_state=has_state,
                          hb=hb),
        grid=(bsz, n_heads // hb, t_all // tc),
        in_specs=in_specs,
        out_specs=[pl.BlockSpec((1, tc, wv), lambda b, h, t: (b, t, h)), st_spec],
        out_shape=[jax.ShapeDtypeStruct((bsz, t_all, n_heads * dv), BF16),
                   jax.ShapeDtypeStruct((bsz, n_heads, HD, dv), F32)],
        scratch_shapes=[pltpu.VMEM((hb, dv, HD), F32)],
        compiler_params=_cparams(("parallel", "parallel", "arbitrary")),
    )(*args)


def _split_cols(w, sizes):
    outs, off = [], 0
    for s in sizes:
        outs.append(w[:, off:off + s])
        off += s
    return outs


def _pad_cols(w, n):
    return jnp.pad(w, ((0, 0), (0, n - w.shape[1])))


def _row(v):
    return v.reshape(1, -1).astype(F32)


def _even_weights(w_in, fq_gain, fk_gain, nq_gain, nk_gain):
    fq, fk, fv, ff, nq, kc, vc, ks, vs, kw, vw, ng = _split_cols(w_in, EVEN_SIZES)
    big = jnp.concatenate([fq, fk, fv, nq, kc, vc, ks, vs, kw, vw], axis=1).astype(BF16)
    small = _pad_cols(jnp.concatenate([ff, ng], axis=1), HD).astype(BF16)
    ones = lambda n: jnp.ones((n,), F32)
    gain = jnp.concatenate([jnp.tile(fq_gain, H_FOX), jnp.tile(fk_gain, H_FOX), ones(H_FOX * HD),
                            jnp.tile(nq_gain, H_NSA), ones(2 * G_NSA * HD), jnp.tile(nk_gain[1], G_NSA),
                            ones(G_NSA * HD), jnp.tile(nk_gain[2], G_NSA), ones(G_NSA * HD)])
    z, o = jnp.zeros, jnp.ones
    flag = jnp.concatenate([o((2 * H_FOX * HD,), F32), z((H_FOX * HD,), F32), o((H_NSA * HD,), F32),
                            z((2 * G_NSA * HD,), F32), o((G_NSA * HD,), F32), z((G_NSA * HD,), F32),
                            o((G_NSA * HD,), F32), z((G_NSA * HD,), F32)])
    return big, small, _row(gain), _row(flag)


def _odd_weights(w_in):
    gq, gk, gv, ga, gg, hq, hf, hi, hg = _split_cols(w_in, ODD_SIZES)
    big = jnp.concatenate([gq, gk, gv, gg, hq, hf, hi, hg], axis=1).astype(BF16)
    small = _pad_cols(ga, HD).astype(BF16)
    gain = jnp.concatenate([jnp.full((H_GLA * DK_GLA,), DK_GLA ** -0.5, F32), jnp.ones((N_BIG_ODD - H_GLA * DK_GLA,), F32)])
    return big, small, _row(gain), jnp.zeros((1, N_BIG_ODD), F32)


def _cmp_weights(pe, w):
    half = CMP_STRIDE * HD
    wcat = jnp.concatenate([w[:half], w[half:]], axis=1).astype(BF16)
    pe8 = jnp.pad(pe.reshape(2, half), ((0, 6), (0, 0)))
    return pe8, wcat


def _ffn(x, g, w1, w3, w2, layer):
    return _mm_res([_ffn_up(x, g, w1, w3, layer)], w2, layer, x)


def _even_layer(x, bsz, t, t_real, ew, past):
    (w_big, w_small, cgain, cflag, g_mix, w_out, b_f_pad, gb_pad, pe_k, wc_k, pe_v, wc_v, nk0) = ew
    z, zs = _proj(x, g_mix, w_big, w_small, cgain, cflag, True)
    z3 = z.reshape(bsz, t, N_BIG_EVEN)
    zs3 = zs.reshape(bsz, t, HD)
    cmp_params = ((pe_k, wc_k, nk0), (pe_v, wc_v, jnp.ones((1, HD), F32)))
    if past is None:
        lf, c = _fox_prep(zs3, b_f_pad)
        logf_new = lf[:, :, :H_FOX]
        tk = min(t, 512)
        bias = jnp.swapaxes(c[:, :, :H_FOX], 1, 2).reshape(bsz * H_FOX, t // tk, tk)
        o_fox = _flash("fox", z3, FQ, z3, FK, z3, FV, n_kv=H_FOX, hpg=1, tq=min(t, 512), tk=tk, bias=bias,
                       out_dtype=BF16)
        n_cmp = t // CMP_STRIDE
        mn = -(-n_cmp // HD) * HD
        kcmp, vcmp = _compress(z3, (KC, VC), cmp_params, (True, False), mn=mn)
        tqn = min(t, 512 // HPG)
        o_cmp, sel = _cmpsel(z3, kcmp, vcmp, tq=min(t, 256), pos0=0, n_sel=-(-t // SEL_BLOCK))
        o_slc = _flash("slc", z3, NQ, z3, KS, z3, VS, n_kv=G_NSA, hpg=HPG, tq=tqn, tk=tk, sel=sel)
        o_win = _flash("win", z3, NQ, z3, KW, z3, VW, n_kv=G_NSA, hpg=HPG, tq=tqn, tk=tk)
        w_buf = WINDOW
        kw_new = jnp.concatenate([jnp.zeros((bsz, w_buf, G_NSA * HD), F32), z3[:, :, KW:KW + G_NSA * HD]], axis=1)[:, -w_buf:]
        vw_new = jnp.concatenate([jnp.zeros((bsz, w_buf, G_NSA * HD), F32), z3[:, :, VW:VW + G_NSA * HD]], axis=1)[:, -w_buf:]
        o_fox = o_fox.reshape(bsz * t, H_FOX * HD)
    else:
        (page_table, pool_off, fk_pool, fv_pool, lf_pool, kc_pool, vc_pool, ks_pool, vs_pool, buf_k, buf_v) = past
        p_len = page_table.shape[1] * PAGE
        lf_all, _ = _fox_prep(zs.reshape(1, bsz * t, HD), b_f_pad)
        lf_new = lf_all.reshape(bsz, t, HD)[:, :t_real, :H_FOX]
        logf_new = lf_new
        lfn_pad = _pad_cols(lf_new.reshape(bsz, t_real * H_FOX), HD).reshape(bsz, 1, HD)
        c_local, c_total = _pool_prefix(lf_pool, H_FOX)
        rows_of = lambda c0, n: z3[:, :t_real, c0:c0 + n * HD].reshape(bsz, t_real * n, HD)
        o_fox = _paged_attn("fox", rows_of(FQ, H_FOX), page_table, fk_pool, fv_pool, pool_off,
                            rows_of(FK, H_FOX), rows_of(FV, H_FOX), hpg_kv=1, pg=8,
                            c_local=c_local[:, None, :], c_total=c_total[:, None, :], lf_new=lfn_pad)
        o_fox = o_fox.reshape(bsz, t_real, H_FOX * HD)
        n_cmp = -(-(p_len + t_real) // CMP_STRIDE)
        mn = -(-n_cmp // HD) * HD
        kcmp, vcmp = _compress(z3, (KC, VC), cmp_params, (True, False), mn=mn, page_table=page_table,
                               pools=(kc_pool, vc_pool), pool_off=pool_off, t_new=t_real)
        n_sel = -(-(p_len + t_real) // SEL_BLOCK)
        o_cmp, sel = _cmpsel(z3, kcmp, vcmp, tq=t, pos0=p_len, n_sel=n_sel)
        nsp = sel.shape[-1]
        sel_rows = jnp.broadcast_to(jnp.swapaxes(sel[:, :, :t_real], 1, 2)[:, :, :, None, :],
                                    (bsz, t_real, G_NSA, HPG, nsp)).reshape(bsz, 32, nsp)
        o_slc = _paged_attn("slc", rows_of(NQ, H_NSA), page_table, ks_pool, vs_pool, pool_off,
                            rows_of(KS, G_NSA), rows_of(VS, G_NSA), hpg_kv=HPG, pg=16,
                            sel_rows=sel_rows).reshape(bsz, t_real, H_NSA * HD)
        kw_all = jnp.concatenate([buf_k, z3[:, :t_real, KW:KW + G_NSA * HD]], axis=1)
        vw_all = jnp.concatenate([buf_v, z3[:, :t_real, VW:VW + G_NSA * HD]], axis=1)
        w_buf = buf_k.shape[1]
        kw_new, vw_new = kw_all[:, -w_buf:], vw_all[:, -w_buf:]
        lk = -(-(w_buf + t) // HD) * HD
        padk = lambda a: jnp.pad(a, ((0, 0), (0, lk - a.shape[1]), (0, 0)))
        o_win = _flash("win", z3, NQ, padk(kw_all), 0, padk(vw_all), 0, n_kv=G_NSA, hpg=HPG, tq=t, tk=HD, off=w_buf)
        padt = lambda a: jnp.pad(a, ((0, 0), (0, t - t_real), (0, 0)))
        o_fox = padt(o_fox).astype(BF16).reshape(bsz * t, H_FOX * HD)
        o_slc = padt(o_slc)
    m = bsz * t
    o_nsa = _nsa_combine(o_cmp.reshape(m, -1), o_slc.reshape(m, -1), o_win.reshape(m, -1), zs, gb_pad)
    x = _mm_res([o_fox, o_nsa], w_out, 0, x)
    zr = z3[:, :t_real]
    grp = lambda c0: zr[:, :, c0:c0 + G_NSA * HD].reshape(1, bsz, t_real, G_NSA, HD)
    fox = lambda c0: zr[:, :, c0:c0 + H_FOX * HD].reshape(1, bsz, t_real, H_FOX, HD)
    w_rows = kw_new.shape[1]
    outs = (fox(FK), fox(FV), logf_new[None], grp(KC), grp(VC), grp(KS), grp(VS),
            kw_new.reshape(1, bsz, w_rows, G_NSA, HD), vw_new.reshape(1, bsz, w_rows, G_NSA, HD))
    return x, outs


def _odd_layer(x, bsz, t, t_real, ow, state):
    (w_big, w_small, cgain, cflag, g_mix, w_out, wa_pad, ba, gla_gain, lb_pad, hg_gain) = ow
    z, zs = _proj(x, g_mix, w_big, w_small, cgain, cflag, False)
    z3 = z.reshape(bsz, t, N_BIG_ODD)
    zs3 = zs.reshape(bsz, t, HD)
    s_gla, s_hg = state if state is not None else (None, None)
    chunk = 64 if t >= 64 else 128
    o_gla, s_gla = _gla("gla", z3, zs3, (GQ, GK, GV, GG), H_GLA, DV_GLA, (wa_pad, ba), gla_gain, s_gla,
                        t_real=t_real, chunk=chunk, hb=H_GLA)
    o_hg, s_hg = _gla("hg", z3, zs3, (HQ, HF, HI, HO), H_HG, DV_HG, lb_pad, hg_gain, s_hg,
                      t_real=t_real, chunk=chunk, hb=H_HG)
    m = bsz * t
    x = _mm_res([o_gla.reshape(m, -1), o_hg.reshape(m, -1)], w_out, 0, x)
    return x, (s_gla[None], s_hg[None])


def kernel(x_prompt, x_sample, cache_fox_k, cache_fox_v, cache_fox_logf, cache_nsa_kc, cache_nsa_vc, cache_nsa_ks, cache_nsa_vs, state_nsa_kw, state_nsa_vw, state_gla, state_hgrn, page_table, norm_mix, norm_ffn, w_in_even, w_out_even, fox_b_f, fox_q_gain, fox_k_gain, nsa_q_gain, nsa_k_gain, nsa_cmp_pe_k, nsa_cmp_pe_v, nsa_cmp_wk, nsa_cmp_wv, nsa_gate_b, w_in_odd, w_out_odd, gla_wa2, gla_ba, gla_norm, hgrn_lb, hgrn_norm, ffn_w1, ffn_w3, ffn_w2):
    bp, tp, d = x_prompt.shape
    bs, ts, _ = x_sample.shape
    n_pool = cache_fox_k.shape[1]
    ts_pad = 8
    pe_k, wc_k = _cmp_weights(nsa_cmp_pe_k[0], nsa_cmp_wk[0])
    pe_v, wc_v = _cmp_weights(nsa_cmp_pe_v[0], nsa_cmp_wv[0])
    b_f_pad = _pad_cols(_row(fox_b_f[0]), HD)
    gb_pad = _pad_cols(jnp.concatenate([jnp.zeros((1, H_FOX), F32), _row(nsa_gate_b[0])], axis=1), HD)
    ew = _even_weights(w_in_even[0], fox_q_gain[0], fox_k_gain[0], nsa_q_gain[0], nsa_k_gain[0]) + (
        _row(norm_mix[0]), w_out_even, b_f_pad, gb_pad, pe_k, wc_k, pe_v, wc_v, _row(nsa_k_gain[0, 0]))
    wa_pad = jnp.pad(gla_wa2[0], ((0, HD - GLA_RANK), (0, 0))).astype(BF16)
    lb_pad = jnp.pad(hgrn_lb.astype(F32), ((0, 8 - hgrn_lb.shape[0]), (0, 0)), constant_values=-1e30)
    ow = _odd_weights(w_in_odd[0]) + (_row(norm_mix[1]), w_out_odd, wa_pad, _row(gla_ba[0]),
                                      _row(gla_norm[0]), lb_pad, _row(hgrn_norm[0]))
    ffn = [(_row(norm_ffn[i]), ffn_w1, ffn_w3, ffn_w2, i) for i in range(2)]

    flat = lambda c: c.reshape((c.shape[0] * c.shape[1], PAGE * c.shape[3], HD))
    lf_pool = cache_fox_logf.reshape(-1, PAGE * H_FOX)
    win = lambda s: s.reshape(bs, s.shape[2], G_NSA * HD)
    past = (page_table, 0, flat(cache_fox_k), flat(cache_fox_v), lf_pool, flat(cache_nsa_kc), flat(cache_nsa_vc),
            flat(cache_nsa_ks), flat(cache_nsa_vs), win(state_nsa_kw), win(state_nsa_vw))

    def run(x, bsz, t, t_real, past_, state_):
        x, ev = _even_layer(x, bsz, t, t_real, ew, past_)
        x = _ffn(x, *ffn[0])
        x, od = _odd_layer(x, bsz, t, t_real, ow, state_)
        x = _ffn(x, *ffn[1])
        return x, ev, od

    yp, ev_p, od_p = run(x_prompt.reshape(bp * tp, d), bp, tp, tp, None, None)
    xs = jnp.pad(x_sample, ((0, 0), (0, ts_pad - ts), (0, 0))).reshape(bs * ts_pad, d)
    ys, ev_s, od_s = run(xs, bs, ts_pad, ts, past, (state_gla.reshape(state_gla.shape[1:]),
                                                    state_hgrn.reshape(state_hgrn.shape[1:])))
    y_prompt = yp.reshape(bp, tp, d)
    y_sample = ys.reshape(bs, ts_pad, d)[:, :ts]
    outs = [y_prompt, y_sample]
    for a, b in zip(ev_p, ev_s):
        outs += [a, b]
    for a, b in zip(od_p, od_s):
        outs += [a, b]
    return tuple(outs)
```

```python
import functools

import jax
import jax.numpy as jnp
from jax import lax
from jax.experimental import pallas as pl
from jax.experimental.pallas import tpu as pltpu

F32 = jnp.float32
BF16 = jnp.bfloat16

D_MODEL = 2048
HD = 128
H_FOX = 8
H_NSA = 8
G_NSA = 2
HPG = H_NSA // G_NSA
CMP_STRIDE = 16
CMP_BLOCK = 2 * CMP_STRIDE
SEL_BLOCK = 64
N_SELECT = 16
WINDOW = 512
FORCE_SCORE = 1.0e4
H_GLA = 4
DK_GLA = 128
DV_GLA = 256
GLA_RANK = 16
GLA_GATE_NORM = 16.0
H_HG = 8
DK_HG = 128
DV_HG = 128
PAGE = 128
EVEN_SIZES = (H_FOX * HD, H_FOX * HD, H_FOX * HD, H_FOX, H_NSA * HD,
              G_NSA * HD, G_NSA * HD, G_NSA * HD, G_NSA * HD, G_NSA * HD, G_NSA * HD, 3 * H_NSA)
ODD_SIZES = (H_GLA * DK_GLA, H_GLA * DK_GLA, H_GLA * DV_GLA, GLA_RANK, H_GLA * DV_GLA,
             H_HG * DK_HG, H_HG * DK_HG, H_HG * DV_HG, H_HG * DV_HG)
FQ, FK, FV, NQ, KC, VC, KS, VS, KW, VW = 0, 1024, 2048, 3072, 4096, 4352, 4608, 4864, 5120, 5376
N_BIG_EVEN = 5632
GQ, GK, GV, GG, HQ, HF, HI, HO = 0, 512, 1024, 2048, 3072, 4096, 5120, 6144
N_BIG_ODD = 7168
ATTN_SCALE = HD ** -0.5
NEG = -1.0e30
EPS = 1.0e-6
EXP_CLAMP = 80.0
VMEM_LIMIT = 56 * 1024 * 1024


def _cparams(sem):
    return pltpu.CompilerParams(dimension_semantics=sem, vmem_limit_bytes=VMEM_LIMIT)


def _dot(a, b):
    return jnp.dot(a, b, preferred_element_type=F32)


def _dot_nt(a, b):
    return lax.dot_general(a, b, (((1,), (1,)), ((), ())), preferred_element_type=F32)


def _dot_tn(a, b):
    return lax.dot_general(a, b, (((0,), (0,)), ((), ())), preferred_element_type=F32)


def _split3(x):
    hi = x.astype(BF16)
    r = x - hi.astype(F32)
    mid = r.astype(BF16)
    lo = (r - mid.astype(F32)).astype(BF16)
    return hi, mid, lo


def _dot3_l(x, w):
    hi, mid, lo = _split3(x)
    return _dot(hi, w) + _dot(mid, w) + _dot(lo, w)


def _dot3_r(w, x):
    hi, mid, lo = _split3(x)
    return _dot(w, hi) + _dot(w, mid) + _dot(w, lo)


def _sigmoid(x):
    return 1.0 / (1.0 + jnp.exp(-x))


def _log_sigmoid(x):
    return jnp.minimum(x, 0.0) - jnp.log1p(jnp.exp(-jnp.abs(x)))


def _cumsum_rows(x):
    n = x.shape[0]
    row = lax.broadcasted_iota(jnp.int32, x.shape, 0)
    s = 1
    while s < n:
        x = x + jnp.where(row >= s, pltpu.roll(x, s, axis=0), 0.0)
        s *= 2
    return x


def _tri(n, upper):
    r = lax.broadcasted_iota(jnp.int32, (n, n), 0)
    c = lax.broadcasted_iota(jnp.int32, (n, n), 1)
    return jnp.where((r <= c) if upper else (r >= c), 1.0, 0.0).astype(BF16)


def _proj_body(x_ref, g_ref, w_ref, ws_ref, cg_ref, cf_ref, z_ref, zs_ref, h_scr, *, tn, has_norm):
    @pl.when(pl.program_id(1) == 0)
    def _():
        x = x_ref[...]
        r = lax.rsqrt(jnp.mean(x * x, axis=-1, keepdims=True) + EPS)
        h = (x * r * g_ref[...]).astype(BF16)
        h_scr[...] = h
        zs_ref[...] = _dot(h, ws_ref[...])

    z = _dot(h_scr[...], w_ref[...])
    if not has_norm:
        z_ref[...] = z * cg_ref[...]
        return
    for c in range(tn // HD):
        sl = slice(c * HD, (c + 1) * HD)
        zc = z[:, sl]
        r = lax.rsqrt(jnp.mean(zc * zc, axis=-1, keepdims=True) + EPS)
        f = cf_ref[:, sl]
        z_ref[:, sl] = zc * (f * r + (1.0 - f)) * cg_ref[:, sl]


def _proj(x, g, w_big, w_small, col_gain, col_flag, has_norm):
    m, d = x.shape
    n = w_big.shape[1]
    tm = min(m, 1024)
    tn = 512
    return pl.pallas_call(
        functools.partial(_proj_body, tn=tn, has_norm=has_norm),
        grid=(m // tm, n // tn),
        in_specs=[pl.BlockSpec((tm, d), lambda i, j: (i, 0)),
                  pl.BlockSpec((1, d), lambda i, j: (0, 0)),
                  pl.BlockSpec((d, tn), lambda i, j: (0, j)),
                  pl.BlockSpec((d, HD), lambda i, j: (0, 0)),
                  pl.BlockSpec((1, tn), lambda i, j: (0, j)),
                  pl.BlockSpec((1, tn), lambda i, j: (0, j))],
        out_specs=[pl.BlockSpec((tm, tn), lambda i, j: (i, j)),
                   pl.BlockSpec((tm, HD), lambda i, j: (i, 0))],
        out_shape=[jax.ShapeDtypeStruct((m, n), F32), jax.ShapeDtypeStruct((m, HD), F32)],
        scratch_shapes=[pltpu.VMEM((tm, d), BF16)],
        compiler_params=_cparams(("parallel", "arbitrary")),
    )(x, g, w_big, w_small, col_gain, col_flag)


def _ffn_up_body(x_ref, g_ref, w1_ref, w3_ref, o_ref, h_scr):
    @pl.when(pl.program_id(1) == 0)
    def _():
        x = x_ref[...]
        r = lax.rsqrt(jnp.mean(x * x, axis=-1, keepdims=True) + EPS)
        h_scr[...] = (x * r * g_ref[...]).astype(BF16)

    h = h_scr[...]
    a = _dot(h, w1_ref[...].astype(BF16))
    b = _dot(h, w3_ref[...].astype(BF16))
    o_ref[...] = (a * _sigmoid(a) * b).astype(BF16)


def _ffn_up(x, g, w1, w3, layer):
    m, d = x.shape
    n = w1.shape[2]
    tm = min(m, 1024)
    tn = 512
    return pl.pallas_call(
        _ffn_up_body,
        grid=(m // tm, n // tn),
        in_specs=[pl.BlockSpec((tm, d), lambda i, j: (i, 0)),
                  pl.BlockSpec((1, d), lambda i, j: (0, 0)),
                  pl.BlockSpec((None, d, tn), lambda i, j: (layer, 0, j)),
                  pl.BlockSpec((None, d, tn), lambda i, j: (layer, 0, j))],
        out_specs=pl.BlockSpec((tm, tn), lambda i, j: (i, j)),
        out_shape=jax.ShapeDtypeStruct((m, n), BF16),
        scratch_shapes=[pltpu.VMEM((tm, d), BF16)],
        compiler_params=_cparams(("parallel", "arbitrary")),
    )(x, g, w1, w3)


def _mm_res_body(*refs, n_in):
    res_ref = refs[2 * n_in]
    o_ref = refs[2 * n_in + 1]
    wb_refs = refs[2 * n_in + 2:]

    @pl.when(pl.program_id(1) == 0)
    def _():
        for w_ref, wb_ref in zip(refs[n_in:2 * n_in], wb_refs):
            wb_ref[...] = w_ref[...].astype(BF16)

    acc = res_ref[...]
    for a_ref, wb_ref in zip(refs[:n_in], wb_refs):
        acc = acc + _dot(a_ref[...], wb_ref[...])
    o_ref[...] = acc


def _mm_res(a_list, w, layer, res):
    m, n = res.shape
    tm = min(m, 512)
    tn = 512
    in_specs, w_specs, scratch, off = [], [], [], 0
    for a in a_list:
        k = a.shape[1]
        in_specs.append(pl.BlockSpec((tm, k), lambda j, i: (i, 0)))
        w_specs.append(pl.BlockSpec((None, k, tn), lambda j, i, _o=off // k: (layer, _o, j)))
        scratch.append(pltpu.VMEM((k, tn), BF16))
        off += k
    return pl.pallas_call(
        functools.partial(_mm_res_body, n_in=len(a_list)),
        grid=(n // tn, m // tm),
        in_specs=in_specs + w_specs + [pl.BlockSpec((tm, tn), lambda j, i: (i, j))],
        out_specs=pl.BlockSpec((tm, tn), lambda j, i: (i, j)),
        out_shape=jax.ShapeDtypeStruct((m, n), F32),
        scratch_shapes=scratch,
        compiler_params=_cparams(("parallel", "arbitrary")),
    )(*a_list, *([w] * len(a_list)), res)


def _fox_prep_body(zs_ref, b_ref, lf_ref, c_ref, carry, *, tc):
    @pl.when(pl.program_id(1) == 0)
    def _():
        carry[...] = jnp.zeros_like(carry)

    lf = _log_sigmoid(zs_ref[0] + b_ref[...])
    c = _dot3_r(_tri(tc, upper=False), lf) + carry[...]
    lf_ref[0] = lf
    c_ref[0] = c
    carry[...] = c[tc - 1:tc, :]


def _fox_prep(zs3, b_pad):
    bsz, t, _ = zs3.shape
    tc = min(t, 256)
    spec = pl.BlockSpec((1, tc, HD), lambda b, i: (b, i, 0))
    return pl.pallas_call(
        functools.partial(_fox_prep_body, tc=tc),
        grid=(bsz, t // tc),
        in_specs=[spec, pl.BlockSpec((1, HD), lambda b, i: (0, 0))],
        out_specs=[spec, spec],
        out_shape=[jax.ShapeDtypeStruct(zs3.shape, F32)] * 2,
        scratch_shapes=[pltpu.VMEM((1, HD), F32)],
        compiler_params=_cparams(("parallel", "arbitrary")),
    )(zs3, b_pad)


def _flash_body(*refs, mode, tq, tk, hpg, off):
    it = iter(refs)
    q_ref, k_ref, v_ref = next(it), next(it), next(it)
    bias_ref = next(it) if mode == "fox" else None
    sel_ref = next(it) if mode == "slc" else None
    o_ref, kb, vb = next(it), next(it), next(it)
    i = pl.program_id(2)

    @pl.when(i == 0)
    def _():
        kb[...] = k_ref[0].astype(BF16)
        vb[...] = v_ref[0].astype(BF16)

    q = q_ref[0]
    if hpg > 1:
        q = jnp.concatenate([q[:, h * HD:(h + 1) * HD] for h in range(hpg)], axis=0)
    qa = (q * ATTN_SCALE).astype(BF16)
    tile_rows = (lambda x: jnp.concatenate([x] * hpg, axis=0)) if hpg > 1 else (lambda x: x)
    q_lo = off + i * tq
    t_k = kb.shape[0]
    if mode == "slc":
        unpicked = (sel_ref[0, 0] - 1.0).astype(BF16)

    def attend(blocks):
        ss = []
        for b, (ks, size, masked) in enumerate(blocks):
            s = _dot_nt(qa, kb[pl.ds(ks, size), :])
            if mode == "fox":
                s = s - bias_ref[0, b:b + 1, :]
            if mode == "slc":
                cb = lax.broadcasted_iota(jnp.int32, (HD, size), 0)
                kk = lax.broadcasted_iota(jnp.int32, (HD, size), 1)
                expand = jnp.where(cb == (ks + kk) // SEL_BLOCK, 2.0 ** 100, 0.0).astype(BF16)
                s = s + tile_rows(_dot(unpicked, expand))
            if masked:
                kidx = ks + lax.broadcasted_iota(jnp.int32, (tq, size), 1)
                qpos = q_lo + lax.broadcasted_iota(jnp.int32, (tq, size), 0)
                valid = kidx <= qpos
                if mode == "win":
                    valid = valid & ((qpos - kidx) < WINDOW)
                s = jnp.where(tile_rows(valid), s, NEG)
            ss.append(s)
        m = ss[0].max(axis=-1, keepdims=True)
        for s in ss[1:]:
            m = jnp.maximum(m, s.max(axis=-1, keepdims=True))
        l, acc = 0.0, 0.0
        for s, (ks, size, _) in zip(ss, blocks):
            p = jnp.exp(s - m)
            l = l + jnp.sum(p, axis=-1, keepdims=True)
            acc = acc + _dot(p.astype(BF16), vb[pl.ds(ks, size), :])
        o = acc / l
        for h in range(hpg):
            o_ref[0, :, h * HD:(h + 1) * HD] = o[h * tq:(h + 1) * tq].astype(o_ref.dtype)

    if mode == "win":
        span = min(t_k, -(-(WINDOW + tq) // HD) * HD + HD)
        start = jnp.clip((q_lo - (WINDOW - 1)) // HD * HD, 0, t_k - span)
        attend([(pl.multiple_of(start, HD), span, True)])
    else:
        n_max = t_k // tk
        need = (q_lo + tq - 1) // tk + 1
        for nb in range(1, n_max + 1):
            @pl.when(need == nb)
            def _(nb=nb):
                attend([(b * tk, tk, b == nb - 1) for b in range(nb)])


def _flash(mode, q_arr, q_col, k_arr, k_col, v_arr, v_col, *, n_kv, hpg, tq, tk, off=0,
           bias=None, sel=None, out_dtype=F32):
    bsz, t_q = q_arr.shape[0], q_arr.shape[1]
    t_k = k_arr.shape[1]
    nq = t_q // tq
    w = hpg * HD
    in_specs = [pl.BlockSpec((1, tq, w), lambda b, g, i: (b, i, q_col // w + g)),
                pl.BlockSpec((1, t_k, HD), lambda b, g, i: (b, 0, k_col // HD + g)),
                pl.BlockSpec((1, t_k, HD), lambda b, g, i: (b, 0, v_col // HD + g))]
    args = [q_arr, k_arr, v_arr]
    if mode == "fox":
        in_specs.append(pl.BlockSpec((1, t_k // tk, tk), lambda b, g, i: (b * n_kv + g, 0, 0)))
        args.append(bias)
    if mode == "slc":
        in_specs.append(pl.BlockSpec((1, 1, tq, HD), lambda b, g, i: (b, g, i, 0)))
        args.append(sel)
    return pl.pallas_call(
        functools.partial(_flash_body, mode=mode, tq=tq, tk=tk, hpg=hpg, off=off),
        grid=(bsz, n_kv, nq),
        in_specs=in_specs,
        out_specs=pl.BlockSpec((1, tq, w), lambda b, g, i: (b, i, g)),
        out_shape=jax.ShapeDtypeStruct((bsz, t_q, n_kv * w), out_dtype),
        scratch_shapes=[pltpu.VMEM((t_k, HD), BF16), pltpu.VMEM((t_k, HD), BF16)],
        compiler_params=_cparams(("parallel", "parallel", "arbitrary")),
    )(*args)


def _compress_body(*refs, paged, pg, n_steps, mn, norms, t_new):
    it = iter(refs)
    if paged:
        next(it)
    n_t = len(norms)
    x_all = [[next(it) for _ in range(pg if paged else G_NSA)] for _ in range(n_t)]
    new_all = [next(it) for _ in range(n_t)] if paged else None
    par_all = [(next(it), next(it), next(it)) for _ in range(n_t)]
    o_all = [next(it) for _ in range(n_t)]
    hcat, tb, perm = next(it), next(it), next(it)
    p = pl.program_id(1)
    mh = mn + 8
    per_page = PAGE // CMP_STRIDE
    n_real = n_steps * pg * per_page
    n_g = G_NSA if paged else 1

    @pl.when(p == 0)
    def _():
        hcat[:, n_real:, :] = jnp.zeros((n_t * G_NSA, mh - n_real, CMP_STRIDE * HD), F32)
        d = lax.broadcasted_iota(jnp.int32, perm.shape, 0)
        s = lax.broadcasted_iota(jnp.int32, perm.shape, 1)
        src = (CMP_STRIDE * (d % per_page) + (d % PAGE) // per_page) * n_g + d // PAGE
        perm[...] = jnp.where(s == src, 1.0, 0.0).astype(BF16)

    for ti in range(n_t):
        for pp in range(pg):
            r0 = pl.multiple_of((p * pg + pp) * per_page, per_page)
            for x_ref, g0 in ([(x_all[ti][pp], 0)] if paged else [(x_all[ti][g], g) for g in range(G_NSA)]):
                rows = _dot(perm[...], x_ref[0].astype(BF16))
                for g in range(n_g):
                    for l in range(CMP_STRIDE):
                        a = g * PAGE + l * per_page
                        hcat[ti * G_NSA + g0 + g, pl.ds(r0, per_page), l * HD:(l + 1) * HD] = rows[a:a + per_page]

    @pl.when(p == n_steps - 1)
    def _():
        for ti in range(n_t):
            pe_ref, w_ref, gain_ref = par_all[ti]
            if paged:
                xn = new_all[ti][0]
                rid = lax.broadcasted_iota(jnp.int32, xn.shape, 0)
                xn = jnp.where(rid < t_new, xn, 0.0)
                for g in range(G_NSA):
                    for l in range(8):
                        hcat[ti * G_NSA + g, n_real:n_real + 1, l * HD:(l + 1) * HD] = xn[l:l + 1, g * HD:(g + 1) * HD]
            w = w_ref[...]
            pe2 = _dot(pe_ref[...].astype(BF16), w)
            pe_bias = pe2[0:1, 0:HD] + pe2[1:2, HD:2 * HD]
            for g in range(G_NSA):
                tb[...] = _dot(hcat[ti * G_NSA + g].astype(BF16), w)
                y = tb[0:mn, 0:HD] + tb[1:mn + 1, HD:2 * HD] + pe_bias
                if norms[ti]:
                    y = y * lax.rsqrt(jnp.mean(y * y, axis=-1, keepdims=True) + EPS) * gain_ref[...]
                o_all[ti][0, :, g * HD:(g + 1) * HD] = y


def _compress(x_arr, x_cols, params, norms, *, mn, page_table=None, pools=None, pool_off=0, t_new=0, pg=16):
    paged = page_table is not None
    bsz = x_arr.shape[0]
    n_t = len(norms)
    mh = mn + 8
    w2 = G_NSA * HD
    const = lambda shape: pl.BlockSpec(shape, (lambda b, p, *_: (0,) * len(shape)))
    tail = [const((8, CMP_STRIDE * HD)), const((CMP_STRIDE * HD, 2 * HD)), const((1, HD))] * n_t
    tail_args = [a for par in params for a in par]
    out_specs = [pl.BlockSpec((1, mn, w2), lambda b, p, *_: (b, 0, 0))] * n_t
    n_perm = G_NSA * PAGE if paged else PAGE
    scratch = [pltpu.VMEM((n_t * G_NSA, mh, CMP_STRIDE * HD), F32), pltpu.VMEM((mh, 2 * HD), F32),
               pltpu.VMEM((n_perm, n_perm), BF16)]
    out_shape = [jax.ShapeDtypeStruct((bsz, mn, w2), F32)] * n_t
    if not paged:
        n_steps = x_arr.shape[1] // PAGE
        body = functools.partial(_compress_body, paged=False, pg=1, n_steps=n_steps, mn=mn, norms=norms, t_new=0)
        return pl.pallas_call(
            body, grid=(bsz, n_steps),
            in_specs=[pl.BlockSpec((1, PAGE, HD), lambda b, p, _c=c // HD + g: (b, p, _c))
                      for c in x_cols for g in range(G_NSA)] + tail,
            out_specs=out_specs, out_shape=out_shape, scratch_shapes=scratch,
            compiler_params=_cparams(("parallel", "arbitrary")),
        )(*([x_arr] * (n_t * G_NSA)), *tail_args)
    pg = min(pg, page_table.shape[1])
    n_steps = page_table.shape[1] // pg
    body = functools.partial(_compress_body, paged=True, pg=pg, n_steps=n_steps, mn=mn, norms=norms, t_new=t_new)
    grid_spec = pltpu.PrefetchScalarGridSpec(
        num_scalar_prefetch=1, grid=(bsz, n_steps),
        in_specs=[pl.BlockSpec((1, G_NSA * PAGE, HD), lambda b, p, pt, _pp=pp: (pool_off + pt[b, p * pg + _pp], 0, 0))
                  for _ in range(n_t) for pp in range(pg)]
        + [pl.BlockSpec((1, 8, w2), lambda b, p, pt, _c=c // w2: (b, 0, _c)) for c in x_cols] + tail,
        out_specs=out_specs, scratch_shapes=scratch)
    return pl.pallas_call(body, grid_spec=grid_spec, out_shape=out_shape,
                          compiler_params=_cparams(("parallel", "arbitrary")))(
        page_table, *[pool for pool in pools for _ in range(pg)], *([x_arr] * n_t), *tail_args)


def _cmpsel_body(q_ref, kc_ref, vc_ref, o_ref, sel_ref, *, tq, ncp, nsp, n_sel, pos0, gs):
    i = pl.program_id(2)
    q = q_ref[0]
    pos = pos0 + i * tq + lax.broadcasted_iota(jnp.int32, (tq, ncp), 0)
    col = lax.broadcasted_iota(jnp.int32, (tq, ncp), 1)
    valid = (col * CMP_STRIDE + (CMP_BLOCK - 1)) <= pos
    validf = jnp.where(valid, 1.0, 0.0)
    imps = []
    for g in range(gs):
        kc = kc_ref[0, :, g * HD:(g + 1) * HD].astype(BF16)
        vc = vc_ref[0, :, g * HD:(g + 1) * HD].astype(BF16)
        imp = jnp.zeros((tq, ncp), F32)
        for h in range(g * HPG, (g + 1) * HPG):
            qh = (q[:, h * HD:(h + 1) * HD] * ATTN_SCALE).astype(BF16)
            s = jnp.where(valid, _dot_nt(qh, kc), NEG)
            e = jnp.exp(s - jnp.max(s, axis=-1, keepdims=True)) * validf
            p = e / jnp.maximum(jnp.sum(e, axis=-1, keepdims=True), 1e-30)
            o_ref[0, :, h * HD:(h + 1) * HD] = _dot(p.astype(BF16), vc)
            imp = imp + p
        imps.append(imp)
    imp = jnp.concatenate(imps, axis=0)
    cc = lax.broadcasted_iota(jnp.int32, (ncp, nsp), 0)
    jj = lax.broadcasted_iota(jnp.int32, (ncp, nsp), 1)
    r = SEL_BLOCK // CMP_STRIDE
    gather = jnp.where((cc >= r * jj - 1) & (cc <= r * jj + r - 1), 1.0, 0.0).astype(BF16)
    score = _dot3_l(imp, gather)
    rows = gs * tq
    blk = lax.broadcasted_iota(jnp.int32, (rows, nsp), 1)
    pq = pos0 + i * tq + lax.broadcasted_iota(jnp.int32, (rows, nsp), 0) % tq
    forced = (blk == pq // SEL_BLOCK) | (blk == 0)
    future = blk * SEL_BLOCK > pq
    score = jnp.where(forced, FORCE_SCORE, jnp.where(future, -1.0, score))
    score = jnp.where(blk < n_sel, score, -2.0)
    rank = jnp.zeros((rows, nsp), F32)
    for c in range(n_sel):
        sc = score[:, c:c + 1]
        ahead = (sc > score) | ((sc == score) & (blk > c))
        rank = rank + jnp.where(ahead, 1.0, 0.0)
    sel = jnp.where(rank < float(N_SELECT), 1.0, 0.0)
    for g in range(gs):
        sel_ref[0, g] = sel[g * tq:(g + 1) * tq]


def _cmpsel(z3, kcmp, vcmp, *, tq, pos0, n_sel, gs):
    bsz, t, _ = z3.shape
    ncp = kcmp.shape[1]
    nsp = -(-n_sel // HD) * HD
    w = gs * HPG * HD
    return pl.pallas_call(
        functools.partial(_cmpsel_body, tq=tq, ncp=ncp, nsp=nsp, n_sel=n_sel, pos0=pos0, gs=gs),
        grid=(bsz, G_NSA // gs, t // tq),
        in_specs=[pl.BlockSpec((1, tq, w), lambda b, g, i: (b, i, NQ // w + g)),
                  pl.BlockSpec((1, ncp, gs * HD), lambda b, g, i: (b, 0, g)),
                  pl.BlockSpec((1, ncp, gs * HD), lambda b, g, i: (b, 0, g))],
        out_specs=[pl.BlockSpec((1, tq, w), lambda b, g, i: (b, i, g)),
                   pl.BlockSpec((1, gs, tq, nsp), lambda b, g, i: (b, g, i, 0))],
        out_shape=[jax.ShapeDtypeStruct((bsz, t, H_NSA * HD), F32),
                   jax.ShapeDtypeStruct((bsz, G_NSA, t, nsp), F32)],
        compiler_params=_cparams(("parallel", "parallel", "parallel")),
    )(z3, kcmp, vcmp)


def _combine_body(oc_ref, os_ref, ow_ref, zs_ref, gb_ref, o_ref):
    gates = _sigmoid(zs_ref[...] + gb_ref[...])
    for h in range(H_NSA):
        sl = slice(h * HD, (h + 1) * HD)
        c0 = H_FOX + 3 * h
        o = (gates[:, c0:c0 + 1] * oc_ref[:, sl] + gates[:, c0 + 1:c0 + 2] * os_ref[:, sl]
             + gates[:, c0 + 2:c0 + 3] * ow_ref[:, sl])
        o_ref[:, sl] = o.astype(BF16)


def _nsa_combine(oc, osl, ow, zs, gb_pad):
    m, n = oc.shape
    tm = min(m, 512)
    big = pl.BlockSpec((tm, n), lambda i: (i, 0))
    return pl.pallas_call(
        _combine_body, grid=(m // tm,),
        in_specs=[big, big, big, pl.BlockSpec((tm, HD), lambda i: (i, 0)), pl.BlockSpec((1, HD), lambda i: (0, 0))],
        out_specs=big, out_shape=jax.ShapeDtypeStruct((m, n), BF16),
        compiler_params=_cparams(("parallel",)),
    )(oc, osl, ow, zs, gb_pad)


def _paged_body(*refs, mode, pg, n_steps, n_w, hpg_kv, t_new):
    it = iter(refs)
    next(it)
    q_ref = next(it)
    k_refs = [next(it) for _ in range(pg)]
    v_refs = [next(it) for _ in range(pg)]
    cl_refs = [next(it) for _ in range(pg)] if mode == "fox" else None
    tf_refs = [next(it) for _ in range(pg)] if mode == "fox" else None
    kn_ref, vn_ref = next(it), next(it)
    lfn_ref = next(it) if mode == "fox" else None
    sel_ref = next(it) if mode == "slc" else None
    o_ref = next(it)
    qb, m_scr, l_scr, acc, kn_scr, vn_scr, carry = (next(it) for _ in range(7))
    p = pl.program_id(1)
    rows = 4 * 8
    r2 = n_w * PAGE
    n_new = t_new * n_w

    def kv_match(ncols):
        row = lax.broadcasted_iota(jnp.int32, (rows, ncols), 0)
        col = lax.broadcasted_iota(jnp.int32, (rows, ncols), 1)
        return row, col, (col % n_w) == (row % 8) // hpg_kv

    @pl.when(p == 0)
    def _():
        qb[...] = (q_ref[0] * ATTN_SCALE).astype(BF16)
        m_scr[...] = jnp.full(m_scr.shape, NEG, F32)
        l_scr[...] = jnp.zeros(l_scr.shape, F32)
        acc[...] = jnp.zeros(acc.shape, F32)
        carry[...] = jnp.zeros(carry.shape, F32)
        kn_scr[...] = jnp.zeros(kn_scr.shape, BF16)
        vn_scr[...] = jnp.zeros(vn_scr.shape, BF16)

    def picked(page_idx, ncols):
        nsp = sel_ref.shape[2]
        jb = lax.broadcasted_iota(jnp.int32, (nsp, ncols), 0)
        key = lax.broadcasted_iota(jnp.int32, (nsp, ncols), 1) // n_w
        expand = jnp.where(jb == page_idx * (PAGE // SEL_BLOCK) + key // SEL_BLOCK, 1.0, 0.0).astype(BF16)
        return _dot(sel_ref[0].astype(BF16), expand) > 0.5

    def update(s_list, vb_list):
        m_old = m_scr[...]
        m_new = m_old
        for s in s_list:
            m_new = jnp.maximum(m_new, jnp.max(s, axis=-1, keepdims=True))
        a = jnp.exp(m_old - m_new)
        l = a * l_scr[...]
        o = a * acc[...]
        for s, vb in zip(s_list, vb_list):
            pr = jnp.exp(s - m_new)
            l = l + jnp.sum(pr, axis=-1, keepdims=True)
            o = o + _dot(pr.astype(BF16), vb)
        m_scr[...] = m_new
        l_scr[...] = l
        acc[...] = o

    _, _, match = kv_match(r2)
    s_list, vb_list = [], []
    c_run = carry[...] if mode == "fox" else None
    for pp in range(pg):
        s = _dot_nt(qb[...], k_refs[pp][0].astype(BF16))
        valid = match
        if mode == "fox":
            s = s - (c_run + cl_refs[pp][0])
            c_run = c_run + tf_refs[pp][0]
        if mode == "slc":
            valid = valid & picked(p * pg + pp, r2)
        s_list.append(jnp.where(valid, s, NEG))
        vb_list.append(v_refs[pp][0].astype(BF16))
    if mode == "fox":
        carry[...] = c_run
    update(s_list, vb_list)

    @pl.when(p == n_steps - 1)
    def _():
        kn_scr[0:n_new, :] = kn_ref[0].astype(BF16)
        vn_scr[0:n_new, :] = vn_ref[0].astype(BF16)
        s = _dot_nt(qb[...], kn_scr[...])
        row, col, valid = kv_match(HD)
        valid = valid & (col // n_w <= row // 8) & (col < n_new)
        if mode == "fox":
            rr = lax.broadcasted_iota(jnp.int32, (HD, HD), 0)
            cc = lax.broadcasted_iota(jnp.int32, (HD, HD), 1)
            pre = jnp.where((rr % n_w == cc % n_w) & (rr // n_w <= cc // n_w), 1.0, 0.0).astype(BF16)
            c_new = _dot3_l(jnp.broadcast_to(lfn_ref[0], (8, HD)), pre)[0:1, :] + carry[:, 0:HD]
            s = s - c_new
        if mode == "slc":
            valid = valid & picked(n_steps * pg, HD)
        update([jnp.where(valid, s, NEG)], [vn_scr[...]])
        o_ref[0] = acc[...] / l_scr[...]


def _paged_attn(mode, q_rows, page_table, k_pool, v_pool, pool_off, k_new, v_new, *, hpg_kv, pg,
                c_local=None, c_total=None, lf_new=None, sel_rows=None):
    bsz = q_rows.shape[0]
    n_pages = page_table.shape[1]
    pg = min(pg, n_pages)
    n_steps = n_pages // pg
    r2 = k_pool.shape[1]
    n_w = r2 // PAGE
    n_new = k_new.shape[1]
    page = lambda pp: (lambda b, p, pt: (pool_off + pt[b, p * pg + pp], 0, 0))
    per_b = lambda b, p, pt: (b, 0, 0)
    in_specs = [pl.BlockSpec((1, 32, HD), per_b)]
    in_specs += [pl.BlockSpec((1, r2, HD), page(pp)) for pp in range(pg)] * 2
    args = [q_rows] + [k_pool] * pg + [v_pool] * pg
    if mode == "fox":
        in_specs += [pl.BlockSpec((1, 1, r2), page(pp)) for pp in range(pg)] * 2
        args += [c_local] * pg + [c_total] * pg
    in_specs += [pl.BlockSpec((1, n_new, HD), per_b)] * 2
    args += [k_new, v_new]
    if mode == "fox":
        in_specs.append(pl.BlockSpec((1, 1, HD), per_b))
        args.append(lf_new)
    if mode == "slc":
        in_specs.append(pl.BlockSpec((1, 32, sel_rows.shape[2]), per_b))
        args.append(sel_rows)
    grid_spec = pltpu.PrefetchScalarGridSpec(
        num_scalar_prefetch=1, grid=(bsz, n_steps), in_specs=in_specs,
        out_specs=pl.BlockSpec((1, 32, HD), per_b),
        scratch_shapes=[pltpu.VMEM((32, HD), BF16), pltpu.VMEM((32, 1), F32), pltpu.VMEM((32, 1), F32),
                        pltpu.VMEM((32, HD), F32), pltpu.VMEM((PAGE, HD), BF16), pltpu.VMEM((PAGE, HD), BF16),
                        pltpu.VMEM((1, r2), F32)])
    return pl.pallas_call(
        functools.partial(_paged_body, mode=mode, pg=pg, n_steps=n_steps, n_w=n_w, hpg_kv=hpg_kv, t_new=n_new // n_w),
        grid_spec=grid_spec, out_shape=jax.ShapeDtypeStruct((bsz, 32, HD), F32),
        compiler_params=_cparams(("parallel", "arbitrary")),
    )(page_table, *args)


def _pool_prefix_body(x_ref, cl_ref, ct_ref, w_scr, *, n_w):
    r2 = x_ref.shape[1]

    @pl.when(pl.program_id(0) == 0)
    def _():
        r = lax.broadcasted_iota(jnp.int32, (r2, 2 * r2), 0)
        c = lax.broadcasted_iota(jnp.int32, (r2, 2 * r2), 1)
        same = (r % n_w) == (c % n_w)
        local = same & (r // n_w <= c // n_w) & (c < r2)
        w_scr[...] = jnp.where(local | (same & (c >= r2)), 1.0, 0.0).astype(BF16)

    y = _dot3_l(x_ref[...], w_scr[...])
    cl_ref[...] = y[:, :r2]
    ct_ref[...] = y[:, r2:]


def _pool_prefix(x, n_w):
    n, r2 = x.shape
    tm = next((c for c in (256, 128, 64, 32, 16, 8) if n % c == 0), n)
    spec = pl.BlockSpec((tm, r2), lambda i: (i, 0))
    return pl.pallas_call(
        functools.partial(_pool_prefix_body, n_w=n_w), grid=(n // tm,),
        in_specs=[spec], out_specs=[spec, spec],
        out_shape=[jax.ShapeDtypeStruct((n, r2), F32)] * 2,
        scratch_shapes=[pltpu.VMEM((r2, 2 * r2), BF16)],
        compiler_params=_cparams(("arbitrary",)),
    )(x)


def _gla_body(*refs, mode, t_in, chunk, dv, t_real, has_state, hb):
    it = iter(refs)
    q_ref, k_ref, v_ref, og_ref = next(it), next(it), next(it), next(it)
    if mode == "gla":
        zs_ref, wa_ref, ba_ref = next(it), next(it), next(it)
    else:
        lb_ref = next(it)
    gain_ref = next(it)
    s0_ref = next(it) if has_state else None
    o_ref, s_ref, st_scr, q_scr, k_scr, g_scr, o_scr = (next(it) for _ in range(7))
    tb = q_scr.shape[1]
    n_chunks = tb // chunk
    pad = tb - t_in
    tri = lax.broadcasted_iota(jnp.int32, (chunk, chunk), 0) >= lax.broadcasted_iota(jnp.int32, (chunk, chunk), 1)
    if mode == "hg":
        lbr = lb_ref[...]
        e = jnp.exp(lbr - jnp.max(lbr, axis=0, keepdims=True))
        lb_all = (e / jnp.sum(e, axis=0, keepdims=True))[0:1, :]
    ti = pl.program_id(2)

    @pl.when(ti == 0)
    def _():
        for h in range(hb):
            st_scr[h] = s0_ref[0, h].T if has_state else jnp.zeros((dv, HD), F32)

    def block(ref, sl):
        x = ref[0, :, sl]
        if pad:
            x = jnp.concatenate([x, jnp.zeros((pad, x.shape[1]), F32)], axis=0)
        return x

    live = (ti * t_in + lax.broadcasted_iota(jnp.int32, (tb, HD), 0)) < t_real
    if mode == "gla":
        ga = block(zs_ref, slice(0, HD)).astype(BF16)
    for h in range(hb):
        ks = slice(h * HD, (h + 1) * HD)
        qr, kr = block(q_ref, ks), block(k_ref, ks)
        if mode == "gla":
            g = _log_sigmoid(_dot(ga, wa_ref[:, ks]) + ba_ref[:, ks]) / GLA_GATE_NORM
            q, k = qr, kr
        else:
            lb = lb_all[:, ks]
            f = lb + (1.0 - lb) * _sigmoid(kr)
            q, k, g = qr * _sigmoid(qr), 1.0 - f, jnp.log(f)
        q_scr[h] = q
        k_scr[h] = jnp.where(live, k, 0.0)
        g_scr[h] = jnp.where(live, g, 0.0)

    def body(c, _):
        r0 = pl.multiple_of(c * chunk, chunk)
        rows = pl.ds(r0, chunk)
        for h in range(hb):
            vs = slice(h * dv, (h + 1) * dv)
            q, k, g = q_scr[h, rows, :], k_scr[h, rows, :], g_scr[h, rows, :]
            if pad:
                v = block(v_ref, vs)
            else:
                v = v_ref[0, rows, vs]
            b = _cumsum_rows(g)
            bm = b[chunk // 2 - 1:chunk // 2, :]
            bl = b[chunk - 1:chunk, :]
            qe = (q * jnp.exp(jnp.minimum(b - bm, EXP_CLAMP))).astype(BF16)
            ke = (k * jnp.exp(jnp.minimum(bm - b, EXP_CLAMP))).astype(BF16)
            a = jnp.where(tri, _dot_nt(qe, ke), 0.0)
            vb = v.astype(BF16)
            st = st_scr[h]
            o_scr[h, rows, :] = _dot(a.astype(BF16), vb) + _dot_nt((q * jnp.exp(b)).astype(BF16), st.astype(BF16))
            kd = (k * jnp.exp(bl - b)).astype(BF16)
            st_scr[h] = st * jnp.exp(bl) + _dot_tn(vb, kd)
        return 0

    lax.fori_loop(0, n_chunks, body, 0)

    for h in range(hb):
        vs = slice(h * dv, (h + 1) * dv)
        o = o_scr[h, 0:t_in, :]
        og = og_ref[0, :, vs]
        o = o * lax.rsqrt(jnp.mean(o * o, axis=-1, keepdims=True) + EPS) * gain_ref[...]
        o_ref[0, :, vs] = (o * (og * _sigmoid(og) if mode == "gla" else _sigmoid(og))).astype(BF16)

    @pl.when(ti == pl.num_programs(2) - 1)
    def _():
        for h in range(hb):
            s_ref[0, h] = st_scr[h].T


def _gla(mode, z3, zs3, cols, n_heads, dv, extra, gain, s0, *, t_real, chunk, hb):
    bsz, t_all, _ = z3.shape
    tc = min(t_all, 512)
    qc, kc, vc, oc = cols
    has_state = s0 is not None
    wk, wv = hb * HD, hb * dv
    in_specs = [pl.BlockSpec((1, tc, wk), lambda b, h, t: (b, t, qc // wk + h)),
                pl.BlockSpec((1, tc, wk), lambda b, h, t: (b, t, kc // wk + h)),
                pl.BlockSpec((1, tc, wv), lambda b, h, t: (b, t, vc // wv + h)),
                pl.BlockSpec((1, tc, wv), lambda b, h, t: (b, t, oc // wv + h))]
    args = [z3, z3, z3, z3]
    if mode == "gla":
        wa, ba = extra
        in_specs += [pl.BlockSpec((1, tc, HD), lambda b, h, t: (b, t, 0)),
                     pl.BlockSpec((HD, wk), lambda b, h, t: (0, h)),
                     pl.BlockSpec((1, wk), lambda b, h, t: (0, h))]
        args += [zs3, wa, ba]
    else:
        in_specs.append(pl.BlockSpec((8, wk), lambda b, h, t: (0, h)))
        args.append(extra)
    in_specs.append(pl.BlockSpec((1, dv), lambda b, h, t: (0, 0)))
    args.append(gain)
    st_spec = pl.BlockSpec((1, hb, HD, dv), lambda b, h, t: (b, h, 0, 0))
    if has_state:
        in_specs.append(st_spec)
        args.append(s0)
    return pl.pallas_call(
        functools.partial(_gla_body, mode=mode, t_in=tc, chunk=chunk, dv=dv, t_real=t_real, has_state=has_state,
                          hb=hb),
        grid=(bsz, n_heads // hb, t_all // tc),
        in_specs=in_specs,
        out_specs=[pl.BlockSpec((1, tc, wv), lambda b, h, t: (b, t, h)), st_spec],
        out_shape=[jax.ShapeDtypeStruct((bsz, t_all, n_heads * dv), BF16),
                   jax.ShapeDtypeStruct((bsz, n_heads, HD, dv), F32)],
        scratch_shapes=[pltpu.VMEM((hb, dv, HD), F32)] + [pltpu.VMEM((hb, max(tc, chunk), HD), F32)] * 3
        + [pltpu.VMEM((hb, max(tc, chunk), dv), F32)],
        compiler_params=_cparams(("parallel", "parallel", "arbitrary")),
    )(*args)


def _split_cols(w, sizes):
    outs, off = [], 0
    for s in sizes:
        outs.append(w[:, off:off + s])
        off += s
    return outs


def _pad_cols(w, n):
    return jnp.pad(w, ((0, 0), (0, n - w.shape[1])))


def _row(v):
    return v.reshape(1, -1).astype(F32)


def _even_weights(w_in, fq_gain, fk_gain, nq_gain, nk_gain):
    fq, fk, fv, ff, nq, kc, vc, ks, vs, kw, vw, ng = _split_cols(w_in, EVEN_SIZES)
    big = jnp.concatenate([fq, fk, fv, nq, kc, vc, ks, vs, kw, vw], axis=1).astype(BF16)
    small = _pad_cols(jnp.concatenate([ff, ng], axis=1), HD).astype(BF16)
    ones = lambda n: jnp.ones((n,), F32)
    gain = jnp.concatenate([jnp.tile(fq_gain, H_FOX), jnp.tile(fk_gain, H_FOX), ones(H_FOX * HD),
                            jnp.tile(nq_gain, H_NSA), ones(2 * G_NSA * HD), jnp.tile(nk_gain[1], G_NSA),
                            ones(G_NSA * HD), jnp.tile(nk_gain[2], G_NSA), ones(G_NSA * HD)])
    z, o = jnp.zeros, jnp.ones
    flag = jnp.concatenate([o((2 * H_FOX * HD,), F32), z((H_FOX * HD,), F32), o((H_NSA * HD,), F32),
                            z((2 * G_NSA * HD,), F32), o((G_NSA * HD,), F32), z((G_NSA * HD,), F32),
                            o((G_NSA * HD,), F32), z((G_NSA * HD,), F32)])
    return big, small, _row(gain), _row(flag)


def _odd_weights(w_in):
    gq, gk, gv, ga, gg, hq, hf, hi, hg = _split_cols(w_in, ODD_SIZES)
    big = jnp.concatenate([gq, gk, gv, gg, hq, hf, hi, hg], axis=1).astype(BF16)
    small = _pad_cols(ga, HD).astype(BF16)
    gain = jnp.concatenate([jnp.full((H_GLA * DK_GLA,), DK_GLA ** -0.5, F32), jnp.ones((N_BIG_ODD - H_GLA * DK_GLA,), F32)])
    return big, small, _row(gain), jnp.zeros((1, N_BIG_ODD), F32)


def _cmp_weights(pe, w):
    half = CMP_STRIDE * HD
    wcat = jnp.concatenate([w[:half], w[half:]], axis=1).astype(BF16)
    pe8 = jnp.pad(pe.reshape(2, half), ((0, 6), (0, 0)))
    return pe8, wcat


def _ffn(x, g, w1, w3, w2, layer):
    return _mm_res([_ffn_up(x, g, w1, w3, layer)], w2, layer, x)


def _even_layer(x, bsz, t, t_real, ew, past):
    (w_big, w_small, cgain, cflag, g_mix, w_out, b_f_pad, gb_pad, pe_k, wc_k, pe_v, wc_v, nk0) = ew
    z, zs = _proj(x, g_mix, w_big, w_small, cgain, cflag, True)
    z3 = z.reshape(bsz, t, N_BIG_EVEN)
    zs3 = zs.reshape(bsz, t, HD)
    cmp_params = ((pe_k, wc_k, nk0), (pe_v, wc_v, jnp.ones((1, HD), F32)))
    if past is None:
        lf, c = _fox_prep(zs3, b_f_pad)
        logf_new = lf[:, :, :H_FOX]
        tk = min(t, 512)
        bias = jnp.swapaxes(c[:, :, :H_FOX], 1, 2).reshape(bsz * H_FOX, t // tk, tk)
        o_fox = _flash("fox", z3, FQ, z3, FK, z3, FV, n_kv=H_FOX, hpg=1, tq=min(t, 512), tk=tk, bias=bias,
                       out_dtype=BF16)
        n_cmp = t // CMP_STRIDE
        mn = -(-n_cmp // HD) * HD
        kcmp, vcmp = _compress(z3, (KC, VC), cmp_params, (True, False), mn=mn)
        tqn = min(t, 512 // HPG)
        o_cmp, sel = _cmpsel(z3, kcmp, vcmp, tq=min(t, 256), pos0=0, n_sel=-(-t // SEL_BLOCK), gs=1)
        o_slc = _flash("slc", z3, NQ, z3, KS, z3, VS, n_kv=G_NSA, hpg=HPG, tq=tqn, tk=tk, sel=sel)
        o_win = _flash("win", z3, NQ, z3, KW, z3, VW, n_kv=G_NSA, hpg=HPG, tq=tqn, tk=tk)
        w_buf = WINDOW
        kw_new = jnp.concatenate([jnp.zeros((bsz, w_buf, G_NSA * HD), F32), z3[:, :, KW:KW + G_NSA * HD]], axis=1)[:, -w_buf:]
        vw_new = jnp.concatenate([jnp.zeros((bsz, w_buf, G_NSA * HD), F32), z3[:, :, VW:VW + G_NSA * HD]], axis=1)[:, -w_buf:]
        o_fox = o_fox.reshape(bsz * t, H_FOX * HD)
    else:
        (page_table, pool_off, fk_pool, fv_pool, lf_pool, kc_pool, vc_pool, ks_pool, vs_pool, buf_k, buf_v) = past
        p_len = page_table.shape[1] * PAGE
        lf_all, _ = _fox_prep(zs.reshape(1, bsz * t, HD), b_f_pad)
        lf_new = lf_all.reshape(bsz, t, HD)[:, :t_real, :H_FOX]
        logf_new = lf_new
        lfn_pad = _pad_cols(lf_new.reshape(bsz, t_real * H_FOX), HD).reshape(bsz, 1, HD)
        c_local, c_total = _pool_prefix(lf_pool, H_FOX)
        rows_of = lambda c0, n: z3[:, :t_real, c0:c0 + n * HD].reshape(bsz, t_real * n, HD)
        o_fox = _paged_attn("fox", rows_of(FQ, H_FOX), page_table, fk_pool, fv_pool, pool_off,
                            rows_of(FK, H_FOX), rows_of(FV, H_FOX), hpg_kv=1, pg=8,
                            c_local=c_local[:, None, :], c_total=c_total[:, None, :], lf_new=lfn_pad)
        o_fox = o_fox.reshape(bsz, t_real, H_FOX * HD)
        n_cmp = -(-(p_len + t_real) // CMP_STRIDE)
        mn = -(-n_cmp // HD) * HD
        kcmp, vcmp = _compress(z3, (KC, VC), cmp_params, (True, False), mn=mn, page_table=page_table,
                               pools=(kc_pool, vc_pool), pool_off=pool_off, t_new=t_real)
        n_sel = -(-(p_len + t_real) // SEL_BLOCK)
        o_cmp, sel = _cmpsel(z3, kcmp, vcmp, tq=t, pos0=p_len, n_sel=n_sel, gs=G_NSA)
        nsp = sel.shape[-1]
        sel_rows = jnp.broadcast_to(jnp.swapaxes(sel[:, :, :t_real], 1, 2)[:, :, :, None, :],
                                    (bsz, t_real, G_NSA, HPG, nsp)).reshape(bsz, 32, nsp)
        o_slc = _paged_attn("slc", rows_of(NQ, H_NSA), page_table, ks_pool, vs_pool, pool_off,
                            rows_of(KS, G_NSA), rows_of(VS, G_NSA), hpg_kv=HPG, pg=16,
                            sel_rows=sel_rows).reshape(bsz, t_real, H_NSA * HD)
        kw_all = jnp.concatenate([buf_k, z3[:, :t_real, KW:KW + G_NSA * HD]], axis=1)
        vw_all = jnp.concatenate([buf_v, z3[:, :t_real, VW:VW + G_NSA * HD]], axis=1)
        w_buf = buf_k.shape[1]
        kw_new, vw_new = kw_all[:, -w_buf:], vw_all[:, -w_buf:]
        lk = -(-(w_buf + t) // HD) * HD
        padk = lambda a: jnp.pad(a, ((0, 0), (0, lk - a.shape[1]), (0, 0)))
        o_win = _flash("win", z3, NQ, padk(kw_all), 0, padk(vw_all), 0, n_kv=G_NSA, hpg=HPG, tq=t, tk=HD, off=w_buf)
        padt = lambda a: jnp.pad(a, ((0, 0), (0, t - t_real), (0, 0)))
        o_fox = padt(o_fox).astype(BF16).reshape(bsz * t, H_FOX * HD)
        o_slc = padt(o_slc)
    m = bsz * t
    o_nsa = _nsa_combine(o_cmp.reshape(m, -1), o_slc.reshape(m, -1), o_win.reshape(m, -1), zs, gb_pad)
    x = _mm_res([o_fox, o_nsa], w_out, 0, x)
    zr = z3[:, :t_real]
    grp = lambda c0: zr[:, :, c0:c0 + G_NSA * HD].reshape(1, bsz, t_real, G_NSA, HD)
    fox = lambda c0: zr[:, :, c0:c0 + H_FOX * HD].reshape(1, bsz, t_real, H_FOX, HD)
    w_rows = kw_new.shape[1]
    outs = (fox(FK), fox(FV), logf_new[None], grp(KC), grp(VC), grp(KS), grp(VS),
            kw_new.reshape(1, bsz, w_rows, G_NSA, HD), vw_new.reshape(1, bsz, w_rows, G_NSA, HD))
    return x, outs


def _odd_layer(x, bsz, t, t_real, ow, state):
    (w_big, w_small, cgain, cflag, g_mix, w_out, wa_pad, ba, gla_gain, lb_pad, hg_gain) = ow
    z, zs = _proj(x, g_mix, w_big, w_small, cgain, cflag, False)
    z3 = z.reshape(bsz, t, N_BIG_ODD)
    zs3 = zs.reshape(bsz, t, HD)
    s_gla, s_hg = state if state is not None else (None, None)
    chunk = 64 if t >= 64 else 128
    o_gla, s_gla = _gla("gla", z3, zs3, (GQ, GK, GV, GG), H_GLA, DV_GLA, (wa_pad, ba), gla_gain, s_gla,
                        t_real=t_real, chunk=chunk, hb=H_GLA)
    o_hg, s_hg = _gla("hg", z3, zs3, (HQ, HF, HI, HO), H_HG, DV_HG, lb_pad, hg_gain, s_hg,
                      t_real=t_real, chunk=chunk, hb=H_HG)
    m = bsz * t
    x = _mm_res([o_gla.reshape(m, -1), o_hg.reshape(m, -1)], w_out, 0, x)
    return x, (s_gla[None], s_hg[None])


def kernel(x_prompt, x_sample, cache_fox_k, cache_fox_v, cache_fox_logf, cache_nsa_kc, cache_nsa_vc, cache_nsa_ks, cache_nsa_vs, state_nsa_kw, state_nsa_vw, state_gla, state_hgrn, page_table, norm_mix, norm_ffn, w_in_even, w_out_even, fox_b_f, fox_q_gain, fox_k_gain, nsa_q_gain, nsa_k_gain, nsa_cmp_pe_k, nsa_cmp_pe_v, nsa_cmp_wk, nsa_cmp_wv, nsa_gate_b, w_in_odd, w_out_odd, gla_wa2, gla_ba, gla_norm, hgrn_lb, hgrn_norm, ffn_w1, ffn_w3, ffn_w2):
    bp, tp, d = x_prompt.shape
    bs, ts, _ = x_sample.shape
    n_pool = cache_fox_k.shape[1]
    ts_pad = 8
    pe_k, wc_k = _cmp_weights(nsa_cmp_pe_k[0], nsa_cmp_wk[0])
    pe_v, wc_v = _cmp_weights(nsa_cmp_pe_v[0], nsa_cmp_wv[0])
    b_f_pad = _pad_cols(_row(fox_b_f[0]), HD)
    gb_pad = _pad_cols(jnp.concatenate([jnp.zeros((1, H_FOX), F32), _row(nsa_gate_b[0])], axis=1), HD)
    ew = _even_weights(w_in_even[0], fox_q_gain[0], fox_k_gain[0], nsa_q_gain[0], nsa_k_gain[0]) + (
        _row(norm_mix[0]), w_out_even, b_f_pad, gb_pad, pe_k, wc_k, pe_v, wc_v, _row(nsa_k_gain[0, 0]))
    wa_pad = jnp.pad(gla_wa2[0], ((0, HD - GLA_RANK), (0, 0))).astype(BF16)
    lb_pad = jnp.pad(hgrn_lb.astype(F32), ((0, 8 - hgrn_lb.shape[0]), (0, 0)), constant_values=-1e30)
    ow = _odd_weights(w_in_odd[0]) + (_row(norm_mix[1]), w_out_odd, wa_pad, _row(gla_ba[0]),
                                      _row(gla_norm[0]), lb_pad, _row(hgrn_norm[0]))
    ffn = [(_row(norm_ffn[i]), ffn_w1, ffn_w3, ffn_w2, i) for i in range(2)]

    flat = lambda c: c.reshape((c.shape[0] * c.shape[1], PAGE * c.shape[3], HD))
    lf_pool = cache_fox_logf.reshape(-1, PAGE * H_FOX)
    win = lambda s: s.reshape(bs, s.shape[2], G_NSA * HD)
    past = (page_table, 0, flat(cache_fox_k), flat(cache_fox_v), lf_pool, flat(cache_nsa_kc), flat(cache_nsa_vc),
            flat(cache_nsa_ks), flat(cache_nsa_vs), win(state_nsa_kw), win(state_nsa_vw))

    def run(x, bsz, t, t_real, past_, state_):
        x, ev = _even_layer(x, bsz, t, t_real, ew, past_)
        x = _ffn(x, *ffn[0])
        x, od = _odd_layer(x, bsz, t, t_real, ow, state_)
        x = _ffn(x, *ffn[1])
        return x, ev, od

    yp, ev_p, od_p = run(x_prompt.reshape(bp * tp, d), bp, tp, tp, None, None)
    xs = jnp.pad(x_sample, ((0, 0), (0, ts_pad - ts), (0, 0))).reshape(bs * ts_pad, d)
    ys, ev_s, od_s = run(xs, bs, ts_pad, ts, past, (state_gla.reshape(state_gla.shape[1:]),
                                                    state_hgrn.reshape(state_hgrn.shape[1:])))
    y_prompt = yp.reshape(bp, tp, d)
    y_sample = ys.reshape(bs, ts_pad, d)[:, :ts]
    outs = [y_prompt, y_sample]
    for a, b in zip(ev_p, ev_s):
        outs += [a, b]
    for a, b in zip(od_p, od_s):
        outs += [a, b]
    return tuple(outs)
```

```python
import functools

import jax
import jax.numpy as jnp
from jax import lax
from jax.experimental import pallas as pl
from jax.experimental.pallas import tpu as pltpu

F32 = jnp.float32
BF16 = jnp.bfloat16

D_MODEL = 2048
HD = 128
H_FOX = 8
H_NSA = 8
G_NSA = 2
HPG = H_NSA // G_NSA
CMP_STRIDE = 16
CMP_BLOCK = 2 * CMP_STRIDE
SEL_BLOCK = 64
N_SELECT = 16
WINDOW = 512
FORCE_SCORE = 1.0e4
H_GLA = 4
DK_GLA = 128
DV_GLA = 256
GLA_RANK = 16
GLA_GATE_NORM = 16.0
H_HG = 8
DK_HG = 128
DV_HG = 128
PAGE = 128
EVEN_SIZES = (H_FOX * HD, H_FOX * HD, H_FOX * HD, H_FOX, H_NSA * HD,
              G_NSA * HD, G_NSA * HD, G_NSA * HD, G_NSA * HD, G_NSA * HD, G_NSA * HD, 3 * H_NSA)
ODD_SIZES = (H_GLA * DK_GLA, H_GLA * DK_GLA, H_GLA * DV_GLA, GLA_RANK, H_GLA * DV_GLA,
             H_HG * DK_HG, H_HG * DK_HG, H_HG * DV_HG, H_HG * DV_HG)
FQ, FK, FV, NQ, KC, VC, KS, VS, KW, VW = 0, 1024, 2048, 3072, 4096, 4352, 4608, 4864, 5120, 5376
N_BIG_EVEN = 5632
GQ, GK, GV, GG, HQ, HF, HI, HO = 0, 512, 1024, 2048, 3072, 4096, 5120, 6144
N_BIG_ODD = 7168
ATTN_SCALE = HD ** -0.5
NEG = -1.0e30
EPS = 1.0e-6
EXP_CLAMP = 80.0
VMEM_LIMIT = 56 * 1024 * 1024


def _cparams(sem):
    return pltpu.CompilerParams(dimension_semantics=sem, vmem_limit_bytes=VMEM_LIMIT)


def _dot(a, b):
    return jnp.dot(a, b, preferred_element_type=F32)


def _dot_nt(a, b):
    return lax.dot_general(a, b, (((1,), (1,)), ((), ())), preferred_element_type=F32)


def _dot_tn(a, b):
    return lax.dot_general(a, b, (((0,), (0,)), ((), ())), preferred_element_type=F32)


def _split3(x):
    hi = x.astype(BF16)
    r = x - hi.astype(F32)
    mid = r.astype(BF16)
    lo = (r - mid.astype(F32)).astype(BF16)
    return hi, mid, lo


def _dot3_l(x, w):
    hi, mid, lo = _split3(x)
    return _dot(hi, w) + _dot(mid, w) + _dot(lo, w)


def _dot3_r(w, x):
    hi, mid, lo = _split3(x)
    return _dot(w, hi) + _dot(w, mid) + _dot(w, lo)


def _sigmoid(x):
    return 1.0 / (1.0 + jnp.exp(-x))


def _log_sigmoid(x):
    return jnp.minimum(x, 0.0) - jnp.log1p(jnp.exp(-jnp.abs(x)))


def _cumsum_rows(x):
    n = x.shape[0]
    row = lax.broadcasted_iota(jnp.int32, x.shape, 0)
    s = 1
    while s < n:
        x = x + jnp.where(row >= s, pltpu.roll(x, s, axis=0), 0.0)
        s *= 2
    return x


def _tri(n, upper):
    r = lax.broadcasted_iota(jnp.int32, (n, n), 0)
    c = lax.broadcasted_iota(jnp.int32, (n, n), 1)
    return jnp.where((r <= c) if upper else (r >= c), 1.0, 0.0).astype(BF16)


def _proj_body(x_ref, g_ref, w_ref, ws_ref, cg_ref, cf_ref, z_ref, zs_ref, h_scr, *, tn, has_norm):
    @pl.when(pl.program_id(1) == 0)
    def _():
        x = x_ref[...]
        r = lax.rsqrt(jnp.mean(x * x, axis=-1, keepdims=True) + EPS)
        h = (x * r * g_ref[...]).astype(BF16)
        h_scr[...] = h
        zs_ref[...] = _dot(h, ws_ref[...])

    z = _dot(h_scr[...], w_ref[...])
    if not has_norm:
        z_ref[...] = z * cg_ref[...]
        return
    for c in range(tn // HD):
        sl = slice(c * HD, (c + 1) * HD)
        zc = z[:, sl]
        r = lax.rsqrt(jnp.mean(zc * zc, axis=-1, keepdims=True) + EPS)
        f = cf_ref[:, sl]
        z_ref[:, sl] = zc * (f * r + (1.0 - f)) * cg_ref[:, sl]


def _proj(x, g, w_big, w_small, col_gain, col_flag, has_norm):
    m, d = x.shape
    n = w_big.shape[1]
    tm = min(m, 1024)
    tn = 512
    return pl.pallas_call(
        functools.partial(_proj_body, tn=tn, has_norm=has_norm),
        grid=(m // tm, n // tn),
        in_specs=[pl.BlockSpec((tm, d), lambda i, j: (i, 0)),
                  pl.BlockSpec((1, d), lambda i, j: (0, 0)),
                  pl.BlockSpec((d, tn), lambda i, j: (0, j)),
                  pl.BlockSpec((d, HD), lambda i, j: (0, 0)),
                  pl.BlockSpec((1, tn), lambda i, j: (0, j)),
                  pl.BlockSpec((1, tn), lambda i, j: (0, j))],
        out_specs=[pl.BlockSpec((tm, tn), lambda i, j: (i, j)),
                   pl.BlockSpec((tm, HD), lambda i, j: (i, 0))],
        out_shape=[jax.ShapeDtypeStruct((m, n), F32), jax.ShapeDtypeStruct((m, HD), F32)],
        scratch_shapes=[pltpu.VMEM((tm, d), BF16)],
        compiler_params=_cparams(("parallel", "arbitrary")),
    )(x, g, w_big, w_small, col_gain, col_flag)


def _ffn_up_body(x_ref, g_ref, w1_ref, w3_ref, o_ref, h_scr):
    @pl.when(pl.program_id(1) == 0)
    def _():
        x = x_ref[...]
        r = lax.rsqrt(jnp.mean(x * x, axis=-1, keepdims=True) + EPS)
        h_scr[...] = (x * r * g_ref[...]).astype(BF16)

    h = h_scr[...]
    a = _dot(h, w1_ref[...].astype(BF16))
    b = _dot(h, w3_ref[...].astype(BF16))
    o_ref[...] = (a * _sigmoid(a) * b).astype(BF16)


def _ffn_up(x, g, w1, w3, layer):
    m, d = x.shape
    n = w1.shape[2]
    tm = min(m, 1024)
    tn = 512
    return pl.pallas_call(
        _ffn_up_body,
        grid=(m // tm, n // tn),
        in_specs=[pl.BlockSpec((tm, d), lambda i, j: (i, 0)),
                  pl.BlockSpec((1, d), lambda i, j: (0, 0)),
                  pl.BlockSpec((None, d, tn), lambda i, j: (layer, 0, j)),
                  pl.BlockSpec((None, d, tn), lambda i, j: (layer, 0, j))],
        out_specs=pl.BlockSpec((tm, tn), lambda i, j: (i, j)),
        out_shape=jax.ShapeDtypeStruct((m, n), BF16),
        scratch_shapes=[pltpu.VMEM((tm, d), BF16)],
        compiler_params=_cparams(("parallel", "arbitrary")),
    )(x, g, w1, w3)


def _mm_res_body(*refs, n_in):
    res_ref = refs[2 * n_in]
    o_ref = refs[2 * n_in + 1]
    wb_refs = refs[2 * n_in + 2:]

    @pl.when(pl.program_id(1) == 0)
    def _():
        for w_ref, wb_ref in zip(refs[n_in:2 * n_in], wb_refs):
            wb_ref[...] = w_ref[...].astype(BF16)

    acc = res_ref[...]
    for a_ref, wb_ref in zip(refs[:n_in], wb_refs):
        acc = acc + _dot(a_ref[...], wb_ref[...])
    o_ref[...] = acc


def _mm_res(a_list, w, layer, res):
    m, n = res.shape
    tm = min(m, 512)
    tn = 512
    in_specs, w_specs, scratch, off = [], [], [], 0
    for a in a_list:
        k = a.shape[1]
        in_specs.append(pl.BlockSpec((tm, k), lambda j, i: (i, 0)))
        w_specs.append(pl.BlockSpec((None, k, tn), lambda j, i, _o=off // k: (layer, _o, j)))
        scratch.append(pltpu.VMEM((k, tn), BF16))
        off += k
    return pl.pallas_call(
        functools.partial(_mm_res_body, n_in=len(a_list)),
        grid=(n // tn, m // tm),
        in_specs=in_specs + w_specs + [pl.BlockSpec((tm, tn), lambda j, i: (i, j))],
        out_specs=pl.BlockSpec((tm, tn), lambda j, i: (i, j)),
        out_shape=jax.ShapeDtypeStruct((m, n), F32),
        scratch_shapes=scratch,
        compiler_params=_cparams(("parallel", "arbitrary")),
    )(*a_list, *([w] * len(a_list)), res)


def _fox_prep_body(zs_ref, b_ref, lf_ref, c_ref, carry, *, tc):
    @pl.when(pl.program_id(1) == 0)
    def _():
        carry[...] = jnp.zeros_like(carry)

    lf = _log_sigmoid(zs_ref[0] + b_ref[...])
    c = _dot3_r(_tri(tc, upper=False), lf) + carry[...]
    lf_ref[0] = lf
    c_ref[0] = c
    carry[...] = c[tc - 1:tc, :]


def _fox_prep(zs3, b_pad):
    bsz, t, _ = zs3.shape
    tc = min(t, 256)
    spec = pl.BlockSpec((1, tc, HD), lambda b, i: (b, i, 0))
    return pl.pallas_call(
        functools.partial(_fox_prep_body, tc=tc),
        grid=(bsz, t // tc),
        in_specs=[spec, pl.BlockSpec((1, HD), lambda b, i: (0, 0))],
        out_specs=[spec, spec],
        out_shape=[jax.ShapeDtypeStruct(zs3.shape, F32)] * 2,
        scratch_shapes=[pltpu.VMEM((1, HD), F32)],
        compiler_params=_cparams(("parallel", "arbitrary")),
    )(zs3, b_pad)


def _flash_body(*refs, mode, tq, tk, hpg, off):
    it = iter(refs)
    q_ref, k_ref, v_ref = next(it), next(it), next(it)
    bias_ref = next(it) if mode == "fox" else None
    sel_ref = next(it) if mode == "slc" else None
    o_ref, kb, vb = next(it), next(it), next(it)
    ex_scr = next(it) if mode == "slc" else None
    i = pl.program_id(2)

    @pl.when(i == 0)
    def _():
        kb[...] = k_ref[0].astype(BF16)
        vb[...] = v_ref[0].astype(BF16)
        if mode == "slc":
            for b in range(ex_scr.shape[0]):
                cb = lax.broadcasted_iota(jnp.int32, (HD, tk), 0)
                kk = lax.broadcasted_iota(jnp.int32, (HD, tk), 1)
                ex_scr[b] = jnp.where(cb == (b * tk + kk) // SEL_BLOCK, 2.0 ** 100, 0.0).astype(BF16)

    q = q_ref[0]
    if hpg > 1:
        q = jnp.concatenate([q[:, h * HD:(h + 1) * HD] for h in range(hpg)], axis=0)
    qa = (q * ATTN_SCALE).astype(BF16)
    tile_rows = (lambda x: jnp.concatenate([x] * hpg, axis=0)) if hpg > 1 else (lambda x: x)
    q_lo = off + i * tq
    t_k = kb.shape[0]
    if mode == "slc":
        unpicked = (sel_ref[0, 0] - 1.0).astype(BF16)

    def attend(blocks):
        ss = []
        for b, (ks, size, masked) in enumerate(blocks):
            s = _dot_nt(qa, kb[pl.ds(ks, size), :])
            if mode == "fox":
                s = s - bias_ref[0, b:b + 1, :]
            if mode == "slc":
                s = s + tile_rows(_dot(unpicked, ex_scr[b]))
            if masked:
                kidx = ks + lax.broadcasted_iota(jnp.int32, (tq, size), 1)
                qpos = q_lo + lax.broadcasted_iota(jnp.int32, (tq, size), 0)
                valid = kidx <= qpos
                if mode == "win":
                    valid = valid & ((qpos - kidx) < WINDOW)
                s = jnp.where(tile_rows(valid), s, NEG)
            ss.append(s)
        m = ss[0].max(axis=-1, keepdims=True)
        for s in ss[1:]:
            m = jnp.maximum(m, s.max(axis=-1, keepdims=True))
        l, acc = 0.0, 0.0
        for s, (ks, size, _) in zip(ss, blocks):
            p = jnp.exp(s - m)
            l = l + jnp.sum(p, axis=-1, keepdims=True)
            acc = acc + _dot(p.astype(BF16), vb[pl.ds(ks, size), :])
        o = acc / l
        for h in range(hpg):
            o_ref[0, :, h * HD:(h + 1) * HD] = o[h * tq:(h + 1) * tq].astype(o_ref.dtype)

    if mode == "win":
        span = min(t_k, -(-(WINDOW + tq) // HD) * HD + HD)
        start = jnp.clip((q_lo - (WINDOW - 1)) // HD * HD, 0, t_k - span)
        attend([(pl.multiple_of(start, HD), span, True)])
    else:
        n_max = t_k // tk
        need = (q_lo + tq - 1) // tk + 1
        for nb in range(1, n_max + 1):
            @pl.when(need == nb)
            def _(nb=nb):
                attend([(b * tk, tk, b == nb - 1) for b in range(nb)])


def _flash(mode, q_arr, q_col, k_arr, k_col, v_arr, v_col, *, n_kv, hpg, tq, tk, off=0,
           bias=None, sel=None, out_dtype=F32):
    bsz, t_q = q_arr.shape[0], q_arr.shape[1]
    t_k = k_arr.shape[1]
    nq = t_q // tq
    w = hpg * HD
    in_specs = [pl.BlockSpec((1, tq, w), lambda b, g, i: (b, i, q_col // w + g)),
                pl.BlockSpec((1, t_k, HD), lambda b, g, i: (b, 0, k_col // HD + g)),
                pl.BlockSpec((1, t_k, HD), lambda b, g, i: (b, 0, v_col // HD + g))]
    args = [q_arr, k_arr, v_arr]
    if mode == "fox":
        in_specs.append(pl.BlockSpec((1, t_k // tk, tk), lambda b, g, i: (b * n_kv + g, 0, 0)))
        args.append(bias)
    if mode == "slc":
        in_specs.append(pl.BlockSpec((1, 1, tq, HD), lambda b, g, i: (b, g, i, 0)))
        args.append(sel)
    return pl.pallas_call(
        functools.partial(_flash_body, mode=mode, tq=tq, tk=tk, hpg=hpg, off=off),
        grid=(bsz, n_kv, nq),
        in_specs=in_specs,
        out_specs=pl.BlockSpec((1, tq, w), lambda b, g, i: (b, i, g)),
        out_shape=jax.ShapeDtypeStruct((bsz, t_q, n_kv * w), out_dtype),
        scratch_shapes=[pltpu.VMEM((t_k, HD), BF16), pltpu.VMEM((t_k, HD), BF16)]
        + ([pltpu.VMEM((t_k // tk, HD, tk), BF16)] if mode == "slc" else []),
        compiler_params=_cparams(("parallel", "parallel", "arbitrary")),
    )(*args)


def _compress_body(*refs, paged, pg, n_steps, mn, norms, t_new):
    it = iter(refs)
    if paged:
        next(it)
    n_t = len(norms)
    x_all = [[next(it) for _ in range(pg if paged else G_NSA)] for _ in range(n_t)]
    new_all = [next(it) for _ in range(n_t)] if paged else None
    par_all = [(next(it), next(it), next(it)) for _ in range(n_t)]
    o_all = [next(it) for _ in range(n_t)]
    hcat, tb, perm = next(it), next(it), next(it)
    p = pl.program_id(1)
    mh = mn + 8
    per_page = PAGE // CMP_STRIDE
    n_real = n_steps * pg * per_page
    n_g = G_NSA if paged else 1

    @pl.when(p == 0)
    def _():
        hcat[:, n_real:, :] = jnp.zeros((n_t * G_NSA, mh - n_real, CMP_STRIDE * HD), F32)
        d = lax.broadcasted_iota(jnp.int32, perm.shape, 0)
        s = lax.broadcasted_iota(jnp.int32, perm.shape, 1)
        src = (CMP_STRIDE * (d % per_page) + (d % PAGE) // per_page) * n_g + d // PAGE
        perm[...] = jnp.where(s == src, 1.0, 0.0).astype(BF16)

    for ti in range(n_t):
        for pp in range(pg):
            r0 = pl.multiple_of((p * pg + pp) * per_page, per_page)
            for x_ref, g0 in ([(x_all[ti][pp], 0)] if paged else [(x_all[ti][g], g) for g in range(G_NSA)]):
                rows = _dot(perm[...], x_ref[0].astype(BF16))
                for g in range(n_g):
                    for l in range(CMP_STRIDE):
                        a = g * PAGE + l * per_page
                        hcat[ti * G_NSA + g0 + g, pl.ds(r0, per_page), l * HD:(l + 1) * HD] = rows[a:a + per_page]

    @pl.when(p == n_steps - 1)
    def _():
        for ti in range(n_t):
            pe_ref, w_ref, gain_ref = par_all[ti]
            if paged:
                xn = new_all[ti][0]
                rid = lax.broadcasted_iota(jnp.int32, xn.shape, 0)
                xn = jnp.where(rid < t_new, xn, 0.0)
                for g in range(G_NSA):
                    for l in range(8):
                        hcat[ti * G_NSA + g, n_real:n_real + 1, l * HD:(l + 1) * HD] = xn[l:l + 1, g * HD:(g + 1) * HD]
            w = w_ref[...]
            pe2 = _dot(pe_ref[...].astype(BF16), w)
            pe_bias = pe2[0:1, 0:HD] + pe2[1:2, HD:2 * HD]
            for g in range(G_NSA):
                tb[...] = _dot(hcat[ti * G_NSA + g].astype(BF16), w)
                y = tb[0:mn, 0:HD] + tb[1:mn + 1, HD:2 * HD] + pe_bias
                if norms[ti]:
                    y = y * lax.rsqrt(jnp.mean(y * y, axis=-1, keepdims=True) + EPS) * gain_ref[...]
                o_all[ti][0, :, g * HD:(g + 1) * HD] = y


def _compress(x_arr, x_cols, params, norms, *, mn, page_table=None, pools=None, pool_off=0, t_new=0, pg=16):
    paged = page_table is not None
    bsz = x_arr.shape[0]
    n_t = len(norms)
    mh = mn + 8
    w2 = G_NSA * HD
    const = lambda shape: pl.BlockSpec(shape, (lambda b, p, *_: (0,) * len(shape)))
    tail = [const((8, CMP_STRIDE * HD)), const((CMP_STRIDE * HD, 2 * HD)), const((1, HD))] * n_t
    tail_args = [a for par in params for a in par]
    out_specs = [pl.BlockSpec((1, mn, w2), lambda b, p, *_: (b, 0, 0))] * n_t
    n_perm = G_NSA * PAGE if paged else PAGE
    scratch = [pltpu.VMEM((n_t * G_NSA, mh, CMP_STRIDE * HD), F32), pltpu.VMEM((mh, 2 * HD), F32),
               pltpu.VMEM((n_perm, n_perm), BF16)]
    out_shape = [jax.ShapeDtypeStruct((bsz, mn, w2), F32)] * n_t
    if not paged:
        n_steps = x_arr.shape[1] // PAGE
        body = functools.partial(_compress_body, paged=False, pg=1, n_steps=n_steps, mn=mn, norms=norms, t_new=0)
        return pl.pallas_call(
            body, grid=(bsz, n_steps),
            in_specs=[pl.BlockSpec((1, PAGE, HD), lambda b, p, _c=c // HD + g: (b, p, _c))
                      for c in x_cols for g in range(G_NSA)] + tail,
            out_specs=out_specs, out_shape=out_shape, scratch_shapes=scratch,
            compiler_params=_cparams(("parallel", "arbitrary")),
        )(*([x_arr] * (n_t * G_NSA)), *tail_args)
    pg = min(pg, page_table.shape[1])
    n_steps = page_table.shape[1] // pg
    body = functools.partial(_compress_body, paged=True, pg=pg, n_steps=n_steps, mn=mn, norms=norms, t_new=t_new)
    grid_spec = pltpu.PrefetchScalarGridSpec(
        num_scalar_prefetch=1, grid=(bsz, n_steps),
        in_specs=[pl.BlockSpec((1, G_NSA * PAGE, HD), lambda b, p, pt, _pp=pp: (pool_off + pt[b, p * pg + _pp], 0, 0))
                  for _ in range(n_t) for pp in range(pg)]
        + [pl.BlockSpec((1, 8, w2), lambda b, p, pt, _c=c // w2: (b, 0, _c)) for c in x_cols] + tail,
        out_specs=out_specs, scratch_shapes=scratch)
    return pl.pallas_call(body, grid_spec=grid_spec, out_shape=out_shape,
                          compiler_params=_cparams(("parallel", "arbitrary")))(
        page_table, *[pool for pool in pools for _ in range(pg)], *([x_arr] * n_t), *tail_args)


def _cmpsel_body(q_ref, kc_ref, vc_ref, o_ref, sel_ref, *, tq, ncp, nsp, n_sel, pos0, gs):
    i = pl.program_id(2)
    q = q_ref[0]
    pos = pos0 + i * tq + lax.broadcasted_iota(jnp.int32, (tq, ncp), 0)
    col = lax.broadcasted_iota(jnp.int32, (tq, ncp), 1)
    valid = (col * CMP_STRIDE + (CMP_BLOCK - 1)) <= pos
    validf = jnp.where(valid, 1.0, 0.0)
    imps = []
    for g in range(gs):
        kc = kc_ref[0, :, g * HD:(g + 1) * HD].astype(BF16)
        vc = vc_ref[0, :, g * HD:(g + 1) * HD].astype(BF16)
        imp = jnp.zeros((tq, ncp), F32)
        for h in range(g * HPG, (g + 1) * HPG):
            qh = (q[:, h * HD:(h + 1) * HD] * ATTN_SCALE).astype(BF16)
            s = jnp.where(valid, _dot_nt(qh, kc), NEG)
            e = jnp.exp(s - jnp.max(s, axis=-1, keepdims=True)) * validf
            p = e / jnp.maximum(jnp.sum(e, axis=-1, keepdims=True), 1e-30)
            o_ref[0, :, h * HD:(h + 1) * HD] = _dot(p.astype(BF16), vc)
            imp = imp + p
        imps.append(imp)
    imp = jnp.concatenate(imps, axis=0)
    cc = lax.broadcasted_iota(jnp.int32, (ncp, nsp), 0)
    jj = lax.broadcasted_iota(jnp.int32, (ncp, nsp), 1)
    r = SEL_BLOCK // CMP_STRIDE
    gather = jnp.where((cc >= r * jj - 1) & (cc <= r * jj + r - 1), 1.0, 0.0).astype(BF16)
    score = _dot3_l(imp, gather)
    rows = gs * tq
    blk = lax.broadcasted_iota(jnp.int32, (rows, nsp), 1)
    pq = pos0 + i * tq + lax.broadcasted_iota(jnp.int32, (rows, nsp), 0) % tq
    forced = (blk == pq // SEL_BLOCK) | (blk == 0)
    future = blk * SEL_BLOCK > pq
    score = jnp.where(forced, FORCE_SCORE, jnp.where(future, -1.0, score))
    score = jnp.where(blk < n_sel, score, -2.0)
    if rows % HD == 0:
        n8 = -(-n_sel // 8) * 8
        st = score.T[:n8]
        bt = lax.broadcasted_iota(jnp.int32, (n8, rows), 0)
        rank = jnp.zeros((n8, rows), F32)
        for c in range(n_sel):
            sc = st[c:c + 1, :]
            ahead = (sc > st) | ((sc == st) & (bt > c))
            rank = rank + jnp.where(ahead, 1.0, 0.0)
        keep = jnp.where(rank < float(N_SELECT), 1.0, 0.0)
        sel = jnp.concatenate([keep, jnp.zeros((nsp - n8, rows), F32)], axis=0).T
    else:
        rank = jnp.zeros((rows, nsp), F32)
        for c in range(n_sel):
            sc = score[:, c:c + 1]
            ahead = (sc > score) | ((sc == score) & (blk > c))
            rank = rank + jnp.where(ahead, 1.0, 0.0)
        sel = jnp.where(rank < float(N_SELECT), 1.0, 0.0)
    for g in range(gs):
        sel_ref[0, g] = sel[g * tq:(g + 1) * tq]


def _cmpsel(z3, kcmp, vcmp, *, tq, pos0, n_sel, gs):
    bsz, t, _ = z3.shape
    ncp = kcmp.shape[1]
    nsp = -(-n_sel // HD) * HD
    w = gs * HPG * HD
    return pl.pallas_call(
        functools.partial(_cmpsel_body, tq=tq, ncp=ncp, nsp=nsp, n_sel=n_sel, pos0=pos0, gs=gs),
        grid=(bsz, G_NSA // gs, t // tq),
        in_specs=[pl.BlockSpec((1, tq, w), lambda b, g, i: (b, i, NQ // w + g)),
                  pl.BlockSpec((1, ncp, gs * HD), lambda b, g, i: (b, 0, g)),
                  pl.BlockSpec((1, ncp, gs * HD), lambda b, g, i: (b, 0, g))],
        out_specs=[pl.BlockSpec((1, tq, w), lambda b, g, i: (b, i, g)),
                   pl.BlockSpec((1, gs, tq, nsp), lambda b, g, i: (b, g, i, 0))],
        out_shape=[jax.ShapeDtypeStruct((bsz, t, H_NSA * HD), F32),
                   jax.ShapeDtypeStruct((bsz, G_NSA, t, nsp), F32)],
        compiler_params=_cparams(("parallel", "parallel", "parallel")),
    )(z3, kcmp, vcmp)


def _combine_body(oc_ref, os_ref, ow_ref, zs_ref, gb_ref, o_ref):
    gates = _sigmoid(zs_ref[...] + gb_ref[...])
    for h in range(H_NSA):
        sl = slice(h * HD, (h + 1) * HD)
        c0 = H_FOX + 3 * h
        o = (gates[:, c0:c0 + 1] * oc_ref[:, sl] + gates[:, c0 + 1:c0 + 2] * os_ref[:, sl]
             + gates[:, c0 + 2:c0 + 3] * ow_ref[:, sl])
        o_ref[:, sl] = o.astype(BF16)


def _nsa_combine(oc, osl, ow, zs, gb_pad):
    m, n = oc.shape
    tm = min(m, 512)
    big = pl.BlockSpec((tm, n), lambda i: (i, 0))
    return pl.pallas_call(
        _combine_body, grid=(m // tm,),
        in_specs=[big, big, big, pl.BlockSpec((tm, HD), lambda i: (i, 0)), pl.BlockSpec((1, HD), lambda i: (0, 0))],
        out_specs=big, out_shape=jax.ShapeDtypeStruct((m, n), BF16),
        compiler_params=_cparams(("parallel",)),
    )(oc, osl, ow, zs, gb_pad)


def _paged_body(*refs, mode, pg, n_steps, n_w, hpg_kv, t_new):
    it = iter(refs)
    next(it)
    q_ref = next(it)
    k_refs = [next(it) for _ in range(pg)]
    v_refs = [next(it) for _ in range(pg)]
    cl_refs = [next(it) for _ in range(pg)] if mode == "fox" else None
    tf_refs = [next(it) for _ in range(pg)] if mode == "fox" else None
    kn_ref, vn_ref = next(it), next(it)
    lfn_ref = next(it) if mode == "fox" else None
    sel_ref, seln_ref = (next(it), next(it)) if mode == "slc" else (None, None)
    o_ref = next(it)
    qb, m_scr, l_scr, acc, kn_scr, vn_scr, carry = (next(it) for _ in range(7))
    p = pl.program_id(1)
    rows = 4 * 8
    r2 = n_w * PAGE
    n_new = t_new * n_w

    def kv_match(ncols):
        row = lax.broadcasted_iota(jnp.int32, (rows, ncols), 0)
        col = lax.broadcasted_iota(jnp.int32, (rows, ncols), 1)
        return row, col, (col % n_w) == (row % 8) // hpg_kv

    @pl.when(p == 0)
    def _():
        qb[...] = (q_ref[0] * ATTN_SCALE).astype(BF16)
        m_scr[...] = jnp.full(m_scr.shape, NEG, F32)
        l_scr[...] = jnp.zeros(l_scr.shape, F32)
        acc[...] = jnp.zeros(acc.shape, F32)
        carry[...] = jnp.zeros(carry.shape, F32)
        kn_scr[...] = jnp.zeros(kn_scr.shape, BF16)
        vn_scr[...] = jnp.zeros(vn_scr.shape, BF16)

    def picked(choice, first_lane, ncols):
        half = (lax.broadcasted_iota(jnp.int32, (rows, ncols), 1) // n_w) // SEL_BLOCK
        out = None
        for j in range(PAGE // SEL_BLOCK):
            term = (half == j) & (choice[:, first_lane + j:first_lane + j + 1] > 0.5)
            out = term if out is None else (out | term)
        return out

    def update(s_list, vb_list):
        m_old = m_scr[...]
        m_new = m_old
        for s in s_list:
            m_new = jnp.maximum(m_new, jnp.max(s, axis=-1, keepdims=True))
        a = jnp.exp(m_old - m_new)
        l = a * l_scr[...]
        o = a * acc[...]
        for s, vb in zip(s_list, vb_list):
            pr = jnp.exp(s - m_new)
            l = l + jnp.sum(pr, axis=-1, keepdims=True)
            o = o + _dot(pr.astype(BF16), vb)
        m_scr[...] = m_new
        l_scr[...] = l
        acc[...] = o

    _, _, match = kv_match(r2)
    s_list, vb_list = [], []
    c_run = carry[...] if mode == "fox" else None
    for pp in range(pg):
        s = _dot_nt(qb[...], k_refs[pp][0].astype(BF16))
        valid = match
        if mode == "fox":
            s = s - (c_run + cl_refs[pp][0])
            c_run = c_run + tf_refs[pp][0]
        if mode == "slc":
            valid = valid & picked(sel_ref[0, 0], pp * (PAGE // SEL_BLOCK), r2)
        s_list.append(jnp.where(valid, s, NEG))
        vb_list.append(v_refs[pp][0].astype(BF16))
    if mode == "fox":
        carry[...] = c_run
    update(s_list, vb_list)

    @pl.when(p == n_steps - 1)
    def _():
        kn_scr[0:n_new, :] = kn_ref[0].astype(BF16)
        vn_scr[0:n_new, :] = vn_ref[0].astype(BF16)
        s = _dot_nt(qb[...], kn_scr[...])
        row, col, valid = kv_match(HD)
        valid = valid & (col // n_w <= row // 8) & (col < n_new)
        if mode == "fox":
            rr = lax.broadcasted_iota(jnp.int32, (HD, HD), 0)
            cc = lax.broadcasted_iota(jnp.int32, (HD, HD), 1)
            pre = jnp.where((rr % n_w == cc % n_w) & (rr // n_w <= cc // n_w), 1.0, 0.0).astype(BF16)
            c_new = _dot3_l(jnp.broadcast_to(lfn_ref[0], (8, HD)), pre)[0:1, :] + carry[:, 0:HD]
            s = s - c_new
        if mode == "slc":
            valid = valid & picked(seln_ref[0], 0, HD)
        update([jnp.where(valid, s, NEG)], [vn_scr[...]])
        o_ref[0] = acc[...] / l_scr[...]


def _paged_attn(mode, q_rows, page_table, k_pool, v_pool, pool_off, k_new, v_new, *, hpg_kv, pg,
                c_local=None, c_total=None, lf_new=None, sel_rows=None):
    bsz = q_rows.shape[0]
    n_pages = page_table.shape[1]
    pg = min(pg, n_pages)
    n_steps = n_pages // pg
    r2 = k_pool.shape[1]
    n_w = r2 // PAGE
    n_new = k_new.shape[1]
    page = lambda pp: (lambda b, p, pt: (pool_off + pt[b, p * pg + pp], 0, 0))
    per_b = lambda b, p, pt: (b, 0, 0)
    in_specs = [pl.BlockSpec((1, 32, HD), per_b)]
    in_specs += [pl.BlockSpec((1, r2, HD), page(pp)) for pp in range(pg)] * 2
    args = [q_rows] + [k_pool] * pg + [v_pool] * pg
    if mode == "fox":
        in_specs += [pl.BlockSpec((1, 1, r2), page(pp)) for pp in range(pg)] * 2
        args += [c_local] * pg + [c_total] * pg
    in_specs += [pl.BlockSpec((1, n_new, HD), per_b)] * 2
    args += [k_new, v_new]
    if mode == "fox":
        in_specs.append(pl.BlockSpec((1, 1, HD), per_b))
        args.append(lf_new)
    if mode == "slc":
        per_page = PAGE // SEL_BLOCK
        past = sel_rows[:, :, :n_pages * per_page].reshape(bsz, 32, n_steps, pg * per_page)
        sel_steps = _pad_cols(jnp.swapaxes(past, 1, 2).reshape(bsz * n_steps * 32, pg * per_page), HD)
        in_specs += [pl.BlockSpec((1, 1, 32, HD), lambda b, p, pt: (b, p, 0, 0)), pl.BlockSpec((1, 32, HD), per_b)]
        args += [sel_steps.reshape(bsz, n_steps, 32, HD),
                 _pad_cols(sel_rows[:, :, n_pages * per_page:n_pages * per_page + 1].reshape(bsz * 32, 1), HD)
                 .reshape(bsz, 32, HD)]
    grid_spec = pltpu.PrefetchScalarGridSpec(
        num_scalar_prefetch=1, grid=(bsz, n_steps), in_specs=in_specs,
        out_specs=pl.BlockSpec((1, 32, HD), per_b),
        scratch_shapes=[pltpu.VMEM((32, HD), BF16), pltpu.VMEM((32, 1), F32), pltpu.VMEM((32, 1), F32),
                        pltpu.VMEM((32, HD), F32), pltpu.VMEM((PAGE, HD), BF16), pltpu.VMEM((PAGE, HD), BF16),
                        pltpu.VMEM((1, r2), F32)])
    return pl.pallas_call(
        functools.partial(_paged_body, mode=mode, pg=pg, n_steps=n_steps, n_w=n_w, hpg_kv=hpg_kv, t_new=n_new // n_w),
        grid_spec=grid_spec, out_shape=jax.ShapeDtypeStruct((bsz, 32, HD), F32),
        compiler_params=_cparams(("parallel", "arbitrary")),
    )(page_table, *args)


def _pool_prefix_body(x_ref, cl_ref, ct_ref, w_scr, *, n_w):
    r2 = x_ref.shape[1]

    @pl.when(pl.program_id(0) == 0)
    def _():
        r = lax.broadcasted_iota(jnp.int32, (r2, 2 * r2), 0)
        c = lax.broadcasted_iota(jnp.int32, (r2, 2 * r2), 1)
        same = (r % n_w) == (c % n_w)
        local = same & (r // n_w <= c // n_w) & (c < r2)
        w_scr[...] = jnp.where(local | (same & (c >= r2)), 1.0, 0.0).astype(BF16)

    y = _dot3_l(x_ref[...], w_scr[...])
    cl_ref[...] = y[:, :r2]
    ct_ref[...] = y[:, r2:]


def _pool_prefix(x, n_w):
    n, r2 = x.shape
    tm = next((c for c in (256, 128, 64, 32, 16, 8) if n % c == 0), n)
    spec = pl.BlockSpec((tm, r2), lambda i: (i, 0))
    return pl.pallas_call(
        functools.partial(_pool_prefix_body, n_w=n_w), grid=(n // tm,),
        in_specs=[spec], out_specs=[spec, spec],
        out_shape=[jax.ShapeDtypeStruct((n, r2), F32)] * 2,
        scratch_shapes=[pltpu.VMEM((r2, 2 * r2), BF16)],
        compiler_params=_cparams(("arbitrary",)),
    )(x)


def _gla_body(*refs, mode, t_in, chunk, dv, t_real, has_state, hb):
    it = iter(refs)
    q_ref, k_ref, v_ref, og_ref = next(it), next(it), next(it), next(it)
    if mode == "gla":
        zs_ref, wa_ref, ba_ref = next(it), next(it), next(it)
    else:
        lb_ref = next(it)
    gain_ref = next(it)
    s0_ref = next(it) if has_state else None
    o_ref, s_ref, st_scr, q_scr, k_scr, g_scr, o_scr = (next(it) for _ in range(7))
    tb = q_scr.shape[1]
    n_chunks = tb // chunk
    pad = tb - t_in
    tri = lax.broadcasted_iota(jnp.int32, (chunk, chunk), 0) >= lax.broadcasted_iota(jnp.int32, (chunk, chunk), 1)
    if mode == "hg":
        lbr = lb_ref[...]
        e = jnp.exp(lbr - jnp.max(lbr, axis=0, keepdims=True))
        lb_all = (e / jnp.sum(e, axis=0, keepdims=True))[0:1, :]
    ti = pl.program_id(2)

    @pl.when(ti == 0)
    def _():
        for h in range(hb):
            st_scr[h] = s0_ref[0, h].T if has_state else jnp.zeros((dv, HD), F32)

    def block(ref, sl):
        x = ref[0, :, sl]
        if pad:
            x = jnp.concatenate([x, jnp.zeros((pad, x.shape[1]), F32)], axis=0)
        return x

    live = (ti * t_in + lax.broadcasted_iota(jnp.int32, (tb, HD), 0)) < t_real
    if mode == "gla":
        ga = block(zs_ref, slice(0, HD)).astype(BF16)
    for h in range(hb):
        ks = slice(h * HD, (h + 1) * HD)
        qr, kr = block(q_ref, ks), block(k_ref, ks)
        if mode == "gla":
            g = _log_sigmoid(_dot(ga, wa_ref[:, ks]) + ba_ref[:, ks]) / GLA_GATE_NORM
            q, k = qr, kr
        else:
            lb = lb_all[:, ks]
            f = lb + (1.0 - lb) * _sigmoid(kr)
            q, k, g = qr * _sigmoid(qr), 1.0 - f, jnp.log(f)
        q_scr[h] = q
        k_scr[h] = jnp.where(live, k, 0.0)
        g_scr[h] = jnp.where(live, g, 0.0)

    def body(c, _):
        r0 = pl.multiple_of(c * chunk, chunk)
        rows = pl.ds(r0, chunk)
        for h in range(hb):
            vs = slice(h * dv, (h + 1) * dv)
            q, k, g = q_scr[h, rows, :], k_scr[h, rows, :], g_scr[h, rows, :]
            if pad:
                v = block(v_ref, vs)
            else:
                v = v_ref[0, rows, vs]
            b = _cumsum_rows(g)
            bm = b[chunk // 2 - 1:chunk // 2, :]
            bl = b[chunk - 1:chunk, :]
            qe = (q * jnp.exp(jnp.minimum(b - bm, EXP_CLAMP))).astype(BF16)
            ke = (k * jnp.exp(jnp.minimum(bm - b, EXP_CLAMP))).astype(BF16)
            a = jnp.where(tri, _dot_nt(qe, ke), 0.0)
            vb = v.astype(BF16)
            st = st_scr[h]
            o_scr[h, rows, :] = _dot(a.astype(BF16), vb) + _dot_nt((q * jnp.exp(b)).astype(BF16), st.astype(BF16))
            kd = (k * jnp.exp(bl - b)).astype(BF16)
            st_scr[h] = st * jnp.exp(bl) + _dot_tn(vb, kd)
        return 0

    lax.fori_loop(0, n_chunks, body, 0)

    for h in range(hb):
        vs = slice(h * dv, (h + 1) * dv)
        o = o_scr[h, 0:t_in, :]
        og = og_ref[0, :, vs]
        o = o * lax.rsqrt(jnp.mean(o * o, axis=-1, keepdims=True) + EPS) * gain_ref[...]
        o_ref[0, :, vs] = (o * (og * _sigmoid(og) if mode == "gla" else _sigmoid(og))).astype(BF16)

    @pl.when(ti == pl.num_programs(2) - 1)
    def _():
        for h in range(hb):
            s_ref[0, h] = st_scr[h].T


def _gla(mode, z3, zs3, cols, n_heads, dv, extra, gain, s0, *, t_real, chunk, hb):
    bsz, t_all, _ = z3.shape
    tc = min(t_all, 512)
    qc, kc, vc, oc = cols
    has_state = s0 is not None
    wk, wv = hb * HD, hb * dv
    in_specs = [pl.BlockSpec((1, tc, wk), lambda b, h, t: (b, t, qc // wk + h)),
                pl.BlockSpec((1, tc, wk), lambda b, h, t: (b, t, kc // wk + h)),
                pl.BlockSpec((1, tc, wv), lambda b, h, t: (b, t, vc // wv + h)),
                pl.BlockSpec((1, tc, wv), lambda b, h, t: (b, t, oc // wv + h))]
    args = [z3, z3, z3, z3]
    if mode == "gla":
        wa, ba = extra
        in_specs += [pl.BlockSpec((1, tc, HD), lambda b, h, t: (b, t, 0)),
                     pl.BlockSpec((HD, wk), lambda b, h, t: (0, h)),
                     pl.BlockSpec((1, wk), lambda b, h, t: (0, h))]
        args += [zs3, wa, ba]
    else:
        in_specs.append(pl.BlockSpec((8, wk), lambda b, h, t: (0, h)))
        args.append(extra)
    in_specs.append(pl.BlockSpec((1, dv), lambda b, h, t: (0, 0)))
    args.append(gain)
    st_spec = pl.BlockSpec((1, hb, HD, dv), lambda b, h, t: (b, h, 0, 0))
    if has_state:
        in_specs.append(st_spec)
        args.append(s0)
    return pl.pallas_call(
        functools.partial(_gla_body, mode=mode, t_in=tc, chunk=chunk, dv=dv, t_real=t_real, has_state=has_state,
                          hb=hb),
        grid=(bsz, n_heads // hb, t_all // tc),
        in_specs=in_specs,
        out_specs=[pl.BlockSpec((1, tc, wv), lambda b, h, t: (b, t, h)), st_spec],
        out_shape=[jax.ShapeDtypeStruct((bsz, t_all, n_heads * dv), BF16),
                   jax.ShapeDtypeStruct((bsz, n_heads, HD, dv), F32)],
        scratch_shapes=[pltpu.VMEM((hb, dv, HD), F32)] + [pltpu.VMEM((hb, max(tc, chunk), HD), F32)] * 3
        + [pltpu.VMEM((hb, max(tc, chunk), dv), F32)],
        compiler_params=_cparams(("parallel", "parallel", "arbitrary")),
    )(*args)


def _split_cols(w, sizes):
    outs, off = [], 0
    for s in sizes:
        outs.append(w[:, off:off + s])
        off += s
    return outs


def _pad_cols(w, n):
    return jnp.pad(w, ((0, 0), (0, n - w.shape[1])))


def _row(v):
    return v.reshape(1, -1).astype(F32)


def _even_weights(w_in, fq_gain, fk_gain, nq_gain, nk_gain):
    fq, fk, fv, ff, nq, kc, vc, ks, vs, kw, vw, ng = _split_cols(w_in, EVEN_SIZES)
    big = jnp.concatenate([fq, fk, fv, nq, kc, vc, ks, vs, kw, vw], axis=1).astype(BF16)
    small = _pad_cols(jnp.concatenate([ff, ng], axis=1), HD).astype(BF16)
    ones = lambda n: jnp.ones((n,), F32)
    gain = jnp.concatenate([jnp.tile(fq_gain, H_FOX), jnp.tile(fk_gain, H_FOX), ones(H_FOX * HD),
                            jnp.tile(nq_gain, H_NSA), ones(2 * G_NSA * HD), jnp.tile(nk_gain[1], G_NSA),
                            ones(G_NSA * HD), jnp.tile(nk_gain[2], G_NSA), ones(G_NSA * HD)])
    z, o = jnp.zeros, jnp.ones
    flag = jnp.concatenate([o((2 * H_FOX * HD,), F32), z((H_FOX * HD,), F32), o((H_NSA * HD,), F32),
                            z((2 * G_NSA * HD,), F32), o((G_NSA * HD,), F32), z((G_NSA * HD,), F32),
                            o((G_NSA * HD,), F32), z((G_NSA * HD,), F32)])
    return big, small, _row(gain), _row(flag)


def _odd_weights(w_in):
    gq, gk, gv, ga, gg, hq, hf, hi, hg = _split_cols(w_in, ODD_SIZES)
    big = jnp.concatenate([gq, gk, gv, gg, hq, hf, hi, hg], axis=1).astype(BF16)
    small = _pad_cols(ga, HD).astype(BF16)
    gain = jnp.concatenate([jnp.full((H_GLA * DK_GLA,), DK_GLA ** -0.5, F32), jnp.ones((N_BIG_ODD - H_GLA * DK_GLA,), F32)])
    return big, small, _row(gain), jnp.zeros((1, N_BIG_ODD), F32)


def _cmp_weights(pe, w):
    half = CMP_STRIDE * HD
    wcat = jnp.concatenate([w[:half], w[half:]], axis=1).astype(BF16)
    pe8 = jnp.pad(pe.reshape(2, half), ((0, 6), (0, 0)))
    return pe8, wcat


def _ffn(x, g, w1, w3, w2, layer):
    return _mm_res([_ffn_up(x, g, w1, w3, layer)], w2, layer, x)


def _even_layer(x, bsz, t, t_real, ew, past):
    (w_big, w_small, cgain, cflag, g_mix, w_out, b_f_pad, gb_pad, pe_k, wc_k, pe_v, wc_v, nk0) = ew
    z, zs = _proj(x, g_mix, w_big, w_small, cgain, cflag, True)
    z3 = z.reshape(bsz, t, N_BIG_EVEN)
    zs3 = zs.reshape(bsz, t, HD)
    cmp_params = ((pe_k, wc_k, nk0), (pe_v, wc_v, jnp.ones((1, HD), F32)))
    if past is None:
        lf, c = _fox_prep(zs3, b_f_pad)
        logf_new = lf[:, :, :H_FOX]
        tk = min(t, 512)
        bias = jnp.swapaxes(c[:, :, :H_FOX], 1, 2).reshape(bsz * H_FOX, t // tk, tk)
        o_fox = _flash("fox", z3, FQ, z3, FK, z3, FV, n_kv=H_FOX, hpg=1, tq=min(t, 512), tk=tk, bias=bias,
                       out_dtype=BF16)
        n_cmp = t // CMP_STRIDE
        mn = -(-n_cmp // HD) * HD
        kcmp, vcmp = _compress(z3, (KC, VC), cmp_params, (True, False), mn=mn)
        tqn = min(t, 512 // HPG)
        o_cmp, sel = _cmpsel(z3, kcmp, vcmp, tq=min(t, 256), pos0=0, n_sel=-(-t // SEL_BLOCK), gs=1)
        o_slc = _flash("slc", z3, NQ, z3, KS, z3, VS, n_kv=G_NSA, hpg=HPG, tq=tqn, tk=tk, sel=sel)
        o_win = _flash("win", z3, NQ, z3, KW, z3, VW, n_kv=G_NSA, hpg=HPG, tq=tqn, tk=tk)
        w_buf = WINDOW
        kw_new = jnp.concatenate([jnp.zeros((bsz, w_buf, G_NSA * HD), F32), z3[:, :, KW:KW + G_NSA * HD]], axis=1)[:, -w_buf:]
        vw_new = jnp.concatenate([jnp.zeros((bsz, w_buf, G_NSA * HD), F32), z3[:, :, VW:VW + G_NSA * HD]], axis=1)[:, -w_buf:]
        o_fox = o_fox.reshape(bsz * t, H_FOX * HD)
    else:
        (page_table, pool_off, fk_pool, fv_pool, lf_pool, kc_pool, vc_pool, ks_pool, vs_pool, buf_k, buf_v) = past
        p_len = page_table.shape[1] * PAGE
        lf_all, _ = _fox_prep(zs.reshape(1, bsz * t, HD), b_f_pad)
        lf_new = lf_all.reshape(bsz, t, HD)[:, :t_real, :H_FOX]
        logf_new = lf_new
        lfn_pad = _pad_cols(lf_new.reshape(bsz, t_real * H_FOX), HD).reshape(bsz, 1, HD)
        c_local, c_total = _pool_prefix(lf_pool, H_FOX)
        rows_of = lambda c0, n: z3[:, :t_real, c0:c0 + n * HD].reshape(bsz, t_real * n, HD)
        o_fox = _paged_attn("fox", rows_of(FQ, H_FOX), page_table, fk_pool, fv_pool, pool_off,
                            rows_of(FK, H_FOX), rows_of(FV, H_FOX), hpg_kv=1, pg=8,
                            c_local=c_local[:, None, :], c_total=c_total[:, None, :], lf_new=lfn_pad)
        o_fox = o_fox.reshape(bsz, t_real, H_FOX * HD)
        n_cmp = -(-(p_len + t_real) // CMP_STRIDE)
        mn = -(-n_cmp // HD) * HD
        kcmp, vcmp = _compress(z3, (KC, VC), cmp_params, (True, False), mn=mn, page_table=page_table,
                               pools=(kc_pool, vc_pool), pool_off=pool_off, t_new=t_real)
        n_sel = -(-(p_len + t_real) // SEL_BLOCK)
        o_cmp, sel = _cmpsel(z3, kcmp, vcmp, tq=t, pos0=p_len, n_sel=n_sel, gs=G_NSA)
        nsp = sel.shape[-1]
        sel_rows = jnp.broadcast_to(jnp.swapaxes(sel[:, :, :t_real], 1, 2)[:, :, :, None, :],
                                    (bsz, t_real, G_NSA, HPG, nsp)).reshape(bsz, 32, nsp)
        o_slc = _paged_attn("slc", rows_of(NQ, H_NSA), page_table, ks_pool, vs_pool, pool_off,
                            rows_of(KS, G_NSA), rows_of(VS, G_NSA), hpg_kv=HPG, pg=16,
                            sel_rows=sel_rows).reshape(bsz, t_real, H_NSA * HD)
        kw_all = jnp.concatenate([buf_k, z3[:, :t_real, KW:KW + G_NSA * HD]], axis=1)
        vw_all = jnp.concatenate([buf_v, z3[:, :t_real, VW:VW + G_NSA * HD]], axis=1)
        w_buf = buf_k.shape[1]
        kw_new, vw_new = kw_all[:, -w_buf:], vw_all[:, -w_buf:]
        lk = -(-(w_buf + t) // HD) * HD
        padk = lambda a: jnp.pad(a, ((0, 0), (0, lk - a.shape[1]), (0, 0)))
        o_win = _flash("win", z3, NQ, padk(kw_all), 0, padk(vw_all), 0, n_kv=G_NSA, hpg=HPG, tq=t, tk=HD, off=w_buf)
        padt = lambda a: jnp.pad(a, ((0, 0), (0, t - t_real), (0, 0)))
        o_fox = padt(o_fox).astype(BF16).reshape(bsz * t, H_FOX * HD)
        o_slc = padt(o_slc)
    m = bsz * t
    o_nsa = _nsa_combine(o_cmp.reshape(m, -1), o_slc.reshape(m, -1), o_win.reshape(m, -1), zs, gb_pad)
    x = _mm_res([o_fox, o_nsa], w_out, 0, x)
    zr = z3[:, :t_real]
    grp = lambda c0: zr[:, :, c0:c0 + G_NSA * HD].reshape(1, bsz, t_real, G_NSA, HD)
    fox = lambda c0: zr[:, :, c0:c0 + H_FOX * HD].reshape(1, bsz, t_real, H_FOX, HD)
    w_rows = kw_new.shape[1]
    outs = (fox(FK), fox(FV), logf_new[None], grp(KC), grp(VC), grp(KS), grp(VS),
            kw_new.reshape(1, bsz, w_rows, G_NSA, HD), vw_new.reshape(1, bsz, w_rows, G_NSA, HD))
    return x, outs


def _odd_layer(x, bsz, t, t_real, ow, state):
    (w_big, w_small, cgain, cflag, g_mix, w_out, wa_pad, ba, gla_gain, lb_pad, hg_gain) = ow
    z, zs = _proj(x, g_mix, w_big, w_small, cgain, cflag, False)
    z3 = z.reshape(bsz, t, N_BIG_ODD)
    zs3 = zs.reshape(bsz, t, HD)
    s_gla, s_hg = state if state is not None else (None, None)
    chunk = 64 if t >= 64 else 128
    o_gla, s_gla = _gla("gla", z3, zs3, (GQ, GK, GV, GG), H_GLA, DV_GLA, (wa_pad, ba), gla_gain, s_gla,
                        t_real=t_real, chunk=chunk, hb=H_GLA)
    o_hg, s_hg = _gla("hg", z3, zs3, (HQ, HF, HI, HO), H_HG, DV_HG, lb_pad, hg_gain, s_hg,
                      t_real=t_real, chunk=chunk, hb=H_HG)
    m = bsz * t
    x = _mm_res([o_gla.reshape(m, -1), o_hg.reshape(m, -1)], w_out, 0, x)
    return x, (s_gla[None], s_hg[None])


def kernel(x_prompt, x_sample, cache_fox_k, cache_fox_v, cache_fox_logf, cache_nsa_kc, cache_nsa_vc, cache_nsa_ks, cache_nsa_vs, state_nsa_kw, state_nsa_vw, state_gla, state_hgrn, page_table, norm_mix, norm_ffn, w_in_even, w_out_even, fox_b_f, fox_q_gain, fox_k_gain, nsa_q_gain, nsa_k_gain, nsa_cmp_pe_k, nsa_cmp_pe_v, nsa_cmp_wk, nsa_cmp_wv, nsa_gate_b, w_in_odd, w_out_odd, gla_wa2, gla_ba, gla_norm, hgrn_lb, hgrn_norm, ffn_w1, ffn_w3, ffn_w2):
    bp, tp, d = x_prompt.shape
    bs, ts, _ = x_sample.shape
    n_pool = cache_fox_k.shape[1]
    ts_pad = 8
    pe_k, wc_k = _cmp_weights(nsa_cmp_pe_k[0], nsa_cmp_wk[0])
    pe_v, wc_v = _cmp_weights(nsa_cmp_pe_v[0], nsa_cmp_wv[0])
    b_f_pad = _pad_cols(_row(fox_b_f[0]), HD)
    gb_pad = _pad_cols(jnp.concatenate([jnp.zeros((1, H_FOX), F32), _row(nsa_gate_b[0])], axis=1), HD)
    ew = _even_weights(w_in_even[0], fox_q_gain[0], fox_k_gain[0], nsa_q_gain[0], nsa_k_gain[0]) + (
        _row(norm_mix[0]), w_out_even, b_f_pad, gb_pad, pe_k, wc_k, pe_v, wc_v, _row(nsa_k_gain[0, 0]))
    wa_pad = jnp.pad(gla_wa2[0], ((0, HD - GLA_RANK), (0, 0))).astype(BF16)
    lb_pad = jnp.pad(hgrn_lb.astype(F32), ((0, 8 - hgrn_lb.shape[0]), (0, 0)), constant_values=-1e30)
    ow = _odd_weights(w_in_odd[0]) + (_row(norm_mix[1]), w_out_odd, wa_pad, _row(gla_ba[0]),
                                      _row(gla_norm[0]), lb_pad, _row(hgrn_norm[0]))
    ffn = [(_row(norm_ffn[i]), ffn_w1, ffn_w3, ffn_w2, i) for i in range(2)]

    flat = lambda c: c.reshape((c.shape[0] * c.shape[1], PAGE * c.shape[3], HD))
    lf_pool = cache_fox_logf.reshape(-1, PAGE * H_FOX)
    win = lambda s: s.reshape(bs, s.shape[2], G_NSA * HD)
    past = (page_table, 0, flat(cache_fox_k), flat(cache_fox_v), lf_pool, flat(cache_nsa_kc), flat(cache_nsa_vc),
            flat(cache_nsa_ks), flat(cache_nsa_vs), win(state_nsa_kw), win(state_nsa_vw))

    def run(x, bsz, t, t_real, past_, state_):
        x, ev = _even_layer(x, bsz, t, t_real, ew, past_)
        x = _ffn(x, *ffn[0])
        x, od = _odd_layer(x, bsz, t, t_real, ow, state_)
        x = _ffn(x, *ffn[1])
        return x, ev, od

    yp, ev_p, od_p = run(x_prompt.reshape(bp * tp, d), bp, tp, tp, None, None)
    xs = jnp.pad(x_sample, ((0, 0), (0, ts_pad - ts), (0, 0))).reshape(bs * ts_pad, d)
    ys, ev_s, od_s = run(xs, bs, ts_pad, ts, past, (state_gla.reshape(state_gla.shape[1:]),
                                                    state_hgrn.reshape(state_hgrn.shape[1:])))
    y_prompt = yp.reshape(bp, tp, d)
    y_sample = ys.reshape(bs, ts_pad, d)[:, :ts]
    outs = [y_prompt, y_sample]
    for a, b in zip(ev_p, ev_s):
        outs += [a, b]
    for a, b in zip(od_p, od_s):
        outs += [a, b]
    return tuple(outs)
```

```python
import functools

import jax
import jax.numpy as jnp
from jax import lax
from jax.experimental import pallas as pl
from jax.experimental.pallas import tpu as pltpu

F32 = jnp.float32
BF16 = jnp.bfloat16

D_MODEL = 2048
HD = 128
H_FOX = 8
H_NSA = 8
G_NSA = 2
HPG = H_NSA // G_NSA
CMP_STRIDE = 16
CMP_BLOCK = 2 * CMP_STRIDE
SEL_BLOCK = 64
N_SELECT = 16
WINDOW = 512
FORCE_SCORE = 1.0e4
H_GLA = 4
DK_GLA = 128
DV_GLA = 256
GLA_RANK = 16
GLA_GATE_NORM = 16.0
H_HG = 8
DK_HG = 128
DV_HG = 128
PAGE = 128
EVEN_SIZES = (H_FOX * HD, H_FOX * HD, H_FOX * HD, H_FOX, H_NSA * HD,
              G_NSA * HD, G_NSA * HD, G_NSA * HD, G_NSA * HD, G_NSA * HD, G_NSA * HD, 3 * H_NSA)
ODD_SIZES = (H_GLA * DK_GLA, H_GLA * DK_GLA, H_GLA * DV_GLA, GLA_RANK, H_GLA * DV_GLA,
             H_HG * DK_HG, H_HG * DK_HG, H_HG * DV_HG, H_HG * DV_HG)
FQ, FK, FV, NQ, KC, VC, KS, VS, KW, VW = 0, 1024, 2048, 3072, 4096, 4352, 4608, 4864, 5120, 5376
N_BIG_EVEN = 5632
GQ, GK, GV, GG, HQ, HF, HI, HO = 0, 512, 1024, 2048, 3072, 4096, 5120, 6144
N_BIG_ODD = 7168
ATTN_SCALE = HD ** -0.5
NEG = -1.0e30
EPS = 1.0e-6
EXP_CLAMP = 80.0
VMEM_LIMIT = 56 * 1024 * 1024


def _cparams(sem):
    return pltpu.CompilerParams(dimension_semantics=sem, vmem_limit_bytes=VMEM_LIMIT)


def _dot(a, b):
    return jnp.dot(a, b, preferred_element_type=F32)


def _dot_nt(a, b):
    return lax.dot_general(a, b, (((1,), (1,)), ((), ())), preferred_element_type=F32)


def _dot_tn(a, b):
    return lax.dot_general(a, b, (((0,), (0,)), ((), ())), preferred_element_type=F32)


def _split3(x):
    hi = x.astype(BF16)
    r = x - hi.astype(F32)
    mid = r.astype(BF16)
    lo = (r - mid.astype(F32)).astype(BF16)
    return hi, mid, lo


def _dot3_l(x, w):
    hi, mid, lo = _split3(x)
    return _dot(hi, w) + _dot(mid, w) + _dot(lo, w)


def _dot3_r(w, x):
    hi, mid, lo = _split3(x)
    return _dot(w, hi) + _dot(w, mid) + _dot(w, lo)


def _sigmoid(x):
    return 1.0 / (1.0 + jnp.exp(-x))


def _log_sigmoid(x):
    return jnp.minimum(x, 0.0) - jnp.log1p(jnp.exp(-jnp.abs(x)))


def _cumsum_rows(x):
    n = x.shape[0]
    row = lax.broadcasted_iota(jnp.int32, x.shape, 0)
    s = 1
    while s < n:
        x = x + jnp.where(row >= s, pltpu.roll(x, s, axis=0), 0.0)
        s *= 2
    return x


def _tri(n, upper):
    r = lax.broadcasted_iota(jnp.int32, (n, n), 0)
    c = lax.broadcasted_iota(jnp.int32, (n, n), 1)
    return jnp.where((r <= c) if upper else (r >= c), 1.0, 0.0).astype(BF16)


def _proj_body(x_ref, g_ref, w_ref, ws_ref, cg_ref, cf_ref, z_ref, zs_ref, h_scr, *, tn, has_norm):
    @pl.when(pl.program_id(1) == 0)
    def _():
        x = x_ref[...]
        r = lax.rsqrt(jnp.mean(x * x, axis=-1, keepdims=True) + EPS)
        h = (x * r * g_ref[...]).astype(BF16)
        h_scr[...] = h
        zs_ref[...] = _dot(h, ws_ref[...])

    z = _dot(h_scr[...], w_ref[...])
    if not has_norm:
        z_ref[...] = z * cg_ref[...]
        return
    for c in range(tn // HD):
        sl = slice(c * HD, (c + 1) * HD)
        zc = z[:, sl]
        r = lax.rsqrt(jnp.mean(zc * zc, axis=-1, keepdims=True) + EPS)
        f = cf_ref[:, sl]
        z_ref[:, sl] = zc * (f * r + (1.0 - f)) * cg_ref[:, sl]


def _proj(x, g, w_big, w_small, col_gain, col_flag, has_norm):
    m, d = x.shape
    n = w_big.shape[1]
    tm = min(m, 1024)
    tn = 512
    return pl.pallas_call(
        functools.partial(_proj_body, tn=tn, has_norm=has_norm),
        grid=(m // tm, n // tn),
        in_specs=[pl.BlockSpec((tm, d), lambda i, j: (i, 0)),
                  pl.BlockSpec((1, d), lambda i, j: (0, 0)),
                  pl.BlockSpec((d, tn), lambda i, j: (0, j)),
                  pl.BlockSpec((d, HD), lambda i, j: (0, 0)),
                  pl.BlockSpec((1, tn), lambda i, j: (0, j)),
                  pl.BlockSpec((1, tn), lambda i, j: (0, j))],
        out_specs=[pl.BlockSpec((tm, tn), lambda i, j: (i, j)),
                   pl.BlockSpec((tm, HD), lambda i, j: (i, 0))],
        out_shape=[jax.ShapeDtypeStruct((m, n), F32), jax.ShapeDtypeStruct((m, HD), F32)],
        scratch_shapes=[pltpu.VMEM((tm, d), BF16)],
        compiler_params=_cparams(("parallel", "arbitrary")),
    )(x, g, w_big, w_small, col_gain, col_flag)


def _ffn_up_body(x_ref, g_ref, w1_ref, w3_ref, o_ref, h_scr):
    @pl.when(pl.program_id(1) == 0)
    def _():
        x = x_ref[...]
        r = lax.rsqrt(jnp.mean(x * x, axis=-1, keepdims=True) + EPS)
        h_scr[...] = (x * r * g_ref[...]).astype(BF16)

    h = h_scr[...]
    a = _dot(h, w1_ref[...].astype(BF16))
    b = _dot(h, w3_ref[...].astype(BF16))
    o_ref[...] = (a * _sigmoid(a) * b).astype(BF16)


def _ffn_up(x, g, w1, w3, layer):
    m, d = x.shape
    n = w1.shape[2]
    tm = min(m, 1024)
    tn = 512
    return pl.pallas_call(
        _ffn_up_body,
        grid=(m // tm, n // tn),
        in_specs=[pl.BlockSpec((tm, d), lambda i, j: (i, 0)),
                  pl.BlockSpec((1, d), lambda i, j: (0, 0)),
                  pl.BlockSpec((None, d, tn), lambda i, j: (layer, 0, j)),
                  pl.BlockSpec((None, d, tn), lambda i, j: (layer, 0, j))],
        out_specs=pl.BlockSpec((tm, tn), lambda i, j: (i, j)),
        out_shape=jax.ShapeDtypeStruct((m, n), BF16),
        scratch_shapes=[pltpu.VMEM((tm, d), BF16)],
        compiler_params=_cparams(("parallel", "arbitrary")),
    )(x, g, w1, w3)


def _mm_res_body(*refs, n_in):
    res_ref = refs[2 * n_in]
    o_ref = refs[2 * n_in + 1]
    wb_refs = refs[2 * n_in + 2:]

    @pl.when(pl.program_id(1) == 0)
    def _():
        for w_ref, wb_ref in zip(refs[n_in:2 * n_in], wb_refs):
            wb_ref[...] = w_ref[...].astype(BF16)

    acc = res_ref[...]
    for a_ref, wb_ref in zip(refs[:n_in], wb_refs):
        acc = acc + _dot(a_ref[...], wb_ref[...])
    o_ref[...] = acc


def _mm_res(a_list, w, layer, res):
    m, n = res.shape
    tm = min(m, 512)
    tn = 512
    in_specs, w_specs, scratch, off = [], [], [], 0
    for a in a_list:
        k = a.shape[1]
        in_specs.append(pl.BlockSpec((tm, k), lambda j, i: (i, 0)))
        w_specs.append(pl.BlockSpec((None, k, tn), lambda j, i, _o=off // k: (layer, _o, j)))
        scratch.append(pltpu.VMEM((k, tn), BF16))
        off += k
    return pl.pallas_call(
        functools.partial(_mm_res_body, n_in=len(a_list)),
        grid=(n // tn, m // tm),
        in_specs=in_specs + w_specs + [pl.BlockSpec((tm, tn), lambda j, i: (i, j))],
        out_specs=pl.BlockSpec((tm, tn), lambda j, i: (i, j)),
        out_shape=jax.ShapeDtypeStruct((m, n), F32),
        scratch_shapes=scratch,
        compiler_params=_cparams(("parallel", "arbitrary")),
    )(*a_list, *([w] * len(a_list)), res)


def _fox_prep_body(zs_ref, b_ref, lf_ref, c_ref, carry, *, tc):
    @pl.when(pl.program_id(1) == 0)
    def _():
        carry[...] = jnp.zeros_like(carry)

    lf = _log_sigmoid(zs_ref[0] + b_ref[...])
    c = _dot3_r(_tri(tc, upper=False), lf) + carry[...]
    lf_ref[0] = lf
    c_ref[0] = c
    carry[...] = c[tc - 1:tc, :]


def _fox_prep(zs3, b_pad):
    bsz, t, _ = zs3.shape
    tc = min(t, 256)
    spec = pl.BlockSpec((1, tc, HD), lambda b, i: (b, i, 0))
    return pl.pallas_call(
        functools.partial(_fox_prep_body, tc=tc),
        grid=(bsz, t // tc),
        in_specs=[spec, pl.BlockSpec((1, HD), lambda b, i: (0, 0))],
        out_specs=[spec, spec],
        out_shape=[jax.ShapeDtypeStruct(zs3.shape, F32)] * 2,
        scratch_shapes=[pltpu.VMEM((1, HD), F32)],
        compiler_params=_cparams(("parallel", "arbitrary")),
    )(zs3, b_pad)


def _flash_body(*refs, mode, tq, tk, hpg, off):
    it = iter(refs)
    q_ref, k_ref, v_ref = next(it), next(it), next(it)
    bias_ref = next(it) if mode == "fox" else None
    sel_ref = next(it) if mode == "slc" else None
    o_ref, kb, vb = next(it), next(it), next(it)
    i = pl.program_id(2)

    @pl.when(i == 0)
    def _():
        kb[...] = k_ref[0].astype(BF16)
        vb[...] = v_ref[0].astype(BF16)

    q = q_ref[0]
    if hpg > 1:
        q = jnp.concatenate([q[:, h * HD:(h + 1) * HD] for h in range(hpg)], axis=0)
    qa = (q * ATTN_SCALE).astype(BF16)
    tile_rows = (lambda x: jnp.concatenate([x] * hpg, axis=0)) if hpg > 1 else (lambda x: x)
    q_lo = off + i * tq
    t_k = kb.shape[0]
    if mode == "slc":
        unpicked = (sel_ref[0, 0] - 1.0).astype(BF16)

    def attend(blocks):
        ss = []
        for b, (ks, size, masked) in enumerate(blocks):
            s = _dot_nt(qa, kb[pl.ds(ks, size), :])
            if mode == "fox":
                s = s - bias_ref[0, b:b + 1, :]
            if mode == "slc":
                cb = lax.broadcasted_iota(jnp.int32, (HD, size), 0)
                kk = lax.broadcasted_iota(jnp.int32, (HD, size), 1)
                expand = jnp.where(cb == (ks + kk) // SEL_BLOCK, 2.0 ** 100, 0.0).astype(BF16)
                s = s + tile_rows(_dot(unpicked, expand))
            if masked:
                kidx = ks + lax.broadcasted_iota(jnp.int32, (tq, size), 1)
                qpos = q_lo + lax.broadcasted_iota(jnp.int32, (tq, size), 0)
                valid = kidx <= qpos
                if mode == "win":
                    valid = valid & ((qpos - kidx) < WINDOW)
                s = jnp.where(tile_rows(valid), s, NEG)
            ss.append(s)
        m = ss[0].max(axis=-1, keepdims=True)
        for s in ss[1:]:
            m = jnp.maximum(m, s.max(axis=-1, keepdims=True))
        l, acc = 0.0, 0.0
        for s, (ks, size, _) in zip(ss, blocks):
            p = jnp.exp(s - m)
            l = l + jnp.sum(p, axis=-1, keepdims=True)
            acc = acc + _dot(p.astype(BF16), vb[pl.ds(ks, size), :])
        o = acc / l
        for h in range(hpg):
            o_ref[0, :, h * HD:(h + 1) * HD] = o[h * tq:(h + 1) * tq].astype(o_ref.dtype)

    if mode == "win":
        span = min(t_k, -(-(WINDOW + tq) // HD) * HD + HD)
        start = jnp.clip((q_lo - (WINDOW - 1)) // HD * HD, 0, t_k - span)
        attend([(pl.multiple_of(start, HD), span, True)])
    else:
        n_max = t_k // tk
        need = (q_lo + tq - 1) // tk + 1
        for nb in range(1, n_max + 1):
            @pl.when(need == nb)
            def _(nb=nb):
                attend([(b * tk, tk, b == nb - 1) for b in range(nb)])


def _flash(mode, q_arr, q_col, k_arr, k_col, v_arr, v_col, *, n_kv, hpg, tq, tk, off=0,
           bias=None, sel=None, out_dtype=F32):
    bsz, t_q = q_arr.shape[0], q_arr.shape[1]
    t_k = k_arr.shape[1]
    nq = t_q // tq
    w = hpg * HD
    in_specs = [pl.BlockSpec((1, tq, w), lambda b, g, i: (b, i, q_col // w + g)),
                pl.BlockSpec((1, t_k, HD), lambda b, g, i: (b, 0, k_col // HD + g)),
                pl.BlockSpec((1, t_k, HD), lambda b, g, i: (b, 0, v_col // HD + g))]
    args = [q_arr, k_arr, v_arr]
    if mode == "fox":
        in_specs.append(pl.BlockSpec((1, t_k // tk, tk), lambda b, g, i: (b * n_kv + g, 0, 0)))
        args.append(bias)
    if mode == "slc":
        in_specs.append(pl.BlockSpec((1, 1, tq, HD), lambda b, g, i: (b, g, i, 0)))
        args.append(sel)
    return pl.pallas_call(
        functools.partial(_flash_body, mode=mode, tq=tq, tk=tk, hpg=hpg, off=off),
        grid=(bsz, n_kv, nq),
        in_specs=in_specs,
        out_specs=pl.BlockSpec((1, tq, w), lambda b, g, i: (b, i, g)),
        out_shape=jax.ShapeDtypeStruct((bsz, t_q, n_kv * w), out_dtype),
        scratch_shapes=[pltpu.VMEM((t_k, HD), BF16), pltpu.VMEM((t_k, HD), BF16)],
        compiler_params=_cparams(("parallel", "parallel", "arbitrary")),
    )(*args)


def _compress_body(*refs, paged, pg, n_steps, mn, norms, t_new):
    it = iter(refs)
    if paged:
        next(it)
    n_t = len(norms)
    x_all = [[next(it) for _ in range(pg if paged else G_NSA)] for _ in range(n_t)]
    new_all = [next(it) for _ in range(n_t)] if paged else None
    par_all = [(next(it), next(it), next(it)) for _ in range(n_t)]
    o_all = [next(it) for _ in range(n_t)]
    hcat, tb, perm = next(it), next(it), next(it)
    p = pl.program_id(1)
    mh = mn + 8
    per_page = PAGE // CMP_STRIDE
    n_real = n_steps * pg * per_page
    n_g = G_NSA if paged else 1

    @pl.when(p == 0)
    def _():
        hcat[:, n_real:, :] = jnp.zeros((n_t * G_NSA, mh - n_real, CMP_STRIDE * HD), F32)
        d = lax.broadcasted_iota(jnp.int32, perm.shape, 0)
        s = lax.broadcasted_iota(jnp.int32, perm.shape, 1)
        src = (CMP_STRIDE * (d % per_page) + (d % PAGE) // per_page) * n_g + d // PAGE
        perm[...] = jnp.where(s == src, 1.0, 0.0).astype(BF16)

    for ti in range(n_t):
        for pp in range(pg):
            r0 = pl.multiple_of((p * pg + pp) * per_page, per_page)
            for x_ref, g0 in ([(x_all[ti][pp], 0)] if paged else [(x_all[ti][g], g) for g in range(G_NSA)]):
                rows = _dot(perm[...], x_ref[0].astype(BF16))
                for g in range(n_g):
                    for l in range(CMP_STRIDE):
                        a = g * PAGE + l * per_page
                        hcat[ti * G_NSA + g0 + g, pl.ds(r0, per_page), l * HD:(l + 1) * HD] = rows[a:a + per_page]

    @pl.when(p == n_steps - 1)
    def _():
        for ti in range(n_t):
            pe_ref, w_ref, gain_ref = par_all[ti]
            if paged:
                xn = new_all[ti][0]
                rid = lax.broadcasted_iota(jnp.int32, xn.shape, 0)
                xn = jnp.where(rid < t_new, xn, 0.0)
                for g in range(G_NSA):
                    for l in range(8):
                        hcat[ti * G_NSA + g, n_real:n_real + 1, l * HD:(l + 1) * HD] = xn[l:l + 1, g * HD:(g + 1) * HD]
            w = w_ref[...]
            pe2 = _dot(pe_ref[...].astype(BF16), w)
            pe_bias = pe2[0:1, 0:HD] + pe2[1:2, HD:2 * HD]
            for g in range(G_NSA):
                tb[...] = _dot(hcat[ti * G_NSA + g].astype(BF16), w)
                y = tb[0:mn, 0:HD] + tb[1:mn + 1, HD:2 * HD] + pe_bias
                if norms[ti]:
                    y = y * lax.rsqrt(jnp.mean(y * y, axis=-1, keepdims=True) + EPS) * gain_ref[...]
                o_all[ti][0, :, g * HD:(g + 1) * HD] = y


def _compress(x_arr, x_cols, params, norms, *, mn, page_table=None, pools=None, pool_off=0, t_new=0, pg=32):
    paged = page_table is not None
    bsz = x_arr.shape[0]
    n_t = len(norms)
    mh = mn + 8
    w2 = G_NSA * HD
    const = lambda shape: pl.BlockSpec(shape, (lambda b, p, *_: (0,) * len(shape)))
    tail = [const((8, CMP_STRIDE * HD)), const((CMP_STRIDE * HD, 2 * HD)), const((1, HD))] * n_t
    tail_args = [a for par in params for a in par]
    out_specs = [pl.BlockSpec((1, mn, w2), lambda b, p, *_: (b, 0, 0))] * n_t
    n_perm = G_NSA * PAGE if paged else PAGE
    scratch = [pltpu.VMEM((n_t * G_NSA, mh, CMP_STRIDE * HD), F32), pltpu.VMEM((mh, 2 * HD), F32),
               pltpu.VMEM((n_perm, n_perm), BF16)]
    out_shape = [jax.ShapeDtypeStruct((bsz, mn, w2), F32)] * n_t
    if not paged:
        n_steps = x_arr.shape[1] // PAGE
        body = functools.partial(_compress_body, paged=False, pg=1, n_steps=n_steps, mn=mn, norms=norms, t_new=0)
        return pl.pallas_call(
            body, grid=(bsz, n_steps),
            in_specs=[pl.BlockSpec((1, PAGE, HD), lambda b, p, _c=c // HD + g: (b, p, _c))
                      for c in x_cols for g in range(G_NSA)] + tail,
            out_specs=out_specs, out_shape=out_shape, scratch_shapes=scratch,
            compiler_params=_cparams(("parallel", "arbitrary")),
        )(*([x_arr] * (n_t * G_NSA)), *tail_args)
    pg = min(pg, page_table.shape[1])
    n_steps = page_table.shape[1] // pg
    body = functools.partial(_compress_body, paged=True, pg=pg, n_steps=n_steps, mn=mn, norms=norms, t_new=t_new)
    grid_spec = pltpu.PrefetchScalarGridSpec(
        num_scalar_prefetch=1, grid=(bsz, n_steps),
        in_specs=[pl.BlockSpec((1, G_NSA * PAGE, HD), lambda b, p, pt, _pp=pp: (pool_off + pt[b, p * pg + _pp], 0, 0))
                  for _ in range(n_t) for pp in range(pg)]
        + [pl.BlockSpec((1, 8, w2), lambda b, p, pt, _c=c // w2: (b, 0, _c)) for c in x_cols] + tail,
        out_specs=out_specs, scratch_shapes=scratch)
    return pl.pallas_call(body, grid_spec=grid_spec, out_shape=out_shape,
                          compiler_params=_cparams(("parallel", "arbitrary")))(
        page_table, *[pool for pool in pools for _ in range(pg)], *([x_arr] * n_t), *tail_args)


def _cmpsel_body(q_ref, kc_ref, vc_ref, o_ref, sel_ref, *, tq, ncp, nsp, n_sel, pos0, gs):
    i = pl.program_id(2)
    q = q_ref[0]
    pos = pos0 + i * tq + lax.broadcasted_iota(jnp.int32, (tq, ncp), 0)
    col = lax.broadcasted_iota(jnp.int32, (tq, ncp), 1)
    valid = (col * CMP_STRIDE + (CMP_BLOCK - 1)) <= pos
    validf = jnp.where(valid, 1.0, 0.0)
    imps = []
    for g in range(gs):
        kc = kc_ref[0, :, g * HD:(g + 1) * HD].astype(BF16)
        vc = vc_ref[0, :, g * HD:(g + 1) * HD].astype(BF16)
        imp = jnp.zeros((tq, ncp), F32)
        for h in range(g * HPG, (g + 1) * HPG):
            qh = (q[:, h * HD:(h + 1) * HD] * ATTN_SCALE).astype(BF16)
            s = jnp.where(valid, _dot_nt(qh, kc), NEG)
            e = jnp.exp(s - jnp.max(s, axis=-1, keepdims=True)) * validf
            p = e / jnp.maximum(jnp.sum(e, axis=-1, keepdims=True), 1e-30)
            o_ref[0, :, h * HD:(h + 1) * HD] = _dot(p.astype(BF16), vc)
            imp = imp + p
        imps.append(imp)
    imp = jnp.concatenate(imps, axis=0)
    cc = lax.broadcasted_iota(jnp.int32, (ncp, nsp), 0)
    jj = lax.broadcasted_iota(jnp.int32, (ncp, nsp), 1)
    r = SEL_BLOCK // CMP_STRIDE
    gather = jnp.where((cc >= r * jj - 1) & (cc <= r * jj + r - 1), 1.0, 0.0).astype(BF16)
    score = _dot3_l(imp, gather)
    rows = gs * tq
    blk = lax.broadcasted_iota(jnp.int32, (rows, nsp), 1)
    pq = pos0 + i * tq + lax.broadcasted_iota(jnp.int32, (rows, nsp), 0) % tq
    forced = (blk == pq // SEL_BLOCK) | (blk == 0)
    future = blk * SEL_BLOCK > pq
    score = jnp.where(forced, FORCE_SCORE, jnp.where(future, -1.0, score))
    score = jnp.where(blk < n_sel, score, -2.0)
    if rows % HD == 0:
        n8 = -(-n_sel // 8) * 8
        st = score.T[:n8]
        bt = lax.broadcasted_iota(jnp.int32, (n8, rows), 0)
        rank = jnp.zeros((n8, rows), F32)
        for c in range(n_sel):
            sc = st[c:c + 1, :]
            ahead = (sc > st) | ((sc == st) & (bt > c))
            rank = rank + jnp.where(ahead, 1.0, 0.0)
        keep = jnp.where(rank < float(N_SELECT), 1.0, 0.0)
        sel = jnp.concatenate([keep, jnp.zeros((nsp - n8, rows), F32)], axis=0).T
    else:
        rank = jnp.zeros((rows, nsp), F32)
        for c in range(n_sel):
            sc = score[:, c:c + 1]
            ahead = (sc > score) | ((sc == score) & (blk > c))
            rank = rank + jnp.where(ahead, 1.0, 0.0)
        sel = jnp.where(rank < float(N_SELECT), 1.0, 0.0)
    for g in range(gs):
        sel_ref[0, g] = sel[g * tq:(g + 1) * tq]


def _cmpsel(z3, kcmp, vcmp, *, tq, pos0, n_sel, gs):
    bsz, t, _ = z3.shape
    ncp = kcmp.shape[1]
    nsp = -(-n_sel // HD) * HD
    w = gs * HPG * HD
    return pl.pallas_call(
        functools.partial(_cmpsel_body, tq=tq, ncp=ncp, nsp=nsp, n_sel=n_sel, pos0=pos0, gs=gs),
        grid=(bsz, G_NSA // gs, t // tq),
        in_specs=[pl.BlockSpec((1, tq, w), lambda b, g, i: (b, i, NQ // w + g)),
                  pl.BlockSpec((1, ncp, gs * HD), lambda b, g, i: (b, 0, g)),
                  pl.BlockSpec((1, ncp, gs * HD), lambda b, g, i: (b, 0, g))],
        out_specs=[pl.BlockSpec((1, tq, w), lambda b, g, i: (b, i, g)),
                   pl.BlockSpec((1, gs, tq, nsp), lambda b, g, i: (b, g, i, 0))],
        out_shape=[jax.ShapeDtypeStruct((bsz, t, H_NSA * HD), F32),
                   jax.ShapeDtypeStruct((bsz, G_NSA, t, nsp), F32)],
        compiler_params=_cparams(("parallel", "parallel", "parallel")),
    )(z3, kcmp, vcmp)


def _combine_body(oc_ref, os_ref, ow_ref, zs_ref, gb_ref, o_ref):
    gates = _sigmoid(zs_ref[...] + gb_ref[...])
    for h in range(H_NSA):
        sl = slice(h * HD, (h + 1) * HD)
        c0 = H_FOX + 3 * h
        o = (gates[:, c0:c0 + 1] * oc_ref[:, sl] + gates[:, c0 + 1:c0 + 2] * os_ref[:, sl]
             + gates[:, c0 + 2:c0 + 3] * ow_ref[:, sl])
        o_ref[:, sl] = o.astype(BF16)


def _nsa_combine(oc, osl, ow, zs, gb_pad):
    m, n = oc.shape
    tm = min(m, 512)
    big = pl.BlockSpec((tm, n), lambda i: (i, 0))
    return pl.pallas_call(
        _combine_body, grid=(m // tm,),
        in_specs=[big, big, big, pl.BlockSpec((tm, HD), lambda i: (i, 0)), pl.BlockSpec((1, HD), lambda i: (0, 0))],
        out_specs=big, out_shape=jax.ShapeDtypeStruct((m, n), BF16),
        compiler_params=_cparams(("parallel",)),
    )(oc, osl, ow, zs, gb_pad)


def _paged_body(*refs, mode, pg, n_steps, n_w, hpg_kv, t_new):
    it = iter(refs)
    next(it)
    q_ref = next(it)
    k_refs = [next(it) for _ in range(pg)]
    v_refs = [next(it) for _ in range(pg)]
    cl_refs = [next(it) for _ in range(pg)] if mode == "fox" else None
    tf_refs = [next(it) for _ in range(pg)] if mode == "fox" else None
    kn_ref, vn_ref = next(it), next(it)
    lfn_ref = next(it) if mode == "fox" else None
    sel_ref, seln_ref = (next(it), next(it)) if mode == "slc" else (None, None)
    o_ref = next(it)
    qb, m_scr, l_scr, acc, kn_scr, vn_scr, carry = (next(it) for _ in range(7))
    p = pl.program_id(1)
    rows = 4 * 8
    r2 = n_w * PAGE
    n_new = t_new * n_w

    def kv_match(ncols):
        row = lax.broadcasted_iota(jnp.int32, (rows, ncols), 0)
        col = lax.broadcasted_iota(jnp.int32, (rows, ncols), 1)
        return row, col, (col % n_w) == (row % 8) // hpg_kv

    @pl.when(p == 0)
    def _():
        qb[...] = (q_ref[0] * ATTN_SCALE).astype(BF16)
        m_scr[...] = jnp.full(m_scr.shape, NEG, F32)
        l_scr[...] = jnp.zeros(l_scr.shape, F32)
        acc[...] = jnp.zeros(acc.shape, F32)
        carry[...] = jnp.zeros(carry.shape, F32)
        kn_scr[...] = jnp.zeros(kn_scr.shape, BF16)
        vn_scr[...] = jnp.zeros(vn_scr.shape, BF16)

    def picked(choice, first_lane, ncols):
        half = (lax.broadcasted_iota(jnp.int32, (rows, ncols), 1) // n_w) // SEL_BLOCK
        out = None
        for j in range(PAGE // SEL_BLOCK):
            term = (half == j) & (choice[:, first_lane + j:first_lane + j + 1] > 0.5)
            out = term if out is None else (out | term)
        return out

    def update(s_list, vb_list):
        m_old = m_scr[...]
        m_new = m_old
        for s in s_list:
            m_new = jnp.maximum(m_new, jnp.max(s, axis=-1, keepdims=True))
        a = jnp.exp(m_old - m_new)
        l = a * l_scr[...]
        o = a * acc[...]
        for s, vb in zip(s_list, vb_list):
            pr = jnp.exp(s - m_new)
            l = l + jnp.sum(pr, axis=-1, keepdims=True)
            o = o + _dot(pr.astype(BF16), vb)
        m_scr[...] = m_new
        l_scr[...] = l
        acc[...] = o

    _, _, match = kv_match(r2)
    s_list, vb_list = [], []
    c_run = carry[...] if mode == "fox" else None
    for pp in range(pg):
        s = _dot_nt(qb[...], k_refs[pp][0].astype(BF16))
        valid = match
        if mode == "fox":
            s = s - (c_run + cl_refs[pp][0])
            c_run = c_run + tf_refs[pp][0]
        if mode == "slc":
            valid = valid & picked(sel_ref[0, 0], pp * (PAGE // SEL_BLOCK), r2)
        s_list.append(jnp.where(valid, s, NEG))
        vb_list.append(v_refs[pp][0].astype(BF16))
    if mode == "fox":
        carry[...] = c_run
    update(s_list, vb_list)

    @pl.when(p == n_steps - 1)
    def _():
        kn_scr[0:n_new, :] = kn_ref[0].astype(BF16)
        vn_scr[0:n_new, :] = vn_ref[0].astype(BF16)
        s = _dot_nt(qb[...], kn_scr[...])
        row, col, valid = kv_match(HD)
        valid = valid & (col // n_w <= row // 8) & (col < n_new)
        if mode == "fox":
            rr = lax.broadcasted_iota(jnp.int32, (HD, HD), 0)
            cc = lax.broadcasted_iota(jnp.int32, (HD, HD), 1)
            pre = jnp.where((rr % n_w == cc % n_w) & (rr // n_w <= cc // n_w), 1.0, 0.0).astype(BF16)
            c_new = _dot3_l(jnp.broadcast_to(lfn_ref[0], (8, HD)), pre)[0:1, :] + carry[:, 0:HD]
            s = s - c_new
        if mode == "slc":
            valid = valid & picked(seln_ref[0], 0, HD)
        update([jnp.where(valid, s, NEG)], [vn_scr[...]])
        o_ref[0] = acc[...] / l_scr[...]


def _paged_attn(mode, q_rows, page_table, k_pool, v_pool, pool_off, k_new, v_new, *, hpg_kv, pg,
                c_local=None, c_total=None, lf_new=None, sel_rows=None):
    bsz = q_rows.shape[0]
    n_pages = page_table.shape[1]
    pg = min(pg, n_pages)
    n_steps = n_pages // pg
    r2 = k_pool.shape[1]
    n_w = r2 // PAGE
    n_new = k_new.shape[1]
    page = lambda pp: (lambda b, p, pt: (pool_off + pt[b, p * pg + pp], 0, 0))
    per_b = lambda b, p, pt: (b, 0, 0)
    in_specs = [pl.BlockSpec((1, 32, HD), per_b)]
    in_specs += [pl.BlockSpec((1, r2, HD), page(pp)) for pp in range(pg)] * 2
    args = [q_rows] + [k_pool] * pg + [v_pool] * pg
    if mode == "fox":
        in_specs += [pl.BlockSpec((1, 1, r2), page(pp)) for pp in range(pg)] * 2
        args += [c_local] * pg + [c_total] * pg
    in_specs += [pl.BlockSpec((1, n_new, HD), per_b)] * 2
    args += [k_new, v_new]
    if mode == "fox":
        in_specs.append(pl.BlockSpec((1, 1, HD), per_b))
        args.append(lf_new)
    if mode == "slc":
        per_page = PAGE // SEL_BLOCK
        past = sel_rows[:, :, :n_pages * per_page].reshape(bsz, 32, n_steps, pg * per_page)
        sel_steps = _pad_cols(jnp.swapaxes(past, 1, 2).reshape(bsz * n_steps * 32, pg * per_page), HD)
        in_specs += [pl.BlockSpec((1, 1, 32, HD), lambda b, p, pt: (b, p, 0, 0)), pl.BlockSpec((1, 32, HD), per_b)]
        args += [sel_steps.reshape(bsz, n_steps, 32, HD),
                 _pad_cols(sel_rows[:, :, n_pages * per_page:n_pages * per_page + 1].reshape(bsz * 32, 1), HD)
                 .reshape(bsz, 32, HD)]
    grid_spec = pltpu.PrefetchScalarGridSpec(
        num_scalar_prefetch=1, grid=(bsz, n_steps), in_specs=in_specs,
        out_specs=pl.BlockSpec((1, 32, HD), per_b),
        scratch_shapes=[pltpu.VMEM((32, HD), BF16), pltpu.VMEM((32, 1), F32), pltpu.VMEM((32, 1), F32),
                        pltpu.VMEM((32, HD), F32), pltpu.VMEM((PAGE, HD), BF16), pltpu.VMEM((PAGE, HD), BF16),
                        pltpu.VMEM((1, r2), F32)])
    return pl.pallas_call(
        functools.partial(_paged_body, mode=mode, pg=pg, n_steps=n_steps, n_w=n_w, hpg_kv=hpg_kv, t_new=n_new // n_w),
        grid_spec=grid_spec, out_shape=jax.ShapeDtypeStruct((bsz, 32, HD), F32),
        compiler_params=_cparams(("parallel", "arbitrary")),
    )(page_table, *args)


def _pool_prefix_body(x_ref, cl_ref, ct_ref, w_scr, *, n_w):
    r2 = x_ref.shape[1]

    @pl.when(pl.program_id(0) == 0)
    def _():
        r = lax.broadcasted_iota(jnp.int32, (r2, 2 * r2), 0)
        c = lax.broadcasted_iota(jnp.int32, (r2, 2 * r2), 1)
        same = (r % n_w) == (c % n_w)
        local = same & (r // n_w <= c // n_w) & (c < r2)
        w_scr[...] = jnp.where(local | (same & (c >= r2)), 1.0, 0.0).astype(BF16)

    y = _dot3_l(x_ref[...], w_scr[...])
    cl_ref[...] = y[:, :r2]
    ct_ref[...] = y[:, r2:]


def _pool_prefix(x, n_w):
    n, r2 = x.shape
    tm = next((c for c in (256, 128, 64, 32, 16, 8) if n % c == 0), n)
    spec = pl.BlockSpec((tm, r2), lambda i: (i, 0))
    return pl.pallas_call(
        functools.partial(_pool_prefix_body, n_w=n_w), grid=(n // tm,),
        in_specs=[spec], out_specs=[spec, spec],
        out_shape=[jax.ShapeDtypeStruct((n, r2), F32)] * 2,
        scratch_shapes=[pltpu.VMEM((r2, 2 * r2), BF16)],
        compiler_params=_cparams(("arbitrary",)),
    )(x)


def _gla_body(*refs, mode, t_in, chunk, dv, t_real, has_state, hb):
    it = iter(refs)
    q_ref, k_ref, v_ref, og_ref = next(it), next(it), next(it), next(it)
    if mode == "gla":
        zs_ref, wa_ref, ba_ref = next(it), next(it), next(it)
    else:
        lb_ref = next(it)
    gain_ref = next(it)
    s0_ref = next(it) if has_state else None
    o_ref, s_ref, st_scr, q_scr, k_scr, g_scr, o_scr = (next(it) for _ in range(7))
    tb = q_scr.shape[1]
    n_chunks = tb // chunk
    pad = tb - t_in
    tri = lax.broadcasted_iota(jnp.int32, (chunk, chunk), 0) >= lax.broadcasted_iota(jnp.int32, (chunk, chunk), 1)
    if mode == "hg":
        lbr = lb_ref[...]
        e = jnp.exp(lbr - jnp.max(lbr, axis=0, keepdims=True))
        lb_all = (e / jnp.sum(e, axis=0, keepdims=True))[0:1, :]
    ti = pl.program_id(2)

    @pl.when(ti == 0)
    def _():
        for h in range(hb):
            st_scr[h] = s0_ref[0, h].T if has_state else jnp.zeros((dv, HD), F32)

    def block(ref, sl):
        x = ref[0, :, sl]
        if pad:
            x = jnp.concatenate([x, jnp.zeros((pad, x.shape[1]), F32)], axis=0)
        return x

    live = (ti * t_in + lax.broadcasted_iota(jnp.int32, (tb, HD), 0)) < t_real
    if mode == "gla":
        ga = block(zs_ref, slice(0, HD)).astype(BF16)
    for h in range(hb):
        ks = slice(h * HD, (h + 1) * HD)
        qr, kr = block(q_ref, ks), block(k_ref, ks)
        if mode == "gla":
            g = _log_sigmoid(_dot(ga, wa_ref[:, ks]) + ba_ref[:, ks]) / GLA_GATE_NORM
            q, k = qr, kr
        else:
            lb = lb_all[:, ks]
            f = lb + (1.0 - lb) * _sigmoid(kr)
            q, k, g = qr * _sigmoid(qr), 1.0 - f, jnp.log(f)
        q_scr[h] = q
        k_scr[h] = jnp.where(live, k, 0.0)
        g_scr[h] = jnp.where(live, g, 0.0)

    def body(c, _):
        r0 = pl.multiple_of(c * chunk, chunk)
        rows = pl.ds(r0, chunk)
        for h in range(hb):
            vs = slice(h * dv, (h + 1) * dv)
            q, k, g = q_scr[h, rows, :], k_scr[h, rows, :], g_scr[h, rows, :]
            if pad:
                v = block(v_ref, vs)
            else:
                v = v_ref[0, rows, vs]
            b = _cumsum_rows(g)
            bm = b[chunk // 2 - 1:chunk // 2, :]
            bl = b[chunk - 1:chunk, :]
            qe = (q * jnp.exp(jnp.minimum(b - bm, EXP_CLAMP))).astype(BF16)
            ke = (k * jnp.exp(jnp.minimum(bm - b, EXP_CLAMP))).astype(BF16)
            a = jnp.where(tri, _dot_nt(qe, ke), 0.0)
            vb = v.astype(BF16)
            st = st_scr[h]
            o_scr[h, rows, :] = _dot(a.astype(BF16), vb) + _dot_nt((q * jnp.exp(b)).astype(BF16), st.astype(BF16))
            kd = (k * jnp.exp(bl - b)).astype(BF16)
            st_scr[h] = st * jnp.exp(bl) + _dot_tn(vb, kd)
        return 0

    lax.fori_loop(0, n_chunks, body, 0)

    for h in range(hb):
        vs = slice(h * dv, (h + 1) * dv)
        o = o_scr[h, 0:t_in, :]
        og = og_ref[0, :, vs]
        o = o * lax.rsqrt(jnp.mean(o * o, axis=-1, keepdims=True) + EPS) * gain_ref[...]
        o_ref[0, :, vs] = (o * (og * _sigmoid(og) if mode == "gla" else _sigmoid(og))).astype(BF16)

    @pl.when(ti == pl.num_programs(2) - 1)
    def _():
        for h in range(hb):
            s_ref[0, h] = st_scr[h].T


def _gla(mode, z3, zs3, cols, n_heads, dv, extra, gain, s0, *, t_real, chunk, hb):
    bsz, t_all, _ = z3.shape
    tc = min(t_all, 512)
    qc, kc, vc, oc = cols
    has_state = s0 is not None
    wk, wv = hb * HD, hb * dv
    in_specs = [pl.BlockSpec((1, tc, wk), lambda b, h, t: (b, t, qc // wk + h)),
                pl.BlockSpec((1, tc, wk), lambda b, h, t: (b, t, kc // wk + h)),
                pl.BlockSpec((1, tc, wv), lambda b, h, t: (b, t, vc // wv + h)),
                pl.BlockSpec((1, tc, wv), lambda b, h, t: (b, t, oc // wv + h))]
    args = [z3, z3, z3, z3]
    if mode == "gla":
        wa, ba = extra
        in_specs += [pl.BlockSpec((1, tc, HD), lambda b, h, t: (b, t, 0)),
                     pl.BlockSpec((HD, wk), lambda b, h, t: (0, h)),
                     pl.BlockSpec((1, wk), lambda b, h, t: (0, h))]
        args += [zs3, wa, ba]
    else:
        in_specs.append(pl.BlockSpec((8, wk), lambda b, h, t: (0, h)))
        args.append(extra)
    in_specs.append(pl.BlockSpec((1, dv), lambda b, h, t: (0, 0)))
    args.append(gain)
    st_spec = pl.BlockSpec((1, hb, HD, dv), lambda b, h, t: (b, h, 0, 0))
    if has_state:
        in_specs.append(st_spec)
        args.append(s0)
    return pl.pallas_call(
        functools.partial(_gla_body, mode=mode, t_in=tc, chunk=chunk, dv=dv, t_real=t_real, has_state=has_state,
                          hb=hb),
        grid=(bsz, n_heads // hb, t_all // tc),
        in_specs=in_specs,
        out_specs=[pl.BlockSpec((1, tc, wv), lambda b, h, t: (b, t, h)), st_spec],
        out_shape=[jax.ShapeDtypeStruct((bsz, t_all, n_heads * dv), BF16),
                   jax.ShapeDtypeStruct((bsz, n_heads, HD, dv), F32)],
        scratch_shapes=[pltpu.VMEM((hb, dv, HD), F32)] + [pltpu.VMEM((hb, max(tc, chunk), HD), F32)] * 3
        + [pltpu.VMEM((hb, max(tc, chunk), dv), F32)],
        compiler_params=_cparams(("parallel", "parallel", "arbitrary")),
    )(*args)


def _split_cols(w, sizes):
    outs, off = [], 0
    for s in sizes:
        outs.append(w[:, off:off + s])
        off += s
    return outs


def _pad_cols(w, n):
    return jnp.pad(w, ((0, 0), (0, n - w.shape[1])))


def _row(v):
    return v.reshape(1, -1).astype(F32)


def _even_weights(w_in, fq_gain, fk_gain, nq_gain, nk_gain):
    fq, fk, fv, ff, nq, kc, vc, ks, vs, kw, vw, ng = _split_cols(w_in, EVEN_SIZES)
    big = jnp.concatenate([fq, fk, fv, nq, kc, vc, ks, vs, kw, vw], axis=1).astype(BF16)
    small = _pad_cols(jnp.concatenate([ff, ng], axis=1), HD).astype(BF16)
    ones = lambda n: jnp.ones((n,), F32)
    gain = jnp.concatenate([jnp.tile(fq_gain, H_FOX), jnp.tile(fk_gain, H_FOX), ones(H_FOX * HD),
                            jnp.tile(nq_gain, H_NSA), ones(2 * G_NSA * HD), jnp.tile(nk_gain[1], G_NSA),
                            ones(G_NSA * HD), jnp.tile(nk_gain[2], G_NSA), ones(G_NSA * HD)])
    z, o = jnp.zeros, jnp.ones
    flag = jnp.concatenate([o((2 * H_FOX * HD,), F32), z((H_FOX * HD,), F32), o((H_NSA * HD,), F32),
                            z((2 * G_NSA * HD,), F32), o((G_NSA * HD,), F32), z((G_NSA * HD,), F32),
                            o((G_NSA * HD,), F32), z((G_NSA * HD,), F32)])
    return big, small, _row(gain), _row(flag)


def _odd_weights(w_in):
    gq, gk, gv, ga, gg, hq, hf, hi, hg = _split_cols(w_in, ODD_SIZES)
    big = jnp.concatenate([gq, gk, gv, gg, hq, hf, hi, hg], axis=1).astype(BF16)
    small = _pad_cols(ga, HD).astype(BF16)
    gain = jnp.concatenate([jnp.full((H_GLA * DK_GLA,), DK_GLA ** -0.5, F32), jnp.ones((N_BIG_ODD - H_GLA * DK_GLA,), F32)])
    return big, small, _row(gain), jnp.zeros((1, N_BIG_ODD), F32)


def _cmp_weights(pe, w):
    half = CMP_STRIDE * HD
    wcat = jnp.concatenate([w[:half], w[half:]], axis=1).astype(BF16)
    pe8 = jnp.pad(pe.reshape(2, half), ((0, 6), (0, 0)))
    return pe8, wcat


def _ffn(x, g, w1, w3, w2, layer):
    return _mm_res([_ffn_up(x, g, w1, w3, layer)], w2, layer, x)


def _even_layer(x, bsz, t, t_real, ew, past):
    (w_big, w_small, cgain, cflag, g_mix, w_out, b_f_pad, gb_pad, pe_k, wc_k, pe_v, wc_v, nk0) = ew
    z, zs = _proj(x, g_mix, w_big, w_small, cgain, cflag, True)
    z3 = z.reshape(bsz, t, N_BIG_EVEN)
    zs3 = zs.reshape(bsz, t, HD)
    cmp_params = ((pe_k, wc_k, nk0), (pe_v, wc_v, jnp.ones((1, HD), F32)))
    if past is None:
        lf, c = _fox_prep(zs3, b_f_pad)
        logf_new = lf[:, :, :H_FOX]
        tk = min(t, 512)
        bias = jnp.swapaxes(c[:, :, :H_FOX], 1, 2).reshape(bsz * H_FOX, t // tk, tk)
        o_fox = _flash("fox", z3, FQ, z3, FK, z3, FV, n_kv=H_FOX, hpg=1, tq=min(t, 512), tk=tk, bias=bias,
                       out_dtype=BF16)
        n_cmp = t // CMP_STRIDE
        mn = -(-n_cmp // HD) * HD
        kcmp, vcmp = _compress(z3, (KC, VC), cmp_params, (True, False), mn=mn)
        tqn = min(t, 512 // HPG)
        o_cmp, sel = _cmpsel(z3, kcmp, vcmp, tq=min(t, 256), pos0=0, n_sel=-(-t // SEL_BLOCK), gs=1)
        o_slc = _flash("slc", z3, NQ, z3, KS, z3, VS, n_kv=G_NSA, hpg=HPG, tq=tqn, tk=tk, sel=sel)
        o_win = _flash("win", z3, NQ, z3, KW, z3, VW, n_kv=G_NSA, hpg=HPG, tq=tqn, tk=tk)
        w_buf = WINDOW
        kw_new = jnp.concatenate([jnp.zeros((bsz, w_buf, G_NSA * HD), F32), z3[:, :, KW:KW + G_NSA * HD]], axis=1)[:, -w_buf:]
        vw_new = jnp.concatenate([jnp.zeros((bsz, w_buf, G_NSA * HD), F32), z3[:, :, VW:VW + G_NSA * HD]], axis=1)[:, -w_buf:]
        o_fox = o_fox.reshape(bsz * t, H_FOX * HD)
    else:
        (page_table, pool_off, fk_pool, fv_pool, lf_pool, kc_pool, vc_pool, ks_pool, vs_pool, buf_k, buf_v) = past
        p_len = page_table.shape[1] * PAGE
        lf_all, _ = _fox_prep(zs.reshape(1, bsz * t, HD), b_f_pad)
        lf_new = lf_all.reshape(bsz, t, HD)[:, :t_real, :H_FOX]
        logf_new = lf_new
        lfn_pad = _pad_cols(lf_new.reshape(bsz, t_real * H_FOX), HD).reshape(bsz, 1, HD)
        c_local, c_total = _pool_prefix(lf_pool, H_FOX)
        rows_of = lambda c0, n: z3[:, :t_real, c0:c0 + n * HD].reshape(bsz, t_real * n, HD)
        o_fox = _paged_attn("fox", rows_of(FQ, H_FOX), page_table, fk_pool, fv_pool, pool_off,
                            rows_of(FK, H_FOX), rows_of(FV, H_FOX), hpg_kv=1, pg=16,
                            c_local=c_local[:, None, :], c_total=c_total[:, None, :], lf_new=lfn_pad)
        o_fox = o_fox.reshape(bsz, t_real, H_FOX * HD)
        n_cmp = -(-(p_len + t_real) // CMP_STRIDE)
        mn = -(-n_cmp // HD) * HD
        kcmp, vcmp = _compress(z3, (KC, VC), cmp_params, (True, False), mn=mn, page_table=page_table,
                               pools=(kc_pool, vc_pool), pool_off=pool_off, t_new=t_real)
        n_sel = -(-(p_len + t_real) // SEL_BLOCK)
        o_cmp, sel = _cmpsel(z3, kcmp, vcmp, tq=t, pos0=p_len, n_sel=n_sel, gs=G_NSA)
        nsp = sel.shape[-1]
        sel_rows = jnp.broadcast_to(jnp.swapaxes(sel[:, :, :t_real], 1, 2)[:, :, :, None, :],
                                    (bsz, t_real, G_NSA, HPG, nsp)).reshape(bsz, 32, nsp)
        o_slc = _paged_attn("slc", rows_of(NQ, H_NSA), page_table, ks_pool, vs_pool, pool_off,
                            rows_of(KS, G_NSA), rows_of(VS, G_NSA), hpg_kv=HPG, pg=32,
                            sel_rows=sel_rows).reshape(bsz, t_real, H_NSA * HD)
        w_buf = buf_k.shape[1]
        lk = -(-(w_buf + t) // HD) * HD
        fill = jnp.zeros((bsz, lk - w_buf - t_real, G_NSA * HD), F32)
        kw_all = jnp.concatenate([buf_k, z3[:, :t_real, KW:KW + G_NSA * HD], fill], axis=1)
        vw_all = jnp.concatenate([buf_v, z3[:, :t_real, VW:VW + G_NSA * HD], fill], axis=1)
        kw_new, vw_new = kw_all[:, t_real:t_real + w_buf], vw_all[:, t_real:t_real + w_buf]
        o_win = _flash("win", z3, NQ, kw_all, 0, vw_all, 0, n_kv=G_NSA, hpg=HPG, tq=t, tk=HD, off=w_buf)
        padt = lambda a: jnp.pad(a, ((0, 0), (0, t - t_real), (0, 0)))
        o_fox = padt(o_fox).astype(BF16).reshape(bsz * t, H_FOX * HD)
        o_slc = padt(o_slc)
    m = bsz * t
    o_nsa = _nsa_combine(o_cmp.reshape(m, -1), o_slc.reshape(m, -1), o_win.reshape(m, -1), zs, gb_pad)
    x = _mm_res([o_fox, o_nsa], w_out, 0, x)
    zr = z3[:, :t_real]
    grp = lambda c0: zr[:, :, c0:c0 + G_NSA * HD].reshape(1, bsz, t_real, G_NSA, HD)
    fox = lambda c0: zr[:, :, c0:c0 + H_FOX * HD].reshape(1, bsz, t_real, H_FOX, HD)
    w_rows = kw_new.shape[1]
    outs = (fox(FK), fox(FV), logf_new[None], grp(KC), grp(VC), grp(KS), grp(VS),
            kw_new.reshape(1, bsz, w_rows, G_NSA, HD), vw_new.reshape(1, bsz, w_rows, G_NSA, HD))
    return x, outs


def _odd_layer(x, bsz, t, t_real, ow, state):
    (w_big, w_small, cgain, cflag, g_mix, w_out, wa_pad, ba, gla_gain, lb_pad, hg_gain) = ow
    z, zs = _proj(x, g_mix, w_big, w_small, cgain, cflag, False)
    z3 = z.reshape(bsz, t, N_BIG_ODD)
    zs3 = zs.reshape(bsz, t, HD)
    s_gla, s_hg = state if state is not None else (None, None)
    chunk = 64 if t >= 64 else 128
    o_gla, s_gla = _gla("gla", z3, zs3, (GQ, GK, GV, GG), H_GLA, DV_GLA, (wa_pad, ba), gla_gain, s_gla,
                        t_real=t_real, chunk=chunk, hb=H_GLA)
    o_hg, s_hg = _gla("hg", z3, zs3, (HQ, HF, HI, HO), H_HG, DV_HG, lb_pad, hg_gain, s_hg,
                      t_real=t_real, chunk=chunk, hb=H_HG)
    m = bsz * t
    x = _mm_res([o_gla.reshape(m, -1), o_hg.reshape(m, -1)], w_out, 0, x)
    return x, (s_gla[None], s_hg[None])


def kernel(x_prompt, x_sample, cache_fox_k, cache_fox_v, cache_fox_logf, cache_nsa_kc, cache_nsa_vc, cache_nsa_ks, cache_nsa_vs, state_nsa_kw, state_nsa_vw, state_gla, state_hgrn, page_table, norm_mix, norm_ffn, w_in_even, w_out_even, fox_b_f, fox_q_gain, fox_k_gain, nsa_q_gain, nsa_k_gain, nsa_cmp_pe_k, nsa_cmp_pe_v, nsa_cmp_wk, nsa_cmp_wv, nsa_gate_b, w_in_odd, w_out_odd, gla_wa2, gla_ba, gla_norm, hgrn_lb, hgrn_norm, ffn_w1, ffn_w3, ffn_w2):
    bp, tp, d = x_prompt.shape
    bs, ts, _ = x_sample.shape
    n_pool = cache_fox_k.shape[1]
    ts_pad = 8
    pe_k, wc_k = _cmp_weights(nsa_cmp_pe_k[0], nsa_cmp_wk[0])
    pe_v, wc_v = _cmp_weights(nsa_cmp_pe_v[0], nsa_cmp_wv[0])
    b_f_pad = _pad_cols(_row(fox_b_f[0]), HD)
    gb_pad = _pad_cols(jnp.concatenate([jnp.zeros((1, H_FOX), F32), _row(nsa_gate_b[0])], axis=1), HD)
    ew = _even_weights(w_in_even[0], fox_q_gain[0], fox_k_gain[0], nsa_q_gain[0], nsa_k_gain[0]) + (
        _row(norm_mix[0]), w_out_even, b_f_pad, gb_pad, pe_k, wc_k, pe_v, wc_v, _row(nsa_k_gain[0, 0]))
    wa_pad = jnp.pad(gla_wa2[0], ((0, HD - GLA_RANK), (0, 0))).astype(BF16)
    lb_pad = jnp.pad(hgrn_lb.astype(F32), ((0, 8 - hgrn_lb.shape[0]), (0, 0)), constant_values=-1e30)
    ow = _odd_weights(w_in_odd[0]) + (_row(norm_mix[1]), w_out_odd, wa_pad, _row(gla_ba[0]),
                                      _row(gla_norm[0]), lb_pad, _row(hgrn_norm[0]))
    ffn = [(_row(norm_ffn[i]), ffn_w1, ffn_w3, ffn_w2, i) for i in range(2)]

    flat = lambda c: c.reshape((c.shape[0] * c.shape[1], PAGE * c.shape[3], HD))
    lf_pool = cache_fox_logf.reshape(-1, PAGE * H_FOX)
    win = lambda s: s.reshape(bs, s.shape[2], G_NSA * HD)
    past = (page_table, 0, flat(cache_fox_k), flat(cache_fox_v), lf_pool, flat(cache_nsa_kc), flat(cache_nsa_vc),
            flat(cache_nsa_ks), flat(cache_nsa_vs), win(state_nsa_kw), win(state_nsa_vw))

    def run(x, bsz, t, t_real, past_, state_):
        x, ev = _even_layer(x, bsz, t, t_real, ew, past_)
        x = _ffn(x, *ffn[0])
        x, od = _odd_layer(x, bsz, t, t_real, ow, state_)
        x = _ffn(x, *ffn[1])
        return x, ev, od

    yp, ev_p, od_p = run(x_prompt.reshape(bp * tp, d), bp, tp, tp, None, None)
    xs = jnp.pad(x_sample, ((0, 0), (0, ts_pad - ts), (0, 0))).reshape(bs * ts_pad, d)
    ys, ev_s, od_s = run(xs, bs, ts_pad, ts, past, (state_gla.reshape(state_gla.shape[1:]),
                                                    state_hgrn.reshape(state_hgrn.shape[1:])))
    y_prompt = yp.reshape(bp, tp, d)
    y_sample = ys.reshape(bs, ts_pad, d)[:, :ts]
    outs = [y_prompt, y_sample]
    for a, b in zip(ev_p, ev_s):
        outs += [a, b]
    for a, b in zip(od_p, od_s):
        outs += [a, b]
    return tuple(outs)
```

```python
import functools

import jax
import jax.numpy as jnp
from jax import lax
from jax.experimental import pallas as pl
from jax.experimental.pallas import tpu as pltpu

F32 = jnp.float32
BF16 = jnp.bfloat16

D_MODEL = 2048
HD = 128
H_FOX = 8
H_NSA = 8
G_NSA = 2
HPG = H_NSA // G_NSA
CMP_STRIDE = 16
CMP_BLOCK = 2 * CMP_STRIDE
SEL_BLOCK = 64
N_SELECT = 16
WINDOW = 512
FORCE_SCORE = 1.0e4
H_GLA = 4
DK_GLA = 128
DV_GLA = 256
GLA_RANK = 16
GLA_GATE_NORM = 16.0
H_HG = 8
DK_HG = 128
DV_HG = 128
PAGE = 128
EVEN_SIZES = (H_FOX * HD, H_FOX * HD, H_FOX * HD, H_FOX, H_NSA * HD,
              G_NSA * HD, G_NSA * HD, G_NSA * HD, G_NSA * HD, G_NSA * HD, G_NSA * HD, 3 * H_NSA)
ODD_SIZES = (H_GLA * DK_GLA, H_GLA * DK_GLA, H_GLA * DV_GLA, GLA_RANK, H_GLA * DV_GLA,
             H_HG * DK_HG, H_HG * DK_HG, H_HG * DV_HG, H_HG * DV_HG)
FQ, FK, FV, NQ, KC, VC, KS, VS, KW, VW = 0, 1024, 2048, 3072, 4096, 4352, 4608, 4864, 5120, 5376
N_BIG_EVEN = 5632
GQ, GK, GV, GG, HQ, HF, HI, HO = 0, 512, 1024, 2048, 3072, 4096, 5120, 6144
N_BIG_ODD = 7168
ATTN_SCALE = HD ** -0.5
NEG = -1.0e30
EPS = 1.0e-6
EXP_CLAMP = 80.0
VMEM_LIMIT = 56 * 1024 * 1024


def _cparams(sem):
    return pltpu.CompilerParams(dimension_semantics=sem, vmem_limit_bytes=VMEM_LIMIT)


def _dot(a, b):
    return jnp.dot(a, b, preferred_element_type=F32)


def _dot_nt(a, b):
    return lax.dot_general(a, b, (((1,), (1,)), ((), ())), preferred_element_type=F32)


def _dot_tn(a, b):
    return lax.dot_general(a, b, (((0,), (0,)), ((), ())), preferred_element_type=F32)


def _split3(x):
    hi = x.astype(BF16)
    r = x - hi.astype(F32)
    mid = r.astype(BF16)
    lo = (r - mid.astype(F32)).astype(BF16)
    return hi, mid, lo


def _dot3_l(x, w):
    hi, mid, lo = _split3(x)
    return _dot(hi, w) + _dot(mid, w) + _dot(lo, w)


def _dot3_r(w, x):
    hi, mid, lo = _split3(x)
    return _dot(w, hi) + _dot(w, mid) + _dot(w, lo)


def _sigmoid(x):
    return 1.0 / (1.0 + jnp.exp(-x))


def _log_sigmoid(x):
    return jnp.minimum(x, 0.0) - jnp.log1p(jnp.exp(-jnp.abs(x)))


def _cumsum_rows(x):
    n = x.shape[0]
    row = lax.broadcasted_iota(jnp.int32, x.shape, 0)
    s = 1
    while s < n:
        x = x + jnp.where(row >= s, pltpu.roll(x, s, axis=0), 0.0)
        s *= 2
    return x


def _tri(n, upper):
    r = lax.broadcasted_iota(jnp.int32, (n, n), 0)
    c = lax.broadcasted_iota(jnp.int32, (n, n), 1)
    return jnp.where((r <= c) if upper else (r >= c), 1.0, 0.0).astype(BF16)


def _proj_body(x_ref, g_ref, w_ref, ws_ref, cg_ref, cf_ref, z_ref, zs_ref, h_scr, *, tn, has_norm):
    @pl.when(pl.program_id(1) == 0)
    def _():
        x = x_ref[...]
        r = lax.rsqrt(jnp.mean(x * x, axis=-1, keepdims=True) + EPS)
        h = (x * r * g_ref[...]).astype(BF16)
        h_scr[...] = h
        zs_ref[...] = _dot(h, ws_ref[...])

    z = _dot(h_scr[...], w_ref[...])
    if not has_norm:
        z_ref[...] = z * cg_ref[...]
        return
    for c in range(tn // HD):
        sl = slice(c * HD, (c + 1) * HD)
        zc = z[:, sl]
        r = lax.rsqrt(jnp.mean(zc * zc, axis=-1, keepdims=True) + EPS)
        f = cf_ref[:, sl]
        z_ref[:, sl] = zc * (f * r + (1.0 - f)) * cg_ref[:, sl]


def _proj(x, g, w_big, w_small, col_gain, col_flag, has_norm):
    m, d = x.shape
    n = w_big.shape[1]
    tm = min(m, 1024)
    tn = next(c for c in (1408, 1024, 512) if n % c == 0)
    return pl.pallas_call(
        functools.partial(_proj_body, tn=tn, has_norm=has_norm),
        grid=(m // tm, n // tn),
        in_specs=[pl.BlockSpec((tm, d), lambda i, j: (i, 0)),
                  pl.BlockSpec((1, d), lambda i, j: (0, 0)),
                  pl.BlockSpec((d, tn), lambda i, j: (0, j)),
                  pl.BlockSpec((d, HD), lambda i, j: (0, 0)),
                  pl.BlockSpec((1, tn), lambda i, j: (0, j)),
                  pl.BlockSpec((1, tn), lambda i, j: (0, j))],
        out_specs=[pl.BlockSpec((tm, tn), lambda i, j: (i, j)),
                   pl.BlockSpec((tm, HD), lambda i, j: (i, 0))],
        out_shape=[jax.ShapeDtypeStruct((m, n), F32), jax.ShapeDtypeStruct((m, HD), F32)],
        scratch_shapes=[pltpu.VMEM((tm, d), BF16)],
        compiler_params=_cparams(("parallel", "arbitrary")),
    )(x, g, w_big, w_small, col_gain, col_flag)


def _ffn_up_body(x_ref, g_ref, w1_ref, w3_ref, o_ref, h_scr):
    @pl.when(pl.program_id(1) == 0)
    def _():
        x = x_ref[...]
        r = lax.rsqrt(jnp.mean(x * x, axis=-1, keepdims=True) + EPS)
        h_scr[...] = (x * r * g_ref[...]).astype(BF16)

    h = h_scr[...]
    a = _dot(h, w1_ref[...].astype(BF16))
    b = _dot(h, w3_ref[...].astype(BF16))
    o_ref[...] = (a * _sigmoid(a) * b).astype(BF16)


def _ffn_up(x, g, w1, w3, layer):
    m, d = x.shape
    n = w1.shape[2]
    tm = min(m, 1024)
    tn = 512
    return pl.pallas_call(
        _ffn_up_body,
        grid=(m // tm, n // tn),
        in_specs=[pl.BlockSpec((tm, d), lambda i, j: (i, 0)),
                  pl.BlockSpec((1, d), lambda i, j: (0, 0)),
                  pl.BlockSpec((None, d, tn), lambda i, j: (layer, 0, j)),
                  pl.BlockSpec((None, d, tn), lambda i, j: (layer, 0, j))],
        out_specs=pl.BlockSpec((tm, tn), lambda i, j: (i, j)),
        out_shape=jax.ShapeDtypeStruct((m, n), BF16),
        scratch_shapes=[pltpu.VMEM((tm, d), BF16)],
        compiler_params=_cparams(("parallel", "arbitrary")),
    )(x, g, w1, w3)


def _mm_res_body(*refs, n_in):
    res_ref = refs[2 * n_in]
    o_ref = refs[2 * n_in + 1]
    wb_refs = refs[2 * n_in + 2:]

    @pl.when(pl.program_id(1) == 0)
    def _():
        for w_ref, wb_ref in zip(refs[n_in:2 * n_in], wb_refs):
            wb_ref[...] = w_ref[...].astype(BF16)

    acc = res_ref[...]
    for a_ref, wb_ref in zip(refs[:n_in], wb_refs):
        acc = acc + _dot(a_ref[...], wb_ref[...])
    o_ref[...] = acc


def _mm_res(a_list, w, layer, res):
    m, n = res.shape
    tm = min(m, 512)
    tn = 512
    in_specs, w_specs, scratch, off = [], [], [], 0
    for a in a_list:
        k = a.shape[1]
        in_specs.append(pl.BlockSpec((tm, k), lambda j, i: (i, 0)))
        w_specs.append(pl.BlockSpec((None, k, tn), lambda j, i, _o=off // k: (layer, _o, j)))
        scratch.append(pltpu.VMEM((k, tn), BF16))
        off += k
    return pl.pallas_call(
        functools.partial(_mm_res_body, n_in=len(a_list)),
        grid=(n // tn, m // tm),
        in_specs=in_specs + w_specs + [pl.BlockSpec((tm, tn), lambda j, i: (i, j))],
        out_specs=pl.BlockSpec((tm, tn), lambda j, i: (i, j)),
        out_shape=jax.ShapeDtypeStruct((m, n), F32),
        scratch_shapes=scratch,
        compiler_params=_cparams(("parallel", "arbitrary")),
    )(*a_list, *([w] * len(a_list)), res)


def _fox_prep_body(zs_ref, b_ref, lf_ref, c_ref, carry, *, tc):
    @pl.when(pl.program_id(1) == 0)
    def _():
        carry[...] = jnp.zeros_like(carry)

    lf = _log_sigmoid(zs_ref[0] + b_ref[...])
    c = _dot3_r(_tri(tc, upper=False), lf) + carry[...]
    lf_ref[0] = lf
    c_ref[0] = c
    carry[...] = c[tc - 1:tc, :]


def _fox_prep(zs3, b_pad):
    bsz, t, _ = zs3.shape
    tc = min(t, 256)
    spec = pl.BlockSpec((1, tc, HD), lambda b, i: (b, i, 0))
    return pl.pallas_call(
        functools.partial(_fox_prep_body, tc=tc),
        grid=(bsz, t // tc),
        in_specs=[spec, pl.BlockSpec((1, HD), lambda b, i: (0, 0))],
        out_specs=[spec, spec],
        out_shape=[jax.ShapeDtypeStruct(zs3.shape, F32)] * 2,
        scratch_shapes=[pltpu.VMEM((1, HD), F32)],
        compiler_params=_cparams(("parallel", "arbitrary")),
    )(zs3, b_pad)


def _flash_body(*refs, mode, tq, tk, hpg, off):
    it = iter(refs)
    q_ref, k_ref, v_ref = next(it), next(it), next(it)
    bias_ref = next(it) if mode == "fox" else None
    sel_ref = next(it) if mode == "slc" else None
    o_ref, kb, vb = next(it), next(it), next(it)
    i = pl.program_id(2)

    @pl.when(i == 0)
    def _():
        kb[...] = k_ref[0].astype(BF16)
        vb[...] = v_ref[0].astype(BF16)

    q = q_ref[0]
    if hpg > 1:
        q = jnp.concatenate([q[:, h * HD:(h + 1) * HD] for h in range(hpg)], axis=0)
    qa = (q * ATTN_SCALE).astype(BF16)
    tile_rows = (lambda x: jnp.concatenate([x] * hpg, axis=0)) if hpg > 1 else (lambda x: x)
    q_lo = off + i * tq
    t_k = kb.shape[0]
    if mode == "slc":
        unpicked = (sel_ref[0, 0] - 1.0).astype(BF16)

    def attend(blocks):
        ss = []
        for b, (ks, size, masked) in enumerate(blocks):
            s = _dot_nt(qa, kb[pl.ds(ks, size), :])
            if mode == "fox":
                s = s - bias_ref[0, b:b + 1, :]
            if mode == "slc":
                cb = lax.broadcasted_iota(jnp.int32, (HD, size), 0)
                kk = lax.broadcasted_iota(jnp.int32, (HD, size), 1)
                expand = jnp.where(cb == (ks + kk) // SEL_BLOCK, 2.0 ** 100, 0.0).astype(BF16)
                s = s + tile_rows(_dot(unpicked, expand))
            if masked:
                kidx = ks + lax.broadcasted_iota(jnp.int32, (tq, size), 1)
                qpos = q_lo + lax.broadcasted_iota(jnp.int32, (tq, size), 0)
                valid = kidx <= qpos
                if mode == "win":
                    valid = valid & ((qpos - kidx) < WINDOW)
                s = jnp.where(tile_rows(valid), s, NEG)
            ss.append(s)
        m = ss[0].max(axis=-1, keepdims=True)
        for s in ss[1:]:
            m = jnp.maximum(m, s.max(axis=-1, keepdims=True))
        l, acc = 0.0, 0.0
        for s, (ks, size, _) in zip(ss, blocks):
            p = jnp.exp(s - m)
            l = l + jnp.sum(p, axis=-1, keepdims=True)
            acc = acc + _dot(p.astype(BF16), vb[pl.ds(ks, size), :])
        o = acc / l
        for h in range(hpg):
            o_ref[0, :, h * HD:(h + 1) * HD] = o[h * tq:(h + 1) * tq].astype(o_ref.dtype)

    if mode == "win":
        span = min(t_k, -(-(WINDOW + tq) // HD) * HD + HD)
        start = jnp.clip((q_lo - (WINDOW - 1)) // HD * HD, 0, t_k - span)
        attend([(pl.multiple_of(start, HD), span, True)])
    else:
        n_max = t_k // tk
        need = (q_lo + tq - 1) // tk + 1
        for nb in range(1, n_max + 1):
            @pl.when(need == nb)
            def _(nb=nb):
                attend([(b * tk, tk, b == nb - 1) for b in range(nb)])


def _flash(mode, q_arr, q_col, k_arr, k_col, v_arr, v_col, *, n_kv, hpg, tq, tk, off=0,
           bias=None, sel=None, out_dtype=F32):
    bsz, t_q = q_arr.shape[0], q_arr.shape[1]
    t_k = k_arr.shape[1]
    nq = t_q // tq
    w = hpg * HD
    in_specs = [pl.BlockSpec((1, tq, w), lambda b, g, i: (b, i, q_col // w + g)),
                pl.BlockSpec((1, t_k, HD), lambda b, g, i: (b, 0, k_col // HD + g)),
                pl.BlockSpec((1, t_k, HD), lambda b, g, i: (b, 0, v_col // HD + g))]
    args = [q_arr, k_arr, v_arr]
    if mode == "fox":
        in_specs.append(pl.BlockSpec((1, t_k // tk, tk), lambda b, g, i: (b * n_kv + g, 0, 0)))
        args.append(bias)
    if mode == "slc":
        in_specs.append(pl.BlockSpec((1, 1, tq, HD), lambda b, g, i: (b, g, i, 0)))
        args.append(sel)
    return pl.pallas_call(
        functools.partial(_flash_body, mode=mode, tq=tq, tk=tk, hpg=hpg, off=off),
        grid=(bsz, n_kv, nq),
        in_specs=in_specs,
        out_specs=pl.BlockSpec((1, tq, w), lambda b, g, i: (b, i, g)),
        out_shape=jax.ShapeDtypeStruct((bsz, t_q, n_kv * w), out_dtype),
        scratch_shapes=[pltpu.VMEM((t_k, HD), BF16), pltpu.VMEM((t_k, HD), BF16)],
        compiler_params=_cparams(("parallel", "parallel", "arbitrary")),
    )(*args)


def _compress_body(*refs, paged, pg, n_steps, mn, norms, t_new):
    it = iter(refs)
    if paged:
        next(it)
    n_t = len(norms)
    x_all = [[next(it) for _ in range(pg if paged else G_NSA)] for _ in range(n_t)]
    new_all = [next(it) for _ in range(n_t)] if paged else None
    par_all = [(next(it), next(it), next(it)) for _ in range(n_t)]
    o_all = [next(it) for _ in range(n_t)]
    hcat, tb, perm = next(it), next(it), next(it)
    p = pl.program_id(1)
    mh = mn + 8
    per_page = PAGE // CMP_STRIDE
    n_real = n_steps * pg * per_page
    n_g = G_NSA if paged else 1

    @pl.when(p == 0)
    def _():
        hcat[:, n_real:, :] = jnp.zeros((n_t * G_NSA, mh - n_real, CMP_STRIDE * HD), F32)
        d = lax.broadcasted_iota(jnp.int32, perm.shape, 0)
        s = lax.broadcasted_iota(jnp.int32, perm.shape, 1)
        src = (CMP_STRIDE * (d % per_page) + (d % PAGE) // per_page) * n_g + d // PAGE
        perm[...] = jnp.where(s == src, 1.0, 0.0).astype(BF16)

    for ti in range(n_t):
        for pp in range(pg):
            r0 = pl.multiple_of((p * pg + pp) * per_page, per_page)
            for x_ref, g0 in ([(x_all[ti][pp], 0)] if paged else [(x_all[ti][g], g) for g in range(G_NSA)]):
                rows = _dot(perm[...], x_ref[0].astype(BF16))
                for g in range(n_g):
                    for l in range(CMP_STRIDE):
                        a = g * PAGE + l * per_page
                        hcat[ti * G_NSA + g0 + g, pl.ds(r0, per_page), l * HD:(l + 1) * HD] = rows[a:a + per_page]

    @pl.when(p == n_steps - 1)
    def _():
        for ti in range(n_t):
            pe_ref, w_ref, gain_ref = par_all[ti]
            if paged:
                xn = new_all[ti][0]
                rid = lax.broadcasted_iota(jnp.int32, xn.shape, 0)
                xn = jnp.where(rid < t_new, xn, 0.0)
                for g in range(G_NSA):
                    for l in range(8):
                        hcat[ti * G_NSA + g, n_real:n_real + 1, l * HD:(l + 1) * HD] = xn[l:l + 1, g * HD:(g + 1) * HD]
            w = w_ref[...]
            pe2 = _dot(pe_ref[...].astype(BF16), w)
            pe_bias = pe2[0:1, 0:HD] + pe2[1:2, HD:2 * HD]
            for g in range(G_NSA):
                tb[...] = _dot(hcat[ti * G_NSA + g].astype(BF16), w)
                y = tb[0:mn, 0:HD] + tb[1:mn + 1, HD:2 * HD] + pe_bias
                if norms[ti]:
                    y = y * lax.rsqrt(jnp.mean(y * y, axis=-1, keepdims=True) + EPS) * gain_ref[...]
                o_all[ti][0, :, g * HD:(g + 1) * HD] = y


def _compress(x_arr, x_cols, params, norms, *, mn, page_table=None, pools=None, pool_off=0, t_new=0, pg=32):
    paged = page_table is not None
    bsz = x_arr.shape[0]
    n_t = len(norms)
    mh = mn + 8
    w2 = G_NSA * HD
    const = lambda shape: pl.BlockSpec(shape, (lambda b, p, *_: (0,) * len(shape)))
    tail = [const((8, CMP_STRIDE * HD)), const((CMP_STRIDE * HD, 2 * HD)), const((1, HD))] * n_t
    tail_args = [a for par in params for a in par]
    out_specs = [pl.BlockSpec((1, mn, w2), lambda b, p, *_: (b, 0, 0))] * n_t
    n_perm = G_NSA * PAGE if paged else PAGE
    scratch = [pltpu.VMEM((n_t * G_NSA, mh, CMP_STRIDE * HD), F32), pltpu.VMEM((mh, 2 * HD), F32),
               pltpu.VMEM((n_perm, n_perm), BF16)]
    out_shape = [jax.ShapeDtypeStruct((bsz, mn, w2), F32)] * n_t
    if not paged:
        n_steps = x_arr.shape[1] // PAGE
        body = functools.partial(_compress_body, paged=False, pg=1, n_steps=n_steps, mn=mn, norms=norms, t_new=0)
        return pl.pallas_call(
            body, grid=(bsz, n_steps),
            in_specs=[pl.BlockSpec((1, PAGE, HD), lambda b, p, _c=c // HD + g: (b, p, _c))
                      for c in x_cols for g in range(G_NSA)] + tail,
            out_specs=out_specs, out_shape=out_shape, scratch_shapes=scratch,
            compiler_params=_cparams(("parallel", "arbitrary")),
        )(*([x_arr] * (n_t * G_NSA)), *tail_args)
    pg = min(pg, page_table.shape[1])
    n_steps = page_table.shape[1] // pg
    body = functools.partial(_compress_body, paged=True, pg=pg, n_steps=n_steps, mn=mn, norms=norms, t_new=t_new)
    grid_spec = pltpu.PrefetchScalarGridSpec(
        num_scalar_prefetch=1, grid=(bsz, n_steps),
        in_specs=[pl.BlockSpec((1, G_NSA * PAGE, HD), lambda b, p, pt, _pp=pp: (pool_off + pt[b, p * pg + _pp], 0, 0))
                  for _ in range(n_t) for pp in range(pg)]
        + [pl.BlockSpec((1, 8, w2), lambda b, p, pt, _c=c // w2: (b, 0, _c)) for c in x_cols] + tail,
        out_specs=out_specs, scratch_shapes=scratch)
    return pl.pallas_call(body, grid_spec=grid_spec, out_shape=out_shape,
                          compiler_params=_cparams(("parallel", "arbitrary")))(
        page_table, *[pool for pool in pools for _ in range(pg)], *([x_arr] * n_t), *tail_args)


def _cmpsel_body(q_ref, kc_ref, vc_ref, o_ref, sel_ref, *, tq, ncp, nsp, n_sel, pos0, gs):
    i = pl.program_id(2)
    q = q_ref[0]
    pos = pos0 + i * tq + lax.broadcasted_iota(jnp.int32, (tq, ncp), 0)
    col = lax.broadcasted_iota(jnp.int32, (tq, ncp), 1)
    valid = (col * CMP_STRIDE + (CMP_BLOCK - 1)) <= pos
    validf = jnp.where(valid, 1.0, 0.0)
    imps = []
    for g in range(gs):
        kc = kc_ref[0, :, g * HD:(g + 1) * HD].astype(BF16)
        vc = vc_ref[0, :, g * HD:(g + 1) * HD].astype(BF16)
        imp = jnp.zeros((tq, ncp), F32)
        for h in range(g * HPG, (g + 1) * HPG):
            qh = (q[:, h * HD:(h + 1) * HD] * ATTN_SCALE).astype(BF16)
            s = jnp.where(valid, _dot_nt(qh, kc), NEG)
            e = jnp.exp(s - jnp.max(s, axis=-1, keepdims=True)) * validf
            p = e / jnp.maximum(jnp.sum(e, axis=-1, keepdims=True), 1e-30)
            o_ref[0, :, h * HD:(h + 1) * HD] = _dot(p.astype(BF16), vc)
            imp = imp + p
        imps.append(imp)
    imp = jnp.concatenate(imps, axis=0)
    cc = lax.broadcasted_iota(jnp.int32, (ncp, nsp), 0)
    jj = lax.broadcasted_iota(jnp.int32, (ncp, nsp), 1)
    r = SEL_BLOCK // CMP_STRIDE
    gather = jnp.where((cc >= r * jj - 1) & (cc <= r * jj + r - 1), 1.0, 0.0).astype(BF16)
    score = _dot3_l(imp, gather)
    rows = gs * tq
    blk = lax.broadcasted_iota(jnp.int32, (rows, nsp), 1)
    pq = pos0 + i * tq + lax.broadcasted_iota(jnp.int32, (rows, nsp), 0) % tq
    forced = (blk == pq // SEL_BLOCK) | (blk == 0)
    future = blk * SEL_BLOCK > pq
    score = jnp.where(forced, FORCE_SCORE, jnp.where(future, -1.0, score))
    score = jnp.where(blk < n_sel, score, -2.0)
    if rows % HD == 0:
        n8 = -(-n_sel // 8) * 8
        st = score.T[:n8]
        bt = lax.broadcasted_iota(jnp.int32, (n8, rows), 0)
        rank = jnp.zeros((n8, rows), F32)
        for c in range(n_sel):
            sc = st[c:c + 1, :]
            ahead = (sc > st) | ((sc == st) & (bt > c))
            rank = rank + jnp.where(ahead, 1.0, 0.0)
        keep = jnp.where(rank < float(N_SELECT), 1.0, 0.0)
        sel = jnp.concatenate([keep, jnp.zeros((nsp - n8, rows), F32)], axis=0).T
    else:
        rank = jnp.zeros((rows, nsp), F32)
        for c in range(n_sel):
            sc = score[:, c:c + 1]
            ahead = (sc > score) | ((sc == score) & (blk > c))
            rank = rank + jnp.where(ahead, 1.0, 0.0)
        sel = jnp.where(rank < float(N_SELECT), 1.0, 0.0)
    for g in range(gs):
        sel_ref[0, g] = sel[g * tq:(g + 1) * tq]


def _cmpsel(z3, kcmp, vcmp, *, tq, pos0, n_sel, gs):
    bsz, t, _ = z3.shape
    ncp = kcmp.shape[1]
    nsp = -(-n_sel // HD) * HD
    w = gs * HPG * HD
    return pl.pallas_call(
        functools.partial(_cmpsel_body, tq=tq, ncp=ncp, nsp=nsp, n_sel=n_sel, pos0=pos0, gs=gs),
        grid=(bsz, G_NSA // gs, t // tq),
        in_specs=[pl.BlockSpec((1, tq, w), lambda b, g, i: (b, i, NQ // w + g)),
                  pl.BlockSpec((1, ncp, gs * HD), lambda b, g, i: (b, 0, g)),
                  pl.BlockSpec((1, ncp, gs * HD), lambda b, g, i: (b, 0, g))],
        out_specs=[pl.BlockSpec((1, tq, w), lambda b, g, i: (b, i, g)),
                   pl.BlockSpec((1, gs, tq, nsp), lambda b, g, i: (b, g, i, 0))],
        out_shape=[jax.ShapeDtypeStruct((bsz, t, H_NSA * HD), F32),
                   jax.ShapeDtypeStruct((bsz, G_NSA, t, nsp), F32)],
        compiler_params=_cparams(("parallel", "parallel", "parallel")),
    )(z3, kcmp, vcmp)


def _combine_body(oc_ref, os_ref, ow_ref, zs_ref, gb_ref, o_ref):
    gates = _sigmoid(zs_ref[...] + gb_ref[...])
    for h in range(H_NSA):
        sl = slice(h * HD, (h + 1) * HD)
        c0 = H_FOX + 3 * h
        o = (gates[:, c0:c0 + 1] * oc_ref[:, sl] + gates[:, c0 + 1:c0 + 2] * os_ref[:, sl]
             + gates[:, c0 + 2:c0 + 3] * ow_ref[:, sl])
        o_ref[:, sl] = o.astype(BF16)


def _nsa_combine(oc, osl, ow, zs, gb_pad):
    m, n = oc.shape
    tm = min(m, 512)
    big = pl.BlockSpec((tm, n), lambda i: (i, 0))
    return pl.pallas_call(
        _combine_body, grid=(m // tm,),
        in_specs=[big, big, big, pl.BlockSpec((tm, HD), lambda i: (i, 0)), pl.BlockSpec((1, HD), lambda i: (0, 0))],
        out_specs=big, out_shape=jax.ShapeDtypeStruct((m, n), BF16),
        compiler_params=_cparams(("parallel",)),
    )(oc, osl, ow, zs, gb_pad)


def _paged_body(*refs, mode, pg, n_steps, n_w, hpg_kv, t_new):
    it = iter(refs)
    next(it)
    q_ref = next(it)
    k_refs = [next(it) for _ in range(pg)]
    v_refs = [next(it) for _ in range(pg)]
    cl_refs = [next(it) for _ in range(pg)] if mode == "fox" else None
    tf_refs = [next(it) for _ in range(pg)] if mode == "fox" else None
    kn_ref, vn_ref = next(it), next(it)
    lfn_ref = next(it) if mode == "fox" else None
    sel_ref, seln_ref = (next(it), next(it)) if mode == "slc" else (None, None)
    o_ref = next(it)
    qb, m_scr, l_scr, acc, kn_scr, vn_scr, carry = (next(it) for _ in range(7))
    p = pl.program_id(1)
    rows = 4 * 8
    r2 = n_w * PAGE
    n_new = t_new * n_w

    def kv_match(ncols):
        row = lax.broadcasted_iota(jnp.int32, (rows, ncols), 0)
        col = lax.broadcasted_iota(jnp.int32, (rows, ncols), 1)
        return row, col, (col % n_w) == (row % 8) // hpg_kv

    @pl.when(p == 0)
    def _():
        qb[...] = (q_ref[0] * ATTN_SCALE).astype(BF16)
        m_scr[...] = jnp.full(m_scr.shape, NEG, F32)
        l_scr[...] = jnp.zeros(l_scr.shape, F32)
        acc[...] = jnp.zeros(acc.shape, F32)
        carry[...] = jnp.zeros(carry.shape, F32)
        kn_scr[...] = jnp.zeros(kn_scr.shape, BF16)
        vn_scr[...] = jnp.zeros(vn_scr.shape, BF16)

    def picked(choice, first_lane, ncols):
        half = (lax.broadcasted_iota(jnp.int32, (rows, ncols), 1) // n_w) // SEL_BLOCK
        out = None
        for j in range(PAGE // SEL_BLOCK):
            term = (half == j) & (choice[:, first_lane + j:first_lane + j + 1] > 0.5)
            out = term if out is None else (out | term)
        return out

    def update(s_list, vb_list):
        m_old = m_scr[...]
        m_new = m_old
        for s in s_list:
            m_new = jnp.maximum(m_new, jnp.max(s, axis=-1, keepdims=True))
        a = jnp.exp(m_old - m_new)
        l = a * l_scr[...]
        o = a * acc[...]
        for s, vb in zip(s_list, vb_list):
            pr = jnp.exp(s - m_new)
            l = l + jnp.sum(pr, axis=-1, keepdims=True)
            o = o + _dot(pr.astype(BF16), vb)
        m_scr[...] = m_new
        l_scr[...] = l
        acc[...] = o

    _, _, match = kv_match(r2)
    s_list, vb_list = [], []
    c_run = carry[...] if mode == "fox" else None
    for pp in range(pg):
        s = _dot_nt(qb[...], k_refs[pp][0].astype(BF16))
        valid = match
        if mode == "fox":
            s = s - (c_run + cl_refs[pp][0])
            c_run = c_run + tf_refs[pp][0]
        if mode == "slc":
            valid = valid & picked(sel_ref[0, 0], pp * (PAGE // SEL_BLOCK), r2)
        s_list.append(jnp.where(valid, s, NEG))
        vb_list.append(v_refs[pp][0].astype(BF16))
    if mode == "fox":
        carry[...] = c_run
    update(s_list, vb_list)

    @pl.when(p == n_steps - 1)
    def _():
        kn_scr[0:n_new, :] = kn_ref[0].astype(BF16)
        vn_scr[0:n_new, :] = vn_ref[0].astype(BF16)
        s = _dot_nt(qb[...], kn_scr[...])
        row, col, valid = kv_match(HD)
        valid = valid & (col // n_w <= row // 8) & (col < n_new)
        if mode == "fox":
            rr = lax.broadcasted_iota(jnp.int32, (HD, HD), 0)
            cc = lax.broadcasted_iota(jnp.int32, (HD, HD), 1)
            pre = jnp.where((rr % n_w == cc % n_w) & (rr // n_w <= cc // n_w), 1.0, 0.0).astype(BF16)
            c_new = _dot3_l(jnp.broadcast_to(lfn_ref[0], (8, HD)), pre)[0:1, :] + carry[:, 0:HD]
            s = s - c_new
        if mode == "slc":
            valid = valid & picked(seln_ref[0], 0, HD)
        update([jnp.where(valid, s, NEG)], [vn_scr[...]])
        o_ref[0] = acc[...] / l_scr[...]


def _paged_attn(mode, q_rows, page_table, k_pool, v_pool, pool_off, k_new, v_new, *, hpg_kv, pg,
                c_local=None, c_total=None, lf_new=None, sel_rows=None):
    bsz = q_rows.shape[0]
    n_pages = page_table.shape[1]
    pg = min(pg, n_pages)
    n_steps = n_pages // pg
    r2 = k_pool.shape[1]
    n_w = r2 // PAGE
    n_new = k_new.shape[1]
    page = lambda pp: (lambda b, p, pt: (pool_off + pt[b, p * pg + pp], 0, 0))
    per_b = lambda b, p, pt: (b, 0, 0)
    in_specs = [pl.BlockSpec((1, 32, HD), per_b)]
    in_specs += [pl.BlockSpec((1, r2, HD), page(pp)) for pp in range(pg)] * 2
    args = [q_rows] + [k_pool] * pg + [v_pool] * pg
    if mode == "fox":
        in_specs += [pl.BlockSpec((1, 1, r2), page(pp)) for pp in range(pg)] * 2
        args += [c_local] * pg + [c_total] * pg
    in_specs += [pl.BlockSpec((1, n_new, HD), per_b)] * 2
    args += [k_new, v_new]
    if mode == "fox":
        in_specs.append(pl.BlockSpec((1, 1, HD), per_b))
        args.append(lf_new)
    if mode == "slc":
        per_page = PAGE // SEL_BLOCK
        past = sel_rows[:, :, :n_pages * per_page].reshape(bsz, 32, n_steps, pg * per_page)
        sel_steps = _pad_cols(jnp.swapaxes(past, 1, 2).reshape(bsz * n_steps * 32, pg * per_page), HD)
        in_specs += [pl.BlockSpec((1, 1, 32, HD), lambda b, p, pt: (b, p, 0, 0)), pl.BlockSpec((1, 32, HD), per_b)]
        args += [sel_steps.reshape(bsz, n_steps, 32, HD),
                 _pad_cols(sel_rows[:, :, n_pages * per_page:n_pages * per_page + 1].reshape(bsz * 32, 1), HD)
                 .reshape(bsz, 32, HD)]
    grid_spec = pltpu.PrefetchScalarGridSpec(
        num_scalar_prefetch=1, grid=(bsz, n_steps), in_specs=in_specs,
        out_specs=pl.BlockSpec((1, 32, HD), per_b),
        scratch_shapes=[pltpu.VMEM((32, HD), BF16), pltpu.VMEM((32, 1), F32), pltpu.VMEM((32, 1), F32),
                        pltpu.VMEM((32, HD), F32), pltpu.VMEM((PAGE, HD), BF16), pltpu.VMEM((PAGE, HD), BF16),
                        pltpu.VMEM((1, r2), F32)])
    return pl.pallas_call(
        functools.partial(_paged_body, mode=mode, pg=pg, n_steps=n_steps, n_w=n_w, hpg_kv=hpg_kv, t_new=n_new // n_w),
        grid_spec=grid_spec, out_shape=jax.ShapeDtypeStruct((bsz, 32, HD), F32),
        compiler_params=_cparams(("parallel", "arbitrary")),
    )(page_table, *args)


def _pool_prefix_body(x_ref, cl_ref, ct_ref, w_scr, *, n_w):
    r2 = x_ref.shape[1]

    @pl.when(pl.program_id(0) == 0)
    def _():
        r = lax.broadcasted_iota(jnp.int32, (r2, 2 * r2), 0)
        c = lax.broadcasted_iota(jnp.int32, (r2, 2 * r2), 1)
        same = (r % n_w) == (c % n_w)
        local = same & (r // n_w <= c // n_w) & (c < r2)
        w_scr[...] = jnp.where(local | (same & (c >= r2)), 1.0, 0.0).astype(BF16)

    y = _dot3_l(x_ref[...], w_scr[...])
    cl_ref[...] = y[:, :r2]
    ct_ref[...] = y[:, r2:]


def _pool_prefix(x, n_w):
    n, r2 = x.shape
    tm = next((c for c in (256, 128, 64, 32, 16, 8) if n % c == 0), n)
    spec = pl.BlockSpec((tm, r2), lambda i: (i, 0))
    return pl.pallas_call(
        functools.partial(_pool_prefix_body, n_w=n_w), grid=(n // tm,),
        in_specs=[spec], out_specs=[spec, spec],
        out_shape=[jax.ShapeDtypeStruct((n, r2), F32)] * 2,
        scratch_shapes=[pltpu.VMEM((r2, 2 * r2), BF16)],
        compiler_params=_cparams(("arbitrary",)),
    )(x)


def _gla_body(*refs, mode, t_in, chunk, dv, t_real, has_state, hb):
    it = iter(refs)
    q_ref, k_ref, v_ref, og_ref = next(it), next(it), next(it), next(it)
    if mode == "gla":
        zs_ref, wa_ref, ba_ref = next(it), next(it), next(it)
    else:
        lb_ref = next(it)
    gain_ref = next(it)
    s0_ref = next(it) if has_state else None
    o_ref, s_ref, st_scr, q_scr, k_scr, g_scr, o_scr = (next(it) for _ in range(7))
    tb = q_scr.shape[1]
    n_chunks = tb // chunk
    pad = tb - t_in
    tri = lax.broadcasted_iota(jnp.int32, (chunk, chunk), 0) >= lax.broadcasted_iota(jnp.int32, (chunk, chunk), 1)
    if mode == "hg":
        lbr = lb_ref[...]
        e = jnp.exp(lbr - jnp.max(lbr, axis=0, keepdims=True))
        lb_all = (e / jnp.sum(e, axis=0, keepdims=True))[0:1, :]
    ti = pl.program_id(2)

    @pl.when(ti == 0)
    def _():
        for h in range(hb):
            st_scr[h] = s0_ref[0, h].T if has_state else jnp.zeros((dv, HD), F32)

    def block(ref, sl):
        x = ref[0, :, sl]
        if pad:
            x = jnp.concatenate([x, jnp.zeros((pad, x.shape[1]), F32)], axis=0)
        return x

    live = (ti * t_in + lax.broadcasted_iota(jnp.int32, (tb, HD), 0)) < t_real
    if mode == "gla":
        ga = block(zs_ref, slice(0, HD)).astype(BF16)
    for h in range(hb):
        ks = slice(h * HD, (h + 1) * HD)
        qr, kr = block(q_ref, ks), block(k_ref, ks)
        if mode == "gla":
            g = _log_sigmoid(_dot(ga, wa_ref[:, ks]) + ba_ref[:, ks]) / GLA_GATE_NORM
            q, k = qr, kr
        else:
            lb = lb_all[:, ks]
            f = lb + (1.0 - lb) * _sigmoid(kr)
            q, k, g = qr * _sigmoid(qr), 1.0 - f, jnp.log(f)
        q_scr[h] = q
        k_scr[h] = jnp.where(live, k, 0.0)
        g_scr[h] = jnp.where(live, g, 0.0)

    def body(c, _):
        r0 = pl.multiple_of(c * chunk, chunk)
        rows = pl.ds(r0, chunk)
        for h in range(hb):
            vs = slice(h * dv, (h + 1) * dv)
            q, k, g = q_scr[h, rows, :], k_scr[h, rows, :], g_scr[h, rows, :]
            if pad:
                v = block(v_ref, vs)
            else:
                v = v_ref[0, rows, vs]
            b = _cumsum_rows(g)
            bm = b[chunk // 2 - 1:chunk // 2, :]
            bl = b[chunk - 1:chunk, :]
            qe = (q * jnp.exp(jnp.minimum(b - bm, EXP_CLAMP))).astype(BF16)
            ke = (k * jnp.exp(jnp.minimum(bm - b, EXP_CLAMP))).astype(BF16)
            a = jnp.where(tri, _dot_nt(qe, ke), 0.0)
            vb = v.astype(BF16)
            st = st_scr[h]
            o_scr[h, rows, :] = _dot(a.astype(BF16), vb) + _dot_nt((q * jnp.exp(b)).astype(BF16), st.astype(BF16))
            kd = (k * jnp.exp(bl - b)).astype(BF16)
            st_scr[h] = st * jnp.exp(bl) + _dot_tn(vb, kd)
        return 0

    lax.fori_loop(0, n_chunks, body, 0)

    for h in range(hb):
        vs = slice(h * dv, (h + 1) * dv)
        o = o_scr[h, 0:t_in, :]
        og = og_ref[0, :, vs]
        o = o * lax.rsqrt(jnp.mean(o * o, axis=-1, keepdims=True) + EPS) * gain_ref[...]
        o_ref[0, :, vs] = (o * (og * _sigmoid(og) if mode == "gla" else _sigmoid(og))).astype(BF16)

    @pl.when(ti == pl.num_programs(2) - 1)
    def _():
        for h in range(hb):
            s_ref[0, h] = st_scr[h].T


def _gla(mode, z3, zs3, cols, n_heads, dv, extra, gain, s0, *, t_real, chunk, hb):
    bsz, t_all, _ = z3.shape
    tc = min(t_all, 512)
    qc, kc, vc, oc = cols
    has_state = s0 is not None
    wk, wv = hb * HD, hb * dv
    in_specs = [pl.BlockSpec((1, tc, wk), lambda b, h, t: (b, t, qc // wk + h)),
                pl.BlockSpec((1, tc, wk), lambda b, h, t: (b, t, kc // wk + h)),
                pl.BlockSpec((1, tc, wv), lambda b, h, t: (b, t, vc // wv + h)),
                pl.BlockSpec((1, tc, wv), lambda b, h, t: (b, t, oc // wv + h))]
    args = [z3, z3, z3, z3]
    if mode == "gla":
        wa, ba = extra
        in_specs += [pl.BlockSpec((1, tc, HD), lambda b, h, t: (b, t, 0)),
                     pl.BlockSpec((HD, wk), lambda b, h, t: (0, h)),
                     pl.BlockSpec((1, wk), lambda b, h, t: (0, h))]
        args += [zs3, wa, ba]
    else:
        in_specs.append(pl.BlockSpec((8, wk), lambda b, h, t: (0, h)))
        args.append(extra)
    in_specs.append(pl.BlockSpec((1, dv), lambda b, h, t: (0, 0)))
    args.append(gain)
    st_spec = pl.BlockSpec((1, hb, HD, dv), lambda b, h, t: (b, h, 0, 0))
    if has_state:
        in_specs.append(st_spec)
        args.append(s0)
    return pl.pallas_call(
        functools.partial(_gla_body, mode=mode, t_in=tc, chunk=chunk, dv=dv, t_real=t_real, has_state=has_state,
                          hb=hb),
        grid=(bsz, n_heads // hb, t_all // tc),
        in_specs=in_specs,
        out_specs=[pl.BlockSpec((1, tc, wv), lambda b, h, t: (b, t, h)), st_spec],
        out_shape=[jax.ShapeDtypeStruct((bsz, t_all, n_heads * dv), BF16),
                   jax.ShapeDtypeStruct((bsz, n_heads, HD, dv), F32)],
        scratch_shapes=[pltpu.VMEM((hb, dv, HD), F32)] + [pltpu.VMEM((hb, max(tc, chunk), HD), F32)] * 3
        + [pltpu.VMEM((hb, max(tc, chunk), dv), F32)],
        compiler_params=_cparams(("parallel", "parallel", "arbitrary")),
    )(*args)


def _split_cols(w, sizes):
    outs, off = [], 0
    for s in sizes:
        outs.append(w[:, off:off + s])
        off += s
    return outs


def _pad_cols(w, n):
    return jnp.pad(w, ((0, 0), (0, n - w.shape[1])))


def _row(v):
    return v.reshape(1, -1).astype(F32)


def _even_weights(w_in, fq_gain, fk_gain, nq_gain, nk_gain):
    fq, fk, fv, ff, nq, kc, vc, ks, vs, kw, vw, ng = _split_cols(w_in, EVEN_SIZES)
    big = jnp.concatenate([fq, fk, fv, nq, kc, vc, ks, vs, kw, vw], axis=1).astype(BF16)
    small = _pad_cols(jnp.concatenate([ff, ng], axis=1), HD).astype(BF16)
    ones = lambda n: jnp.ones((n,), F32)
    gain = jnp.concatenate([jnp.tile(fq_gain, H_FOX), jnp.tile(fk_gain, H_FOX), ones(H_FOX * HD),
                            jnp.tile(nq_gain, H_NSA), ones(2 * G_NSA * HD), jnp.tile(nk_gain[1], G_NSA),
                            ones(G_NSA * HD), jnp.tile(nk_gain[2], G_NSA), ones(G_NSA * HD)])
    z, o = jnp.zeros, jnp.ones
    flag = jnp.concatenate([o((2 * H_FOX * HD,), F32), z((H_FOX * HD,), F32), o((H_NSA * HD,), F32),
                            z((2 * G_NSA * HD,), F32), o((G_NSA * HD,), F32), z((G_NSA * HD,), F32),
                            o((G_NSA * HD,), F32), z((G_NSA * HD,), F32)])
    return big, small, _row(gain), _row(flag)


def _odd_weights(w_in):
    gq, gk, gv, ga, gg, hq, hf, hi, hg = _split_cols(w_in, ODD_SIZES)
    big = jnp.concatenate([gq, gk, gv, gg, hq, hf, hi, hg], axis=1).astype(BF16)
    small = _pad_cols(ga, HD).astype(BF16)
    gain = jnp.concatenate([jnp.full((H_GLA * DK_GLA,), DK_GLA ** -0.5, F32), jnp.ones((N_BIG_ODD - H_GLA * DK_GLA,), F32)])
    return big, small, _row(gain), jnp.zeros((1, N_BIG_ODD), F32)


def _cmp_weights(pe, w):
    half = CMP_STRIDE * HD
    wcat = jnp.concatenate([w[:half], w[half:]], axis=1).astype(BF16)
    pe8 = jnp.pad(pe.reshape(2, half), ((0, 6), (0, 0)))
    return pe8, wcat


def _ffn(x, g, w1, w3, w2, layer):
    return _mm_res([_ffn_up(x, g, w1, w3, layer)], w2, layer, x)


def _even_layer(x, bsz, t, t_real, ew, past):
    (w_big, w_small, cgain, cflag, g_mix, w_out, b_f_pad, gb_pad, pe_k, wc_k, pe_v, wc_v, nk0) = ew
    z, zs = _proj(x, g_mix, w_big, w_small, cgain, cflag, True)
    z3 = z.reshape(bsz, t, N_BIG_EVEN)
    zs3 = zs.reshape(bsz, t, HD)
    cmp_params = ((pe_k, wc_k, nk0), (pe_v, wc_v, jnp.ones((1, HD), F32)))
    if past is None:
        lf, c = _fox_prep(zs3, b_f_pad)
        logf_new = lf[:, :, :H_FOX]
        tk = min(t, 512)
        bias = jnp.swapaxes(c[:, :, :H_FOX], 1, 2).reshape(bsz * H_FOX, t // tk, tk)
        o_fox = _flash("fox", z3, FQ, z3, FK, z3, FV, n_kv=H_FOX, hpg=1, tq=min(t, 512), tk=tk, bias=bias,
                       out_dtype=BF16)
        n_cmp = t // CMP_STRIDE
        mn = -(-n_cmp // HD) * HD
        kcmp, vcmp = _compress(z3, (KC, VC), cmp_params, (True, False), mn=mn)
        tqn = min(t, 1024 // HPG)
        o_cmp, sel = _cmpsel(z3, kcmp, vcmp, tq=min(t, 256), pos0=0, n_sel=-(-t // SEL_BLOCK), gs=1)
        o_slc = _flash("slc", z3, NQ, z3, KS, z3, VS, n_kv=G_NSA, hpg=HPG, tq=tqn, tk=tk, sel=sel)
        o_win = _flash("win", z3, NQ, z3, KW, z3, VW, n_kv=G_NSA, hpg=HPG, tq=tqn, tk=tk)
        w_buf = WINDOW
        kw_new = jnp.concatenate([jnp.zeros((bsz, w_buf, G_NSA * HD), F32), z3[:, :, KW:KW + G_NSA * HD]], axis=1)[:, -w_buf:]
        vw_new = jnp.concatenate([jnp.zeros((bsz, w_buf, G_NSA * HD), F32), z3[:, :, VW:VW + G_NSA * HD]], axis=1)[:, -w_buf:]
        o_fox = o_fox.reshape(bsz * t, H_FOX * HD)
    else:
        (page_table, pool_off, fk_pool, fv_pool, lf_pool, kc_pool, vc_pool, ks_pool, vs_pool, buf_k, buf_v) = past
        p_len = page_table.shape[1] * PAGE
        lf_all, _ = _fox_prep(zs.reshape(1, bsz * t, HD), b_f_pad)
        lf_new = lf_all.reshape(bsz, t, HD)[:, :t_real, :H_FOX]
        logf_new = lf_new
        lfn_pad = _pad_cols(lf_new.reshape(bsz, t_real * H_FOX), HD).reshape(bsz, 1, HD)
        c_local, c_total = _pool_prefix(lf_pool, H_FOX)
        rows_of = lambda c0, n: z3[:, :t_real, c0:c0 + n * HD].reshape(bsz, t_real * n, HD)
        o_fox = _paged_attn("fox", rows_of(FQ, H_FOX), page_table, fk_pool, fv_pool, pool_off,
                            rows_of(FK, H_FOX), rows_of(FV, H_FOX), hpg_kv=1, pg=16,
                            c_local=c_local[:, None, :], c_total=c_total[:, None, :], lf_new=lfn_pad)
        o_fox = o_fox.reshape(bsz, t_real, H_FOX * HD)
        n_cmp = -(-(p_len + t_real) // CMP_STRIDE)
        mn = -(-n_cmp // HD) * HD
        kcmp, vcmp = _compress(z3, (KC, VC), cmp_params, (True, False), mn=mn, page_table=page_table,
                               pools=(kc_pool, vc_pool), pool_off=pool_off, t_new=t_real)
        n_sel = -(-(p_len + t_real) // SEL_BLOCK)
        o_cmp, sel = _cmpsel(z3, kcmp, vcmp, tq=t, pos0=p_len, n_sel=n_sel, gs=G_NSA)
        nsp = sel.shape[-1]
        sel_rows = jnp.broadcast_to(jnp.swapaxes(sel[:, :, :t_real], 1, 2)[:, :, :, None, :],
                                    (bsz, t_real, G_NSA, HPG, nsp)).reshape(bsz, 32, nsp)
        o_slc = _paged_attn("slc", rows_of(NQ, H_NSA), page_table, ks_pool, vs_pool, pool_off,
                            rows_of(KS, G_NSA), rows_of(VS, G_NSA), hpg_kv=HPG, pg=32,
                            sel_rows=sel_rows).reshape(bsz, t_real, H_NSA * HD)
        w_buf = buf_k.shape[1]
        lk = -(-(w_buf + t) // HD) * HD
        fill = jnp.zeros((bsz, lk - w_buf - t_real, G_NSA * HD), F32)
        kw_all = jnp.concatenate([buf_k, z3[:, :t_real, KW:KW + G_NSA * HD], fill], axis=1)
        vw_all = jnp.concatenate([buf_v, z3[:, :t_real, VW:VW + G_NSA * HD], fill], axis=1)
        kw_new, vw_new = kw_all[:, t_real:t_real + w_buf], vw_all[:, t_real:t_real + w_buf]
        o_win = _flash("win", z3, NQ, kw_all, 0, vw_all, 0, n_kv=G_NSA, hpg=HPG, tq=t, tk=HD, off=w_buf)
        padt = lambda a: jnp.pad(a, ((0, 0), (0, t - t_real), (0, 0)))
        o_fox = padt(o_fox).astype(BF16).reshape(bsz * t, H_FOX * HD)
        o_slc = padt(o_slc)
    m = bsz * t
    o_nsa = _nsa_combine(o_cmp.reshape(m, -1), o_slc.reshape(m, -1), o_win.reshape(m, -1), zs, gb_pad)
    x = _mm_res([o_fox, o_nsa], w_out, 0, x)
    zr = z3[:, :t_real]
    grp = lambda c0: zr[:, :, c0:c0 + G_NSA * HD].reshape(1, bsz, t_real, G_NSA, HD)
    fox = lambda c0: zr[:, :, c0:c0 + H_FOX * HD].reshape(1, bsz, t_real, H_FOX, HD)
    w_rows = kw_new.shape[1]
    outs = (fox(FK), fox(FV), logf_new[None], grp(KC), grp(VC), grp(KS), grp(VS),
            kw_new.reshape(1, bsz, w_rows, G_NSA, HD), vw_new.reshape(1, bsz, w_rows, G_NSA, HD))
    return x, outs


def _odd_layer(x, bsz, t, t_real, ow, state):
    (w_big, w_small, cgain, cflag, g_mix, w_out, wa_pad, ba, gla_gain, lb_pad, hg_gain) = ow
    z, zs = _proj(x, g_mix, w_big, w_small, cgain, cflag, False)
    z3 = z.reshape(bsz, t, N_BIG_ODD)
    zs3 = zs.reshape(bsz, t, HD)
    s_gla, s_hg = state if state is not None else (None, None)
    chunk = 64 if t >= 64 else 128
    o_gla, s_gla = _gla("gla", z3, zs3, (GQ, GK, GV, GG), H_GLA, DV_GLA, (wa_pad, ba), gla_gain, s_gla,
                        t_real=t_real, chunk=chunk, hb=H_GLA)
    o_hg, s_hg = _gla("hg", z3, zs3, (HQ, HF, HI, HO), H_HG, DV_HG, lb_pad, hg_gain, s_hg,
                      t_real=t_real, chunk=chunk, hb=H_HG)
    m = bsz * t
    x = _mm_res([o_gla.reshape(m, -1), o_hg.reshape(m, -1)], w_out, 0, x)
    return x, (s_gla[None], s_hg[None])


def kernel(x_prompt, x_sample, cache_fox_k, cache_fox_v, cache_fox_logf, cache_nsa_kc, cache_nsa_vc, cache_nsa_ks, cache_nsa_vs, state_nsa_kw, state_nsa_vw, state_gla, state_hgrn, page_table, norm_mix, norm_ffn, w_in_even, w_out_even, fox_b_f, fox_q_gain, fox_k_gain, nsa_q_gain, nsa_k_gain, nsa_cmp_pe_k, nsa_cmp_pe_v, nsa_cmp_wk, nsa_cmp_wv, nsa_gate_b, w_in_odd, w_out_odd, gla_wa2, gla_ba, gla_norm, hgrn_lb, hgrn_norm, ffn_w1, ffn_w3, ffn_w2):
    bp, tp, d = x_prompt.shape
    bs, ts, _ = x_sample.shape
    n_pool = cache_fox_k.shape[1]
    ts_pad = 8
    pe_k, wc_k = _cmp_weights(nsa_cmp_pe_k[0], nsa_cmp_wk[0])
    pe_v, wc_v = _cmp_weights(nsa_cmp_pe_v[0], nsa_cmp_wv[0])
    b_f_pad = _pad_cols(_row(fox_b_f[0]), HD)
    gb_pad = _pad_cols(jnp.concatenate([jnp.zeros((1, H_FOX), F32), _row(nsa_gate_b[0])], axis=1), HD)
    ew = _even_weights(w_in_even[0], fox_q_gain[0], fox_k_gain[0], nsa_q_gain[0], nsa_k_gain[0]) + (
        _row(norm_mix[0]), w_out_even, b_f_pad, gb_pad, pe_k, wc_k, pe_v, wc_v, _row(nsa_k_gain[0, 0]))
    wa_pad = jnp.pad(gla_wa2[0], ((0, HD - GLA_RANK), (0, 0))).astype(BF16)
    lb_pad = jnp.pad(hgrn_lb.astype(F32), ((0, 8 - hgrn_lb.shape[0]), (0, 0)), constant_values=-1e30)
    ow = _odd_weights(w_in_odd[0]) + (_row(norm_mix[1]), w_out_odd, wa_pad, _row(gla_ba[0]),
                                      _row(gla_norm[0]), lb_pad, _row(hgrn_norm[0]))
    ffn = [(_row(norm_ffn[i]), ffn_w1, ffn_w3, ffn_w2, i) for i in range(2)]

    flat = lambda c: c.reshape((c.shape[0] * c.shape[1], PAGE * c.shape[3], HD))
    lf_pool = cache_fox_logf.reshape(-1, PAGE * H_FOX)
    win = lambda s: s.reshape(bs, s.shape[2], G_NSA * HD)
    past = (page_table, 0, flat(cache_fox_k), flat(cache_fox_v), lf_pool, flat(cache_nsa_kc), flat(cache_nsa_vc),
            flat(cache_nsa_ks), flat(cache_nsa_vs), win(state_nsa_kw), win(state_nsa_vw))

    def run(x, bsz, t, t_real, past_, state_):
        x, ev = _even_layer(x, bsz, t, t_real, ew, past_)
        x = _ffn(x, *ffn[0])
        x, od = _odd_layer(x, bsz, t, t_real, ow, state_)
        x = _ffn(x, *ffn[1])
        return x, ev, od

    yp, ev_p, od_p = run(x_prompt.reshape(bp * tp, d), bp, tp, tp, None, None)
    xs = jnp.pad(x_sample, ((0, 0), (0, ts_pad - ts), (0, 0))).reshape(bs * ts_pad, d)
    ys, ev_s, od_s = run(xs, bs, ts_pad, ts, past, (state_gla.reshape(state_gla.shape[1:]),
                                                    state_hgrn.reshape(state_hgrn.shape[1:])))
    y_prompt = yp.reshape(bp, tp, d)
    y_sample = ys.reshape(bs, ts_pad, d)[:, :ts]
    outs = [y_prompt, y_sample]
    for a, b in zip(ev_p, ev_s):
        outs += [a, b]
    for a, b in zip(od_p, od_s):
        outs += [a, b]
    return tuple(outs)
```

```python
import functools

import jax
import jax.numpy as jnp
from jax import lax
from jax.experimental import pallas as pl
from jax.experimental.pallas import tpu as pltpu

F32 = jnp.float32
BF16 = jnp.bfloat16

D_MODEL = 2048
HD = 128
H_FOX = 8
H_NSA = 8
G_NSA = 2
HPG = H_NSA // G_NSA
CMP_STRIDE = 16
CMP_BLOCK = 2 * CMP_STRIDE
SEL_BLOCK = 64
N_SELECT = 16
WINDOW = 512
FORCE_SCORE = 1.0e4
H_GLA = 4
DK_GLA = 128
DV_GLA = 256
GLA_RANK = 16
GLA_GATE_NORM = 16.0
H_HG = 8
DK_HG = 128
DV_HG = 128
PAGE = 128
EVEN_SIZES = (H_FOX * HD, H_FOX * HD, H_FOX * HD, H_FOX, H_NSA * HD,
              G_NSA * HD, G_NSA * HD, G_NSA * HD, G_NSA * HD, G_NSA * HD, G_NSA * HD, 3 * H_NSA)
ODD_SIZES = (H_GLA * DK_GLA, H_GLA * DK_GLA, H_GLA * DV_GLA, GLA_RANK, H_GLA * DV_GLA,
             H_HG * DK_HG, H_HG * DK_HG, H_HG * DV_HG, H_HG * DV_HG)
FQ, FK, FV, NQ, KC, VC, KS, VS, KW, VW = 0, 1024, 2048, 3072, 4096, 4352, 4608, 4864, 5120, 5376
N_BIG_EVEN = 5632
GQ, GK, GV, GG, HQ, HF, HI, HO = 0, 512, 1024, 2048, 3072, 4096, 5120, 6144
N_BIG_ODD = 7168
ATTN_SCALE = HD ** -0.5
NEG = -1.0e30
EPS = 1.0e-6
EXP_CLAMP = 80.0
VMEM_LIMIT = 56 * 1024 * 1024
W_TILE_BYTES = 16 * 1024 * 1024


def _cparams(sem):
    return pltpu.CompilerParams(dimension_semantics=sem, vmem_limit_bytes=VMEM_LIMIT)


def _dot(a, b):
    return jnp.dot(a, b, preferred_element_type=F32)


def _dot_nt(a, b):
    return lax.dot_general(a, b, (((1,), (1,)), ((), ())), preferred_element_type=F32)


def _dot_tn(a, b):
    return lax.dot_general(a, b, (((0,), (0,)), ((), ())), preferred_element_type=F32)


def _split3(x):
    hi = x.astype(BF16)
    r = x - hi.astype(F32)
    mid = r.astype(BF16)
    lo = (r - mid.astype(F32)).astype(BF16)
    return hi, mid, lo


def _dot3_l(x, w):
    hi, mid, lo = _split3(x)
    return _dot(hi, w) + _dot(mid, w) + _dot(lo, w)


def _dot3_r(w, x):
    hi, mid, lo = _split3(x)
    return _dot(w, hi) + _dot(w, mid) + _dot(w, lo)


def _sigmoid(x):
    return 1.0 / (1.0 + jnp.exp(-x))


def _log_sigmoid(x):
    return jnp.minimum(x, 0.0) - jnp.log1p(jnp.exp(-jnp.abs(x)))


def _cumsum_rows(x):
    n = x.shape[0]
    row = lax.broadcasted_iota(jnp.int32, x.shape, 0)
    s = 1
    while s < n:
        x = x + jnp.where(row >= s, pltpu.roll(x, s, axis=0), 0.0)
        s *= 2
    return x


def _tri(n, upper):
    r = lax.broadcasted_iota(jnp.int32, (n, n), 0)
    c = lax.broadcasted_iota(jnp.int32, (n, n), 1)
    return jnp.where((r <= c) if upper else (r >= c), 1.0, 0.0).astype(BF16)


def _proj_body(x_ref, g_ref, w_ref, ws_ref, cg_ref, cf_ref, z_ref, zs_ref, h_scr, *, tn, has_norm):
    @pl.when(pl.program_id(1) == 0)
    def _():
        x = x_ref[...]
        r = lax.rsqrt(jnp.mean(x * x, axis=-1, keepdims=True) + EPS)
        h = (x * r * g_ref[...]).astype(BF16)
        h_scr[...] = h
        zs_ref[...] = _dot(h, ws_ref[...])

    z = _dot(h_scr[...], w_ref[...])
    if not has_norm:
        z_ref[...] = z * cg_ref[...]
        return
    for c in range(tn // HD):
        sl = slice(c * HD, (c + 1) * HD)
        zc = z[:, sl]
        r = lax.rsqrt(jnp.mean(zc * zc, axis=-1, keepdims=True) + EPS)
        f = cf_ref[:, sl]
        z_ref[:, sl] = zc * (f * r + (1.0 - f)) * cg_ref[:, sl]


def _proj(x, g, w_big, w_small, col_gain, col_flag, has_norm):
    m, d = x.shape
    n = w_big.shape[1]
    tm = min(m, 1024)
    tn = next(c for c in (1408, 1024, 512) if n % c == 0)
    return pl.pallas_call(
        functools.partial(_proj_body, tn=tn, has_norm=has_norm),
        grid=(m // tm, n // tn),
        in_specs=[pl.BlockSpec((tm, d), lambda i, j: (i, 0)),
                  pl.BlockSpec((1, d), lambda i, j: (0, 0)),
                  pl.BlockSpec((d, tn), lambda i, j: (0, j)),
                  pl.BlockSpec((d, HD), lambda i, j: (0, 0)),
                  pl.BlockSpec((1, tn), lambda i, j: (0, j)),
                  pl.BlockSpec((1, tn), lambda i, j: (0, j))],
        out_specs=[pl.BlockSpec((tm, tn), lambda i, j: (i, j)),
                   pl.BlockSpec((tm, HD), lambda i, j: (i, 0))],
        out_shape=[jax.ShapeDtypeStruct((m, n), F32), jax.ShapeDtypeStruct((m, HD), F32)],
        scratch_shapes=[pltpu.VMEM((tm, d), BF16)],
        compiler_params=_cparams(("parallel", "arbitrary")),
    )(x, g, w_big, w_small, col_gain, col_flag)


def _ffn_up_body(x_ref, g_ref, w1_ref, w3_ref, o_ref, h_scr):
    @pl.when(pl.program_id(1) == 0)
    def _():
        x = x_ref[...]
        r = lax.rsqrt(jnp.mean(x * x, axis=-1, keepdims=True) + EPS)
        h_scr[...] = (x * r * g_ref[...]).astype(BF16)

    h = h_scr[...]
    a = _dot(h, w1_ref[...].astype(BF16))
    b = _dot(h, w3_ref[...].astype(BF16))
    o_ref[...] = (a * _sigmoid(a) * b).astype(BF16)


def _ffn_up(x, g, w1, w3, layer):
    m, d = x.shape
    n = w1.shape[2]
    tm = min(m, 1024)
    tn = 512
    return pl.pallas_call(
        _ffn_up_body,
        grid=(m // tm, n // tn),
        in_specs=[pl.BlockSpec((tm, d), lambda i, j: (i, 0)),
                  pl.BlockSpec((1, d), lambda i, j: (0, 0)),
                  pl.BlockSpec((None, d, tn), lambda i, j: (layer, 0, j)),
                  pl.BlockSpec((None, d, tn), lambda i, j: (layer, 0, j))],
        out_specs=pl.BlockSpec((tm, tn), lambda i, j: (i, j)),
        out_shape=jax.ShapeDtypeStruct((m, n), BF16),
        scratch_shapes=[pltpu.VMEM((tm, d), BF16)],
        compiler_params=_cparams(("parallel", "arbitrary")),
    )(x, g, w1, w3)


def _mm_res_body(*refs, n_in):
    res_ref = refs[2 * n_in]
    o_ref = refs[2 * n_in + 1]
    wb_refs = refs[2 * n_in + 2:]

    @pl.when(pl.program_id(1) == 0)
    def _():
        for w_ref, wb_ref in zip(refs[n_in:2 * n_in], wb_refs):
            wb_ref[...] = w_ref[...].astype(BF16)

    acc = res_ref[...]
    for a_ref, wb_ref in zip(refs[:n_in], wb_refs):
        acc = acc + _dot(a_ref[...], wb_ref[...])
    o_ref[...] = acc


def _mm_res(a_list, w, layer, res):
    m, n = res.shape
    k_all = sum(a.shape[1] for a in a_list)
    t = 1024 if 2 * 4 * k_all * 1024 <= W_TILE_BYTES and n % 1024 == 0 else 512
    tm, tn = min(m, t), t
    in_specs, w_specs, scratch, off = [], [], [], 0
    for a in a_list:
        k = a.shape[1]
        in_specs.append(pl.BlockSpec((tm, k), lambda j, i: (i, 0)))
        w_specs.append(pl.BlockSpec((None, k, tn), lambda j, i, _o=off // k: (layer, _o, j)))
        scratch.append(pltpu.VMEM((k, tn), BF16))
        off += k
    return pl.pallas_call(
        functools.partial(_mm_res_body, n_in=len(a_list)),
        grid=(n // tn, m // tm),
        in_specs=in_specs + w_specs + [pl.BlockSpec((tm, tn), lambda j, i: (i, j))],
        out_specs=pl.BlockSpec((tm, tn), lambda j, i: (i, j)),
        out_shape=jax.ShapeDtypeStruct((m, n), F32),
        scratch_shapes=scratch,
        compiler_params=_cparams(("parallel", "arbitrary")),
    )(*a_list, *([w] * len(a_list)), res)


def _fox_prep_body(zs_ref, b_ref, lf_ref, c_ref, carry, *, tc):
    @pl.when(pl.program_id(1) == 0)
    def _():
        carry[...] = jnp.zeros_like(carry)

    lf = _log_sigmoid(zs_ref[0] + b_ref[...])
    c = _dot3_r(_tri(tc, upper=False), lf) + carry[...]
    lf_ref[0] = lf
    c_ref[0] = c
    carry[...] = c[tc - 1:tc, :]


def _fox_prep(zs3, b_pad):
    bsz, t, _ = zs3.shape
    tc = min(t, 256)
    spec = pl.BlockSpec((1, tc, HD), lambda b, i: (b, i, 0))
    return pl.pallas_call(
        functools.partial(_fox_prep_body, tc=tc),
        grid=(bsz, t // tc),
        in_specs=[spec, pl.BlockSpec((1, HD), lambda b, i: (0, 0))],
        out_specs=[spec, spec],
        out_shape=[jax.ShapeDtypeStruct(zs3.shape, F32)] * 2,
        scratch_shapes=[pltpu.VMEM((1, HD), F32)],
        compiler_params=_cparams(("parallel", "arbitrary")),
    )(zs3, b_pad)


def _flash_body(*refs, mode, tq, tk, hpg, off):
    it = iter(refs)
    q_ref, k_ref, v_ref = next(it), next(it), next(it)
    bias_ref = next(it) if mode == "fox" else None
    sel_ref = next(it) if mode == "slc" else None
    o_ref, kb, vb = next(it), next(it), next(it)
    i = pl.program_id(2)

    @pl.when(i == 0)
    def _():
        kb[...] = k_ref[0].astype(BF16)
        vb[...] = v_ref[0].astype(BF16)

    q = q_ref[0]
    if hpg > 1:
        q = jnp.concatenate([q[:, h * HD:(h + 1) * HD] for h in range(hpg)], axis=0)
    qa = (q * ATTN_SCALE).astype(BF16)
    tile_rows = (lambda x: jnp.concatenate([x] * hpg, axis=0)) if hpg > 1 else (lambda x: x)
    q_lo = off + i * tq
    t_k = kb.shape[0]
    if mode == "slc":
        unpicked = (sel_ref[0, 0] - 1.0).astype(BF16)

    def attend(blocks):
        ss = []
        for b, (ks, size, masked) in enumerate(blocks):
            s = _dot_nt(qa, kb[pl.ds(ks, size), :])
            if mode == "fox":
                s = s - bias_ref[0, b:b + 1, :]
            if mode == "slc":
                cb = lax.broadcasted_iota(jnp.int32, (HD, size), 0)
                kk = lax.broadcasted_iota(jnp.int32, (HD, size), 1)
                expand = jnp.where(cb == (ks + kk) // SEL_BLOCK, 2.0 ** 100, 0.0).astype(BF16)
                s = s + tile_rows(_dot(unpicked, expand))
            if masked:
                kidx = ks + lax.broadcasted_iota(jnp.int32, (tq, size), 1)
                qpos = q_lo + lax.broadcasted_iota(jnp.int32, (tq, size), 0)
                valid = kidx <= qpos
                if mode == "win":
                    valid = valid & ((qpos - kidx) < WINDOW)
                s = jnp.where(tile_rows(valid), s, NEG)
            ss.append(s)
        m = ss[0].max(axis=-1, keepdims=True)
        for s in ss[1:]:
            m = jnp.maximum(m, s.max(axis=-1, keepdims=True))
        l, acc = 0.0, 0.0
        for s, (ks, size, _) in zip(ss, blocks):
            p = jnp.exp(s - m)
            l = l + jnp.sum(p, axis=-1, keepdims=True)
            acc = acc + _dot(p.astype(BF16), vb[pl.ds(ks, size), :])
        o = acc / l
        for h in range(hpg):
            o_ref[0, :, h * HD:(h + 1) * HD] = o[h * tq:(h + 1) * tq].astype(o_ref.dtype)

    if mode == "win":
        span = min(t_k, -(-(WINDOW + tq) // HD) * HD + HD)
        start = jnp.clip((q_lo - (WINDOW - 1)) // HD * HD, 0, t_k - span)
        attend([(pl.multiple_of(start, HD), span, True)])
    else:
        n_max = t_k // tk
        need = (q_lo + tq - 1) // tk + 1
        for nb in range(1, n_max + 1):
            @pl.when(need == nb)
            def _(nb=nb):
                attend([(b * tk, tk, b == nb - 1) for b in range(nb)])


def _flash(mode, q_arr, q_col, k_arr, k_col, v_arr, v_col, *, n_kv, hpg, tq, tk=None, off=0,
           bias=None, sel=None, out_dtype=F32):
    bsz, t_q = q_arr.shape[0], q_arr.shape[1]
    t_k = k_arr.shape[1]
    nq = t_q // tq
    w = hpg * HD
    in_specs = [pl.BlockSpec((1, tq, w), lambda b, g, i: (b, i, q_col // w + g)),
                pl.BlockSpec((1, t_k, HD), lambda b, g, i: (b, 0, k_col // HD + g)),
                pl.BlockSpec((1, t_k, HD), lambda b, g, i: (b, 0, v_col // HD + g))]
    args = [q_arr, k_arr, v_arr]
    if mode == "fox":
        in_specs.append(pl.BlockSpec((1, t_k // tk, tk), lambda b, g, i: (b * n_kv + g, 0, 0)))
        args.append(bias)
    if mode == "slc":
        in_specs.append(pl.BlockSpec((1, 1, tq, HD), lambda b, g, i: (b, g, i, 0)))
        args.append(sel)
    return pl.pallas_call(
        functools.partial(_flash_body, mode=mode, tq=tq, tk=tk, hpg=hpg, off=off),
        grid=(bsz, n_kv, nq),
        in_specs=in_specs,
        out_specs=pl.BlockSpec((1, tq, w), lambda b, g, i: (b, i, g)),
        out_shape=jax.ShapeDtypeStruct((bsz, t_q, n_kv * w), out_dtype),
        scratch_shapes=[pltpu.VMEM((t_k, HD), BF16), pltpu.VMEM((t_k, HD), BF16)],
        compiler_params=_cparams(("parallel", "parallel", "arbitrary")),
    )(*args)


def _compress_body(*refs, paged, pg, n_steps, mn, norms, t_new):
    it = iter(refs)
    if paged:
        next(it)
    n_t = len(norms)
    x_all = [[next(it) for _ in range(pg if paged else G_NSA)] for _ in range(n_t)]
    new_all = [next(it) for _ in range(n_t)] if paged else None
    par_all = [(next(it), next(it), next(it)) for _ in range(n_t)]
    o_all = [next(it) for _ in range(n_t)]
    hcat, tb, perm = next(it), next(it), next(it)
    p = pl.program_id(1)
    mh = mn + 8
    per_page = PAGE // CMP_STRIDE
    n_real = n_steps * pg * per_page
    n_g = G_NSA if paged else 1

    @pl.when(p == 0)
    def _():
        hcat[:, n_real:, :] = jnp.zeros((n_t * G_NSA, mh - n_real, CMP_STRIDE * HD), F32)
        d = lax.broadcasted_iota(jnp.int32, perm.shape, 0)
        s = lax.broadcasted_iota(jnp.int32, perm.shape, 1)
        src = (CMP_STRIDE * (d % per_page) + (d % PAGE) // per_page) * n_g + d // PAGE
        perm[...] = jnp.where(s == src, 1.0, 0.0).astype(BF16)

    for ti in range(n_t):
        for pp in range(pg):
            r0 = pl.multiple_of((p * pg + pp) * per_page, per_page)
            for x_ref, g0 in ([(x_all[ti][pp], 0)] if paged else [(x_all[ti][g], g) for g in range(G_NSA)]):
                rows = _dot(perm[...], x_ref[0].astype(BF16))
                for g in range(n_g):
                    for l in range(CMP_STRIDE):
                        a = g * PAGE + l * per_page
                        hcat[ti * G_NSA + g0 + g, pl.ds(r0, per_page), l * HD:(l + 1) * HD] = rows[a:a + per_page]

    @pl.when(p == n_steps - 1)
    def _():
        for ti in range(n_t):
            pe_ref, w_ref, gain_ref = par_all[ti]
            if paged:
                xn = new_all[ti][0]
                rid = lax.broadcasted_iota(jnp.int32, xn.shape, 0)
                xn = jnp.where(rid < t_new, xn, 0.0)
                for g in range(G_NSA):
                    for l in range(8):
                        hcat[ti * G_NSA + g, n_real:n_real + 1, l * HD:(l + 1) * HD] = xn[l:l + 1, g * HD:(g + 1) * HD]
            w = w_ref[...]
            pe2 = _dot(pe_ref[...].astype(BF16), w)
            pe_bias = pe2[0:1, 0:HD] + pe2[1:2, HD:2 * HD]
            for g in range(G_NSA):
                tb[...] = _dot(hcat[ti * G_NSA + g].astype(BF16), w)
                y = tb[0:mn, 0:HD] + tb[1:mn + 1, HD:2 * HD] + pe_bias
                if norms[ti]:
                    y = y * lax.rsqrt(jnp.mean(y * y, axis=-1, keepdims=True) + EPS) * gain_ref[...]
                o_all[ti][0, :, g * HD:(g + 1) * HD] = y


def _compress(x_arr, x_cols, params, norms, *, mn, page_table=None, pools=None, pool_off=0, t_new=0, pg=32):
    paged = page_table is not None
    bsz = x_arr.shape[0]
    n_t = len(norms)
    mh = mn + 8
    w2 = G_NSA * HD
    const = lambda shape: pl.BlockSpec(shape, (lambda b, p, *_: (0,) * len(shape)))
    tail = [const((8, CMP_STRIDE * HD)), const((CMP_STRIDE * HD, 2 * HD)), const((1, HD))] * n_t
    tail_args = [a for par in params for a in par]
    out_specs = [pl.BlockSpec((1, mn, w2), lambda b, p, *_: (b, 0, 0))] * n_t
    n_perm = G_NSA * PAGE if paged else PAGE
    scratch = [pltpu.VMEM((n_t * G_NSA, mh, CMP_STRIDE * HD), F32), pltpu.VMEM((mh, 2 * HD), F32),
               pltpu.VMEM((n_perm, n_perm), BF16)]
    out_shape = [jax.ShapeDtypeStruct((bsz, mn, w2), F32)] * n_t
    if not paged:
        n_steps = x_arr.shape[1] // PAGE
        body = functools.partial(_compress_body, paged=False, pg=1, n_steps=n_steps, mn=mn, norms=norms, t_new=0)
        return pl.pallas_call(
            body, grid=(bsz, n_steps),
            in_specs=[pl.BlockSpec((1, PAGE, HD), lambda b, p, _c=c // HD + g: (b, p, _c))
                      for c in x_cols for g in range(G_NSA)] + tail,
            out_specs=out_specs, out_shape=out_shape, scratch_shapes=scratch,
            compiler_params=_cparams(("parallel", "arbitrary")),
        )(*([x_arr] * (n_t * G_NSA)), *tail_args)
    pg = min(pg, page_table.shape[1])
    n_steps = page_table.shape[1] // pg
    body = functools.partial(_compress_body, paged=True, pg=pg, n_steps=n_steps, mn=mn, norms=norms, t_new=t_new)
    grid_spec = pltpu.PrefetchScalarGridSpec(
        num_scalar_prefetch=1, grid=(bsz, n_steps),
        in_specs=[pl.BlockSpec((1, G_NSA * PAGE, HD), lambda b, p, pt, _pp=pp: (pool_off + pt[b, p * pg + _pp], 0, 0))
                  for _ in range(n_t) for pp in range(pg)]
        + [pl.BlockSpec((1, 8, w2), lambda b, p, pt, _c=c // w2: (b, 0, _c)) for c in x_cols] + tail,
        out_specs=out_specs, scratch_shapes=scratch)
    return pl.pallas_call(body, grid_spec=grid_spec, out_shape=out_shape,
                          compiler_params=_cparams(("parallel", "arbitrary")))(
        page_table, *[pool for pool in pools for _ in range(pg)], *([x_arr] * n_t), *tail_args)


def _cmpsel_body(q_ref, kc_ref, vc_ref, o_ref, sel_ref, *, tq, ncp, nsp, n_sel, pos0, gs):
    i = pl.program_id(2)
    q = q_ref[0]
    pos = pos0 + i * tq + lax.broadcasted_iota(jnp.int32, (tq, ncp), 0)
    col = lax.broadcasted_iota(jnp.int32, (tq, ncp), 1)
    valid = (col * CMP_STRIDE + (CMP_BLOCK - 1)) <= pos
    validf = jnp.where(valid, 1.0, 0.0)
    imps = []
    for g in range(gs):
        kc = kc_ref[0, :, g * HD:(g + 1) * HD].astype(BF16)
        vc = vc_ref[0, :, g * HD:(g + 1) * HD].astype(BF16)
        imp = jnp.zeros((tq, ncp), F32)
        for h in range(g * HPG, (g + 1) * HPG):
            qh = (q[:, h * HD:(h + 1) * HD] * ATTN_SCALE).astype(BF16)
            s = jnp.where(valid, _dot_nt(qh, kc), NEG)
            e = jnp.exp(s - jnp.max(s, axis=-1, keepdims=True)) * validf
            p = e / jnp.maximum(jnp.sum(e, axis=-1, keepdims=True), 1e-30)
            o_ref[0, :, h * HD:(h + 1) * HD] = _dot(p.astype(BF16), vc)
            imp = imp + p
        imps.append(imp)
    imp = jnp.concatenate(imps, axis=0)
    cc = lax.broadcasted_iota(jnp.int32, (ncp, nsp), 0)
    jj = lax.broadcasted_iota(jnp.int32, (ncp, nsp), 1)
    r = SEL_BLOCK // CMP_STRIDE
    gather = jnp.where((cc >= r * jj - 1) & (cc <= r * jj + r - 1), 1.0, 0.0).astype(BF16)
    score = _dot3_l(imp, gather)
    rows = gs * tq
    blk = lax.broadcasted_iota(jnp.int32, (rows, nsp), 1)
    pq = pos0 + i * tq + lax.broadcasted_iota(jnp.int32, (rows, nsp), 0) % tq
    forced = (blk == pq // SEL_BLOCK) | (blk == 0)
    future = blk * SEL_BLOCK > pq
    score = jnp.where(forced, FORCE_SCORE, jnp.where(future, -1.0, score))
    score = jnp.where(blk < n_sel, score, -2.0)
    if rows % HD == 0:
        n8 = -(-n_sel // 8) * 8
        st = score.T[:n8]
        bt = lax.broadcasted_iota(jnp.int32, (n8, rows), 0)
        rank = jnp.zeros((n8, rows), F32)
        for c in range(n_sel):
            sc = st[c:c + 1, :]
            ahead = (sc > st) | ((sc == st) & (bt > c))
            rank = rank + jnp.where(ahead, 1.0, 0.0)
        keep = jnp.where(rank < float(N_SELECT), 1.0, 0.0)
        sel = jnp.concatenate([keep, jnp.zeros((nsp - n8, rows), F32)], axis=0).T
    else:
        rank = jnp.zeros((rows, nsp), F32)
        for c in range(n_sel):
            sc = score[:, c:c + 1]
            ahead = (sc > score) | ((sc == score) & (blk > c))
            rank = rank + jnp.where(ahead, 1.0, 0.0)
        sel = jnp.where(rank < float(N_SELECT), 1.0, 0.0)
    for g in range(gs):
        sel_ref[0, g] = sel[g * tq:(g + 1) * tq]


def _cmpsel(z3, kcmp, vcmp, *, tq, pos0, n_sel, gs):
    bsz, t, _ = z3.shape
    ncp = kcmp.shape[1]
    nsp = -(-n_sel // HD) * HD
    w = gs * HPG * HD
    return pl.pallas_call(
        functools.partial(_cmpsel_body, tq=tq, ncp=ncp, nsp=nsp, n_sel=n_sel, pos0=pos0, gs=gs),
        grid=(bsz, G_NSA // gs, t // tq),
        in_specs=[pl.BlockSpec((1, tq, w), lambda b, g, i: (b, i, NQ // w + g)),
                  pl.BlockSpec((1, ncp, gs * HD), lambda b, g, i: (b, 0, g)),
                  pl.BlockSpec((1, ncp, gs * HD), lambda b, g, i: (b, 0, g))],
        out_specs=[pl.BlockSpec((1, tq, w), lambda b, g, i: (b, i, g)),
                   pl.BlockSpec((1, gs, tq, nsp), lambda b, g, i: (b, g, i, 0))],
        out_shape=[jax.ShapeDtypeStruct((bsz, t, H_NSA * HD), F32),
                   jax.ShapeDtypeStruct((bsz, G_NSA, t, nsp), F32)],
        compiler_params=_cparams(("parallel", "parallel", "parallel")),
    )(z3, kcmp, vcmp)


def _combine_body(oc_ref, os_ref, ow_ref, zs_ref, gb_ref, o_ref):
    gates = _sigmoid(zs_ref[...] + gb_ref[...])
    for h in range(H_NSA):
        sl = slice(h * HD, (h + 1) * HD)
        c0 = H_FOX + 3 * h
        o = (gates[:, c0:c0 + 1] * oc_ref[:, sl] + gates[:, c0 + 1:c0 + 2] * os_ref[:, sl]
             + gates[:, c0 + 2:c0 + 3] * ow_ref[:, sl])
        o_ref[:, sl] = o.astype(BF16)


def _nsa_combine(oc, osl, ow, zs, gb_pad):
    m, n = oc.shape
    tm = min(m, 512)
    big = pl.BlockSpec((tm, n), lambda i: (i, 0))
    return pl.pallas_call(
        _combine_body, grid=(m // tm,),
        in_specs=[big, big, big, pl.BlockSpec((tm, HD), lambda i: (i, 0)), pl.BlockSpec((1, HD), lambda i: (0, 0))],
        out_specs=big, out_shape=jax.ShapeDtypeStruct((m, n), BF16),
        compiler_params=_cparams(("parallel",)),
    )(oc, osl, ow, zs, gb_pad)


def _paged_body(*refs, mode, pg, n_steps, n_w, hpg_kv, t_new):
    it = iter(refs)
    next(it)
    q_ref = next(it)
    k_refs = [next(it) for _ in range(pg)]
    v_refs = [next(it) for _ in range(pg)]
    cl_refs = [next(it) for _ in range(pg)] if mode == "fox" else None
    tf_refs = [next(it) for _ in range(pg)] if mode == "fox" else None
    kn_ref, vn_ref = next(it), next(it)
    lfn_ref = next(it) if mode == "fox" else None
    sel_ref, seln_ref = (next(it), next(it)) if mode == "slc" else (None, None)
    o_ref = next(it)
    qb, m_scr, l_scr, acc, kn_scr, vn_scr, carry = (next(it) for _ in range(7))
    p = pl.program_id(1)
    rows = 4 * 8
    r2 = n_w * PAGE
    n_new = t_new * n_w

    def kv_match(ncols):
        row = lax.broadcasted_iota(jnp.int32, (rows, ncols), 0)
        col = lax.broadcasted_iota(jnp.int32, (rows, ncols), 1)
        return row, col, (col % n_w) == (row % 8) // hpg_kv

    @pl.when(p == 0)
    def _():
        qb[...] = (q_ref[0] * ATTN_SCALE).astype(BF16)
        m_scr[...] = jnp.full(m_scr.shape, NEG, F32)
        l_scr[...] = jnp.zeros(l_scr.shape, F32)
        acc[...] = jnp.zeros(acc.shape, F32)
        carry[...] = jnp.zeros(carry.shape, F32)
        kn_scr[...] = jnp.zeros(kn_scr.shape, BF16)
        vn_scr[...] = jnp.zeros(vn_scr.shape, BF16)

    def picked(choice, first_lane, ncols):
        half = (lax.broadcasted_iota(jnp.int32, (rows, ncols), 1) // n_w) // SEL_BLOCK
        out = None
        for j in range(PAGE // SEL_BLOCK):
            term = (half == j) & (choice[:, first_lane + j:first_lane + j + 1] > 0.5)
            out = term if out is None else (out | term)
        return out

    def update(s_list, vb_list):
        m_old = m_scr[...]
        m_new = m_old
        for s in s_list:
            m_new = jnp.maximum(m_new, jnp.max(s, axis=-1, keepdims=True))
        a = jnp.exp(m_old - m_new)
        l = a * l_scr[...]
        o = a * acc[...]
        for s, vb in zip(s_list, vb_list):
            pr = jnp.exp(s - m_new)
            l = l + jnp.sum(pr, axis=-1, keepdims=True)
            o = o + _dot(pr.astype(BF16), vb)
        m_scr[...] = m_new
        l_scr[...] = l
        acc[...] = o

    _, _, match = kv_match(r2)
    s_list, vb_list = [], []
    c_run = carry[...] if mode == "fox" else None
    for pp in range(pg):
        s = _dot_nt(qb[...], k_refs[pp][0].astype(BF16))
        valid = match
        if mode == "fox":
            s = s - (c_run + cl_refs[pp][0])
            c_run = c_run + tf_refs[pp][0]
        if mode == "slc":
            valid = valid & picked(sel_ref[0, 0], pp * (PAGE // SEL_BLOCK), r2)
        s_list.append(jnp.where(valid, s, NEG))
        vb_list.append(v_refs[pp][0].astype(BF16))
    if mode == "fox":
        carry[...] = c_run
    update(s_list, vb_list)

    @pl.when(p == n_steps - 1)
    def _():
        kn_scr[0:n_new, :] = kn_ref[0].astype(BF16)
        vn_scr[0:n_new, :] = vn_ref[0].astype(BF16)
        s = _dot_nt(qb[...], kn_scr[...])
        row, col, valid = kv_match(HD)
        valid = valid & (col // n_w <= row // 8) & (col < n_new)
        if mode == "fox":
            rr = lax.broadcasted_iota(jnp.int32, (HD, HD), 0)
            cc = lax.broadcasted_iota(jnp.int32, (HD, HD), 1)
            pre = jnp.where((rr % n_w == cc % n_w) & (rr // n_w <= cc // n_w), 1.0, 0.0).astype(BF16)
            c_new = _dot3_l(jnp.broadcast_to(lfn_ref[0], (8, HD)), pre)[0:1, :] + carry[:, 0:HD]
            s = s - c_new
        if mode == "slc":
            valid = valid & picked(seln_ref[0], 0, HD)
        update([jnp.where(valid, s, NEG)], [vn_scr[...]])
        o_ref[0] = acc[...] / l_scr[...]


def _paged_attn(mode, q_rows, page_table, k_pool, v_pool, pool_off, k_new, v_new, *, hpg_kv, pg,
                c_local=None, c_total=None, lf_new=None, sel_rows=None):
    bsz = q_rows.shape[0]
    n_pages = page_table.shape[1]
    pg = min(pg, n_pages)
    n_steps = n_pages // pg
    r2 = k_pool.shape[1]
    n_w = r2 // PAGE
    n_new = k_new.shape[1]
    page = lambda pp: (lambda b, p, pt: (pool_off + pt[b, p * pg + pp], 0, 0))
    per_b = lambda b, p, pt: (b, 0, 0)
    in_specs = [pl.BlockSpec((1, 32, HD), per_b)]
    in_specs += [pl.BlockSpec((1, r2, HD), page(pp)) for pp in range(pg)] * 2
    args = [q_rows] + [k_pool] * pg + [v_pool] * pg
    if mode == "fox":
        in_specs += [pl.BlockSpec((1, 1, r2), page(pp)) for pp in range(pg)] * 2
        args += [c_local] * pg + [c_total] * pg
    in_specs += [pl.BlockSpec((1, n_new, HD), per_b)] * 2
    args += [k_new, v_new]
    if mode == "fox":
        in_specs.append(pl.BlockSpec((1, 1, HD), per_b))
        args.append(lf_new)
    if mode == "slc":
        per_page = PAGE // SEL_BLOCK
        past = sel_rows[:, :, :n_pages * per_page].reshape(bsz, 32, n_steps, pg * per_page)
        sel_steps = _pad_cols(jnp.swapaxes(past, 1, 2).reshape(bsz * n_steps * 32, pg * per_page), HD)
        in_specs += [pl.BlockSpec((1, 1, 32, HD), lambda b, p, pt: (b, p, 0, 0)), pl.BlockSpec((1, 32, HD), per_b)]
        args += [sel_steps.reshape(bsz, n_steps, 32, HD),
                 _pad_cols(sel_rows[:, :, n_pages * per_page:n_pages * per_page + 1].reshape(bsz * 32, 1), HD)
                 .reshape(bsz, 32, HD)]
    grid_spec = pltpu.PrefetchScalarGridSpec(
        num_scalar_prefetch=1, grid=(bsz, n_steps), in_specs=in_specs,
        out_specs=pl.BlockSpec((1, 32, HD), per_b),
        scratch_shapes=[pltpu.VMEM((32, HD), BF16), pltpu.VMEM((32, 1), F32), pltpu.VMEM((32, 1), F32),
                        pltpu.VMEM((32, HD), F32), pltpu.VMEM((PAGE, HD), BF16), pltpu.VMEM((PAGE, HD), BF16),
                        pltpu.VMEM((1, r2), F32)])
    return pl.pallas_call(
        functools.partial(_paged_body, mode=mode, pg=pg, n_steps=n_steps, n_w=n_w, hpg_kv=hpg_kv, t_new=n_new // n_w),
        grid_spec=grid_spec, out_shape=jax.ShapeDtypeStruct((bsz, 32, HD), F32),
        compiler_params=_cparams(("parallel", "arbitrary")),
    )(page_table, *args)


def _pool_prefix_body(x_ref, cl_ref, ct_ref, w_scr, *, n_w):
    r2 = x_ref.shape[1]

    @pl.when(pl.program_id(0) == 0)
    def _():
        r = lax.broadcasted_iota(jnp.int32, (r2, 2 * r2), 0)
        c = lax.broadcasted_iota(jnp.int32, (r2, 2 * r2), 1)
        same = (r % n_w) == (c % n_w)
        local = same & (r // n_w <= c // n_w) & (c < r2)
        w_scr[...] = jnp.where(local | (same & (c >= r2)), 1.0, 0.0).astype(BF16)

    y = _dot3_l(x_ref[...], w_scr[...])
    cl_ref[...] = y[:, :r2]
    ct_ref[...] = y[:, r2:]


def _pool_prefix(x, n_w):
    n, r2 = x.shape
    tm = next((c for c in (256, 128, 64, 32, 16, 8) if n % c == 0), n)
    spec = pl.BlockSpec((tm, r2), lambda i: (i, 0))
    return pl.pallas_call(
        functools.partial(_pool_prefix_body, n_w=n_w), grid=(n // tm,),
        in_specs=[spec], out_specs=[spec, spec],
        out_shape=[jax.ShapeDtypeStruct((n, r2), F32)] * 2,
        scratch_shapes=[pltpu.VMEM((r2, 2 * r2), BF16)],
        compiler_params=_cparams(("arbitrary",)),
    )(x)


def _gla_body(*refs, mode, t_in, chunk, dv, t_real, has_state, hb):
    it = iter(refs)
    q_ref, k_ref, v_ref, og_ref = next(it), next(it), next(it), next(it)
    if mode == "gla":
        zs_ref, wa_ref, ba_ref = next(it), next(it), next(it)
    else:
        lb_ref = next(it)
    gain_ref = next(it)
    s0_ref = next(it) if has_state else None
    o_ref, s_ref, st_scr, q_scr, k_scr, g_scr, o_scr = (next(it) for _ in range(7))
    tb = q_scr.shape[1]
    n_chunks = tb // chunk
    pad = tb - t_in
    tri = lax.broadcasted_iota(jnp.int32, (chunk, chunk), 0) >= lax.broadcasted_iota(jnp.int32, (chunk, chunk), 1)
    if mode == "hg":
        lbr = lb_ref[...]
        e = jnp.exp(lbr - jnp.max(lbr, axis=0, keepdims=True))
        lb_all = (e / jnp.sum(e, axis=0, keepdims=True))[0:1, :]
    ti = pl.program_id(2)

    @pl.when(ti == 0)
    def _():
        for h in range(hb):
            st_scr[h] = s0_ref[0, h].T if has_state else jnp.zeros((dv, HD), F32)

    def block(ref, sl):
        x = ref[0, :, sl]
        if pad:
            x = jnp.concatenate([x, jnp.zeros((pad, x.shape[1]), F32)], axis=0)
        return x

    live = (ti * t_in + lax.broadcasted_iota(jnp.int32, (tb, HD), 0)) < t_real
    if mode == "gla":
        ga = block(zs_ref, slice(0, HD)).astype(BF16)
    for h in range(hb):
        ks = slice(h * HD, (h + 1) * HD)
        qr, kr = block(q_ref, ks), block(k_ref, ks)
        if mode == "gla":
            g = _log_sigmoid(_dot(ga, wa_ref[:, ks]) + ba_ref[:, ks]) / GLA_GATE_NORM
            q, k = qr, kr
        else:
            lb = lb_all[:, ks]
            f = lb + (1.0 - lb) * _sigmoid(kr)
            q, k, g = qr * _sigmoid(qr), 1.0 - f, jnp.log(f)
        q_scr[h] = q
        k_scr[h] = jnp.where(live, k, 0.0)
        g_scr[h] = jnp.where(live, g, 0.0)

    def body(c, _):
        r0 = pl.multiple_of(c * chunk, chunk)
        rows = pl.ds(r0, chunk)
        for h in range(hb):
            vs = slice(h * dv, (h + 1) * dv)
            q, k, g = q_scr[h, rows, :], k_scr[h, rows, :], g_scr[h, rows, :]
            if pad:
                v = block(v_ref, vs)
            else:
                v = v_ref[0, rows, vs]
            b = _cumsum_rows(g)
            bm = b[chunk // 2 - 1:chunk // 2, :]
            bl = b[chunk - 1:chunk, :]
            qe = (q * jnp.exp(jnp.minimum(b - bm, EXP_CLAMP))).astype(BF16)
            ke = (k * jnp.exp(jnp.minimum(bm - b, EXP_CLAMP))).astype(BF16)
            a = jnp.where(tri, _dot_nt(qe, ke), 0.0)
            vb = v.astype(BF16)
            st = st_scr[h]
            o_scr[h, rows, :] = _dot(a.astype(BF16), vb) + _dot_nt((q * jnp.exp(b)).astype(BF16), st.astype(BF16))
            kd = (k * jnp.exp(bl - b)).astype(BF16)
            st_scr[h] = st * jnp.exp(bl) + _dot_tn(vb, kd)
        return 0

    lax.fori_loop(0, n_chunks, body, 0)

    for h in range(hb):
        vs = slice(h * dv, (h + 1) * dv)
        o = o_scr[h, 0:t_in, :]
        og = og_ref[0, :, vs]
        o = o * lax.rsqrt(jnp.mean(o * o, axis=-1, keepdims=True) + EPS) * gain_ref[...]
        o_ref[0, :, vs] = (o * (og * _sigmoid(og) if mode == "gla" else _sigmoid(og))).astype(BF16)

    @pl.when(ti == pl.num_programs(2) - 1)
    def _():
        for h in range(hb):
            s_ref[0, h] = st_scr[h].T


def _gla(mode, z3, zs3, cols, n_heads, dv, extra, gain, s0, *, t_real, chunk, hb):
    bsz, t_all, _ = z3.shape
    tc = min(t_all, 512)
    qc, kc, vc, oc = cols
    has_state = s0 is not None
    wk, wv = hb * HD, hb * dv
    in_specs = [pl.BlockSpec((1, tc, wk), lambda b, h, t: (b, t, qc // wk + h)),
                pl.BlockSpec((1, tc, wk), lambda b, h, t: (b, t, kc // wk + h)),
                pl.BlockSpec((1, tc, wv), lambda b, h, t: (b, t, vc // wv + h)),
                pl.BlockSpec((1, tc, wv), lambda b, h, t: (b, t, oc // wv + h))]
    args = [z3, z3, z3, z3]
    if mode == "gla":
        wa, ba = extra
        in_specs += [pl.BlockSpec((1, tc, HD), lambda b, h, t: (b, t, 0)),
                     pl.BlockSpec((HD, wk), lambda b, h, t: (0, h)),
                     pl.BlockSpec((1, wk), lambda b, h, t: (0, h))]
        args += [zs3, wa, ba]
    else:
        in_specs.append(pl.BlockSpec((8, wk), lambda b, h, t: (0, h)))
        args.append(extra)
    in_specs.append(pl.BlockSpec((1, dv), lambda b, h, t: (0, 0)))
    args.append(gain)
    st_spec = pl.BlockSpec((1, hb, HD, dv), lambda b, h, t: (b, h, 0, 0))
    if has_state:
        in_specs.append(st_spec)
        args.append(s0)
    return pl.pallas_call(
        functools.partial(_gla_body, mode=mode, t_in=tc, chunk=chunk, dv=dv, t_real=t_real, has_state=has_state,
                          hb=hb),
        grid=(bsz, n_heads // hb, t_all // tc),
        in_specs=in_specs,
        out_specs=[pl.BlockSpec((1, tc, wv), lambda b, h, t: (b, t, h)), st_spec],
        out_shape=[jax.ShapeDtypeStruct((bsz, t_all, n_heads * dv), BF16),
                   jax.ShapeDtypeStruct((bsz, n_heads, HD, dv), F32)],
        scratch_shapes=[pltpu.VMEM((hb, dv, HD), F32)] + [pltpu.VMEM((hb, max(tc, chunk), HD), F32)] * 3
        + [pltpu.VMEM((hb, max(tc, chunk), dv), F32)],
        compiler_params=_cparams(("parallel", "parallel", "arbitrary")),
    )(*args)


def _split_cols(w, sizes):
    outs, off = [], 0
    for s in sizes:
        outs.append(w[:, off:off + s])
        off += s
    return outs


def _pad_cols(w, n):
    return jnp.pad(w, ((0, 0), (0, n - w.shape[1])))


def _row(v):
    return v.reshape(1, -1).astype(F32)


def _even_weights(w_in, fq_gain, fk_gain, nq_gain, nk_gain):
    fq, fk, fv, ff, nq, kc, vc, ks, vs, kw, vw, ng = _split_cols(w_in, EVEN_SIZES)
    big = jnp.concatenate([fq, fk, fv, nq, kc, vc, ks, vs, kw, vw], axis=1).astype(BF16)
    small = _pad_cols(jnp.concatenate([ff, ng], axis=1), HD).astype(BF16)
    ones = lambda n: jnp.ones((n,), F32)
    gain = jnp.concatenate([jnp.tile(fq_gain, H_FOX), jnp.tile(fk_gain, H_FOX), ones(H_FOX * HD),
                            jnp.tile(nq_gain, H_NSA), ones(2 * G_NSA * HD), jnp.tile(nk_gain[1], G_NSA),
                            ones(G_NSA * HD), jnp.tile(nk_gain[2], G_NSA), ones(G_NSA * HD)])
    z, o = jnp.zeros, jnp.ones
    flag = jnp.concatenate([o((2 * H_FOX * HD,), F32), z((H_FOX * HD,), F32), o((H_NSA * HD,), F32),
                            z((2 * G_NSA * HD,), F32), o((G_NSA * HD,), F32), z((G_NSA * HD,), F32),
                            o((G_NSA * HD,), F32), z((G_NSA * HD,), F32)])
    return big, small, _row(gain), _row(flag)


def _odd_weights(w_in):
    gq, gk, gv, ga, gg, hq, hf, hi, hg = _split_cols(w_in, ODD_SIZES)
    big = jnp.concatenate([gq, gk, gv, gg, hq, hf, hi, hg], axis=1).astype(BF16)
    small = _pad_cols(ga, HD).astype(BF16)
    gain = jnp.concatenate([jnp.full((H_GLA * DK_GLA,), DK_GLA ** -0.5, F32), jnp.ones((N_BIG_ODD - H_GLA * DK_GLA,), F32)])
    return big, small, _row(gain), jnp.zeros((1, N_BIG_ODD), F32)


def _cmp_weights(pe, w):
    half = CMP_STRIDE * HD
    wcat = jnp.concatenate([w[:half], w[half:]], axis=1).astype(BF16)
    pe8 = jnp.pad(pe.reshape(2, half), ((0, 6), (0, 0)))
    return pe8, wcat


def _ffn(x, g, w1, w3, w2, layer):
    return _mm_res([_ffn_up(x, g, w1, w3, layer)], w2, layer, x)


def _even_layer(x, bsz, t, t_real, ew, past):
    (w_big, w_small, cgain, cflag, g_mix, w_out, b_f_pad, gb_pad, pe_k, wc_k, pe_v, wc_v, nk0) = ew
    z, zs = _proj(x, g_mix, w_big, w_small, cgain, cflag, True)
    z3 = z.reshape(bsz, t, N_BIG_EVEN)
    zs3 = zs.reshape(bsz, t, HD)
    cmp_params = ((pe_k, wc_k, nk0), (pe_v, wc_v, jnp.ones((1, HD), F32)))
    if past is None:
        lf, c = _fox_prep(zs3, b_f_pad)
        logf_new = lf[:, :, :H_FOX]
        tk = min(t, 512)
        bias = jnp.swapaxes(c[:, :, :H_FOX], 1, 2).reshape(bsz * H_FOX, t // tk, tk)
        o_fox = _flash("fox", z3, FQ, z3, FK, z3, FV, n_kv=H_FOX, hpg=1, tq=min(t, 512), tk=tk, bias=bias,
                       out_dtype=BF16)
        n_cmp = t // CMP_STRIDE
        mn = -(-n_cmp // HD) * HD
        kcmp, vcmp = _compress(z3, (KC, VC), cmp_params, (True, False), mn=mn)
        tqn = min(t, 1024 // HPG)
        o_cmp, sel = _cmpsel(z3, kcmp, vcmp, tq=min(t, 256), pos0=0, n_sel=-(-t // SEL_BLOCK), gs=1)
        o_slc = _flash("slc", z3, NQ, z3, KS, z3, VS, n_kv=G_NSA, hpg=HPG, tq=tqn, tk=tk, sel=sel)
        o_win = _flash("win", z3, NQ, z3, KW, z3, VW, n_kv=G_NSA, hpg=HPG, tq=tqn)
        w_buf = WINDOW
        kw_new = jnp.concatenate([jnp.zeros((bsz, w_buf, G_NSA * HD), F32), z3[:, :, KW:KW + G_NSA * HD]], axis=1)[:, -w_buf:]
        vw_new = jnp.concatenate([jnp.zeros((bsz, w_buf, G_NSA * HD), F32), z3[:, :, VW:VW + G_NSA * HD]], axis=1)[:, -w_buf:]
        o_fox = o_fox.reshape(bsz * t, H_FOX * HD)
    else:
        (page_table, pool_off, fk_pool, fv_pool, lf_pool, kc_pool, vc_pool, ks_pool, vs_pool, buf_k, buf_v) = past
        p_len = page_table.shape[1] * PAGE
        lf_all, _ = _fox_prep(zs.reshape(1, bsz * t, HD), b_f_pad)
        lf_new = lf_all.reshape(bsz, t, HD)[:, :t_real, :H_FOX]
        logf_new = lf_new
        lfn_pad = _pad_cols(lf_new.reshape(bsz, t_real * H_FOX), HD).reshape(bsz, 1, HD)
        c_local, c_total = _pool_prefix(lf_pool, H_FOX)
        rows_of = lambda c0, n: z3[:, :t_real, c0:c0 + n * HD].reshape(bsz, t_real * n, HD)
        o_fox = _paged_attn("fox", rows_of(FQ, H_FOX), page_table, fk_pool, fv_pool, pool_off,
                            rows_of(FK, H_FOX), rows_of(FV, H_FOX), hpg_kv=1, pg=16,
                            c_local=c_local[:, None, :], c_total=c_total[:, None, :], lf_new=lfn_pad)
        o_fox = o_fox.reshape(bsz, t_real, H_FOX * HD)
        n_cmp = -(-(p_len + t_real) // CMP_STRIDE)
        mn = -(-n_cmp // HD) * HD
        kcmp, vcmp = _compress(z3, (KC, VC), cmp_params, (True, False), mn=mn, page_table=page_table,
                               pools=(kc_pool, vc_pool), pool_off=pool_off, t_new=t_real)
        n_sel = -(-(p_len + t_real) // SEL_BLOCK)
        o_cmp, sel = _cmpsel(z3, kcmp, vcmp, tq=t, pos0=p_len, n_sel=n_sel, gs=G_NSA)
        nsp = sel.shape[-1]
        sel_rows = jnp.broadcast_to(jnp.swapaxes(sel[:, :, :t_real], 1, 2)[:, :, :, None, :],
                                    (bsz, t_real, G_NSA, HPG, nsp)).reshape(bsz, 32, nsp)
        o_slc = _paged_attn("slc", rows_of(NQ, H_NSA), page_table, ks_pool, vs_pool, pool_off,
                            rows_of(KS, G_NSA), rows_of(VS, G_NSA), hpg_kv=HPG, pg=32,
                            sel_rows=sel_rows).reshape(bsz, t_real, H_NSA * HD)
        w_buf = buf_k.shape[1]
        lk = -(-(w_buf + t) // HD) * HD
        fill = jnp.zeros((bsz, lk - w_buf - t_real, G_NSA * HD), F32)
        kw_all = jnp.concatenate([buf_k, z3[:, :t_real, KW:KW + G_NSA * HD], fill], axis=1)
        vw_all = jnp.concatenate([buf_v, z3[:, :t_real, VW:VW + G_NSA * HD], fill], axis=1)
        kw_new, vw_new = kw_all[:, t_real:t_real + w_buf], vw_all[:, t_real:t_real + w_buf]
        o_win = _flash("win", z3, NQ, kw_all, 0, vw_all, 0, n_kv=G_NSA, hpg=HPG, tq=t, off=w_buf)
        padt = lambda a: jnp.pad(a, ((0, 0), (0, t - t_real), (0, 0)))
        o_fox = padt(o_fox).astype(BF16).reshape(bsz * t, H_FOX * HD)
        o_slc = padt(o_slc)
    m = bsz * t
    o_nsa = _nsa_combine(o_cmp.reshape(m, -1), o_slc.reshape(m, -1), o_win.reshape(m, -1), zs, gb_pad)
    x = _mm_res([o_fox, o_nsa], w_out, 0, x)
    zr = z3[:, :t_real]
    grp = lambda c0: zr[:, :, c0:c0 + G_NSA * HD].reshape(1, bsz, t_real, G_NSA, HD)
    fox = lambda c0: zr[:, :, c0:c0 + H_FOX * HD].reshape(1, bsz, t_real, H_FOX, HD)
    w_rows = kw_new.shape[1]
    outs = (fox(FK), fox(FV), logf_new[None], grp(KC), grp(VC), grp(KS), grp(VS),
            kw_new.reshape(1, bsz, w_rows, G_NSA, HD), vw_new.reshape(1, bsz, w_rows, G_NSA, HD))
    return x, outs


def _odd_layer(x, bsz, t, t_real, ow, state):
    (w_big, w_small, cgain, cflag, g_mix, w_out, wa_pad, ba, gla_gain, lb_pad, hg_gain) = ow
    z, zs = _proj(x, g_mix, w_big, w_small, cgain, cflag, False)
    z3 = z.reshape(bsz, t, N_BIG_ODD)
    zs3 = zs.reshape(bsz, t, HD)
    s_gla, s_hg = state if state is not None else (None, None)
    chunk = 64 if t >= 64 else 128
    o_gla, s_gla = _gla("gla", z3, zs3, (GQ, GK, GV, GG), H_GLA, DV_GLA, (wa_pad, ba), gla_gain, s_gla,
                        t_real=t_real, chunk=chunk, hb=H_GLA)
    o_hg, s_hg = _gla("hg", z3, zs3, (HQ, HF, HI, HO), H_HG, DV_HG, lb_pad, hg_gain, s_hg,
                      t_real=t_real, chunk=chunk, hb=H_HG)
    m = bsz * t
    x = _mm_res([o_gla.reshape(m, -1), o_hg.reshape(m, -1)], w_out, 0, x)
    return x, (s_gla[None], s_hg[None])


def kernel(x_prompt, x_sample, cache_fox_k, cache_fox_v, cache_fox_logf, cache_nsa_kc, cache_nsa_vc, cache_nsa_ks, cache_nsa_vs, state_nsa_kw, state_nsa_vw, state_gla, state_hgrn, page_table, norm_mix, norm_ffn, w_in_even, w_out_even, fox_b_f, fox_q_gain, fox_k_gain, nsa_q_gain, nsa_k_gain, nsa_cmp_pe_k, nsa_cmp_pe_v, nsa_cmp_wk, nsa_cmp_wv, nsa_gate_b, w_in_odd, w_out_odd, gla_wa2, gla_ba, gla_norm, hgrn_lb, hgrn_norm, ffn_w1, ffn_w3, ffn_w2):
    bp, tp, d = x_prompt.shape
    bs, ts, _ = x_sample.shape
    n_pool = cache_fox_k.shape[1]
    ts_pad = 8
    pe_k, wc_k = _cmp_weights(nsa_cmp_pe_k[0], nsa_cmp_wk[0])
    pe_v, wc_v = _cmp_weights(nsa_cmp_pe_v[0], nsa_cmp_wv[0])
    b_f_pad = _pad_cols(_row(fox_b_f[0]), HD)
    gb_pad = _pad_cols(jnp.concatenate([jnp.zeros((1, H_FOX), F32), _row(nsa_gate_b[0])], axis=1), HD)
    ew = _even_weights(w_in_even[0], fox_q_gain[0], fox_k_gain[0], nsa_q_gain[0], nsa_k_gain[0]) + (
        _row(norm_mix[0]), w_out_even, b_f_pad, gb_pad, pe_k, wc_k, pe_v, wc_v, _row(nsa_k_gain[0, 0]))
    wa_pad = jnp.pad(gla_wa2[0], ((0, HD - GLA_RANK), (0, 0))).astype(BF16)
    lb_pad = jnp.pad(hgrn_lb.astype(F32), ((0, 8 - hgrn_lb.shape[0]), (0, 0)), constant_values=-1e30)
    ow = _odd_weights(w_in_odd[0]) + (_row(norm_mix[1]), w_out_odd, wa_pad, _row(gla_ba[0]),
                                      _row(gla_norm[0]), lb_pad, _row(hgrn_norm[0]))
    ffn = [(_row(norm_ffn[i]), ffn_w1, ffn_w3, ffn_w2, i) for i in range(2)]

    flat = lambda c: c.reshape((c.shape[0] * c.shape[1], PAGE * c.shape[3], HD))
    lf_pool = cache_fox_logf.reshape(-1, PAGE * H_FOX)
    win = lambda s: s.reshape(bs, s.shape[2], G_NSA * HD)
    past = (page_table, 0, flat(cache_fox_k), flat(cache_fox_v), lf_pool, flat(cache_nsa_kc), flat(cache_nsa_vc),
            flat(cache_nsa_ks), flat(cache_nsa_vs), win(state_nsa_kw), win(state_nsa_vw))

    def run(x, bsz, t, t_real, past_, state_):
        x, ev = _even_layer(x, bsz, t, t_real, ew, past_)
        x = _ffn(x, *ffn[0])
        x, od = _odd_layer(x, bsz, t, t_real, ow, state_)
        x = _ffn(x, *ffn[1])
        return x, ev, od

    yp, ev_p, od_p = run(x_prompt.reshape(bp * tp, d), bp, tp, tp, None, None)
    xs = jnp.pad(x_sample, ((0, 0), (0, ts_pad - ts), (0, 0))).reshape(bs * ts_pad, d)
    ys, ev_s, od_s = run(xs, bs, ts_pad, ts, past, (state_gla.reshape(state_gla.shape[1:]),
                                                    state_hgrn.reshape(state_hgrn.shape[1:])))
    y_prompt = yp.reshape(bp, tp, d)
    y_sample = ys.reshape(bs, ts_pad, d)[:, :ts]
    outs = [y_prompt, y_sample]
    for a, b in zip(ev_p, ev_s):
        outs += [a, b]
    for a, b in zip(od_p, od_s):
        outs += [a, b]
    return tuple(outs)
```

```python
import functools

import jax
import jax.numpy as jnp
from jax import lax
from jax.experimental import pallas as pl
from jax.experimental.pallas import tpu as pltpu

F32 = jnp.float32
BF16 = jnp.bfloat16

D_MODEL = 2048
HD = 128
H_FOX = 8
H_NSA = 8
G_NSA = 2
HPG = H_NSA // G_NSA
CMP_STRIDE = 16
CMP_BLOCK = 2 * CMP_STRIDE
SEL_BLOCK = 64
N_SELECT = 16
WINDOW = 512
FORCE_SCORE = 1.0e4
H_GLA = 4
DK_GLA = 128
DV_GLA = 256
GLA_RANK = 16
GLA_GATE_NORM = 16.0
H_HG = 8
DK_HG = 128
DV_HG = 128
PAGE = 128
EVEN_SIZES = (H_FOX * HD, H_FOX * HD, H_FOX * HD, H_FOX, H_NSA * HD,
              G_NSA * HD, G_NSA * HD, G_NSA * HD, G_NSA * HD, G_NSA * HD, G_NSA * HD, 3 * H_NSA)
ODD_SIZES = (H_GLA * DK_GLA, H_GLA * DK_GLA, H_GLA * DV_GLA, GLA_RANK, H_GLA * DV_GLA,
             H_HG * DK_HG, H_HG * DK_HG, H_HG * DV_HG, H_HG * DV_HG)
FQ, FK, FV, NQ, KC, VC, KS, VS, KW, VW = 0, 1024, 2048, 3072, 4096, 4352, 4608, 4864, 5120, 5376
N_BIG_EVEN = 5632
GQ, GK, GV, GG, HQ, HF, HI, HO = 0, 512, 1024, 2048, 3072, 4096, 5120, 6144
N_BIG_ODD = 7168
ATTN_SCALE = HD ** -0.5
NEG = -1.0e30
LOG2E = 1.4426950408889634
EPS = 1.0e-6
EXP_CLAMP = 80.0
VMEM_LIMIT = 56 * 1024 * 1024
W_TILE_BYTES = 16 * 1024 * 1024


def _cparams(sem):
    return pltpu.CompilerParams(dimension_semantics=sem, vmem_limit_bytes=VMEM_LIMIT)


def _dot(a, b):
    return jnp.dot(a, b, preferred_element_type=F32)


def _dot_nt(a, b):
    return lax.dot_general(a, b, (((1,), (1,)), ((), ())), preferred_element_type=F32)


def _dot_tn(a, b):
    return lax.dot_general(a, b, (((0,), (0,)), ((), ())), preferred_element_type=F32)


def _split3(x):
    hi = x.astype(BF16)
    r = x - hi.astype(F32)
    mid = r.astype(BF16)
    lo = (r - mid.astype(F32)).astype(BF16)
    return hi, mid, lo


def _dot3_l(x, w):
    hi, mid, lo = _split3(x)
    return _dot(hi, w) + _dot(mid, w) + _dot(lo, w)


def _dot3_r(w, x):
    hi, mid, lo = _split3(x)
    return _dot(w, hi) + _dot(w, mid) + _dot(w, lo)


def _sigmoid(x):
    return 1.0 / (1.0 + jnp.exp(-x))


def _log_sigmoid(x):
    return jnp.minimum(x, 0.0) - jnp.log1p(jnp.exp(-jnp.abs(x)))


def _cumsum_rows(x):
    n = x.shape[0]
    row = lax.broadcasted_iota(jnp.int32, x.shape, 0)
    s = 1
    while s < n:
        x = x + jnp.where(row >= s, pltpu.roll(x, s, axis=0), 0.0)
        s *= 2
    return x


def _tri(n, upper):
    r = lax.broadcasted_iota(jnp.int32, (n, n), 0)
    c = lax.broadcasted_iota(jnp.int32, (n, n), 1)
    return jnp.where((r <= c) if upper else (r >= c), 1.0, 0.0).astype(BF16)


def _proj_body(x_ref, g_ref, w_ref, ws_ref, cg_ref, cf_ref, z_ref, zs_ref, h_scr, *, tn, has_norm):
    @pl.when(pl.program_id(1) == 0)
    def _():
        x = x_ref[...]
        r = lax.rsqrt(jnp.mean(x * x, axis=-1, keepdims=True) + EPS)
        h = (x * r * g_ref[...]).astype(BF16)
        h_scr[...] = h
        zs_ref[...] = _dot(h, ws_ref[...])

    z = _dot(h_scr[...], w_ref[...])
    if not has_norm:
        z_ref[...] = z * cg_ref[...]
        return
    for c in range(tn // HD):
        sl = slice(c * HD, (c + 1) * HD)
        zc = z[:, sl]
        r = lax.rsqrt(jnp.mean(zc * zc, axis=-1, keepdims=True) + EPS)
        f = cf_ref[:, sl]
        z_ref[:, sl] = zc * (f * r + (1.0 - f)) * cg_ref[:, sl]


def _proj(x, g, w_big, w_small, col_gain, col_flag, has_norm):
    m, d = x.shape
    n = w_big.shape[1]
    tm = min(m, 1024)
    tn = next(c for c in (1408, 1024, 512) if n % c == 0)
    return pl.pallas_call(
        functools.partial(_proj_body, tn=tn, has_norm=has_norm),
        grid=(m // tm, n // tn),
        in_specs=[pl.BlockSpec((tm, d), lambda i, j: (i, 0)),
                  pl.BlockSpec((1, d), lambda i, j: (0, 0)),
                  pl.BlockSpec((d, tn), lambda i, j: (0, j)),
                  pl.BlockSpec((d, HD), lambda i, j: (0, 0)),
                  pl.BlockSpec((1, tn), lambda i, j: (0, j)),
                  pl.BlockSpec((1, tn), lambda i, j: (0, j))],
        out_specs=[pl.BlockSpec((tm, tn), lambda i, j: (i, j)),
                   pl.BlockSpec((tm, HD), lambda i, j: (i, 0))],
        out_shape=[jax.ShapeDtypeStruct((m, n), F32), jax.ShapeDtypeStruct((m, HD), F32)],
        scratch_shapes=[pltpu.VMEM((tm, d), BF16)],
        compiler_params=_cparams(("parallel", "arbitrary")),
    )(x, g, w_big, w_small, col_gain, col_flag)


def _ffn_up_body(x_ref, g_ref, w1_ref, w3_ref, o_ref, h_scr):
    @pl.when(pl.program_id(1) == 0)
    def _():
        x = x_ref[...]
        r = lax.rsqrt(jnp.mean(x * x, axis=-1, keepdims=True) + EPS)
        h_scr[...] = (x * r * g_ref[...]).astype(BF16)

    h = h_scr[...]
    a = _dot(h, w1_ref[...].astype(BF16))
    b = _dot(h, w3_ref[...].astype(BF16))
    o_ref[...] = (a * _sigmoid(a) * b).astype(BF16)


def _ffn_up(x, g, w1, w3, layer):
    m, d = x.shape
    n = w1.shape[2]
    tm = min(m, 1024)
    tn = 512
    return pl.pallas_call(
        _ffn_up_body,
        grid=(m // tm, n // tn),
        in_specs=[pl.BlockSpec((tm, d), lambda i, j: (i, 0)),
                  pl.BlockSpec((1, d), lambda i, j: (0, 0)),
                  pl.BlockSpec((None, d, tn), lambda i, j: (layer, 0, j)),
                  pl.BlockSpec((None, d, tn), lambda i, j: (layer, 0, j))],
        out_specs=pl.BlockSpec((tm, tn), lambda i, j: (i, j)),
        out_shape=jax.ShapeDtypeStruct((m, n), BF16),
        scratch_shapes=[pltpu.VMEM((tm, d), BF16)],
        compiler_params=_cparams(("parallel", "arbitrary")),
    )(x, g, w1, w3)


def _mm_res_body(*refs, n_in):
    res_ref = refs[2 * n_in]
    o_ref = refs[2 * n_in + 1]
    wb_refs = refs[2 * n_in + 2:]

    @pl.when(pl.program_id(1) == 0)
    def _():
        for w_ref, wb_ref in zip(refs[n_in:2 * n_in], wb_refs):
            wb_ref[...] = w_ref[...].astype(BF16)

    acc = res_ref[...]
    for a_ref, wb_ref in zip(refs[:n_in], wb_refs):
        acc = acc + _dot(a_ref[...], wb_ref[...])
    o_ref[...] = acc


def _mm_res(a_list, w, layer, res):
    m, n = res.shape
    k_all = sum(a.shape[1] for a in a_list)
    t = 1024 if 2 * 4 * k_all * 1024 <= W_TILE_BYTES and n % 1024 == 0 else 512
    tm, tn = min(m, t), t
    in_specs, w_specs, scratch, off = [], [], [], 0
    for a in a_list:
        k = a.shape[1]
        in_specs.append(pl.BlockSpec((tm, k), lambda j, i: (i, 0)))
        w_specs.append(pl.BlockSpec((None, k, tn), lambda j, i, _o=off // k: (layer, _o, j)))
        scratch.append(pltpu.VMEM((k, tn), BF16))
        off += k
    return pl.pallas_call(
        functools.partial(_mm_res_body, n_in=len(a_list)),
        grid=(n // tn, m // tm),
        in_specs=in_specs + w_specs + [pl.BlockSpec((tm, tn), lambda j, i: (i, j))],
        out_specs=pl.BlockSpec((tm, tn), lambda j, i: (i, j)),
        out_shape=jax.ShapeDtypeStruct((m, n), F32),
        scratch_shapes=scratch,
        compiler_params=_cparams(("parallel", "arbitrary")),
    )(*a_list, *([w] * len(a_list)), res)


def _fox_prep_body(zs_ref, b_ref, lf_ref, c_ref, carry, *, tc):
    @pl.when(pl.program_id(1) == 0)
    def _():
        carry[...] = jnp.zeros_like(carry)

    lf = _log_sigmoid(zs_ref[0] + b_ref[...])
    c = _dot3_r(_tri(tc, upper=False), lf) + carry[...]
    lf_ref[0] = lf
    c_ref[0] = c
    carry[...] = c[tc - 1:tc, :]


def _fox_prep(zs3, b_pad):
    bsz, t, _ = zs3.shape
    tc = min(t, 256)
    spec = pl.BlockSpec((1, tc, HD), lambda b, i: (b, i, 0))
    return pl.pallas_call(
        functools.partial(_fox_prep_body, tc=tc),
        grid=(bsz, t // tc),
        in_specs=[spec, pl.BlockSpec((1, HD), lambda b, i: (0, 0))],
        out_specs=[spec, spec],
        out_shape=[jax.ShapeDtypeStruct(zs3.shape, F32)] * 2,
        scratch_shapes=[pltpu.VMEM((1, HD), F32)],
        compiler_params=_cparams(("parallel", "arbitrary")),
    )(zs3, b_pad)


def _flash_body(*refs, mode, tq, tk, hpg, off):
    it = iter(refs)
    q_ref, k_ref, v_ref = next(it), next(it), next(it)
    bias_ref = next(it) if mode == "fox" else None
    sel_ref = next(it) if mode == "slc" else None
    o_ref, kb, vb = next(it), next(it), next(it)
    i = pl.program_id(2)

    @pl.when(i == 0)
    def _():
        kb[...] = k_ref[0].astype(BF16)
        vb[...] = v_ref[0].astype(BF16)

    q = q_ref[0]
    if hpg > 1:
        q = jnp.concatenate([q[:, h * HD:(h + 1) * HD] for h in range(hpg)], axis=0)
    qa = (q * (ATTN_SCALE * LOG2E)).astype(BF16)
    tile_rows = (lambda x: jnp.concatenate([x] * hpg, axis=0)) if hpg > 1 else (lambda x: x)
    q_lo = off + i * tq
    t_k = kb.shape[0]
    if mode == "slc":
        unpicked = (sel_ref[0, 0] - 1.0).astype(BF16)

    def attend(blocks):
        ss = []
        for b, (ks, size, masked) in enumerate(blocks):
            s = _dot_nt(qa, kb[pl.ds(ks, size), :])
            if mode == "fox":
                s = s - bias_ref[0, b:b + 1, :] * LOG2E
            if mode == "slc":
                cb = lax.broadcasted_iota(jnp.int32, (HD, size), 0)
                kk = lax.broadcasted_iota(jnp.int32, (HD, size), 1)
                expand = jnp.where(cb == (ks + kk) // SEL_BLOCK, 2.0 ** 100, 0.0).astype(BF16)
                s = s + tile_rows(_dot(unpicked, expand))
            if masked:
                kidx = ks + lax.broadcasted_iota(jnp.int32, (tq, size), 1)
                qpos = q_lo + lax.broadcasted_iota(jnp.int32, (tq, size), 0)
                valid = kidx <= qpos
                if mode == "win":
                    valid = valid & ((qpos - kidx) < WINDOW)
                s = jnp.where(tile_rows(valid), s, NEG)
            ss.append(s)
        m = ss[0].max(axis=-1, keepdims=True)
        for s in ss[1:]:
            m = jnp.maximum(m, s.max(axis=-1, keepdims=True))
        l, acc = 0.0, 0.0
        for s, (ks, size, _) in zip(ss, blocks):
            p = jnp.exp2(s - m)
            l = l + jnp.sum(p, axis=-1, keepdims=True)
            acc = acc + _dot(p.astype(BF16), vb[pl.ds(ks, size), :])
        o = acc / l
        for h in range(hpg):
            o_ref[0, :, h * HD:(h + 1) * HD] = o[h * tq:(h + 1) * tq].astype(o_ref.dtype)

    if mode == "win":
        span = min(t_k, -(-(WINDOW + tq) // HD) * HD + HD)
        start = jnp.clip((q_lo - (WINDOW - 1)) // HD * HD, 0, t_k - span)
        attend([(pl.multiple_of(start, HD), span, True)])
    else:
        n_max = t_k // tk
        need = (q_lo + tq - 1) // tk + 1
        for nb in range(1, n_max + 1):
            @pl.when(need == nb)
            def _(nb=nb):
                attend([(b * tk, tk, b == nb - 1) for b in range(nb)])


def _flash(mode, q_arr, q_col, k_arr, k_col, v_arr, v_col, *, n_kv, hpg, tq, tk=None, off=0,
           bias=None, sel=None, out_dtype=F32):
    bsz, t_q = q_arr.shape[0], q_arr.shape[1]
    t_k = k_arr.shape[1]
    nq = t_q // tq
    w = hpg * HD
    in_specs = [pl.BlockSpec((1, tq, w), lambda b, g, i: (b, i, q_col // w + g)),
                pl.BlockSpec((1, t_k, HD), lambda b, g, i: (b, 0, k_col // HD + g)),
                pl.BlockSpec((1, t_k, HD), lambda b, g, i: (b, 0, v_col // HD + g))]
    args = [q_arr, k_arr, v_arr]
    if mode == "fox":
        in_specs.append(pl.BlockSpec((1, t_k // tk, tk), lambda b, g, i: (b * n_kv + g, 0, 0)))
        args.append(bias)
    if mode == "slc":
        in_specs.append(pl.BlockSpec((1, 1, tq, HD), lambda b, g, i: (b, g, i, 0)))
        args.append(sel)
    return pl.pallas_call(
        functools.partial(_flash_body, mode=mode, tq=tq, tk=tk, hpg=hpg, off=off),
        grid=(bsz, n_kv, nq),
        in_specs=in_specs,
        out_specs=pl.BlockSpec((1, tq, w), lambda b, g, i: (b, i, g)),
        out_shape=jax.ShapeDtypeStruct((bsz, t_q, n_kv * w), out_dtype),
        scratch_shapes=[pltpu.VMEM((t_k, HD), BF16), pltpu.VMEM((t_k, HD), BF16)],
        compiler_params=_cparams(("parallel", "parallel", "arbitrary")),
    )(*args)


def _compress_body(*refs, paged, pg, n_steps, mn, norms, t_new):
    it = iter(refs)
    if paged:
        next(it)
    n_t = len(norms)
    x_all = [[next(it) for _ in range(pg if paged else G_NSA)] for _ in range(n_t)]
    new_all = [next(it) for _ in range(n_t)] if paged else None
    par_all = [(next(it), next(it), next(it)) for _ in range(n_t)]
    o_all = [next(it) for _ in range(n_t)]
    hcat, tb, perm = next(it), next(it), next(it)
    p = pl.program_id(1)
    mh = mn + 8
    per_page = PAGE // CMP_STRIDE
    n_real = n_steps * pg * per_page
    n_g = G_NSA if paged else 1

    @pl.when(p == 0)
    def _():
        hcat[:, n_real:, :] = jnp.zeros((n_t * G_NSA, mh - n_real, CMP_STRIDE * HD), F32)
        d = lax.broadcasted_iota(jnp.int32, perm.shape, 0)
        s = lax.broadcasted_iota(jnp.int32, perm.shape, 1)
        src = (CMP_STRIDE * (d % per_page) + (d % PAGE) // per_page) * n_g + d // PAGE
        perm[...] = jnp.where(s == src, 1.0, 0.0).astype(BF16)

    for ti in range(n_t):
        for pp in range(pg):
            r0 = pl.multiple_of((p * pg + pp) * per_page, per_page)
            for x_ref, g0 in ([(x_all[ti][pp], 0)] if paged else [(x_all[ti][g], g) for g in range(G_NSA)]):
                rows = _dot(perm[...], x_ref[0].astype(BF16))
                for g in range(n_g):
                    for l in range(CMP_STRIDE):
                        a = g * PAGE + l * per_page
                        hcat[ti * G_NSA + g0 + g, pl.ds(r0, per_page), l * HD:(l + 1) * HD] = rows[a:a + per_page]

    @pl.when(p == n_steps - 1)
    def _():
        for ti in range(n_t):
            pe_ref, w_ref, gain_ref = par_all[ti]
            if paged:
                xn = new_all[ti][0]
                rid = lax.broadcasted_iota(jnp.int32, xn.shape, 0)
                xn = jnp.where(rid < t_new, xn, 0.0)
                for g in range(G_NSA):
                    for l in range(8):
                        hcat[ti * G_NSA + g, n_real:n_real + 1, l * HD:(l + 1) * HD] = xn[l:l + 1, g * HD:(g + 1) * HD]
            w = w_ref[...]
            pe2 = _dot(pe_ref[...].astype(BF16), w)
            pe_bias = pe2[0:1, 0:HD] + pe2[1:2, HD:2 * HD]
            for g in range(G_NSA):
                tb[...] = _dot(hcat[ti * G_NSA + g].astype(BF16), w)
                y = tb[0:mn, 0:HD] + tb[1:mn + 1, HD:2 * HD] + pe_bias
                if norms[ti]:
                    y = y * lax.rsqrt(jnp.mean(y * y, axis=-1, keepdims=True) + EPS) * gain_ref[...]
                o_all[ti][0, :, g * HD:(g + 1) * HD] = y


def _compress(x_arr, x_cols, params, norms, *, mn, page_table=None, pools=None, pool_off=0, t_new=0, pg=32):
    paged = page_table is not None
    bsz = x_arr.shape[0]
    n_t = len(norms)
    mh = mn + 8
    w2 = G_NSA * HD
    const = lambda shape: pl.BlockSpec(shape, (lambda b, p, *_: (0,) * len(shape)))
    tail = [const((8, CMP_STRIDE * HD)), const((CMP_STRIDE * HD, 2 * HD)), const((1, HD))] * n_t
    tail_args = [a for par in params for a in par]
    out_specs = [pl.BlockSpec((1, mn, w2), lambda b, p, *_: (b, 0, 0))] * n_t
    n_perm = G_NSA * PAGE if paged else PAGE
    scratch = [pltpu.VMEM((n_t * G_NSA, mh, CMP_STRIDE * HD), F32), pltpu.VMEM((mh, 2 * HD), F32),
               pltpu.VMEM((n_perm, n_perm), BF16)]
    out_shape = [jax.ShapeDtypeStruct((bsz, mn, w2), F32)] * n_t
    if not paged:
        n_steps = x_arr.shape[1] // PAGE
        body = functools.partial(_compress_body, paged=False, pg=1, n_steps=n_steps, mn=mn, norms=norms, t_new=0)
        return pl.pallas_call(
            body, grid=(bsz, n_steps),
            in_specs=[pl.BlockSpec((1, PAGE, HD), lambda b, p, _c=c // HD + g: (b, p, _c))
                      for c in x_cols for g in range(G_NSA)] + tail,
            out_specs=out_specs, out_shape=out_shape, scratch_shapes=scratch,
            compiler_params=_cparams(("parallel", "arbitrary")),
        )(*([x_arr] * (n_t * G_NSA)), *tail_args)
    pg = min(pg, page_table.shape[1])
    n_steps = page_table.shape[1] // pg
    body = functools.partial(_compress_body, paged=True, pg=pg, n_steps=n_steps, mn=mn, norms=norms, t_new=t_new)
    grid_spec = pltpu.PrefetchScalarGridSpec(
        num_scalar_prefetch=1, grid=(bsz, n_steps),
        in_specs=[pl.BlockSpec((1, G_NSA * PAGE, HD), lambda b, p, pt, _pp=pp: (pool_off + pt[b, p * pg + _pp], 0, 0))
                  for _ in range(n_t) for pp in range(pg)]
        + [pl.BlockSpec((1, 8, w2), lambda b, p, pt, _c=c // w2: (b, 0, _c)) for c in x_cols] + tail,
        out_specs=out_specs, scratch_shapes=scratch)
    return pl.pallas_call(body, grid_spec=grid_spec, out_shape=out_shape,
                          compiler_params=_cparams(("parallel", "arbitrary")))(
        page_table, *[pool for pool in pools for _ in range(pg)], *([x_arr] * n_t), *tail_args)


def _cmpsel_body(q_ref, kc_ref, vc_ref, o_ref, sel_ref, *, tq, ncp, nsp, n_sel, pos0, gs):
    i = pl.program_id(2)
    q = q_ref[0]
    pos = pos0 + i * tq + lax.broadcasted_iota(jnp.int32, (tq, ncp), 0)
    col = lax.broadcasted_iota(jnp.int32, (tq, ncp), 1)
    valid = (col * CMP_STRIDE + (CMP_BLOCK - 1)) <= pos
    validf = jnp.where(valid, 1.0, 0.0)
    imps = []
    for g in range(gs):
        kc = kc_ref[0, :, g * HD:(g + 1) * HD].astype(BF16)
        vc = vc_ref[0, :, g * HD:(g + 1) * HD].astype(BF16)
        imp = jnp.zeros((tq, ncp), F32)
        for h in range(g * HPG, (g + 1) * HPG):
            qh = (q[:, h * HD:(h + 1) * HD] * ATTN_SCALE).astype(BF16)
            s = jnp.where(valid, _dot_nt(qh, kc), NEG)
            e = jnp.exp(s - jnp.max(s, axis=-1, keepdims=True)) * validf
            p = e / jnp.maximum(jnp.sum(e, axis=-1, keepdims=True), 1e-30)
            o_ref[0, :, h * HD:(h + 1) * HD] = _dot(p.astype(BF16), vc)
            imp = imp + p
        imps.append(imp)
    imp = jnp.concatenate(imps, axis=0)
    cc = lax.broadcasted_iota(jnp.int32, (ncp, nsp), 0)
    jj = lax.broadcasted_iota(jnp.int32, (ncp, nsp), 1)
    r = SEL_BLOCK // CMP_STRIDE
    gather = jnp.where((cc >= r * jj - 1) & (cc <= r * jj + r - 1), 1.0, 0.0).astype(BF16)
    score = _dot3_l(imp, gather)
    rows = gs * tq
    blk = lax.broadcasted_iota(jnp.int32, (rows, nsp), 1)
    pq = pos0 + i * tq + lax.broadcasted_iota(jnp.int32, (rows, nsp), 0) % tq
    forced = (blk == pq // SEL_BLOCK) | (blk == 0)
    future = blk * SEL_BLOCK > pq
    score = jnp.where(forced, FORCE_SCORE, jnp.where(future, -1.0, score))
    score = jnp.where(blk < n_sel, score, -2.0)
    if rows % HD == 0:
        n8 = -(-n_sel // 8) * 8
        st = score.T[:n8]
        bt = lax.broadcasted_iota(jnp.int32, (n8, rows), 0)
        rank = jnp.zeros((n8, rows), F32)
        for c in range(n_sel):
            sc = st[c:c + 1, :]
            ahead = (sc > st) | ((sc == st) & (bt > c))
            rank = rank + jnp.where(ahead, 1.0, 0.0)
        keep = jnp.where(rank < float(N_SELECT), 1.0, 0.0)
        sel = jnp.concatenate([keep, jnp.zeros((nsp - n8, rows), F32)], axis=0).T
    else:
        rank = jnp.zeros((rows, nsp), F32)
        for c in range(n_sel):
            sc = score[:, c:c + 1]
            ahead = (sc > score) | ((sc == score) & (blk > c))
            rank = rank + jnp.where(ahead, 1.0, 0.0)
        sel = jnp.where(rank < float(N_SELECT), 1.0, 0.0)
    for g in range(gs):
        sel_ref[0, g] = sel[g * tq:(g + 1) * tq]


def _cmpsel(z3, kcmp, vcmp, *, tq, pos0, n_sel, gs):
    bsz, t, _ = z3.shape
    ncp = kcmp.shape[1]
    nsp = -(-n_sel // HD) * HD
    w = gs * HPG * HD
    return pl.pallas_call(
        functools.partial(_cmpsel_body, tq=tq, ncp=ncp, nsp=nsp, n_sel=n_sel, pos0=pos0, gs=gs),
        grid=(bsz, G_NSA // gs, t // tq),
        in_specs=[pl.BlockSpec((1, tq, w), lambda b, g, i: (b, i, NQ // w + g)),
                  pl.BlockSpec((1, ncp, gs * HD), lambda b, g, i: (b, 0, g)),
                  pl.BlockSpec((1, ncp, gs * HD), lambda b, g, i: (b, 0, g))],
        out_specs=[pl.BlockSpec((1, tq, w), lambda b, g, i: (b, i, g)),
                   pl.BlockSpec((1, gs, tq, nsp), lambda b, g, i: (b, g, i, 0))],
        out_shape=[jax.ShapeDtypeStruct((bsz, t, H_NSA * HD), F32),
                   jax.ShapeDtypeStruct((bsz, G_NSA, t, nsp), F32)],
        compiler_params=_cparams(("parallel", "parallel", "parallel")),
    )(z3, kcmp, vcmp)


def _combine_body(oc_ref, os_ref, ow_ref, zs_ref, gb_ref, o_ref):
    gates = _sigmoid(zs_ref[...] + gb_ref[...])
    for h in range(H_NSA):
        sl = slice(h * HD, (h + 1) * HD)
        c0 = H_FOX + 3 * h
        o = (gates[:, c0:c0 + 1] * oc_ref[:, sl] + gates[:, c0 + 1:c0 + 2] * os_ref[:, sl]
             + gates[:, c0 + 2:c0 + 3] * ow_ref[:, sl])
        o_ref[:, sl] = o.astype(BF16)


def _nsa_combine(oc, osl, ow, zs, gb_pad):
    m, n = oc.shape
    tm = min(m, 512)
    big = pl.BlockSpec((tm, n), lambda i: (i, 0))
    return pl.pallas_call(
        _combine_body, grid=(m // tm,),
        in_specs=[big, big, big, pl.BlockSpec((tm, HD), lambda i: (i, 0)), pl.BlockSpec((1, HD), lambda i: (0, 0))],
        out_specs=big, out_shape=jax.ShapeDtypeStruct((m, n), BF16),
        compiler_params=_cparams(("parallel",)),
    )(oc, osl, ow, zs, gb_pad)


def _paged_body(*refs, mode, pg, n_steps, n_w, hpg_kv, t_new):
    it = iter(refs)
    next(it)
    q_ref = next(it)
    k_refs = [next(it) for _ in range(pg)]
    v_refs = [next(it) for _ in range(pg)]
    cl_refs = [next(it) for _ in range(pg)] if mode == "fox" else None
    tf_refs = [next(it) for _ in range(pg)] if mode == "fox" else None
    kn_ref, vn_ref = next(it), next(it)
    lfn_ref = next(it) if mode == "fox" else None
    sel_ref, seln_ref = (next(it), next(it)) if mode == "slc" else (None, None)
    o_ref = next(it)
    qb, m_scr, l_scr, acc, kn_scr, vn_scr, carry = (next(it) for _ in range(7))
    p = pl.program_id(1)
    rows = 4 * 8
    r2 = n_w * PAGE
    n_new = t_new * n_w

    def kv_match(ncols):
        row = lax.broadcasted_iota(jnp.int32, (rows, ncols), 0)
        col = lax.broadcasted_iota(jnp.int32, (rows, ncols), 1)
        return row, col, (col % n_w) == (row % 8) // hpg_kv

    @pl.when(p == 0)
    def _():
        qb[...] = (q_ref[0] * ATTN_SCALE).astype(BF16)
        m_scr[...] = jnp.full(m_scr.shape, NEG, F32)
        l_scr[...] = jnp.zeros(l_scr.shape, F32)
        acc[...] = jnp.zeros(acc.shape, F32)
        carry[...] = jnp.zeros(carry.shape, F32)
        kn_scr[...] = jnp.zeros(kn_scr.shape, BF16)
        vn_scr[...] = jnp.zeros(vn_scr.shape, BF16)

    def picked(choice, first_lane, ncols):
        half = (lax.broadcasted_iota(jnp.int32, (rows, ncols), 1) // n_w) // SEL_BLOCK
        out = None
        for j in range(PAGE // SEL_BLOCK):
            term = (half == j) & (choice[:, first_lane + j:first_lane + j + 1] > 0.5)
            out = term if out is None else (out | term)
        return out

    def update(s_list, vb_list):
        m_old = m_scr[...]
        m_new = m_old
        for s in s_list:
            m_new = jnp.maximum(m_new, jnp.max(s, axis=-1, keepdims=True))
        a = jnp.exp(m_old - m_new)
        l = a * l_scr[...]
        o = a * acc[...]
        for s, vb in zip(s_list, vb_list):
            pr = jnp.exp(s - m_new)
            l = l + jnp.sum(pr, axis=-1, keepdims=True)
            o = o + _dot(pr.astype(BF16), vb)
        m_scr[...] = m_new
        l_scr[...] = l
        acc[...] = o

    _, _, match = kv_match(r2)
    s_list, vb_list = [], []
    c_run = carry[...] if mode == "fox" else None
    for pp in range(pg):
        s = _dot_nt(qb[...], k_refs[pp][0].astype(BF16))
        valid = match
        if mode == "fox":
            s = s - (c_run + cl_refs[pp][0])
            c_run = c_run + tf_refs[pp][0]
        if mode == "slc":
            valid = valid & picked(sel_ref[0, 0], pp * (PAGE // SEL_BLOCK), r2)
        s_list.append(jnp.where(valid, s, NEG))
        vb_list.append(v_refs[pp][0].astype(BF16))
    if mode == "fox":
        carry[...] = c_run
    update(s_list, vb_list)

    @pl.when(p == n_steps - 1)
    def _():
        kn_scr[0:n_new, :] = kn_ref[0].astype(BF16)
        vn_scr[0:n_new, :] = vn_ref[0].astype(BF16)
        s = _dot_nt(qb[...], kn_scr[...])
        row, col, valid = kv_match(HD)
        valid = valid & (col // n_w <= row // 8) & (col < n_new)
        if mode == "fox":
            rr = lax.broadcasted_iota(jnp.int32, (HD, HD), 0)
            cc = lax.broadcasted_iota(jnp.int32, (HD, HD), 1)
            pre = jnp.where((rr % n_w == cc % n_w) & (rr // n_w <= cc // n_w), 1.0, 0.0).astype(BF16)
            c_new = _dot3_l(jnp.broadcast_to(lfn_ref[0], (8, HD)), pre)[0:1, :] + carry[:, 0:HD]
            s = s - c_new
        if mode == "slc":
            valid = valid & picked(seln_ref[0], 0, HD)
        update([jnp.where(valid, s, NEG)], [vn_scr[...]])
        o_ref[0] = acc[...] / l_scr[...]


def _paged_attn(mode, q_rows, page_table, k_pool, v_pool, pool_off, k_new, v_new, *, hpg_kv, pg,
                c_local=None, c_total=None, lf_new=None, sel_rows=None):
    bsz = q_rows.shape[0]
    n_pages = page_table.shape[1]
    pg = min(pg, n_pages)
    n_steps = n_pages // pg
    r2 = k_pool.shape[1]
    n_w = r2 // PAGE
    n_new = k_new.shape[1]
    page = lambda pp: (lambda b, p, pt: (pool_off + pt[b, p * pg + pp], 0, 0))
    per_b = lambda b, p, pt: (b, 0, 0)
    in_specs = [pl.BlockSpec((1, 32, HD), per_b)]
    in_specs += [pl.BlockSpec((1, r2, HD), page(pp)) for pp in range(pg)] * 2
    args = [q_rows] + [k_pool] * pg + [v_pool] * pg
    if mode == "fox":
        in_specs += [pl.BlockSpec((1, 1, r2), page(pp)) for pp in range(pg)] * 2
        args += [c_local] * pg + [c_total] * pg
    in_specs += [pl.BlockSpec((1, n_new, HD), per_b)] * 2
    args += [k_new, v_new]
    if mode == "fox":
        in_specs.append(pl.BlockSpec((1, 1, HD), per_b))
        args.append(lf_new)
    if mode == "slc":
        per_page = PAGE // SEL_BLOCK
        past = sel_rows[:, :, :n_pages * per_page].reshape(bsz, 32, n_steps, pg * per_page)
        sel_steps = _pad_cols(jnp.swapaxes(past, 1, 2).reshape(bsz * n_steps * 32, pg * per_page), HD)
        in_specs += [pl.BlockSpec((1, 1, 32, HD), lambda b, p, pt: (b, p, 0, 0)), pl.BlockSpec((1, 32, HD), per_b)]
        args += [sel_steps.reshape(bsz, n_steps, 32, HD),
                 _pad_cols(sel_rows[:, :, n_pages * per_page:n_pages * per_page + 1].reshape(bsz * 32, 1), HD)
                 .reshape(bsz, 32, HD)]
    grid_spec = pltpu.PrefetchScalarGridSpec(
        num_scalar_prefetch=1, grid=(bsz, n_steps), in_specs=in_specs,
        out_specs=pl.BlockSpec((1, 32, HD), per_b),
        scratch_shapes=[pltpu.VMEM((32, HD), BF16), pltpu.VMEM((32, 1), F32), pltpu.VMEM((32, 1), F32),
                        pltpu.VMEM((32, HD), F32), pltpu.VMEM((PAGE, HD), BF16), pltpu.VMEM((PAGE, HD), BF16),
                        pltpu.VMEM((1, r2), F32)])
    return pl.pallas_call(
        functools.partial(_paged_body, mode=mode, pg=pg, n_steps=n_steps, n_w=n_w, hpg_kv=hpg_kv, t_new=n_new // n_w),
        grid_spec=grid_spec, out_shape=jax.ShapeDtypeStruct((bsz, 32, HD), F32),
        compiler_params=_cparams(("parallel", "arbitrary")),
    )(page_table, *args)


def _pool_prefix_body(x_ref, cl_ref, ct_ref, w_scr, *, n_w):
    r2 = x_ref.shape[1]

    @pl.when(pl.program_id(0) == 0)
    def _():
        r = lax.broadcasted_iota(jnp.int32, (r2, 2 * r2), 0)
        c = lax.broadcasted_iota(jnp.int32, (r2, 2 * r2), 1)
        same = (r % n_w) == (c % n_w)
        local = same & (r // n_w <= c // n_w) & (c < r2)
        w_scr[...] = jnp.where(local | (same & (c >= r2)), 1.0, 0.0).astype(BF16)

    y = _dot3_l(x_ref[...], w_scr[...])
    cl_ref[...] = y[:, :r2]
    ct_ref[...] = y[:, r2:]


def _pool_prefix(x, n_w):
    n, r2 = x.shape
    tm = next((c for c in (256, 128, 64, 32, 16, 8) if n % c == 0), n)
    spec = pl.BlockSpec((tm, r2), lambda i: (i, 0))
    return pl.pallas_call(
        functools.partial(_pool_prefix_body, n_w=n_w), grid=(n // tm,),
        in_specs=[spec], out_specs=[spec, spec],
        out_shape=[jax.ShapeDtypeStruct((n, r2), F32)] * 2,
        scratch_shapes=[pltpu.VMEM((r2, 2 * r2), BF16)],
        compiler_params=_cparams(("arbitrary",)),
    )(x)


def _gla_body(*refs, mode, t_in, chunk, dv, t_real, has_state, hb):
    it = iter(refs)
    q_ref, k_ref, v_ref, og_ref = next(it), next(it), next(it), next(it)
    if mode == "gla":
        zs_ref, wa_ref, ba_ref = next(it), next(it), next(it)
    else:
        lb_ref = next(it)
    gain_ref = next(it)
    s0_ref = next(it) if has_state else None
    o_ref, s_ref, st_scr, q_scr, k_scr, g_scr, o_scr = (next(it) for _ in range(7))
    tb = q_scr.shape[1]
    n_chunks = tb // chunk
    pad = tb - t_in
    tri = lax.broadcasted_iota(jnp.int32, (chunk, chunk), 0) >= lax.broadcasted_iota(jnp.int32, (chunk, chunk), 1)
    if mode == "hg":
        lbr = lb_ref[...]
        e = jnp.exp(lbr - jnp.max(lbr, axis=0, keepdims=True))
        lb_all = (e / jnp.sum(e, axis=0, keepdims=True))[0:1, :]
    ti = pl.program_id(2)

    @pl.when(ti == 0)
    def _():
        for h in range(hb):
            st_scr[h] = s0_ref[0, h].T if has_state else jnp.zeros((dv, HD), F32)

    def block(ref, sl):
        x = ref[0, :, sl]
        if pad:
            x = jnp.concatenate([x, jnp.zeros((pad, x.shape[1]), F32)], axis=0)
        return x

    live = (ti * t_in + lax.broadcasted_iota(jnp.int32, (tb, HD), 0)) < t_real
    if mode == "gla":
        ga = block(zs_ref, slice(0, HD)).astype(BF16)
    for h in range(hb):
        ks = slice(h * HD, (h + 1) * HD)
        qr, kr = block(q_ref, ks), block(k_ref, ks)
        if mode == "gla":
            g = _log_sigmoid(_dot(ga, wa_ref[:, ks]) + ba_ref[:, ks]) / GLA_GATE_NORM
            q, k = qr, kr
        else:
            lb = lb_all[:, ks]
            f = lb + (1.0 - lb) * _sigmoid(kr)
            q, k, g = qr * _sigmoid(qr), 1.0 - f, jnp.log(f)
        q_scr[h] = q
        k_scr[h] = jnp.where(live, k, 0.0)
        g_scr[h] = jnp.where(live, g, 0.0)

    def body(c, _):
        r0 = pl.multiple_of(c * chunk, chunk)
        rows = pl.ds(r0, chunk)
        for h in range(hb):
            vs = slice(h * dv, (h + 1) * dv)
            q, k, g = q_scr[h, rows, :], k_scr[h, rows, :], g_scr[h, rows, :]
            if pad:
                v = block(v_ref, vs)
            else:
                v = v_ref[0, rows, vs]
            b = _cumsum_rows(g)
            bm = b[chunk // 2 - 1:chunk // 2, :]
            bl = b[chunk - 1:chunk, :]
            qe = (q * jnp.exp(jnp.minimum(b - bm, EXP_CLAMP))).astype(BF16)
            ke = (k * jnp.exp(jnp.minimum(bm - b, EXP_CLAMP))).astype(BF16)
            a = jnp.where(tri, _dot_nt(qe, ke), 0.0)
            vb = v.astype(BF16)
            st = st_scr[h]
            o_scr[h, rows, :] = _dot(a.astype(BF16), vb) + _dot_nt((q * jnp.exp(b)).astype(BF16), st.astype(BF16))
            kd = (k * jnp.exp(bl - b)).astype(BF16)
            st_scr[h] = st * jnp.exp(bl) + _dot_tn(vb, kd)
        return 0

    lax.fori_loop(0, n_chunks, body, 0)

    for h in range(hb):
        vs = slice(h * dv, (h + 1) * dv)
        o = o_scr[h, 0:t_in, :]
        og = og_ref[0, :, vs]
        o = o * lax.rsqrt(jnp.mean(o * o, axis=-1, keepdims=True) + EPS) * gain_ref[...]
        o_ref[0, :, vs] = (o * (og * _sigmoid(og) if mode == "gla" else _sigmoid(og))).astype(BF16)

    @pl.when(ti == pl.num_programs(2) - 1)
    def _():
        for h in range(hb):
            s_ref[0, h] = st_scr[h].T


def _gla(mode, z3, zs3, cols, n_heads, dv, extra, gain, s0, *, t_real, chunk, hb):
    bsz, t_all, _ = z3.shape
    tc = min(t_all, 512)
    qc, kc, vc, oc = cols
    has_state = s0 is not None
    wk, wv = hb * HD, hb * dv
    in_specs = [pl.BlockSpec((1, tc, wk), lambda b, h, t: (b, t, qc // wk + h)),
                pl.BlockSpec((1, tc, wk), lambda b, h, t: (b, t, kc // wk + h)),
                pl.BlockSpec((1, tc, wv), lambda b, h, t: (b, t, vc // wv + h)),
                pl.BlockSpec((1, tc, wv), lambda b, h, t: (b, t, oc // wv + h))]
    args = [z3, z3, z3, z3]
    if mode == "gla":
        wa, ba = extra
        in_specs += [pl.BlockSpec((1, tc, HD), lambda b, h, t: (b, t, 0)),
                     pl.BlockSpec((HD, wk), lambda b, h, t: (0, h)),
                     pl.BlockSpec((1, wk), lambda b, h, t: (0, h))]
        args += [zs3, wa, ba]
    else:
        in_specs.append(pl.BlockSpec((8, wk), lambda b, h, t: (0, h)))
        args.append(extra)
    in_specs.append(pl.BlockSpec((1, dv), lambda b, h, t: (0, 0)))
    args.append(gain)
    st_spec = pl.BlockSpec((1, hb, HD, dv), lambda b, h, t: (b, h, 0, 0))
    if has_state:
        in_specs.append(st_spec)
        args.append(s0)
    return pl.pallas_call(
        functools.partial(_gla_body, mode=mode, t_in=tc, chunk=chunk, dv=dv, t_real=t_real, has_state=has_state,
                          hb=hb),
        grid=(bsz, n_heads // hb, t_all // tc),
        in_specs=in_specs,
        out_specs=[pl.BlockSpec((1, tc, wv), lambda b, h, t: (b, t, h)), st_spec],
        out_shape=[jax.ShapeDtypeStruct((bsz, t_all, n_heads * dv), BF16),
                   jax.ShapeDtypeStruct((bsz, n_heads, HD, dv), F32)],
        scratch_shapes=[pltpu.VMEM((hb, dv, HD), F32)] + [pltpu.VMEM((hb, max(tc, chunk), HD), F32)] * 3
        + [pltpu.VMEM((hb, max(tc, chunk), dv), F32)],
        compiler_params=_cparams(("parallel", "parallel", "arbitrary")),
    )(*args)


def _split_cols(w, sizes):
    outs, off = [], 0
    for s in sizes:
        outs.append(w[:, off:off + s])
        off += s
    return outs


def _pad_cols(w, n):
    return jnp.pad(w, ((0, 0), (0, n - w.shape[1])))


def _row(v):
    return v.reshape(1, -1).astype(F32)


def _even_weights(w_in, fq_gain, fk_gain, nq_gain, nk_gain):
    fq, fk, fv, ff, nq, kc, vc, ks, vs, kw, vw, ng = _split_cols(w_in, EVEN_SIZES)
    big = jnp.concatenate([fq, fk, fv, nq, kc, vc, ks, vs, kw, vw], axis=1).astype(BF16)
    small = _pad_cols(jnp.concatenate([ff, ng], axis=1), HD).astype(BF16)
    ones = lambda n: jnp.ones((n,), F32)
    gain = jnp.concatenate([jnp.tile(fq_gain, H_FOX), jnp.tile(fk_gain, H_FOX), ones(H_FOX * HD),
                            jnp.tile(nq_gain, H_NSA), ones(2 * G_NSA * HD), jnp.tile(nk_gain[1], G_NSA),
                            ones(G_NSA * HD), jnp.tile(nk_gain[2], G_NSA), ones(G_NSA * HD)])
    z, o = jnp.zeros, jnp.ones
    flag = jnp.concatenate([o((2 * H_FOX * HD,), F32), z((H_FOX * HD,), F32), o((H_NSA * HD,), F32),
                            z((2 * G_NSA * HD,), F32), o((G_NSA * HD,), F32), z((G_NSA * HD,), F32),
                            o((G_NSA * HD,), F32), z((G_NSA * HD,), F32)])
    return big, small, _row(gain), _row(flag)


def _odd_weights(w_in):
    gq, gk, gv, ga, gg, hq, hf, hi, hg = _split_cols(w_in, ODD_SIZES)
    big = jnp.concatenate([gq, gk, gv, gg, hq, hf, hi, hg], axis=1).astype(BF16)
    small = _pad_cols(ga, HD).astype(BF16)
    gain = jnp.concatenate([jnp.full((H_GLA * DK_GLA,), DK_GLA ** -0.5, F32), jnp.ones((N_BIG_ODD - H_GLA * DK_GLA,), F32)])
    return big, small, _row(gain), jnp.zeros((1, N_BIG_ODD), F32)


def _cmp_weights(pe, w):
    half = CMP_STRIDE * HD
    wcat = jnp.concatenate([w[:half], w[half:]], axis=1).astype(BF16)
    pe8 = jnp.pad(pe.reshape(2, half), ((0, 6), (0, 0)))
    return pe8, wcat


def _ffn(x, g, w1, w3, w2, layer):
    return _mm_res([_ffn_up(x, g, w1, w3, layer)], w2, layer, x)


def _even_layer(x, bsz, t, t_real, ew, past):
    (w_big, w_small, cgain, cflag, g_mix, w_out, b_f_pad, gb_pad, pe_k, wc_k, pe_v, wc_v, nk0) = ew
    z, zs = _proj(x, g_mix, w_big, w_small, cgain, cflag, True)
    z3 = z.reshape(bsz, t, N_BIG_EVEN)
    zs3 = zs.reshape(bsz, t, HD)
    cmp_params = ((pe_k, wc_k, nk0), (pe_v, wc_v, jnp.ones((1, HD), F32)))
    if past is None:
        lf, c = _fox_prep(zs3, b_f_pad)
        logf_new = lf[:, :, :H_FOX]
        tk = min(t, 512)
        bias = jnp.swapaxes(c[:, :, :H_FOX], 1, 2).reshape(bsz * H_FOX, t // tk, tk)
        o_fox = _flash("fox", z3, FQ, z3, FK, z3, FV, n_kv=H_FOX, hpg=1, tq=min(t, 512), tk=tk, bias=bias,
                       out_dtype=BF16)
        n_cmp = t // CMP_STRIDE
        mn = -(-n_cmp // HD) * HD
        kcmp, vcmp = _compress(z3, (KC, VC), cmp_params, (True, False), mn=mn)
        tqn = min(t, 1024 // HPG)
        o_cmp, sel = _cmpsel(z3, kcmp, vcmp, tq=min(t, 256), pos0=0, n_sel=-(-t // SEL_BLOCK), gs=1)
        o_slc = _flash("slc", z3, NQ, z3, KS, z3, VS, n_kv=G_NSA, hpg=HPG, tq=tqn, tk=tk, sel=sel)
        o_win = _flash("win", z3, NQ, z3, KW, z3, VW, n_kv=G_NSA, hpg=HPG, tq=tqn)
        w_buf = WINDOW
        kw_new = jnp.concatenate([jnp.zeros((bsz, w_buf, G_NSA * HD), F32), z3[:, :, KW:KW + G_NSA * HD]], axis=1)[:, -w_buf:]
        vw_new = jnp.concatenate([jnp.zeros((bsz, w_buf, G_NSA * HD), F32), z3[:, :, VW:VW + G_NSA * HD]], axis=1)[:, -w_buf:]
        o_fox = o_fox.reshape(bsz * t, H_FOX * HD)
    else:
        (page_table, pool_off, fk_pool, fv_pool, lf_pool, kc_pool, vc_pool, ks_pool, vs_pool, buf_k, buf_v) = past
        p_len = page_table.shape[1] * PAGE
        lf_all, _ = _fox_prep(zs.reshape(1, bsz * t, HD), b_f_pad)
        lf_new = lf_all.reshape(bsz, t, HD)[:, :t_real, :H_FOX]
        logf_new = lf_new
        lfn_pad = _pad_cols(lf_new.reshape(bsz, t_real * H_FOX), HD).reshape(bsz, 1, HD)
        c_local, c_total = _pool_prefix(lf_pool, H_FOX)
        rows_of = lambda c0, n: z3[:, :t_real, c0:c0 + n * HD].reshape(bsz, t_real * n, HD)
        o_fox = _paged_attn("fox", rows_of(FQ, H_FOX), page_table, fk_pool, fv_pool, pool_off,
                            rows_of(FK, H_FOX), rows_of(FV, H_FOX), hpg_kv=1, pg=16,
                            c_local=c_local[:, None, :], c_total=c_total[:, None, :], lf_new=lfn_pad)
        o_fox = o_fox.reshape(bsz, t_real, H_FOX * HD)
        n_cmp = -(-(p_len + t_real) // CMP_STRIDE)
        mn = -(-n_cmp // HD) * HD
        kcmp, vcmp = _compress(z3, (KC, VC), cmp_params, (True, False), mn=mn, page_table=page_table,
                               pools=(kc_pool, vc_pool), pool_off=pool_off, t_new=t_real)
        n_sel = -(-(p_len + t_real) // SEL_BLOCK)
        o_cmp, sel = _cmpsel(z3, kcmp, vcmp, tq=t, pos0=p_len, n_sel=n_sel, gs=G_NSA)
        nsp = sel.shape[-1]
        sel_rows = jnp.broadcast_to(jnp.swapaxes(sel[:, :, :t_real], 1, 2)[:, :, :, None, :],
                                    (bsz, t_real, G_NSA, HPG, nsp)).reshape(bsz, 32, nsp)
        o_slc = _paged_attn("slc", rows_of(NQ, H_NSA), page_table, ks_pool, vs_pool, pool_off,
                            rows_of(KS, G_NSA), rows_of(VS, G_NSA), hpg_kv=HPG, pg=32,
                            sel_rows=sel_rows).reshape(bsz, t_real, H_NSA * HD)
        w_buf = buf_k.shape[1]
        lk = -(-(w_buf + t) // HD) * HD
        fill = jnp.zeros((bsz, lk - w_buf - t_real, G_NSA * HD), F32)
        kw_all = jnp.concatenate([buf_k, z3[:, :t_real, KW:KW + G_NSA * HD], fill], axis=1)
        vw_all = jnp.concatenate([buf_v, z3[:, :t_real, VW:VW + G_NSA * HD], fill], axis=1)
        kw_new, vw_new = kw_all[:, t_real:t_real + w_buf], vw_all[:, t_real:t_real + w_buf]
        o_win = _flash("win", z3, NQ, kw_all, 0, vw_all, 0, n_kv=G_NSA, hpg=HPG, tq=t, off=w_buf)
        padt = lambda a: jnp.pad(a, ((0, 0), (0, t - t_real), (0, 0)))
        o_fox = padt(o_fox).astype(BF16).reshape(bsz * t, H_FOX * HD)
        o_slc = padt(o_slc)
    m = bsz * t
    o_nsa = _nsa_combine(o_cmp.reshape(m, -1), o_slc.reshape(m, -1), o_win.reshape(m, -1), zs, gb_pad)
    x = _mm_res([o_fox, o_nsa], w_out, 0, x)
    zr = z3[:, :t_real]
    grp = lambda c0: zr[:, :, c0:c0 + G_NSA * HD].reshape(1, bsz, t_real, G_NSA, HD)
    fox = lambda c0: zr[:, :, c0:c0 + H_FOX * HD].reshape(1, bsz, t_real, H_FOX, HD)
    w_rows = kw_new.shape[1]
    outs = (fox(FK), fox(FV), logf_new[None], grp(KC), grp(VC), grp(KS), grp(VS),
            kw_new.reshape(1, bsz, w_rows, G_NSA, HD), vw_new.reshape(1, bsz, w_rows, G_NSA, HD))
    return x, outs


def _odd_layer(x, bsz, t, t_real, ow, state):
    (w_big, w_small, cgain, cflag, g_mix, w_out, wa_pad, ba, gla_gain, lb_pad, hg_gain) = ow
    z, zs = _proj(x, g_mix, w_big, w_small, cgain, cflag, False)
    z3 = z.reshape(bsz, t, N_BIG_ODD)
    zs3 = zs.reshape(bsz, t, HD)
    s_gla, s_hg = state if state is not None else (None, None)
    chunk = 64 if t >= 64 else 128
    o_gla, s_gla = _gla("gla", z3, zs3, (GQ, GK, GV, GG), H_GLA, DV_GLA, (wa_pad, ba), gla_gain, s_gla,
                        t_real=t_real, chunk=chunk, hb=H_GLA)
    o_hg, s_hg = _gla("hg", z3, zs3, (HQ, HF, HI, HO), H_HG, DV_HG, lb_pad, hg_gain, s_hg,
                      t_real=t_real, chunk=chunk, hb=H_HG)
    m = bsz * t
    x = _mm_res([o_gla.reshape(m, -1), o_hg.reshape(m, -1)], w_out, 0, x)
    return x, (s_gla[None], s_hg[None])


def kernel(x_prompt, x_sample, cache_fox_k, cache_fox_v, cache_fox_logf, cache_nsa_kc, cache_nsa_vc, cache_nsa_ks, cache_nsa_vs, state_nsa_kw, state_nsa_vw, state_gla, state_hgrn, page_table, norm_mix, norm_ffn, w_in_even, w_out_even, fox_b_f, fox_q_gain, fox_k_gain, nsa_q_gain, nsa_k_gain, nsa_cmp_pe_k, nsa_cmp_pe_v, nsa_cmp_wk, nsa_cmp_wv, nsa_gate_b, w_in_odd, w_out_odd, gla_wa2, gla_ba, gla_norm, hgrn_lb, hgrn_norm, ffn_w1, ffn_w3, ffn_w2):
    bp, tp, d = x_prompt.shape
    bs, ts, _ = x_sample.shape
    n_pool = cache_fox_k.shape[1]
    ts_pad = 8
    pe_k, wc_k = _cmp_weights(nsa_cmp_pe_k[0], nsa_cmp_wk[0])
    pe_v, wc_v = _cmp_weights(nsa_cmp_pe_v[0], nsa_cmp_wv[0])
    b_f_pad = _pad_cols(_row(fox_b_f[0]), HD)
    gb_pad = _pad_cols(jnp.concatenate([jnp.zeros((1, H_FOX), F32), _row(nsa_gate_b[0])], axis=1), HD)
    ew = _even_weights(w_in_even[0], fox_q_gain[0], fox_k_gain[0], nsa_q_gain[0], nsa_k_gain[0]) + (
        _row(norm_mix[0]), w_out_even, b_f_pad, gb_pad, pe_k, wc_k, pe_v, wc_v, _row(nsa_k_gain[0, 0]))
    wa_pad = jnp.pad(gla_wa2[0], ((0, HD - GLA_RANK), (0, 0))).astype(BF16)
    lb_pad = jnp.pad(hgrn_lb.astype(F32), ((0, 8 - hgrn_lb.shape[0]), (0, 0)), constant_values=-1e30)
    ow = _odd_weights(w_in_odd[0]) + (_row(norm_mix[1]), w_out_odd, wa_pad, _row(gla_ba[0]),
                                      _row(gla_norm[0]), lb_pad, _row(hgrn_norm[0]))
    ffn = [(_row(norm_ffn[i]), ffn_w1, ffn_w3, ffn_w2, i) for i in range(2)]

    flat = lambda c: c.reshape((c.shape[0] * c.shape[1], PAGE * c.shape[3], HD))
    lf_pool = cache_fox_logf.reshape(-1, PAGE * H_FOX)
    win = lambda s: s.reshape(bs, s.shape[2], G_NSA * HD)
    past = (page_table, 0, flat(cache_fox_k), flat(cache_fox_v), lf_pool, flat(cache_nsa_kc), flat(cache_nsa_vc),
            flat(cache_nsa_ks), flat(cache_nsa_vs), win(state_nsa_kw), win(state_nsa_vw))

    def run(x, bsz, t, t_real, past_, state_):
        x, ev = _even_layer(x, bsz, t, t_real, ew, past_)
        x = _ffn(x, *ffn[0])
        x, od = _odd_layer(x, bsz, t, t_real, ow, state_)
        x = _ffn(x, *ffn[1])
        return x, ev, od

    yp, ev_p, od_p = run(x_prompt.reshape(bp * tp, d), bp, tp, tp, None, None)
    xs = jnp.pad(x_sample, ((0, 0), (0, ts_pad - ts), (0, 0))).reshape(bs * ts_pad, d)
    ys, ev_s, od_s = run(xs, bs, ts_pad, ts, past, (state_gla.reshape(state_gla.shape[1:]),
                                                    state_hgrn.reshape(state_hgrn.shape[1:])))
    y_prompt = yp.reshape(bp, tp, d)
    y_sample = ys.reshape(bs, ts_pad, d)[:, :ts]
    outs = [y_prompt, y_sample]
    for a, b in zip(ev_p, ev_s):
        outs += [a, b]
    for a, b in zip(od_p, od_s):
        outs += [a, b]
    return tuple(outs)
```
